```python
import math
import functools
import jax
import jax.numpy as jnp
from jax import lax
import numpy as np

D_MODEL = 2048
BATCH = 16
SEQ = 2048
DEPTH = 1
DEC_BATCH = 128
DEC_SEQ = 4
PAST_LEN = 16384
PAGE_SIZE = 128

N_HEADS = 16
N_KV_HEADS = 2
HEAD_DIM = 64
Q_PER_KV = N_HEADS // N_KV_HEADS
WINDOW = 128
ATTN_WIDTH = N_HEADS * HEAD_DIM
KV_WIDTH = N_KV_HEADS * HEAD_DIM
M_HEADS = 4
M_QK_DIM = 128
M_V_DIM = 256
M_QK_WIDTH = M_HEADS * M_QK_DIM
M_V_WIDTH = M_HEADS * M_V_DIM
M_CHUNK = 64
N_BRANCHES = 2
IN_SPLITS = (ATTN_WIDTH, KV_WIDTH, KV_WIDTH, M_QK_WIDTH, M_QK_WIDTH, M_V_WIDTH, M_V_WIDTH, M_HEADS, M_HEADS, N_BRANCHES * D_MODEL)
IN_WIDTH = sum(IN_SPLITS)
N_EXPERTS = 32
TOP_K = 4
D_FF = D_MODEL
SWIGLU_LIMIT = 7.0
SWIGLU_ALPHA = 1.702
MOE_BLOCK = 128
RMS_EPS = 1e-5
NEG = -1e30

kernel_name = 'hybrid_swa_sink_mlstm_moe_step'


def rms_norm(x, g):
    xf = x.astype(jnp.float32)
    y = xf * lax.rsqrt(jnp.mean(xf * xf, axis=-1, keepdims=True) + RMS_EPS)
    return (y * g.astype(jnp.float32)).astype(x.dtype)


def split_cols(z):
    parts = []
    off = 0
    for width in IN_SPLITS:
        parts.append(z[..., off:off + width])
        off += width
    return parts


def sink_softmax(s, mask, sink):
    s = jnp.where(mask, s, NEG)
    m = jnp.maximum(jnp.max(s, axis=-1, keepdims=True), sink)
    p = jnp.exp(s - m)
    return p / (jnp.sum(p, axis=-1, keepdims=True) + jnp.exp(sink - m))


def swa_prompt(q, k, v, sinks):
    b, s = q.shape[0], q.shape[1]
    nb = s // WINDOW
    qb = q.reshape(b, nb, WINDOW, N_KV_HEADS, Q_PER_KV, HEAD_DIM)
    kb = k.reshape(b, nb, WINDOW, N_KV_HEADS, HEAD_DIM)
    vb = v.reshape(b, nb, WINDOW, N_KV_HEADS, HEAD_DIM)
    pad = ((0, 0), (1, 0), (0, 0), (0, 0), (0, 0))
    kk = jnp.concatenate([jnp.pad(kb, pad)[:, :-1], kb], axis=2)
    vv = jnp.concatenate([jnp.pad(vb, pad)[:, :-1], vb], axis=2)
    sc = jnp.einsum('bnqkgd,bnjkd->bnkgqj', qb, kk, preferred_element_type=jnp.float32) * (HEAD_DIM ** -0.5)
    blk = jnp.arange(nb)[:, None, None] * WINDOW
    qpos = blk + jnp.arange(WINDOW)[None, :, None]
    kpos = blk - WINDOW + jnp.arange(2 * WINDOW)[None, None, :]
    mask = (kpos <= qpos) & (kpos > qpos - WINDOW) & (kpos >= 0)
    sink = sinks.astype(jnp.float32).reshape(N_KV_HEADS, Q_PER_KV)[:, :, None, None]
    p = sink_softmax(sc, mask[None, :, None, None], sink)
    o = jnp.einsum('bnkgqj,bnjkd->bnqkgd', p.astype(v.dtype), vv)
    return o.reshape(b, s, ATTN_WIDTH), k[:, s - WINDOW:], v[:, s - WINDOW:]


def swa_sample(q, k, v, sinks, cache_k, cache_v):
    b, l = q.shape[0], q.shape[1]
    kk = jnp.concatenate([cache_k.astype(k.dtype), k], axis=1)
    vv = jnp.concatenate([cache_v.astype(v.dtype), v], axis=1)
    qg = q.reshape(b, l, N_KV_HEADS, Q_PER_KV, HEAD_DIM)
    sc = jnp.einsum('bqkgd,bjkd->bkgqj', qg, kk, preferred_element_type=jnp.float32) * (HEAD_DIM ** -0.5)
    qpos = jnp.arange(l)[:, None]
    kpos = jnp.arange(WINDOW + l)[None, :] - WINDOW
    mask = (kpos <= qpos) & (kpos > qpos - WINDOW)
    sink = sinks.astype(jnp.float32).reshape(N_KV_HEADS, Q_PER_KV)[:, :, None, None]
    p = sink_softmax(sc, mask, sink)
    o = jnp.einsum('bkgqj,bjkd->bqkgd', p.astype(v.dtype), vv)
    return o.reshape(b, l, ATTN_WIDTH), kk[:, l:], vv[:, l:]


def mlstm_chunkwise(q, k, v, ig, logf, c0, n0, m0):
    b, h, l = q.shape[0], q.shape[1], q.shape[2]
    lc = math.gcd(l, M_CHUNK)
    nc = l // lc

    def chunks(t):
        return jnp.moveaxis(t.reshape((b, h, nc, lc) + t.shape[3:]), 2, 0)

    causal = jnp.tril(jnp.ones((lc, lc), dtype=bool))

    def step(carry, xs):
        c, n, m = carry
        qc, kc, vc, ic, fc = xs
        bcum = jnp.cumsum(fc, axis=-1)
        dlog = jnp.where(causal, bcum[..., :, None] - bcum[..., None, :] + ic[..., None, :], NEG)
        inter = bcum + m[..., None]
        mt = jnp.maximum(inter, jnp.max(dlog, axis=-1))
        w_inter = jnp.exp(inter - mt)
        qk = jnp.einsum('bhtd,bhsd->bhts', qc, kc) * jnp.exp(dlog - mt[..., None])
        num = w_inter[..., None] * jnp.einsum('bhtd,bhde->bhte', qc, c) + jnp.einsum('bhts,bhse->bhte', qk, vc)
        den = w_inter * jnp.einsum('bhtd,bhd->bht', qc, n) + jnp.sum(qk, axis=-1)
        h_t = num / jnp.maximum(jnp.abs(den), jnp.exp(-mt))[..., None]
        m_new = mt[..., -1]
        w_state = jnp.exp(bcum[..., -1:] - bcum + ic - m_new[..., None])
        decay = jnp.exp(inter[..., -1] - m_new)
        kw = kc * w_state[..., None]
        c_new = decay[..., None, None] * c + jnp.einsum('bhsd,bhse->bhde', kw, vc)
        n_new = decay[..., None] * n + jnp.sum(kw, axis=2)
        return (c_new, n_new, m_new), h_t

    (c, n, m), hs = lax.scan(step, (c0, n0, m0), (chunks(q), chunks(k), chunks(v), chunks(ig), chunks(logf)))
    hs = jnp.moveaxis(hs, 0, 2).reshape(b, h, l, v.shape[-1])
    return hs, c, n, m


def mlstm_branch(mq, mk, mv, mo, mi, mf, norm_w, c0, n0, m0):
    b, l = mq.shape[0], mq.shape[1]
    f32 = jnp.float32

    def heads(t, d):
        return t.astype(f32).reshape(b, l, M_HEADS, d).transpose(0, 2, 1, 3)

    q = heads(mq, M_QK_DIM)
    k = heads(mk, M_QK_DIM) * (M_QK_DIM ** -0.5)
    v = heads(mv, M_V_DIM)
    ig = mi.astype(f32).transpose(0, 2, 1)
    logf = jax.nn.log_sigmoid(mf.astype(f32)).transpose(0, 2, 1)
    hs, c, n, m = mlstm_chunkwise(q, k, v, ig, logf, c0.astype(f32), n0.astype(f32), m0.astype(f32))
    hs = hs * lax.rsqrt(jnp.mean(hs * hs, axis=-1, keepdims=True) + RMS_EPS)
    hs = hs.transpose(0, 2, 1, 3).reshape(b, l, M_V_WIDTH) * norm_w.astype(f32)
    out = hs * jax.nn.sigmoid(mo.astype(f32))
    return out.astype(mq.dtype), c, n, m


def expert_ffn(xb, w1, b1, w2, b2):
    z = jnp.dot(xb, w1) + b1
    gate = jnp.minimum(z[..., :D_FF], SWIGLU_LIMIT)
    up = jnp.clip(z[..., D_FF:], -SWIGLU_LIMIT, SWIGLU_LIMIT)
    a = gate * jax.nn.sigmoid(SWIGLU_ALPHA * gate) * (up + 1.0)
    return jnp.dot(a, w2) + b2


def moe(h, w_router, b_router, w_e1, b_e1, w_e2, b_e2):
    shp = h.shape
    x = h.reshape(-1, D_MODEL)
    t = x.shape[0]
    logits = jnp.dot(x.astype(jnp.float32), w_router.astype(jnp.float32)) + b_router.astype(jnp.float32)
    top_v, top_e = lax.top_k(logits, TOP_K)
    gates = jax.nn.softmax(top_v, axis=-1)
    n_assign = t * TOP_K
    flat_e = top_e.reshape(-1)
    flat_tok = jnp.arange(n_assign, dtype=jnp.int32) // TOP_K
    flat_g = gates.reshape(-1)
    order = jnp.argsort(flat_e)
    se = flat_e[order]
    counts = jnp.bincount(flat_e, length=N_EXPERTS)
    starts = jnp.cumsum(counts) - counts
    padded = (counts + MOE_BLOCK - 1) // MOE_BLOCK * MOE_BLOCK
    pends = jnp.cumsum(padded)
    pstarts = pends - padded
    dest = pstarts[se] + jnp.arange(n_assign) - starts[se]
    n_blocks = -(-n_assign // MOE_BLOCK) + N_EXPERTS
    n_slots = n_blocks * MOE_BLOCK
    slot_tok = jnp.zeros((n_slots,), jnp.int32).at[dest].set(flat_tok[order])
    slot_g = jnp.zeros((n_slots,), jnp.float32).at[dest].set(flat_g[order])
    block_e = jnp.minimum(jnp.searchsorted(pends, jnp.arange(n_blocks) * MOE_BLOCK, side='right'), N_EXPERTS - 1)
    xs = x[slot_tok].reshape(n_blocks, MOE_BLOCK, D_MODEL)
    gs = slot_g.reshape(n_blocks, MOE_BLOCK)

    def run_block(args):
        xb, e, g = args
        y = expert_ffn(xb, w_e1[e], b_e1[e], w_e2[e], b_e2[e])
        return y.astype(jnp.float32) * g[:, None]

    ys = lax.map(run_block, (xs, block_e, gs)).reshape(n_slots, D_MODEL)
    out = jax.ops.segment_sum(ys, slot_tok, num_segments=t)
    return out.astype(h.dtype).reshape(shp)


def decoder_layer(x, attend, c0, n0, m0, g_mix, w_in, b_if, sinks, norm_w, w_ap, w_mp, w_out,
                  g_ffn, w_router, b_router, w_e1, b_e1, w_e2, b_e2):
    b, l = x.shape[0], x.shape[1]
    h = rms_norm(x, g_mix)
    z = jnp.dot(h, w_in)
    aq, ak, av, mq, mk, mv, mo, mi, mf, gz = split_cols(z)
    a, new_k, new_v = attend(aq.reshape(b, l, N_HEADS, HEAD_DIM), ak.reshape(b, l, N_KV_HEADS, HEAD_DIM),
                             av.reshape(b, l, N_KV_HEADS, HEAD_DIM), sinks)
    mh, c, n, m = mlstm_branch(mq, mk, mv, mo, mi + b_if[:M_HEADS], mf + b_if[M_HEADS:], norm_w, c0, n0, m0)
    gate = jax.nn.sigmoid(gz.astype(jnp.float32))
    merged = gate[..., :D_MODEL] * jnp.dot(a, w_ap) + gate[..., D_MODEL:] * jnp.dot(mh, w_mp)
    x = x + jnp.dot(merged.astype(x.dtype), w_out)
    x = x + moe(rms_norm(x, g_ffn), w_router, b_router, w_e1, b_e1, w_e2, b_e2)
    return x, new_k, new_v, c, n, m


def setup_inputs(seed: int = 0) -> dict:
    key = jax.random.key(seed)
    ks = jax.random.split(key, 26)
    f32 = jnp.float32

    def nrm(k, shape, scale):
        return jax.random.normal(k, shape, f32) * scale

    return {
        'x_prompt': nrm(ks[0], (BATCH, SEQ, D_MODEL), 1.0),
        'x_sample': nrm(ks[1], (DEC_BATCH, DEC_SEQ, D_MODEL), 1.0),
        'cache_swa_k': nrm(ks[2], (DEPTH, DEC_BATCH, WINDOW, N_KV_HEADS, HEAD_DIM), 1.0),
        'cache_swa_v': nrm(ks[3], (DEPTH, DEC_BATCH, WINDOW, N_KV_HEADS, HEAD_DIM), 1.0),
        'state_mlstm_C': nrm(ks[4], (DEPTH, DEC_BATCH, M_HEADS, M_QK_DIM, M_V_DIM), 0.1),
        'state_mlstm_n': nrm(ks[5], (DEPTH, DEC_BATCH, M_HEADS, M_QK_DIM), 0.1),
        'state_mlstm_m': nrm(ks[6], (DEPTH, DEC_BATCH, M_HEADS), 0.5),
        'g_mix': 1.0 + nrm(ks[7], (DEPTH, D_MODEL), 0.01),
        'w_in': nrm(ks[8], (DEPTH, D_MODEL, IN_WIDTH), D_MODEL ** -0.5),
        'b_if': jnp.concatenate([nrm(ks[9], (DEPTH, M_HEADS), 0.1),
                                 3.0 + nrm(ks[10], (DEPTH, M_HEADS), 0.5)], axis=-1),
        'attn_sinks': nrm(ks[11], (DEPTH, N_HEADS), 0.5),
        'mlstm_norm_w': 1.0 + nrm(ks[12], (DEPTH, M_V_WIDTH), 0.01),
        'w_attn_proj': nrm(ks[13], (DEPTH, ATTN_WIDTH, D_MODEL), ATTN_WIDTH ** -0.5),
        'w_mlstm_proj': nrm(ks[14], (DEPTH, M_V_WIDTH, D_MODEL), M_V_WIDTH ** -0.5),
        'w_out': nrm(ks[15], (DEPTH, D_MODEL, D_MODEL), D_MODEL ** -0.5),
        'g_ffn': 1.0 + nrm(ks[16], (DEPTH, D_MODEL), 0.01),
        'w_router': nrm(ks[17], (DEPTH, D_MODEL, N_EXPERTS), D_MODEL ** -0.5),
        'b_router': nrm(ks[18], (DEPTH, N_EXPERTS), 0.01),
        'w_e1': nrm(ks[19], (DEPTH, N_EXPERTS, D_MODEL, 2 * D_FF), D_MODEL ** -0.5),
        'b_e1': nrm(ks[20], (DEPTH, N_EXPERTS, 2 * D_FF), 0.01),
        'w_e2': nrm(ks[21], (DEPTH, N_EXPERTS, D_FF, D_MODEL), D_FF ** -0.5),
        'b_e2': nrm(ks[22], (DEPTH, N_EXPERTS, D_MODEL), 0.01),
        'g_final': 1.0 + nrm(ks[23], (D_MODEL,), 0.01),
    }


def reference(x_prompt, x_sample, cache_swa_k, cache_swa_v, state_mlstm_C, state_mlstm_n, state_mlstm_m,
              g_mix, w_in, b_if, attn_sinks, mlstm_norm_w, w_attn_proj, w_mlstm_proj, w_out,
              g_ffn, w_router, b_router, w_e1, b_e1, w_e2, b_e2, g_final):
    bp = x_prompt.shape[0]
    yp, ys = x_prompt, x_sample
    pk, pv, pc, pn, pm = [], [], [], [], []
    sk, sv, sc, sn, sm = [], [], [], [], []
    for l in range(DEPTH):
        lp = (g_mix[l], w_in[l], b_if[l], attn_sinks[l], mlstm_norm_w[l], w_attn_proj[l], w_mlstm_proj[l],
              w_out[l], g_ffn[l], w_router[l], b_router[l], w_e1[l], b_e1[l], w_e2[l], b_e2[l])
        c0 = jnp.zeros((bp, M_HEADS, M_QK_DIM, M_V_DIM), jnp.float32)
        n0 = jnp.zeros((bp, M_HEADS, M_QK_DIM), jnp.float32)
        m0 = jnp.full((bp, M_HEADS), NEG, jnp.float32)
        yp, k1, v1, c1, n1, m1 = decoder_layer(yp, swa_prompt, c0, n0, m0, *lp)
        attend_s = functools.partial(swa_sample, cache_k=cache_swa_k[l], cache_v=cache_swa_v[l])
        ys, k2, v2, c2, n2, m2 = decoder_layer(ys, attend_s, state_mlstm_C[l], state_mlstm_n[l], state_mlstm_m[l], *lp)
        pk.append(k1); pv.append(v1); pc.append(c1); pn.append(n1); pm.append(m1)
        sk.append(k2); sv.append(v2); sc.append(c2); sn.append(n2); sm.append(m2)
    y_prompt = rms_norm(yp, g_final)
    y_sample = rms_norm(ys, g_final)
    prompt_swa_k = jnp.stack(pk, 0)
    prompt_swa_v = jnp.stack(pv, 0)
    prompt_C = jnp.stack(pc, 0)
    prompt_n = jnp.stack(pn, 0)
    prompt_m = jnp.stack(pm, 0)
    sample_swa_k = jnp.stack(sk, 0)
    sample_swa_v = jnp.stack(sv, 0)
    sample_C = jnp.stack(sc, 0)
    sample_n = jnp.stack(sn, 0)
    sample_m = jnp.stack(sm, 0)
    return (y_prompt, y_sample, prompt_swa_k, prompt_swa_v, prompt_C, prompt_n, prompt_m,
            sample_swa_k, sample_swa_v, sample_C, sample_n, sample_m)
```

```python
import functools
import math

import jax
import jax.numpy as jnp
from jax import lax
from jax.experimental import pallas as pl
from jax.experimental.pallas import tpu as pltpu

D_MODEL = 2048
DEPTH = 1
N_HEADS = 16
N_KV_HEADS = 2
HEAD_DIM = 64
Q_PER_KV = N_HEADS // N_KV_HEADS
WINDOW = 128
ATTN_WIDTH = N_HEADS * HEAD_DIM
KV_WIDTH = N_KV_HEADS * HEAD_DIM
M_HEADS = 4
M_QK_DIM = 128
M_V_DIM = 256
M_QK_WIDTH = M_HEADS * M_QK_DIM
M_V_WIDTH = M_HEADS * M_V_DIM
M_CHUNK = 64
N_EXPERTS = 32
TOP_K = 4
D_FF = D_MODEL
SWIGLU_LIMIT = 7.0
SWIGLU_ALPHA = 1.702
RMS_EPS = 1e-5
NEG = -1e30

F32 = jnp.float32
BF16 = jnp.bfloat16

LANES = 128
VMEM_LIMIT_BYTES = 56 * 1024 * 1024

COL_GZ = 0
COL_AQ = COL_GZ + 2 * D_MODEL
COL_AK = COL_AQ + ATTN_WIDTH
COL_AV = COL_AK + KV_WIDTH
COL_MQ = COL_AV + KV_WIDTH
COL_MK = COL_MQ + M_QK_WIDTH
COL_MV = COL_MK + M_QK_WIDTH
COL_MO = COL_MV + M_V_WIDTH
COL_IF = COL_MO + M_V_WIDTH
IN_TILE_N = 512
IN_WIDTH_PAD = -(-(COL_IF + LANES) // IN_TILE_N) * IN_TILE_N

MOE_BLOCK_M = 512
MOE_FF_CHUNK = 1024
ROUTER_PAD = LANES


def _permute_w_in(w_in):
    splits = (ATTN_WIDTH, KV_WIDTH, KV_WIDTH, M_QK_WIDTH, M_QK_WIDTH, M_V_WIDTH, M_V_WIDTH, M_HEADS, M_HEADS,
              2 * D_MODEL)
    parts, off = [], 0
    for width in splits:
        parts.append(w_in[:, off:off + width])
        off += width
    aq, ak, av, mq, mk, mv, mo, mi, mf, gz = parts
    cols = jnp.concatenate([gz, aq, ak, av, mq, mk, mv, mo, mi, mf], axis=1)
    cols = jnp.pad(cols, ((0, 0), (0, IN_WIDTH_PAD - cols.shape[1])))
    return cols.astype(BF16)


def _inproj_body(x_ref, g_ref, w_ref, z_ref, h_scr):
    @pl.when(pl.program_id(1) == 0)
    def _():
        x = x_ref[...]
        y = x * lax.rsqrt(jnp.mean(x * x, axis=-1, keepdims=True) + RMS_EPS)
        h_scr[...] = (y * g_ref[...]).astype(BF16)

    z_ref[...] = jnp.dot(h_scr[...], w_ref[...], preferred_element_type=F32)


def _inproj(x, g, w_bf16, tile_m):
    t = x.shape[0]
    n = w_bf16.shape[1]
    return pl.pallas_call(
        _inproj_body,
        grid=(t // tile_m, n // IN_TILE_N),
        in_specs=[
            pl.BlockSpec((tile_m, D_MODEL), lambda i, j: (i, 0)),
            pl.BlockSpec((1, D_MODEL), lambda i, j: (0, 0)),
            pl.BlockSpec((D_MODEL, IN_TILE_N), lambda i, j: (0, j)),
        ],
        out_specs=pl.BlockSpec((tile_m, IN_TILE_N), lambda i, j: (i, j)),
        out_shape=jax.ShapeDtypeStruct((t, n), F32),
        scratch_shapes=[pltpu.VMEM((tile_m, D_MODEL), BF16)],
        compiler_params=pltpu.CompilerParams(
            dimension_semantics=("arbitrary", "arbitrary"), vmem_limit_bytes=VMEM_LIMIT_BYTES),
        name="inproj",
    )(x, g.reshape(1, D_MODEL), w_bf16)


def _merge_body(x_ref, a_ref, mh_ref, gza_ref, gzm_ref, wa_ref, wm_ref, wo_ref, gf_ref, wr_ref, br_ref,
                x1_ref, h2_ref, lg_ref):
    pa = jnp.dot(a_ref[...], wa_ref[...], preferred_element_type=F32)
    pm = jnp.dot(mh_ref[...], wm_ref[...], preferred_element_type=F32)
    merged = jax.nn.sigmoid(gza_ref[...]) * pa + jax.nn.sigmoid(gzm_ref[...]) * pm
    x1 = x_ref[...] + jnp.dot(merged.astype(BF16), wo_ref[...], preferred_element_type=F32)
    x1_ref[...] = x1
    h2 = x1 * lax.rsqrt(jnp.mean(x1 * x1, axis=-1, keepdims=True) + RMS_EPS) * gf_ref[...]
    h2_ref[...] = h2.astype(BF16)
    lg_ref[...] = jnp.dot(h2, wr_ref[...], preferred_element_type=F32,
                          precision=lax.Precision.HIGHEST) + br_ref[...]


def _merge(x, a, mh, z, wa, wm, wo, g_ffn, wr_pad, br_pad, tile_m):
    t = x.shape[0]
    const = lambda i: (0, 0)
    return pl.pallas_call(
        _merge_body,
        grid=(t // tile_m,),
        in_specs=[
            pl.BlockSpec((tile_m, D_MODEL), lambda i: (i, 0)),
            pl.BlockSpec((tile_m, ATTN_WIDTH), lambda i: (i, 0)),
            pl.BlockSpec((tile_m, M_V_WIDTH), lambda i: (i, 0)),
            pl.BlockSpec((tile_m, D_MODEL), lambda i: (i, COL_GZ // D_MODEL)),
            pl.BlockSpec((tile_m, D_MODEL), lambda i: (i, COL_GZ // D_MODEL + 1)),
            pl.BlockSpec((ATTN_WIDTH, D_MODEL), const, pipeline_mode=pl.Buffered(1)),
            pl.BlockSpec((M_V_WIDTH, D_MODEL), const, pipeline_mode=pl.Buffered(1)),
            pl.BlockSpec((D_MODEL, D_MODEL), const, pipeline_mode=pl.Buffered(1)),
            pl.BlockSpec((1, D_MODEL), const),
            pl.BlockSpec((D_MODEL, ROUTER_PAD), const),
            pl.BlockSpec((1, ROUTER_PAD), const),
        ],
        out_specs=[
            pl.BlockSpec((tile_m, D_MODEL), lambda i: (i, 0)),
            pl.BlockSpec((tile_m, D_MODEL), lambda i: (i, 0)),
            pl.BlockSpec((tile_m, ROUTER_PAD), lambda i: (i, 0)),
        ],
        out_shape=[
            jax.ShapeDtypeStruct((t, D_MODEL), F32),
            jax.ShapeDtypeStruct((t, D_MODEL), BF16),
            jax.ShapeDtypeStruct((t, ROUTER_PAD), F32),
        ],
        compiler_params=pltpu.CompilerParams(
            dimension_semantics=("arbitrary",), vmem_limit_bytes=VMEM_LIMIT_BYTES),
        name="merge",
    )(x, a, mh, z, z, wa, wm, wo, g_ffn.reshape(1, D_MODEL), wr_pad, br_pad)


def _expert_body(be_ref, nb_ref, xs_ref, w1g_ref, w1u_ref, b1g_ref, b1u_ref, w2_ref, b2_ref, y_ref):
    i = pl.program_id(0)
    c = pl.program_id(1)
    used = i < nb_ref[0]

    @pl.when(used)
    def _():
        x = xs_ref[...]
        zg = jnp.dot(x, w1g_ref[...], preferred_element_type=F32) + b1g_ref[...]
        zu = jnp.dot(x, w1u_ref[...], preferred_element_type=F32) + b1u_ref[...]
        gate = jnp.minimum(zg, SWIGLU_LIMIT)
        up = jnp.clip(zu, -SWIGLU_LIMIT, SWIGLU_LIMIT)
        act = gate * jax.nn.sigmoid(SWIGLU_ALPHA * gate) * (up + 1.0)
        y = jnp.dot(act.astype(BF16), w2_ref[...], preferred_element_type=F32)

        @pl.when(c == 0)
        def _():
            y_ref[...] = y + b2_ref[...]

        @pl.when(c != 0)
        def _():
            y_ref[...] += y

    @pl.when(jnp.logical_not(used) & (c == 0))
    def _():
        y_ref[...] = jnp.zeros_like(y_ref)


def _experts(block_e, n_used, xs, w1, b1, w2, b2):
    n_slots = xs.shape[0]
    n_blocks = n_slots // MOE_BLOCK_M
    n_chunks = D_FF // MOE_FF_CHUNK

    def chunk(i, c, nb):
        return jnp.where(i < nb[0], c, n_chunks - 1)

    grid_spec = pltpu.PrefetchScalarGridSpec(
        num_scalar_prefetch=2,
        grid=(n_blocks, n_chunks),
        in_specs=[
            pl.BlockSpec((MOE_BLOCK_M, D_MODEL), lambda i, c, be, nb: (i, 0)),
            pl.BlockSpec((None, D_MODEL, MOE_FF_CHUNK), lambda i, c, be, nb: (be[i], 0, chunk(i, c, nb))),
            pl.BlockSpec((None, D_MODEL, MOE_FF_CHUNK),
                         lambda i, c, be, nb: (be[i], 0, n_chunks + chunk(i, c, nb))),
            pl.BlockSpec((None, 1, MOE_FF_CHUNK), lambda i, c, be, nb: (be[i], 0, chunk(i, c, nb))),
            pl.BlockSpec((None, 1, MOE_FF_CHUNK), lambda i, c, be, nb: (be[i], 0, n_chunks + chunk(i, c, nb))),
            pl.BlockSpec((None, MOE_FF_CHUNK, D_MODEL), lambda i, c, be, nb: (be[i], chunk(i, c, nb), 0)),
            pl.BlockSpec((None, 1, D_MODEL), lambda i, c, be, nb: (be[i], 0, 0)),
        ],
        out_specs=pl.BlockSpec((MOE_BLOCK_M, D_MODEL), lambda i, c, be, nb: (i, 0)),
    )
    return pl.pallas_call(
        _expert_body,
        grid_spec=grid_spec,
        out_shape=jax.ShapeDtypeStruct((n_slots, D_MODEL), F32),
        compiler_params=pltpu.CompilerParams(
            dimension_semantics=("arbitrary", "arbitrary"), vmem_limit_bytes=VMEM_LIMIT_BYTES),
        name="experts",
    )(block_e, n_used, xs, w1, w1, b1, b1, w2, b2)


def _sink_softmax(s, mask, sink):
    s = jnp.where(mask, s, NEG)
    m = jnp.maximum(jnp.max(s, axis=-1, keepdims=True), sink)
    p = jnp.exp(s - m)
    return p / (jnp.sum(p, axis=-1, keepdims=True) + jnp.exp(sink - m))


def _swa_prompt(q, k, v, sinks):
    b, s = q.shape[0], q.shape[1]
    nb = s // WINDOW
    qb = q.reshape(b, nb, WINDOW, N_KV_HEADS, Q_PER_KV, HEAD_DIM)
    kb = k.reshape(b, nb, WINDOW, N_KV_HEADS, HEAD_DIM)
    vb = v.reshape(b, nb, WINDOW, N_KV_HEADS, HEAD_DIM)
    pad = ((0, 0), (1, 0), (0, 0), (0, 0), (0, 0))
    kk = jnp.concatenate([jnp.pad(kb, pad)[:, :-1], kb], axis=2)
    vv = jnp.concatenate([jnp.pad(vb, pad)[:, :-1], vb], axis=2)
    sc = jnp.einsum('bnqkgd,bnjkd->bnkgqj', qb, kk, preferred_element_type=F32) * (HEAD_DIM ** -0.5)
    blk = jnp.arange(nb)[:, None, None] * WINDOW
    qpos = blk + jnp.arange(WINDOW)[None, :, None]
    kpos = blk - WINDOW + jnp.arange(2 * WINDOW)[None, None, :]
    mask = (kpos <= qpos) & (kpos > qpos - WINDOW) & (kpos >= 0)
    sink = sinks.astype(F32).reshape(N_KV_HEADS, Q_PER_KV)[:, :, None, None]
    p = _sink_softmax(sc, mask[None, :, None, None], sink)
    o = jnp.einsum('bnkgqj,bnjkd->bnqkgd', p, vv)
    return o.reshape(b, s, ATTN_WIDTH), k[:, s - WINDOW:], v[:, s - WINDOW:]


def _swa_sample(q, k, v, sinks, cache_k, cache_v):
    b, l = q.shape[0], q.shape[1]
    kk = jnp.concatenate([cache_k, k], axis=1)
    vv = jnp.concatenate([cache_v, v], axis=1)
    qg = q.reshape(b, l, N_KV_HEADS, Q_PER_KV, HEAD_DIM)
    sc = jnp.einsum('bqkgd,bjkd->bkgqj', qg, kk, preferred_element_type=F32) * (HEAD_DIM ** -0.5)
    qpos = jnp.arange(l)[:, None]
    kpos = jnp.arange(WINDOW + l)[None, :] - WINDOW
    mask = (kpos <= qpos) & (kpos > qpos - WINDOW)
    sink = sinks.astype(F32).reshape(N_KV_HEADS, Q_PER_KV)[:, :, None, None]
    p = _sink_softmax(sc, mask, sink)
    o = jnp.einsum('bkgqj,bjkd->bqkgd', p, vv)
    return o.reshape(b, l, ATTN_WIDTH), kk[:, l:], vv[:, l:]


def _mlstm_chunkwise(q, k, v, ig, logf, c0, n0, m0):
    b, h, l = q.shape[0], q.shape[1], q.shape[2]
    lc = math.gcd(l, M_CHUNK)
    nc = l // lc

    def chunks(t):
        return jnp.moveaxis(t.reshape((b, h, nc, lc) + t.shape[3:]), 2, 0)

    causal = jnp.tril(jnp.ones((lc, lc), dtype=bool))

    def step(carry, xs):
        c, n, m = carry
        qc, kc, vc, ic, fc = xs
        bcum = jnp.cumsum(fc, axis=-1)
        dlog = jnp.where(causal, bcum[..., :, None] - bcum[..., None, :] + ic[..., None, :], NEG)
        inter = bcum + m[..., None]
        mt = jnp.maximum(inter, jnp.max(dlog, axis=-1))
        w_inter = jnp.exp(inter - mt)
        qk = jnp.einsum('bhtd,bhsd->bhts', qc, kc) * jnp.exp(dlog - mt[..., None])
        num = w_inter[..., None] * jnp.einsum('bhtd,bhde->bhte', qc, c) + jnp.einsum('bhts,bhse->bhte', qk, vc)
        den = w_inter * jnp.einsum('bhtd,bhd->bht', qc, n) + jnp.sum(qk, axis=-1)
        h_t = num / jnp.maximum(jnp.abs(den), jnp.exp(-mt))[..., None]
        m_new = mt[..., -1]
        w_state = jnp.exp(bcum[..., -1:] - bcum + ic - m_new[..., None])
        decay = jnp.exp(inter[..., -1] - m_new)
        kw = kc * w_state[..., None]
        c_new = decay[..., None, None] * c + jnp.einsum('bhsd,bhse->bhde', kw, vc)
        n_new = decay[..., None] * n + jnp.sum(kw, axis=2)
        return (c_new, n_new, m_new), h_t

    (c, n, m), hs = lax.scan(step, (c0, n0, m0), (chunks(q), chunks(k), chunks(v), chunks(ig), chunks(logf)))
    hs = jnp.moveaxis(hs, 0, 2).reshape(b, h, l, v.shape[-1])
    return hs, c, n, m


def _mlstm_branch(mq, mk, mv, mo, mi, mf, norm_w, c0, n0, m0):
    b, l = mq.shape[0], mq.shape[1]

    def heads(t, d):
        return t.reshape(b, l, M_HEADS, d).transpose(0, 2, 1, 3)

    q = heads(mq, M_QK_DIM)
    k = heads(mk, M_QK_DIM) * (M_QK_DIM ** -0.5)
    v = heads(mv, M_V_DIM)
    ig = mi.transpose(0, 2, 1)
    logf = jax.nn.log_sigmoid(mf).transpose(0, 2, 1)
    hs, c, n, m = _mlstm_chunkwise(q, k, v, ig, logf, c0, n0, m0)
    hs = hs * lax.rsqrt(jnp.mean(hs * hs, axis=-1, keepdims=True) + RMS_EPS)
    hs = hs.transpose(0, 2, 1, 3).reshape(b, l, M_V_WIDTH) * norm_w
    return hs * jax.nn.sigmoid(mo), c, n, m


def _mixers(z, b, l, attend, b_if, norm_w, c0, n0, m0, sinks):
    z3 = z.reshape(b, l, -1)
    aq = z3[..., COL_AQ:COL_AQ + ATTN_WIDTH].reshape(b, l, N_HEADS, HEAD_DIM)
    ak = z3[..., COL_AK:COL_AK + KV_WIDTH].reshape(b, l, N_KV_HEADS, HEAD_DIM)
    av = z3[..., COL_AV:COL_AV + KV_WIDTH].reshape(b, l, N_KV_HEADS, HEAD_DIM)
    a, new_k, new_v = attend(aq, ak, av, sinks)
    mq = z3[..., COL_MQ:COL_MQ + M_QK_WIDTH]
    mk = z3[..., COL_MK:COL_MK + M_QK_WIDTH]
    mv = z3[..., COL_MV:COL_MV + M_V_WIDTH]
    mo = z3[..., COL_MO:COL_MO + M_V_WIDTH]
    mi = z3[..., COL_IF:COL_IF + M_HEADS] + b_if[:M_HEADS]
    mf = z3[..., COL_IF + M_HEADS:COL_IF + 2 * M_HEADS] + b_if[M_HEADS:]
    mh, c, n, m = _mlstm_branch(mq, mk, mv, mo, mi, mf, norm_w, c0, n0, m0)
    return (a.reshape(b * l, ATTN_WIDTH).astype(BF16), mh.reshape(b * l, M_V_WIDTH).astype(BF16),
            new_k, new_v, c, n, m)


def _route(logits):
    top_v, top_e = lax.top_k(logits[:, :N_EXPERTS], TOP_K)
    return jax.nn.softmax(top_v, axis=-1), top_e


def _dispatch(top_e):
    t = top_e.shape[0]
    n_assign = t * TOP_K
    flat_e = top_e.reshape(-1)
    order = jnp.argsort(flat_e)
    se = flat_e[order]
    counts = jnp.bincount(flat_e, length=N_EXPERTS)
    starts = jnp.cumsum(counts) - counts
    padded = (counts + MOE_BLOCK_M - 1) // MOE_BLOCK_M * MOE_BLOCK_M
    pends = jnp.cumsum(padded)
    pstarts = pends - padded
    dest_sorted = pstarts[se] + jnp.arange(n_assign) - starts[se]
    n_blocks = -(-n_assign // MOE_BLOCK_M) + N_EXPERTS
    n_slots = n_blocks * MOE_BLOCK_M
    slot_tok = jnp.zeros((n_slots,), jnp.int32).at[dest_sorted].set((order // TOP_K).astype(jnp.int32))
    dest = jnp.zeros((n_assign,), jnp.int32).at[order].set(dest_sorted.astype(jnp.int32))
    block_e = jnp.minimum(jnp.searchsorted(pends, jnp.arange(n_blocks) * MOE_BLOCK_M, side='right'),
                          N_EXPERTS - 1).astype(jnp.int32)
    n_used = (pends[-1] // MOE_BLOCK_M).astype(jnp.int32).reshape(1)
    return slot_tok, dest.reshape(t, TOP_K), block_e, n_used


def _rms_norm(x, g):
    return x * lax.rsqrt(jnp.mean(x * x, axis=-1, keepdims=True) + RMS_EPS) * g


def kernel(x_prompt, x_sample, cache_swa_k, cache_swa_v, state_mlstm_C, state_mlstm_n, state_mlstm_m,
           g_mix, w_in, b_if, attn_sinks, mlstm_norm_w, w_attn_proj, w_mlstm_proj, w_out,
           g_ffn, w_router, b_router, w_e1, b_e1, w_e2, b_e2, g_final):
    bp, sp = x_prompt.shape[0], x_prompt.shape[1]
    bs, ls = x_sample.shape[0], x_sample.shape[1]
    tp, ts = bp * sp, bs * ls
    l = 0
    w_in_p = _permute_w_in(w_in[l])
    wa, wm, wo = w_attn_proj[l].astype(BF16), w_mlstm_proj[l].astype(BF16), w_out[l].astype(BF16)
    wr_pad = jnp.pad(w_router[l], ((0, 0), (0, ROUTER_PAD - N_EXPERTS)))
    br_pad = jnp.pad(b_router[l], (0, ROUTER_PAD - N_EXPERTS)).reshape(1, ROUTER_PAD)
    w1, w2 = w_e1[l].astype(BF16), w_e2[l].astype(BF16)
    b1, b2 = b_e1[l].reshape(N_EXPERTS, 1, 2 * D_FF), b_e2[l].reshape(N_EXPERTS, 1, D_MODEL)

    xp = x_prompt.reshape(tp, D_MODEL)
    xs = x_sample.reshape(ts, D_MODEL)
    zp = _inproj(xp, g_mix[l], w_in_p, 1024)
    zs = _inproj(xs, g_mix[l], w_in_p, 512)

    c0 = jnp.zeros((bp, M_HEADS, M_QK_DIM, M_V_DIM), F32)
    n0 = jnp.zeros((bp, M_HEADS, M_QK_DIM), F32)
    m0 = jnp.full((bp, M_HEADS), NEG, F32)
    ap, mhp, k1, v1, c1, n1, m1 = _mixers(zp, bp, sp, _swa_prompt, b_if[l], mlstm_norm_w[l], c0, n0, m0,
                                          attn_sinks[l])
    attend_s = functools.partial(_swa_sample, cache_k=cache_swa_k[l], cache_v=cache_swa_v[l])
    as_, mhs, k2, v2, c2, n2, m2 = _mixers(zs, bs, ls, attend_s, b_if[l], mlstm_norm_w[l], state_mlstm_C[l],
                                           state_mlstm_n[l], state_mlstm_m[l], attn_sinks[l])

    x1p, h2p, lgp = _merge(xp, ap, mhp, zp, wa, wm, wo, g_ffn[l], wr_pad, br_pad, 256)
    x1s, h2s, lgs = _merge(xs, as_, mhs, zs, wa, wm, wo, g_ffn[l], wr_pad, br_pad, 256)

    x1 = jnp.concatenate([x1p, x1s], axis=0)
    h2 = jnp.concatenate([h2p, h2s], axis=0)
    gates, top_e = _route(jnp.concatenate([lgp, lgs], axis=0))
    slot_tok, dest, block_e, n_used = _dispatch(top_e)
    ys = _experts(block_e, n_used, h2[slot_tok], w1, b1, w2, b2)
    moe = jnp.sum(ys[dest] * gates[..., None], axis=1)
    y = _rms_norm(x1 + moe, g_final)

    y_prompt = y[:tp].reshape(bp, sp, D_MODEL)
    y_sample = y[tp:].reshape(bs, ls, D_MODEL)
    st = lambda t: t[None]
    return (y_prompt, y_sample, st(k1), st(v1), st(c1), st(n1), st(m1), st(k2), st(v2), st(c2), st(n2), st(m2))
```

```python
import functools

import jax
import jax.numpy as jnp
from jax import lax
from jax.experimental import pallas as pl
from jax.experimental.pallas import tpu as pltpu

D_MODEL = 2048
N_HEADS = 16
N_KV_HEADS = 2
HEAD_DIM = 64
Q_PER_KV = N_HEADS // N_KV_HEADS
WINDOW = 128
ATTN_WIDTH = N_HEADS * HEAD_DIM
KV_WIDTH = N_KV_HEADS * HEAD_DIM
M_HEADS = 4
M_QK_DIM = 128
M_V_DIM = 256
M_QK_WIDTH = M_HEADS * M_QK_DIM
M_V_WIDTH = M_HEADS * M_V_DIM
M_CHUNK = 64
N_EXPERTS = 32
TOP_K = 4
D_FF = D_MODEL
SWIGLU_LIMIT = 7.0
SWIGLU_ALPHA = 1.702
RMS_EPS = 1e-5
NEG = -1e30

F32 = jnp.float32
BF16 = jnp.bfloat16

LANES = 128
SUBLANES = 8
VMEM_LIMIT_BYTES = 56 * 1024 * 1024

COL_GZ = 0
COL_AQ = COL_GZ + 2 * D_MODEL
COL_MV = COL_AQ + ATTN_WIDTH
COL_MO = COL_MV + M_V_WIDTH
COL_MQ = COL_MO + M_V_WIDTH
COL_MK = COL_MQ + M_QK_WIDTH
COL_AK = COL_MK + M_QK_WIDTH
COL_IF = COL_AK + 2 * KV_WIDTH
IN_TILE_N = 512
IN_WIDTH_PAD = -(-(COL_IF + LANES) // IN_TILE_N) * IN_TILE_N

SAMPLE_ROWS = SUBLANES
MOE_BLOCK_M = 512
MOE_FF_CHUNK = 1024
ROUTER_PAD = LANES


def _permute_w_in(w_in):
    splits = (ATTN_WIDTH, KV_WIDTH, KV_WIDTH, M_QK_WIDTH, M_QK_WIDTH, M_V_WIDTH, M_V_WIDTH, M_HEADS, M_HEADS,
              2 * D_MODEL)
    parts, off = [], 0
    for width in splits:
        parts.append(w_in[:, off:off + width])
        off += width
    aq, ak, av, mq, mk, mv, mo, mi, mf, gz = parts
    cols = jnp.concatenate([gz, aq, mv, mo, mq, mk, ak, av, mi, mf], axis=1)
    cols = jnp.pad(cols, ((0, 0), (0, IN_WIDTH_PAD - cols.shape[1])))
    return cols.astype(BF16)


def _params(*semantics):
    return pltpu.CompilerParams(dimension_semantics=semantics, vmem_limit_bytes=VMEM_LIMIT_BYTES)


def _log_sigmoid(x):
    return jnp.minimum(x, 0.0) - jnp.log1p(jnp.exp(-jnp.abs(x)))


def _cast_body(x_ref, o_ref):
    o_ref[...] = x_ref[...].astype(BF16)


def _cast_bf16(w, block_rows):
    e, r, c = w.shape
    return pl.pallas_call(
        _cast_body,
        grid=(e, r // block_rows),
        in_specs=[pl.BlockSpec((None, block_rows, c), lambda i, j: (i, j, 0))],
        out_specs=pl.BlockSpec((None, block_rows, c), lambda i, j: (i, j, 0)),
        out_shape=jax.ShapeDtypeStruct(w.shape, BF16),
        compiler_params=_params("arbitrary", "arbitrary"),
        name="cast_bf16",
    )(w)


def _inproj_body(*refs):
    x_ref, g_ref, w_ref = refs[:3]
    z_ref, h_scr = refs[-2:]

    @pl.when(pl.program_id(1) == 0)
    def _():
        x = x_ref[...]
        y = x * lax.rsqrt(jnp.mean(x * x, axis=-1, keepdims=True) + RMS_EPS)
        h_scr[...] = (y * g_ref[...]).astype(BF16)

    z_ref[...] = jnp.dot(h_scr[...], w_ref[...], preferred_element_type=F32)


def _inproj(x, g, w_bf16, tile_m, t_all, row_off, z_all=None):
    t = x.shape[0]
    n = w_bf16.shape[1]
    rb = row_off // tile_m
    in_specs = [
        pl.BlockSpec((tile_m, D_MODEL), lambda i, j: (i, 0)),
        pl.BlockSpec((1, D_MODEL), lambda i, j: (0, 0)),
        pl.BlockSpec((D_MODEL, IN_TILE_N), lambda i, j: (0, j)),
    ]
    args = [x, g.reshape(1, D_MODEL), w_bf16]
    aliases = {}
    if z_all is not None:
        in_specs.append(pl.BlockSpec(memory_space=pl.ANY))
        args.append(z_all)
        aliases = {3: 0}
    return pl.pallas_call(
        _inproj_body,
        grid=(t // tile_m, n // IN_TILE_N),
        in_specs=in_specs,
        out_specs=pl.BlockSpec((tile_m, IN_TILE_N), lambda i, j: (i + rb, j)),
        out_shape=jax.ShapeDtypeStruct((t_all, n), F32),
        scratch_shapes=[pltpu.VMEM((tile_m, D_MODEL), BF16)],
        input_output_aliases=aliases,
        compiler_params=_params("arbitrary", "arbitrary"),
        name="inproj",
    )(*args)


def _attn_prompt_body(q_ref, kvc_ref, kvp_ref, sink_ref, a_ref):
    n = pl.program_id(1)
    q = q_ref[...].astype(BF16)
    kvc = kvc_ref[...].astype(BF16)
    kvp = kvp_ref[...].astype(BF16)
    row = lax.broadcasted_iota(jnp.int32, (WINDOW, WINDOW), 0)
    col = lax.broadcasted_iota(jnp.int32, (WINDOW, WINDOW), 1)
    mask = jnp.concatenate([(col > row) & (n > 0), col <= row], axis=1)
    for g in range(N_KV_HEADS):
        kk = jnp.concatenate([kvp[:, g * HEAD_DIM:(g + 1) * HEAD_DIM], kvc[:, g * HEAD_DIM:(g + 1) * HEAD_DIM]],
                             axis=0)
        vv = jnp.concatenate([kvp[:, KV_WIDTH + g * HEAD_DIM:KV_WIDTH + (g + 1) * HEAD_DIM],
                              kvc[:, KV_WIDTH + g * HEAD_DIM:KV_WIDTH + (g + 1) * HEAD_DIM]], axis=0)
        for hq in range(Q_PER_KV):
            h = g * Q_PER_KV + hq
            qh = q[:, h * HEAD_DIM:(h + 1) * HEAD_DIM]
            s = lax.dot_general(qh, kk, (((1,), (1,)), ((), ())), preferred_element_type=F32) * (HEAD_DIM ** -0.5)
            s = jnp.where(mask, s, NEG)
            sink = sink_ref[:, h:h + 1]
            m = jnp.maximum(jnp.max(s, axis=1, keepdims=True), sink)
            p = jnp.exp(s - m)
            denom = jnp.sum(p, axis=1, keepdims=True) + jnp.exp(sink - m)
            o = jnp.dot((p / denom).astype(BF16), vv, preferred_element_type=F32)
            a_ref[:, h * HEAD_DIM:(h + 1) * HEAD_DIM] = o.astype(BF16)


def _attn_prompt(z_all, sinks_row, batch, seq):
    nb = seq // WINDOW
    t_all = z_all.shape[0]
    return pl.pallas_call(
        _attn_prompt_body,
        grid=(batch, nb),
        in_specs=[
            pl.BlockSpec((WINDOW, ATTN_WIDTH), lambda b, n: (b * nb + n, COL_AQ // ATTN_WIDTH)),
            pl.BlockSpec((WINDOW, 2 * KV_WIDTH), lambda b, n: (b * nb + n, COL_AK // (2 * KV_WIDTH))),
            pl.BlockSpec((WINDOW, 2 * KV_WIDTH),
                         lambda b, n: (b * nb + jnp.maximum(n - 1, 0), COL_AK // (2 * KV_WIDTH))),
            pl.BlockSpec((1, LANES), lambda b, n: (0, 0)),
        ],
        out_specs=pl.BlockSpec((WINDOW, ATTN_WIDTH), lambda b, n: (b * nb + n, 0)),
        out_shape=jax.ShapeDtypeStruct((t_all, ATTN_WIDTH), BF16),
        compiler_params=_params("arbitrary", "arbitrary"),
        name="attn_prompt",
    )(z_all, z_all, z_all, sinks_row)


ATTN_SAMPLE_GROUP = 2


def _attn_sample_body(n_valid, q_ref, kv_ref, ck_ref, cv_ref, sink_ref, a_in_ref, a_ref, nk_ref, nv_ref, o_scr):
    del a_in_ref
    r8 = lax.broadcasted_iota(jnp.int32, (SAMPLE_ROWS, KV_WIDTH), 0)
    qi = lax.broadcasted_iota(jnp.int32, (SAMPLE_ROWS, WINDOW), 0)
    kj = lax.broadcasted_iota(jnp.int32, (SAMPLE_ROWS, WINDOW), 1)
    mask_c = kj > qi
    mask_n = (lax.broadcasted_iota(jnp.int32, (SAMPLE_ROWS, SAMPLE_ROWS), 1)
              <= lax.broadcasted_iota(jnp.int32, (SAMPLE_ROWS, SAMPLE_ROWS), 0))
    for g in range(ATTN_SAMPLE_GROUP):
        rows = slice(g * SAMPLE_ROWS, (g + 1) * SAMPLE_ROWS)
        q = q_ref[rows, :].astype(BF16)
        kv_new = kv_ref[rows, :]
        ck = ck_ref[g]
        cv = cv_ref[g]
        for src, new, dst in ((ck, kv_new[:, :KV_WIDTH], nk_ref), (cv, kv_new[:, KV_WIDTH:], nv_ref)):
            rolled = pltpu.roll(src, WINDOW - n_valid, axis=0)
            dst[g] = rolled
            tail = jnp.where(r8 >= SAMPLE_ROWS - n_valid, pltpu.roll(new, SAMPLE_ROWS - n_valid, axis=0),
                             rolled[WINDOW - SAMPLE_ROWS:, :])
            dst[g, WINDOW - SAMPLE_ROWS:, :] = tail
        ckb, cvb, kvb = ck.astype(BF16), cv.astype(BF16), kv_new.astype(BF16)
        for h in range(N_HEADS):
            kg = h // Q_PER_KV
            ks = slice(kg * HEAD_DIM, (kg + 1) * HEAD_DIM)
            vs = slice(KV_WIDTH + kg * HEAD_DIM, KV_WIDTH + (kg + 1) * HEAD_DIM)
            qh = q[:, h * HEAD_DIM:(h + 1) * HEAD_DIM]
            nt = (((1,), (1,)), ((), ()))
            s_c = lax.dot_general(qh, ckb[:, ks], nt, preferred_element_type=F32) * (HEAD_DIM ** -0.5)
            s_n = lax.dot_general(qh, kvb[:, ks], nt, preferred_element_type=F32) * (HEAD_DIM ** -0.5)
            s_c = jnp.where(mask_c, s_c, NEG)
            s_n = jnp.where(mask_n, s_n, NEG)
            sink = sink_ref[:, h:h + 1]
            m = jnp.maximum(jnp.maximum(jnp.max(s_c, axis=1, keepdims=True), jnp.max(s_n, axis=1, keepdims=True)),
                            sink)
            p_c = jnp.exp(s_c - m)
            p_n = jnp.exp(s_n - m)
            denom = (jnp.sum(p_c, axis=1, keepdims=True) + jnp.sum(p_n, axis=1, keepdims=True)
                     + jnp.exp(sink - m))
            o = (jnp.dot((p_c / denom).astype(BF16), cvb[:, ks], preferred_element_type=F32)
                 + jnp.dot((p_n / denom).astype(BF16), kvb[:, vs], preferred_element_type=F32))
            o_scr[rows, h * HEAD_DIM:(h + 1) * HEAD_DIM] = o
    a_ref[...] = o_scr[...].astype(BF16)


def _attn_sample(z_all, sinks_row, cache_k, cache_v, a_all, row_off, n_valid):
    b = cache_k.shape[0]
    gr = ATTN_SAMPLE_GROUP
    rows = gr * SAMPLE_ROWS
    rb = row_off // rows
    return pl.pallas_call(
        functools.partial(_attn_sample_body, n_valid),
        grid=(b // gr,),
        in_specs=[
            pl.BlockSpec((rows, ATTN_WIDTH), lambda i: (rb + i, COL_AQ // ATTN_WIDTH)),
            pl.BlockSpec((rows, 2 * KV_WIDTH), lambda i: (rb + i, COL_AK // (2 * KV_WIDTH))),
            pl.BlockSpec((gr, WINDOW, KV_WIDTH), lambda i: (i, 0, 0)),
            pl.BlockSpec((gr, WINDOW, KV_WIDTH), lambda i: (i, 0, 0)),
            pl.BlockSpec((1, LANES), lambda i: (0, 0)),
            pl.BlockSpec(memory_space=pl.ANY),
        ],
        out_specs=[
            pl.BlockSpec((rows, ATTN_WIDTH), lambda i: (rb + i, 0)),
            pl.BlockSpec((gr, WINDOW, KV_WIDTH), lambda i: (i, 0, 0)),
            pl.BlockSpec((gr, WINDOW, KV_WIDTH), lambda i: (i, 0, 0)),
        ],
        out_shape=[
            jax.ShapeDtypeStruct(a_all.shape, BF16),
            jax.ShapeDtypeStruct(cache_k.shape, F32),
            jax.ShapeDtypeStruct(cache_v.shape, F32),
        ],
        scratch_shapes=[pltpu.VMEM((rows, ATTN_WIDTH), F32)],
        input_output_aliases={5: 0},
        compiler_params=_params("arbitrary"),
        name="attn_sample",
    )(z_all, z_all, cache_k, cache_v, sinks_row, a_all)


def _mlstm_gates(if_blk, bias_row, n_valid):
    length = if_blk.shape[0]
    g = if_blk + bias_row
    lane = lax.broadcasted_iota(jnp.int32, g.shape, 1)
    g = jnp.where(lane < M_HEADS, g, _log_sigmoid(g))
    if n_valid < length:
        row = lax.broadcasted_iota(jnp.int32, g.shape, 0)
        g = jnp.where(row < n_valid, g, jnp.where(lane < M_HEADS, NEG, 0.0))
    return g


def _mlstm_chunk(q, k, v, gates, gates_t, h, c, n, m):
    length = q.shape[0]
    i_col = gates[:, h:h + 1]
    f_col = gates[:, M_HEADS + h:M_HEADS + h + 1]
    i_row = gates_t[h:h + 1, :]
    f_row = gates_t[M_HEADS + h:M_HEADS + h + 1, :]
    t_idx = lax.broadcasted_iota(jnp.int32, (length, length), 0)
    s_idx = lax.broadcasted_iota(jnp.int32, (length, length), 1)
    causal = s_idx <= t_idx
    bcum_col = jnp.sum(jnp.where(causal, f_row, 0.0), axis=1, keepdims=True)
    bcum_row = jnp.sum(jnp.where(t_idx <= s_idx, f_col, 0.0), axis=0, keepdims=True)
    dlog = jnp.where(causal, bcum_col - bcum_row + i_row, NEG)
    inter = bcum_col + m
    mt = jnp.maximum(inter, jnp.max(dlog, axis=1, keepdims=True))
    w_inter = jnp.exp(inter - mt)
    qb = q.astype(BF16)
    ks = k * (M_QK_DIM ** -0.5)
    vb = v.astype(BF16)
    qk = lax.dot_general(qb, ks.astype(BF16), (((1,), (1,)), ((), ())), preferred_element_type=F32)
    qk = qk * jnp.exp(dlog - mt)
    num = (w_inter * jnp.dot(qb, c.astype(BF16), preferred_element_type=F32)
           + jnp.dot(qk.astype(BF16), vb, preferred_element_type=F32))
    den = w_inter * jnp.sum(q * n, axis=1, keepdims=True) + jnp.sum(qk, axis=1, keepdims=True)
    h_t = num / jnp.maximum(jnp.abs(den), jnp.exp(-mt))
    m_new = mt[length - 1:length, :]
    w_state = jnp.exp(bcum_col[length - 1:length, :] - bcum_col + i_col - m_new)
    decay = jnp.exp(inter[length - 1:length, :] - m_new)
    kw = ks * w_state
    c_new = decay * c + jnp.dot(kw.T.astype(BF16), vb, preferred_element_type=F32)
    n_new = decay * n + jnp.sum(kw, axis=0, keepdims=True)
    return h_t, c_new, n_new, m_new


def _mlstm_head_out(h_t, norm_w, o_gate):
    hs = h_t * lax.rsqrt(jnp.mean(h_t * h_t, axis=-1, keepdims=True) + RMS_EPS)
    return hs * norm_w * jax.nn.sigmoid(o_gate)


MLSTM_SEG = 512


def _mlstm_prompt_body(q_ref, k_ref, v_ref, o_ref, if_ref, bias_ref, nw_ref, mh_ref, c_out, n_out, m_out,
                       c_scr, n_scr, m_scr):
    seg = pl.program_id(1)

    @pl.when(seg == 0)
    def _():
        c_scr[...] = jnp.zeros_like(c_scr)
        n_scr[...] = jnp.zeros_like(n_scr)
        m_scr[...] = jnp.full_like(m_scr, NEG)

    def chunk_body(ci, carry):
        r = pl.multiple_of(ci * M_CHUNK, M_CHUNK)
        rows = pl.ds(r, M_CHUNK)
        gates = _mlstm_gates(if_ref[rows, :], bias_ref[...], M_CHUNK)
        gates_t = gates.T
        for h in range(M_HEADS):
            qs = slice(h * M_QK_DIM, (h + 1) * M_QK_DIM)
            vs = slice(h * M_V_DIM, (h + 1) * M_V_DIM)
            h_t, c_new, n_new, m_new = _mlstm_chunk(
                q_ref[rows, qs], k_ref[rows, qs], v_ref[rows, vs], gates, gates_t, h,
                c_scr[h], n_scr[h:h + 1, :], m_scr[h:h + 1, 0:1])
            c_scr[h] = c_new
            n_scr[h:h + 1, :] = n_new
            m_scr[h:h + 1, :] = jnp.broadcast_to(m_new, (1, LANES))
            mh_ref[rows, vs] = _mlstm_head_out(h_t, nw_ref[:, vs], o_ref[rows, vs]).astype(BF16)
        return carry

    lax.fori_loop(0, MLSTM_SEG // M_CHUNK, chunk_body, 0)

    @pl.when(seg == pl.num_programs(1) - 1)
    def _():
        c_out[...] = c_scr[...]
        n_out[...] = n_scr[0:M_HEADS, :]
        m_out[...] = m_scr[0:M_HEADS, :]


def _mlstm_prompt(z_all, bias_row, norm_w_row, batch, seq):
    ns = seq // MLSTM_SEG
    t_all = z_all.shape[0]
    row = lambda b, s: b * ns + s
    return pl.pallas_call(
        _mlstm_prompt_body,
        grid=(batch, ns),
        in_specs=[
            pl.BlockSpec((MLSTM_SEG, M_QK_WIDTH), lambda b, s: (row(b, s), COL_MQ // M_QK_WIDTH)),
            pl.BlockSpec((MLSTM_SEG, M_QK_WIDTH), lambda b, s: (row(b, s), COL_MK // M_QK_WIDTH)),
            pl.BlockSpec((MLSTM_SEG, M_V_WIDTH), lambda b, s: (row(b, s), COL_MV // M_V_WIDTH)),
            pl.BlockSpec((MLSTM_SEG, M_V_WIDTH), lambda b, s: (row(b, s), COL_MO // M_V_WIDTH)),
            pl.BlockSpec((MLSTM_SEG, LANES), lambda b, s: (row(b, s), COL_IF // LANES)),
            pl.BlockSpec((1, LANES), lambda b, s: (0, 0)),
            pl.BlockSpec((1, M_V_WIDTH), lambda b, s: (0, 0)),
        ],
        out_specs=[
            pl.BlockSpec((MLSTM_SEG, M_V_WIDTH), lambda b, s: (row(b, s), 0)),
            pl.BlockSpec((None, M_HEADS, M_QK_DIM, M_V_DIM), lambda b, s: (b, 0, 0, 0)),
            pl.BlockSpec((None, M_HEADS, M_QK_DIM), lambda b, s: (b, 0, 0)),
            pl.BlockSpec((None, M_HEADS, LANES), lambda b, s: (b, 0, 0)),
        ],
        out_shape=[
            jax.ShapeDtypeStruct((t_all, M_V_WIDTH), BF16),
            jax.ShapeDtypeStruct((batch, M_HEADS, M_QK_DIM, M_V_DIM), F32),
            jax.ShapeDtypeStruct((batch, M_HEADS, M_QK_DIM), F32),
            jax.ShapeDtypeStruct((batch, M_HEADS, LANES), F32),
        ],
        scratch_shapes=[
            pltpu.VMEM((M_HEADS, M_QK_DIM, M_V_DIM), F32),
            pltpu.VMEM((SUBLANES, M_QK_DIM), F32),
            pltpu.VMEM((SUBLANES, LANES), F32),
        ],
        compiler_params=_params("arbitrary", "arbitrary"),
        name="mlstm_prompt",
    )(z_all, z_all, z_all, z_all, z_all, bias_row, norm_w_row)


MLSTM_SAMPLE_GROUP = 2


def _mlstm_sample_body(n_valid, q_ref, k_ref, v_ref, o_ref, if_ref, bias_ref, nw_ref, c0_ref, n0_ref, m0_ref,
                       mh_in_ref, mh_ref, c_out, n_out, m_out, h_scr):
    del mh_in_ref
    for g in range(MLSTM_SAMPLE_GROUP):
        rows = slice(g * SAMPLE_ROWS, (g + 1) * SAMPLE_ROWS)
        gates = _mlstm_gates(if_ref[rows, :], bias_ref[...], n_valid)
        gates_t = jnp.transpose(jnp.concatenate([gates] * (LANES // SAMPLE_ROWS), axis=0))[:, :SAMPLE_ROWS]
        for h in range(M_HEADS):
            qs = slice(h * M_QK_DIM, (h + 1) * M_QK_DIM)
            vs = slice(h * M_V_DIM, (h + 1) * M_V_DIM)
            h_t, c_new, n_new, m_new = _mlstm_chunk(
                q_ref[rows, qs], k_ref[rows, qs], v_ref[rows, vs], gates, gates_t, h,
                c0_ref[g, h], n0_ref[g, h:h + 1, :], m0_ref[g:g + 1, h:h + 1])
            c_out[g, h] = c_new
            n_out[g, h:h + 1, :] = n_new
            m_out[g:g + 1, h:h + 1] = m_new
            h_scr[rows, vs] = _mlstm_head_out(h_t, nw_ref[:, vs], o_ref[rows, vs])
    mh_ref[...] = h_scr[...].astype(BF16)


def _mlstm_sample(z_all, bias_row, norm_w_row, c0, n0, m0, mh_all, row_off, n_valid):
    b = c0.shape[0]
    gr = MLSTM_SAMPLE_GROUP
    rows = gr * SAMPLE_ROWS
    rb = row_off // rows
    m0g = m0.reshape(b // gr, gr, M_HEADS)
    outs = pl.pallas_call(
        functools.partial(_mlstm_sample_body, n_valid),
        grid=(b // gr,),
        in_specs=[
            pl.BlockSpec((rows, M_QK_WIDTH), lambda i: (rb + i, COL_MQ // M_QK_WIDTH)),
            pl.BlockSpec((rows, M_QK_WIDTH), lambda i: (rb + i, COL_MK // M_QK_WIDTH)),
            pl.BlockSpec((rows, M_V_WIDTH), lambda i: (rb + i, COL_MV // M_V_WIDTH)),
            pl.BlockSpec((rows, M_V_WIDTH), lambda i: (rb + i, COL_MO // M_V_WIDTH)),
            pl.BlockSpec((rows, LANES), lambda i: (rb + i, COL_IF // LANES)),
            pl.BlockSpec((1, LANES), lambda i: (0, 0)),
            pl.BlockSpec((1, M_V_WIDTH), lambda i: (0, 0)),
            pl.BlockSpec((gr, M_HEADS, M_QK_DIM, M_V_DIM), lambda i: (i, 0, 0, 0)),
            pl.BlockSpec((gr, M_HEADS, M_QK_DIM), lambda i: (i, 0, 0)),
            pl.BlockSpec((None, gr, M_HEADS), lambda i: (i, 0, 0)),
            pl.BlockSpec(memory_space=pl.ANY),
        ],
        out_specs=[
            pl.BlockSpec((rows, M_V_WIDTH), lambda i: (rb + i, 0)),
            pl.BlockSpec((gr, M_HEADS, M_QK_DIM, M_V_DIM), lambda i: (i, 0, 0, 0)),
            pl.BlockSpec((gr, M_HEADS, M_QK_DIM), lambda i: (i, 0, 0)),
            pl.BlockSpec((None, gr, M_HEADS), lambda i: (i, 0, 0)),
        ],
        out_shape=[
            jax.ShapeDtypeStruct(mh_all.shape, BF16),
            jax.ShapeDtypeStruct(c0.shape, F32),
            jax.ShapeDtypeStruct(n0.shape, F32),
            jax.ShapeDtypeStruct(m0g.shape, F32),
        ],
        scratch_shapes=[pltpu.VMEM((rows, M_V_WIDTH), F32)],
        input_output_aliases={10: 0},
        compiler_params=_params("arbitrary"),
        name="mlstm_sample",
    )(z_all, z_all, z_all, z_all, z_all, bias_row, norm_w_row, c0, n0, m0g, mh_all)
    mh_all, c, n, m = outs
    return mh_all, c, n, m.reshape(b, M_HEADS)


def _merge_body(x_ref, a_ref, mh_ref, gza_ref, gzm_ref, wa_ref, wm_ref, wo_ref, gf_ref, wr_ref, br_ref, *refs):
    x1_ref, h2_ref, lg_ref = refs[-3:]
    pa = jnp.dot(a_ref[...], wa_ref[...], preferred_element_type=F32)
    pm = jnp.dot(mh_ref[...], wm_ref[...], preferred_element_type=F32)
    merged = jax.nn.sigmoid(gza_ref[...]) * pa + jax.nn.sigmoid(gzm_ref[...]) * pm
    x1 = x_ref[...] + jnp.dot(merged.astype(BF16), wo_ref[...], preferred_element_type=F32)
    x1_ref[...] = x1
    h2 = x1 * lax.rsqrt(jnp.mean(x1 * x1, axis=-1, keepdims=True) + RMS_EPS) * gf_ref[...]
    h2_ref[...] = h2.astype(BF16)
    lg_ref[...] = jnp.dot(h2, wr_ref[...], preferred_element_type=F32,
                          precision=lax.Precision.HIGHEST) + br_ref[...]


def _merge(x, a_all, mh_all, z_all, wa, wm, wo, g_ffn, wr_pad, br_pad, tile_m, row_off, prev=None):
    t = x.shape[0]
    t_all = z_all.shape[0]
    rb = row_off // tile_m
    const = lambda i: (0, 0)
    in_specs = [
        pl.BlockSpec((tile_m, D_MODEL), lambda i: (i, 0)),
        pl.BlockSpec((tile_m, ATTN_WIDTH), lambda i: (rb + i, 0)),
        pl.BlockSpec((tile_m, M_V_WIDTH), lambda i: (rb + i, 0)),
        pl.BlockSpec((tile_m, D_MODEL), lambda i: (rb + i, COL_GZ // D_MODEL)),
        pl.BlockSpec((tile_m, D_MODEL), lambda i: (rb + i, COL_GZ // D_MODEL + 1)),
        pl.BlockSpec((ATTN_WIDTH, D_MODEL), const, pipeline_mode=pl.Buffered(1)),
        pl.BlockSpec((M_V_WIDTH, D_MODEL), const, pipeline_mode=pl.Buffered(1)),
        pl.BlockSpec((D_MODEL, D_MODEL), const, pipeline_mode=pl.Buffered(1)),
        pl.BlockSpec((1, D_MODEL), const),
        pl.BlockSpec((D_MODEL, ROUTER_PAD), const),
        pl.BlockSpec((1, ROUTER_PAD), const),
    ]
    args = [x, a_all, mh_all, z_all, z_all, wa, wm, wo, g_ffn.reshape(1, D_MODEL), wr_pad, br_pad]
    aliases = {}
    if prev is not None:
        in_specs += [pl.BlockSpec(memory_space=pl.ANY)] * 3
        args += list(prev)
        aliases = {11: 0, 12: 1, 13: 2}
    return pl.pallas_call(
        _merge_body,
        grid=(t // tile_m,),
        in_specs=in_specs,
        out_specs=[
            pl.BlockSpec((tile_m, D_MODEL), lambda i: (rb + i, 0)),
            pl.BlockSpec((tile_m, D_MODEL), lambda i: (rb + i, 0)),
            pl.BlockSpec((tile_m, ROUTER_PAD), lambda i: (rb + i, 0)),
        ],
        out_shape=[
            jax.ShapeDtypeStruct((t_all, D_MODEL), F32),
            jax.ShapeDtypeStruct((t_all, D_MODEL), BF16),
            jax.ShapeDtypeStruct((t_all, ROUTER_PAD), F32),
        ],
        input_output_aliases=aliases,
        compiler_params=_params("arbitrary"),
        name="merge",
    )(*args)


def _expert_body(be_ref, nb_ref, xs_ref, w1g_ref, w1u_ref, b1g_ref, b1u_ref, w2_ref, b2_ref, y_ref):
    i = pl.program_id(0)
    c = pl.program_id(1)
    used = i < nb_ref[0]

    @pl.when(used)
    def _():
        x = xs_ref[...]
        zg = jnp.dot(x, w1g_ref[...], preferred_element_type=F32) + b1g_ref[...]
        zu = jnp.dot(x, w1u_ref[...], preferred_element_type=F32) + b1u_ref[...]
        gate = jnp.minimum(zg, SWIGLU_LIMIT)
        up = jnp.clip(zu, -SWIGLU_LIMIT, SWIGLU_LIMIT)
        act = gate * jax.nn.sigmoid(SWIGLU_ALPHA * gate) * (up + 1.0)
        y = jnp.dot(act.astype(BF16), w2_ref[...], preferred_element_type=F32)

        @pl.when(c == 0)
        def _():
            y_ref[...] = y + b2_ref[...]

        @pl.when(c != 0)
        def _():
            y_ref[...] += y

    @pl.when(jnp.logical_not(used) & (c == 0))
    def _():
        y_ref[...] = jnp.zeros_like(y_ref)


def _experts(block_e, n_used, xs, w1, b1, w2, b2):
    n_slots = xs.shape[0]
    n_blocks = n_slots // MOE_BLOCK_M
    n_chunks = D_FF // MOE_FF_CHUNK

    def chunk(i, c, nb):
        return jnp.where(i < nb[0], c, n_chunks - 1)

    grid_spec = pltpu.PrefetchScalarGridSpec(
        num_scalar_prefetch=2,
        grid=(n_blocks, n_chunks),
        in_specs=[
            pl.BlockSpec((MOE_BLOCK_M, D_MODEL), lambda i, c, be, nb: (i, 0)),
            pl.BlockSpec((None, D_MODEL, MOE_FF_CHUNK), lambda i, c, be, nb: (be[i], 0, chunk(i, c, nb))),
            pl.BlockSpec((None, D_MODEL, MOE_FF_CHUNK),
                         lambda i, c, be, nb: (be[i], 0, n_chunks + chunk(i, c, nb))),
            pl.BlockSpec((None, 1, MOE_FF_CHUNK), lambda i, c, be, nb: (be[i], 0, chunk(i, c, nb))),
            pl.BlockSpec((None, 1, MOE_FF_CHUNK), lambda i, c, be, nb: (be[i], 0, n_chunks + chunk(i, c, nb))),
            pl.BlockSpec((None, MOE_FF_CHUNK, D_MODEL), lambda i, c, be, nb: (be[i], chunk(i, c, nb), 0)),
            pl.BlockSpec((None, 1, D_MODEL), lambda i, c, be, nb: (be[i], 0, 0)),
        ],
        out_specs=pl.BlockSpec((MOE_BLOCK_M, D_MODEL), lambda i, c, be, nb: (i, 0)),
    )
    return pl.pallas_call(
        _expert_body,
        grid_spec=grid_spec,
        out_shape=jax.ShapeDtypeStruct((n_slots, D_MODEL), F32),
        compiler_params=_params("arbitrary", "arbitrary"),
        name="experts",
    )(block_e, n_used, xs, w1, w1, b1, b1, w2, b2)


def _route(logits):
    top_v, top_e = lax.top_k(logits[:, :N_EXPERTS], TOP_K)
    return jax.nn.softmax(top_v, axis=-1), top_e


def _dispatch(top_e):
    t = top_e.shape[0]
    n_assign = t * TOP_K
    flat_e = top_e.reshape(-1)
    order = jnp.argsort(flat_e)
    se = flat_e[order]
    counts = jnp.bincount(flat_e, length=N_EXPERTS)
    starts = jnp.cumsum(counts) - counts
    padded = (counts + MOE_BLOCK_M - 1) // MOE_BLOCK_M * MOE_BLOCK_M
    pends = jnp.cumsum(padded)
    pstarts = pends - padded
    dest_sorted = pstarts[se] + jnp.arange(n_assign) - starts[se]
    n_blocks = -(-n_assign // MOE_BLOCK_M) + N_EXPERTS
    n_slots = n_blocks * MOE_BLOCK_M
    slot_tok = jnp.zeros((n_slots,), jnp.int32).at[dest_sorted].set((order // TOP_K).astype(jnp.int32))
    dest = jnp.zeros((n_assign,), jnp.int32).at[order].set(dest_sorted.astype(jnp.int32))
    block_e = jnp.minimum(jnp.searchsorted(pends, jnp.arange(n_blocks) * MOE_BLOCK_M, side='right'),
                          N_EXPERTS - 1).astype(jnp.int32)
    n_used = (pends[-1] // MOE_BLOCK_M).astype(jnp.int32).reshape(1)
    return slot_tok, dest.reshape(t, TOP_K), block_e, n_used


def _rms_norm(x, g):
    return x * lax.rsqrt(jnp.mean(x * x, axis=-1, keepdims=True) + RMS_EPS) * g


def _lane_row(v):
    return jnp.pad(v.astype(F32), (0, LANES - v.shape[0])).reshape(1, LANES)


def kernel(x_prompt, x_sample, cache_swa_k, cache_swa_v, state_mlstm_C, state_mlstm_n, state_mlstm_m,
           g_mix, w_in, b_if, attn_sinks, mlstm_norm_w, w_attn_proj, w_mlstm_proj, w_out,
           g_ffn, w_router, b_router, w_e1, b_e1, w_e2, b_e2, g_final):
    bp, sp = x_prompt.shape[0], x_prompt.shape[1]
    bs, ls = x_sample.shape[0], x_sample.shape[1]
    tp = bp * sp
    ts = bs * SAMPLE_ROWS
    t_all = tp + ts
    l = 0
    w_in_p = _permute_w_in(w_in[l])
    wa, wm, wo = w_attn_proj[l].astype(BF16), w_mlstm_proj[l].astype(BF16), w_out[l].astype(BF16)
    wr_pad = jnp.pad(w_router[l], ((0, 0), (0, ROUTER_PAD - N_EXPERTS)))
    br_pad = _lane_row(b_router[l])
    w1 = _cast_bf16(w_e1[l], 512)
    w2 = _cast_bf16(w_e2[l], 512)
    b1, b2 = b_e1[l].reshape(N_EXPERTS, 1, 2 * D_FF), b_e2[l].reshape(N_EXPERTS, 1, D_MODEL)
    sinks_row = _lane_row(attn_sinks[l])
    bias_row = _lane_row(b_if[l])
    norm_w_row = mlstm_norm_w[l].reshape(1, M_V_WIDTH)

    xp = x_prompt.reshape(tp, D_MODEL)
    xs = jnp.pad(x_sample, ((0, 0), (0, SAMPLE_ROWS - ls), (0, 0))).reshape(ts, D_MODEL)
    z_all = _inproj(xp, g_mix[l], w_in_p, 1024, t_all, 0)
    z_all = _inproj(xs, g_mix[l], w_in_p, 1024, t_all, tp, z_all)

    a_all = _attn_prompt(z_all, sinks_row, bp, sp)
    a_all, k2, v2 = _attn_sample(z_all, sinks_row, cache_swa_k[l].reshape(bs, WINDOW, KV_WIDTH),
                                 cache_swa_v[l].reshape(bs, WINDOW, KV_WIDTH), a_all, tp, ls)
    mh_all, c1, n1, m1 = _mlstm_prompt(z_all, bias_row, norm_w_row, bp, sp)
    mh_all, c2, n2, m2 = _mlstm_sample(z_all, bias_row, norm_w_row, state_mlstm_C[l], state_mlstm_n[l],
                                       state_mlstm_m[l], mh_all, tp, ls)

    merged = _merge(xp, a_all, mh_all, z_all, wa, wm, wo, g_ffn[l], wr_pad, br_pad, 256, 0)
    x1, h2, logits = _merge(xs, a_all, mh_all, z_all, wa, wm, wo, g_ffn[l], wr_pad, br_pad, 256, tp, merged)

    gates, top_e = _route(logits)
    slot_tok, dest, block_e, n_used = _dispatch(top_e)
    ys = _experts(block_e, n_used, h2[slot_tok], w1, b1, w2, b2)
    moe = jnp.sum(ys[dest] * gates[..., None], axis=1)
    y = _rms_norm(x1 + moe, g_final)

    y_prompt = y[:tp].reshape(bp, sp, D_MODEL)
    y_sample = y[tp:].reshape(bs, SAMPLE_ROWS, D_MODEL)[:, :ls]
    kv_last = z_all[:tp].reshape(bp, sp, IN_WIDTH_PAD)[:, sp - WINDOW:, COL_AK:COL_AK + 2 * KV_WIDTH]
    k1 = kv_last[..., :KV_WIDTH].reshape(bp, WINDOW, N_KV_HEADS, HEAD_DIM)
    v1 = kv_last[..., KV_WIDTH:].reshape(bp, WINDOW, N_KV_HEADS, HEAD_DIM)
    k2 = k2.reshape(bs, WINDOW, N_KV_HEADS, HEAD_DIM)
    v2 = v2.reshape(bs, WINDOW, N_KV_HEADS, HEAD_DIM)
    m1 = m1[:, :, 0]
    st = lambda t: t[None]
    return (y_prompt, y_sample, st(k1), st(v1), st(c1), st(n1), st(m1), st(k2), st(v2), st(c2), st(n2), st(m2))
```

```python
import functools

import jax
import jax.numpy as jnp
from jax import lax
from jax.experimental import pallas as pl
from jax.experimental.pallas import tpu as pltpu

D_MODEL = 2048
N_HEADS = 16
N_KV_HEADS = 2
HEAD_DIM = 64
Q_PER_KV = N_HEADS // N_KV_HEADS
WINDOW = 128
ATTN_WIDTH = N_HEADS * HEAD_DIM
KV_WIDTH = N_KV_HEADS * HEAD_DIM
M_HEADS = 4
M_QK_DIM = 128
M_V_DIM = 256
M_QK_WIDTH = M_HEADS * M_QK_DIM
M_V_WIDTH = M_HEADS * M_V_DIM
M_CHUNK = 64
N_EXPERTS = 32
TOP_K = 4
D_FF = D_MODEL
SWIGLU_LIMIT = 7.0
SWIGLU_ALPHA = 1.702
RMS_EPS = 1e-5
NEG = -1e30

F32 = jnp.float32
BF16 = jnp.bfloat16

LANES = 128
SUBLANES = 8
VMEM_LIMIT_BYTES = 56 * 1024 * 1024

COL_GZ = 0
COL_AQ = COL_GZ + 2 * D_MODEL
COL_MV = COL_AQ + ATTN_WIDTH
COL_MO = COL_MV + M_V_WIDTH
COL_MQ = COL_MO + M_V_WIDTH
COL_MK = COL_MQ + M_QK_WIDTH
COL_AK = COL_MK + M_QK_WIDTH
COL_IF = COL_AK + 2 * KV_WIDTH
IN_TILE_N = 512
IN_WIDTH_PAD = -(-(COL_IF + LANES) // IN_TILE_N) * IN_TILE_N

SAMPLE_ROWS = SUBLANES
MOE_BLOCK_M = 512
MOE_FF_CHUNK = 1024
ROUTER_PAD = LANES


def _permute_w_in(w_in):
    splits = (ATTN_WIDTH, KV_WIDTH, KV_WIDTH, M_QK_WIDTH, M_QK_WIDTH, M_V_WIDTH, M_V_WIDTH, M_HEADS, M_HEADS,
              2 * D_MODEL)
    parts, off = [], 0
    for width in splits:
        parts.append(w_in[:, off:off + width])
        off += width
    aq, ak, av, mq, mk, mv, mo, mi, mf, gz = parts
    cols = jnp.concatenate([gz, aq, mv, mo, mq, mk, ak, av, mi, mf], axis=1)
    cols = jnp.pad(cols, ((0, 0), (0, IN_WIDTH_PAD - cols.shape[1])))
    return cols.astype(BF16)


def _params(*semantics):
    return pltpu.CompilerParams(dimension_semantics=semantics, vmem_limit_bytes=VMEM_LIMIT_BYTES)


def _log_sigmoid(x):
    return jnp.minimum(x, 0.0) - jnp.log1p(jnp.exp(-jnp.abs(x)))


def _cast_body(x_ref, o_ref):
    o_ref[...] = x_ref[...].astype(BF16)


def _cast_bf16(w, block_rows):
    e, r, c = w.shape
    return pl.pallas_call(
        _cast_body,
        grid=(e, r // block_rows),
        in_specs=[pl.BlockSpec((None, block_rows, c), lambda i, j: (i, j, 0))],
        out_specs=pl.BlockSpec((None, block_rows, c), lambda i, j: (i, j, 0)),
        out_shape=jax.ShapeDtypeStruct(w.shape, BF16),
        compiler_params=_params("arbitrary", "arbitrary"),
        name="cast_bf16",
    )(w)


def _inproj_body(*refs):
    x_ref, g_ref, w_ref = refs[:3]
    z_ref, h_scr = refs[-2:]

    @pl.when(pl.program_id(1) == 0)
    def _():
        x = x_ref[...]
        y = x * lax.rsqrt(jnp.mean(x * x, axis=-1, keepdims=True) + RMS_EPS)
        h_scr[...] = (y * g_ref[...]).astype(BF16)

    z_ref[...] = jnp.dot(h_scr[...], w_ref[...], preferred_element_type=F32)


def _inproj(x, g, w_bf16, tile_m, t_all, row_off, z_all=None):
    t = x.shape[0]
    n = w_bf16.shape[1]
    rb = row_off // tile_m
    in_specs = [
        pl.BlockSpec((tile_m, D_MODEL), lambda i, j: (i, 0)),
        pl.BlockSpec((1, D_MODEL), lambda i, j: (0, 0)),
        pl.BlockSpec((D_MODEL, IN_TILE_N), lambda i, j: (0, j)),
    ]
    args = [x, g.reshape(1, D_MODEL), w_bf16]
    aliases = {}
    if z_all is not None:
        in_specs.append(pl.BlockSpec(memory_space=pl.ANY))
        args.append(z_all)
        aliases = {3: 0}
    return pl.pallas_call(
        _inproj_body,
        grid=(t // tile_m, n // IN_TILE_N),
        in_specs=in_specs,
        out_specs=pl.BlockSpec((tile_m, IN_TILE_N), lambda i, j: (i + rb, j)),
        out_shape=jax.ShapeDtypeStruct((t_all, n), F32),
        scratch_shapes=[pltpu.VMEM((tile_m, D_MODEL), BF16)],
        input_output_aliases=aliases,
        compiler_params=_params("arbitrary", "arbitrary"),
        name="inproj",
    )(*args)


assert KV_WIDTH == LANES and N_KV_HEADS == 2 and LANES == 2 * HEAD_DIM
Q_TILES = ATTN_WIDTH // LANES
TILES_PER_KV = Q_TILES // N_KV_HEADS
_NT = (((1,), (1,)), ((), ()))


def _kv_placements(x):
    lo = lax.broadcasted_iota(jnp.int32, x.shape, 1) < HEAD_DIM
    swapped = pltpu.roll(x, HEAD_DIM, axis=1)
    return ((jnp.where(lo, x, 0.0).astype(BF16), jnp.where(lo, 0.0, swapped).astype(BF16)),
            (jnp.where(lo, swapped, 0.0).astype(BF16), jnp.where(lo, 0.0, x).astype(BF16)))


def _slab_head(g, p, j):
    return g * Q_PER_KV + 2 * j + p


def _attn_prompt_body(q_ref, kvc_ref, kvp_ref, sink_ref, a_ref, k_ref, v_ref):
    n = pl.program_id(1)
    kc, vc = kvc_ref[:, :KV_WIDTH], kvc_ref[:, KV_WIDTH:]
    k_ref[...] = kc
    v_ref[...] = vc
    kk = _kv_placements(jnp.concatenate([kvp_ref[:, :KV_WIDTH], kc], axis=0))
    vv = _kv_placements(jnp.concatenate([kvp_ref[:, KV_WIDTH:], vc], axis=0))
    q = q_ref[...].astype(BF16)
    row = lax.broadcasted_iota(jnp.int32, (WINDOW, WINDOW), 0)
    col = lax.broadcasted_iota(jnp.int32, (WINDOW, WINDOW), 1)
    mask = jnp.concatenate([(col > row) & (n > 0), col <= row], axis=1)
    slabs, sinks = [], []
    for g in range(N_KV_HEADS):
        qt = jnp.concatenate([q[:, (TILES_PER_KV * g + j) * LANES:(TILES_PER_KV * g + j + 1) * LANES]
                              for j in range(TILES_PER_KV)], axis=0)
        for p in range(2):
            s = lax.dot_general(qt, kk[g][p], _NT, preferred_element_type=F32) * (HEAD_DIM ** -0.5)
            for j in range(TILES_PER_KV):
                h = _slab_head(g, p, j)
                slabs.append(jnp.where(mask, s[j * WINDOW:(j + 1) * WINDOW], NEG))
                sinks.append(jnp.broadcast_to(sink_ref[:, h:h + 1], (WINDOW, 1)))
    s_all = jnp.concatenate(slabs, axis=0)
    sink = jnp.concatenate(sinks, axis=0)
    m = jnp.maximum(jnp.max(s_all, axis=1, keepdims=True), sink)
    p_all = jnp.exp(s_all - m)
    denom = jnp.sum(p_all, axis=1, keepdims=True) + jnp.exp(sink - m)
    pn = (p_all * (1.0 / denom)).astype(BF16)
    half = TILES_PER_KV * WINDOW
    for g in range(N_KV_HEADS):
        base = 2 * half * g
        o = (jnp.dot(pn[base:base + half], vv[g][0], preferred_element_type=F32)
             + jnp.dot(pn[base + half:base + 2 * half], vv[g][1], preferred_element_type=F32))
        for j in range(TILES_PER_KV):
            t = TILES_PER_KV * g + j
            a_ref[:, t * LANES:(t + 1) * LANES] = o[j * WINDOW:(j + 1) * WINDOW].astype(BF16)


def _attn_prompt(z_all, sinks_row, batch, seq):
    nb = seq // WINDOW
    t_all = z_all.shape[0]
    return pl.pallas_call(
        _attn_prompt_body,
        grid=(batch, nb),
        in_specs=[
            pl.BlockSpec((WINDOW, ATTN_WIDTH), lambda b, n: (b * nb + n, COL_AQ // ATTN_WIDTH)),
            pl.BlockSpec((WINDOW, 2 * KV_WIDTH), lambda b, n: (b * nb + n, COL_AK // (2 * KV_WIDTH))),
            pl.BlockSpec((WINDOW, 2 * KV_WIDTH),
                         lambda b, n: (b * nb + jnp.maximum(n - 1, 0), COL_AK // (2 * KV_WIDTH))),
            pl.BlockSpec((1, LANES), lambda b, n: (0, 0)),
        ],
        out_specs=[
            pl.BlockSpec((WINDOW, ATTN_WIDTH), lambda b, n: (b * nb + n, 0)),
            pl.BlockSpec((None, WINDOW, KV_WIDTH), lambda b, n: (b, 0, 0)),
            pl.BlockSpec((None, WINDOW, KV_WIDTH), lambda b, n: (b, 0, 0)),
        ],
        out_shape=[
            jax.ShapeDtypeStruct((t_all, ATTN_WIDTH), BF16),
            jax.ShapeDtypeStruct((batch, WINDOW, KV_WIDTH), F32),
            jax.ShapeDtypeStruct((batch, WINDOW, KV_WIDTH), F32),
        ],
        compiler_params=_params("arbitrary", "arbitrary"),
        name="attn_prompt",
    )(z_all, z_all, z_all, sinks_row)


ATTN_SAMPLE_GROUP = 8


def _attn_sample_body(n_valid, q_ref, kv_ref, ck_ref, cv_ref, sink_ref, a_in_ref, a_ref, nk_ref, nv_ref, o_scr):
    del a_in_ref
    rs = SAMPLE_ROWS
    slab = TILES_PER_KV * rs
    r8 = lax.broadcasted_iota(jnp.int32, (rs, KV_WIDTH), 0)
    qi_c = jnp.bitwise_and(lax.broadcasted_iota(jnp.int32, (slab, WINDOW), 0), rs - 1)
    mask_c = lax.broadcasted_iota(jnp.int32, (slab, WINDOW), 1) > qi_c
    qi_n = jnp.bitwise_and(lax.broadcasted_iota(jnp.int32, (slab, rs), 0), rs - 1)
    mask_n = lax.broadcasted_iota(jnp.int32, (slab, rs), 1) <= qi_n
    sc_list, sn_list, sink_list, values = [], [], [], []
    for b in range(ATTN_SAMPLE_GROUP):
        rows = slice(b * rs, (b + 1) * rs)
        q = q_ref[rows, :]
        kn, vn = kv_ref[rows, :KV_WIDTH], kv_ref[rows, KV_WIDTH:]
        ck, cv = ck_ref[b], cv_ref[b]
        for src, new, dst in ((ck, kn, nk_ref), (cv, vn, nv_ref)):
            rolled = pltpu.roll(src, WINDOW - n_valid, axis=0)
            dst[b] = rolled
            tail = jnp.where(r8 >= rs - n_valid, pltpu.roll(new, rs - n_valid, axis=0), rolled[WINDOW - rs:, :])
            dst[b, WINDOW - rs:, :] = tail
        kk_c, kk_n = _kv_placements(ck), _kv_placements(kn)
        values.append((_kv_placements(cv), _kv_placements(vn)))
        for g in range(N_KV_HEADS):
            qt = jnp.concatenate([q[:, (TILES_PER_KV * g + j) * LANES:(TILES_PER_KV * g + j + 1) * LANES]
                                  for j in range(TILES_PER_KV)], axis=0).astype(BF16)
            for p in range(2):
                s_c = lax.dot_general(qt, kk_c[g][p], _NT, preferred_element_type=F32) * (HEAD_DIM ** -0.5)
                s_n = lax.dot_general(qt, kk_n[g][p], _NT, preferred_element_type=F32) * (HEAD_DIM ** -0.5)
                sc_list.append(jnp.where(mask_c, s_c, NEG))
                sn_list.append(jnp.where(mask_n, s_n, NEG))
                for j in range(TILES_PER_KV):
                    h = _slab_head(g, p, j)
                    sink_list.append(jnp.broadcast_to(sink_ref[:, h:h + 1], (rs, 1)))
    s_c = jnp.concatenate(sc_list, axis=0)
    s_n = jnp.concatenate(sn_list, axis=0)
    sink = jnp.concatenate(sink_list, axis=0)
    m = jnp.maximum(jnp.maximum(jnp.max(s_c, axis=1, keepdims=True), jnp.max(s_n, axis=1, keepdims=True)), sink)
    p_c = jnp.exp(s_c - m)
    p_n = jnp.exp(s_n - m)
    inv = 1.0 / (jnp.sum(p_c, axis=1, keepdims=True) + jnp.sum(p_n, axis=1, keepdims=True) + jnp.exp(sink - m))
    p_c = (p_c * inv).astype(BF16)
    p_n = (p_n * inv).astype(BF16)
    for b in range(ATTN_SAMPLE_GROUP):
        vv_c, vv_n = values[b]
        for g in range(N_KV_HEADS):
            base = (b * N_KV_HEADS + g) * 2 * slab
            o = (jnp.dot(p_c[base:base + slab], vv_c[g][0], preferred_element_type=F32)
                 + jnp.dot(p_c[base + slab:base + 2 * slab], vv_c[g][1], preferred_element_type=F32)
                 + jnp.dot(p_n[base:base + slab], vv_n[g][0], preferred_element_type=F32)
                 + jnp.dot(p_n[base + slab:base + 2 * slab], vv_n[g][1], preferred_element_type=F32))
            for j in range(TILES_PER_KV):
                t = TILES_PER_KV * g + j
                o_scr[b * rs:(b + 1) * rs, t * LANES:(t + 1) * LANES] = o[j * rs:(j + 1) * rs]
    a_ref[...] = o_scr[...].astype(BF16)


def _attn_sample(z_all, sinks_row, cache_k, cache_v, a_all, row_off, n_valid):
    b = cache_k.shape[0]
    gr = ATTN_SAMPLE_GROUP
    rows = gr * SAMPLE_ROWS
    rb = row_off // rows
    return pl.pallas_call(
        functools.partial(_attn_sample_body, n_valid),
        grid=(b // gr,),
        in_specs=[
            pl.BlockSpec((rows, ATTN_WIDTH), lambda i: (rb + i, COL_AQ // ATTN_WIDTH)),
            pl.BlockSpec((rows, 2 * KV_WIDTH), lambda i: (rb + i, COL_AK // (2 * KV_WIDTH))),
            pl.BlockSpec((gr, WINDOW, KV_WIDTH), lambda i: (i, 0, 0)),
            pl.BlockSpec((gr, WINDOW, KV_WIDTH), lambda i: (i, 0, 0)),
            pl.BlockSpec((1, LANES), lambda i: (0, 0)),
            pl.BlockSpec(memory_space=pl.ANY),
        ],
        out_specs=[
            pl.BlockSpec((rows, ATTN_WIDTH), lambda i: (rb + i, 0)),
            pl.BlockSpec((gr, WINDOW, KV_WIDTH), lambda i: (i, 0, 0)),
            pl.BlockSpec((gr, WINDOW, KV_WIDTH), lambda i: (i, 0, 0)),
        ],
        out_shape=[
            jax.ShapeDtypeStruct(a_all.shape, BF16),
            jax.ShapeDtypeStruct(cache_k.shape, F32),
            jax.ShapeDtypeStruct(cache_v.shape, F32),
        ],
        scratch_shapes=[pltpu.VMEM((rows, ATTN_WIDTH), F32)],
        input_output_aliases={5: 0},
        compiler_params=_params("arbitrary"),
        name="attn_sample",
    )(z_all, z_all, cache_k, cache_v, sinks_row, a_all)


def _mlstm_gates(if_blk, bias_row, n_valid):
    length = if_blk.shape[0]
    g = if_blk + bias_row
    lane = lax.broadcasted_iota(jnp.int32, g.shape, 1)
    g = jnp.where(lane < M_HEADS, g, _log_sigmoid(g))
    if n_valid < length:
        row = lax.broadcasted_iota(jnp.int32, g.shape, 0)
        g = jnp.where(row < n_valid, g, jnp.where(lane < M_HEADS, NEG, 0.0))
    return g


def _mlstm_chunk(q, k, v, gates, gates_t, h, c, n, m):
    length = q.shape[0]
    i_col = gates[:, h:h + 1]
    f_col = gates[:, M_HEADS + h:M_HEADS + h + 1]
    i_row = gates_t[h:h + 1, :]
    f_row = gates_t[M_HEADS + h:M_HEADS + h + 1, :]
    t_idx = lax.broadcasted_iota(jnp.int32, (length, length), 0)
    s_idx = lax.broadcasted_iota(jnp.int32, (length, length), 1)
    causal = s_idx <= t_idx
    bcum_col = jnp.sum(jnp.where(causal, f_row, 0.0), axis=1, keepdims=True)
    bcum_row = jnp.sum(jnp.where(t_idx <= s_idx, f_col, 0.0), axis=0, keepdims=True)
    dlog = jnp.where(causal, bcum_col - bcum_row + i_row, NEG)
    inter = bcum_col + m
    mt = jnp.maximum(inter, jnp.max(dlog, axis=1, keepdims=True))
    w_inter = jnp.exp(inter - mt)
    qb = q.astype(BF16)
    ks = k * (M_QK_DIM ** -0.5)
    vb = v.astype(BF16)
    qk = lax.dot_general(qb, ks.astype(BF16), (((1,), (1,)), ((), ())), preferred_element_type=F32)
    qk = qk * jnp.exp(dlog - mt)
    num = (w_inter * jnp.dot(qb, c.astype(BF16), preferred_element_type=F32)
           + jnp.dot(qk.astype(BF16), vb, preferred_element_type=F32))
    den = w_inter * jnp.sum(q * n, axis=1, keepdims=True) + jnp.sum(qk, axis=1, keepdims=True)
    h_t = num / jnp.maximum(jnp.abs(den), jnp.exp(-mt))
    m_new = mt[length - 1:length, :]
    w_state = jnp.exp(bcum_col[length - 1:length, :] - bcum_col + i_col - m_new)
    decay = jnp.exp(inter[length - 1:length, :] - m_new)
    kw = ks * w_state
    c_new = decay * c + jnp.dot(kw.T.astype(BF16), vb, preferred_element_type=F32)
    n_new = decay * n + jnp.sum(kw, axis=0, keepdims=True)
    return h_t, c_new, n_new, m_new


def _mlstm_head_out(h_t, norm_w, o_gate):
    hs = h_t * lax.rsqrt(jnp.mean(h_t * h_t, axis=-1, keepdims=True) + RMS_EPS)
    return hs * norm_w * jax.nn.sigmoid(o_gate)


MLSTM_SEG = 512


def _mlstm_prompt_body(q_ref, k_ref, v_ref, o_ref, if_ref, bias_ref, nw_ref, mh_ref, c_out, n_out, m_out,
                       c_scr, n_scr, m_scr):
    seg = pl.program_id(1)

    @pl.when(seg == 0)
    def _():
        c_scr[...] = jnp.zeros_like(c_scr)
        n_scr[...] = jnp.zeros_like(n_scr)
        m_scr[...] = jnp.full_like(m_scr, NEG)

    def chunk_body(ci, carry):
        r = pl.multiple_of(ci * M_CHUNK, M_CHUNK)
        rows = pl.ds(r, M_CHUNK)
        gates = _mlstm_gates(if_ref[rows, :], bias_ref[...], M_CHUNK)
        gates_t = gates.T
        for h in range(M_HEADS):
            qs = slice(h * M_QK_DIM, (h + 1) * M_QK_DIM)
            vs = slice(h * M_V_DIM, (h + 1) * M_V_DIM)
            h_t, c_new, n_new, m_new = _mlstm_chunk(
                q_ref[rows, qs], k_ref[rows, qs], v_ref[rows, vs], gates, gates_t, h,
                c_scr[h], n_scr[h:h + 1, :], m_scr[h:h + 1, 0:1])
            c_scr[h] = c_new
            n_scr[h:h + 1, :] = n_new
            m_scr[h:h + 1, :] = jnp.broadcast_to(m_new, (1, LANES))
            mh_ref[rows, vs] = _mlstm_head_out(h_t, nw_ref[:, vs], o_ref[rows, vs]).astype(BF16)
        return carry

    lax.fori_loop(0, MLSTM_SEG // M_CHUNK, chunk_body, 0)

    @pl.when(seg == pl.num_programs(1) - 1)
    def _():
        c_out[...] = c_scr[...]
        n_out[...] = n_scr[0:M_HEADS, :]
        m_out[...] = m_scr[0:M_HEADS, :]


def _mlstm_prompt(z_all, bias_row, norm_w_row, batch, seq):
    ns = seq // MLSTM_SEG
    t_all = z_all.shape[0]
    row = lambda b, s: b * ns + s
    return pl.pallas_call(
        _mlstm_prompt_body,
        grid=(batch, ns),
        in_specs=[
            pl.BlockSpec((MLSTM_SEG, M_QK_WIDTH), lambda b, s: (row(b, s), COL_MQ // M_QK_WIDTH)),
            pl.BlockSpec((MLSTM_SEG, M_QK_WIDTH), lambda b, s: (row(b, s), COL_MK // M_QK_WIDTH)),
            pl.BlockSpec((MLSTM_SEG, M_V_WIDTH), lambda b, s: (row(b, s), COL_MV // M_V_WIDTH)),
            pl.BlockSpec((MLSTM_SEG, M_V_WIDTH), lambda b, s: (row(b, s), COL_MO // M_V_WIDTH)),
            pl.BlockSpec((MLSTM_SEG, LANES), lambda b, s: (row(b, s), COL_IF // LANES)),
            pl.BlockSpec((1, LANES), lambda b, s: (0, 0)),
            pl.BlockSpec((1, M_V_WIDTH), lambda b, s: (0, 0)),
        ],
        out_specs=[
            pl.BlockSpec((MLSTM_SEG, M_V_WIDTH), lambda b, s: (row(b, s), 0)),
            pl.BlockSpec((None, M_HEADS, M_QK_DIM, M_V_DIM), lambda b, s: (b, 0, 0, 0)),
            pl.BlockSpec((None, M_HEADS, M_QK_DIM), lambda b, s: (b, 0, 0)),
            pl.BlockSpec((None, M_HEADS, LANES), lambda b, s: (b, 0, 0)),
        ],
        out_shape=[
            jax.ShapeDtypeStruct((t_all, M_V_WIDTH), BF16),
            jax.ShapeDtypeStruct((batch, M_HEADS, M_QK_DIM, M_V_DIM), F32),
            jax.ShapeDtypeStruct((batch, M_HEADS, M_QK_DIM), F32),
            jax.ShapeDtypeStruct((batch, M_HEADS, LANES), F32),
        ],
        scratch_shapes=[
            pltpu.VMEM((M_HEADS, M_QK_DIM, M_V_DIM), F32),
            pltpu.VMEM((SUBLANES, M_QK_DIM), F32),
            pltpu.VMEM((SUBLANES, LANES), F32),
        ],
        compiler_params=_params("arbitrary", "arbitrary"),
        name="mlstm_prompt",
    )(z_all, z_all, z_all, z_all, z_all, bias_row, norm_w_row)


MLSTM_SAMPLE_GROUP = 2


def _mlstm_sample_body(n_valid, q_ref, k_ref, v_ref, o_ref, if_ref, bias_ref, nw_ref, c0_ref, n0_ref, m0_ref,
                       mh_in_ref, mh_ref, c_out, n_out, m_out, h_scr):
    del mh_in_ref
    for g in range(MLSTM_SAMPLE_GROUP):
        rows = slice(g * SAMPLE_ROWS, (g + 1) * SAMPLE_ROWS)
        gates = _mlstm_gates(if_ref[rows, :], bias_ref[...], n_valid)
        gates_t = jnp.transpose(jnp.concatenate([gates] * (LANES // SAMPLE_ROWS), axis=0))[:, :SAMPLE_ROWS]
        for h in range(M_HEADS):
            qs = slice(h * M_QK_DIM, (h + 1) * M_QK_DIM)
            vs = slice(h * M_V_DIM, (h + 1) * M_V_DIM)
            h_t, c_new, n_new, m_new = _mlstm_chunk(
                q_ref[rows, qs], k_ref[rows, qs], v_ref[rows, vs], gates, gates_t, h,
                c0_ref[g, h], n0_ref[g, h:h + 1, :], m0_ref[g:g + 1, h:h + 1])
            c_out[g, h] = c_new
            n_out[g, h:h + 1, :] = n_new
            m_out[g:g + 1, h:h + 1] = m_new
            h_scr[rows, vs] = _mlstm_head_out(h_t, nw_ref[:, vs], o_ref[rows, vs])
    mh_ref[...] = h_scr[...].astype(BF16)


def _mlstm_sample(z_all, bias_row, norm_w_row, c0, n0, m0, mh_all, row_off, n_valid):
    b = c0.shape[0]
    gr = MLSTM_SAMPLE_GROUP
    rows = gr * SAMPLE_ROWS
    rb = row_off // rows
    m0g = m0.reshape(b // gr, gr, M_HEADS)
    outs = pl.pallas_call(
        functools.partial(_mlstm_sample_body, n_valid),
        grid=(b // gr,),
        in_specs=[
            pl.BlockSpec((rows, M_QK_WIDTH), lambda i: (rb + i, COL_MQ // M_QK_WIDTH)),
            pl.BlockSpec((rows, M_QK_WIDTH), lambda i: (rb + i, COL_MK // M_QK_WIDTH)),
            pl.BlockSpec((rows, M_V_WIDTH), lambda i: (rb + i, COL_MV // M_V_WIDTH)),
            pl.BlockSpec((rows, M_V_WIDTH), lambda i: (rb + i, COL_MO // M_V_WIDTH)),
            pl.BlockSpec((rows, LANES), lambda i: (rb + i, COL_IF // LANES)),
            pl.BlockSpec((1, LANES), lambda i: (0, 0)),
            pl.BlockSpec((1, M_V_WIDTH), lambda i: (0, 0)),
            pl.BlockSpec((gr, M_HEADS, M_QK_DIM, M_V_DIM), lambda i: (i, 0, 0, 0)),
            pl.BlockSpec((gr, M_HEADS, M_QK_DIM), lambda i: (i, 0, 0)),
            pl.BlockSpec((None, gr, M_HEADS), lambda i: (i, 0, 0)),
            pl.BlockSpec(memory_space=pl.ANY),
        ],
        out_specs=[
            pl.BlockSpec((rows, M_V_WIDTH), lambda i: (rb + i, 0)),
            pl.BlockSpec((gr, M_HEADS, M_QK_DIM, M_V_DIM), lambda i: (i, 0, 0, 0)),
            pl.BlockSpec((gr, M_HEADS, M_QK_DIM), lambda i: (i, 0, 0)),
            pl.BlockSpec((None, gr, M_HEADS), lambda i: (i, 0, 0)),
        ],
        out_shape=[
            jax.ShapeDtypeStruct(mh_all.shape, BF16),
            jax.ShapeDtypeStruct(c0.shape, F32),
            jax.ShapeDtypeStruct(n0.shape, F32),
            jax.ShapeDtypeStruct(m0g.shape, F32),
        ],
        scratch_shapes=[pltpu.VMEM((rows, M_V_WIDTH), F32)],
        input_output_aliases={10: 0},
        compiler_params=_params("arbitrary"),
        name="mlstm_sample",
    )(z_all, z_all, z_all, z_all, z_all, bias_row, norm_w_row, c0, n0, m0g, mh_all)
    mh_all, c, n, m = outs
    return mh_all, c, n, m.reshape(b, M_HEADS)


def _merge_body(x_ref, a_ref, mh_ref, gza_ref, gzm_ref, wa_ref, wm_ref, wo_ref, gf_ref, wr_ref, br_ref, *refs):
    x1_ref, h2_ref, lg_ref = refs[-3:]
    pa = jnp.dot(a_ref[...], wa_ref[...], preferred_element_type=F32)
    pm = jnp.dot(mh_ref[...], wm_ref[...], preferred_element_type=F32)
    merged = jax.nn.sigmoid(gza_ref[...]) * pa + jax.nn.sigmoid(gzm_ref[...]) * pm
    x1 = x_ref[...] + jnp.dot(merged.astype(BF16), wo_ref[...], preferred_element_type=F32)
    x1_ref[...] = x1
    h2 = x1 * lax.rsqrt(jnp.mean(x1 * x1, axis=-1, keepdims=True) + RMS_EPS) * gf_ref[...]
    h2_ref[...] = h2.astype(BF16)
    lg_ref[...] = jnp.dot(h2, wr_ref[...], preferred_element_type=F32,
                          precision=lax.Precision.HIGHEST) + br_ref[...]


def _merge(x, a_all, mh_all, z_all, wa, wm, wo, g_ffn, wr_pad, br_pad, tile_m, row_off, prev=None):
    t = x.shape[0]
    t_all = z_all.shape[0]
    rb = row_off // tile_m
    const = lambda i: (0, 0)
    in_specs = [
        pl.BlockSpec((tile_m, D_MODEL), lambda i: (i, 0)),
        pl.BlockSpec((tile_m, ATTN_WIDTH), lambda i: (rb + i, 0)),
        pl.BlockSpec((tile_m, M_V_WIDTH), lambda i: (rb + i, 0)),
        pl.BlockSpec((tile_m, D_MODEL), lambda i: (rb + i, COL_GZ // D_MODEL)),
        pl.BlockSpec((tile_m, D_MODEL), lambda i: (rb + i, COL_GZ // D_MODEL + 1)),
        pl.BlockSpec((ATTN_WIDTH, D_MODEL), const, pipeline_mode=pl.Buffered(1)),
        pl.BlockSpec((M_V_WIDTH, D_MODEL), const, pipeline_mode=pl.Buffered(1)),
        pl.BlockSpec((D_MODEL, D_MODEL), const, pipeline_mode=pl.Buffered(1)),
        pl.BlockSpec((1, D_MODEL), const),
        pl.BlockSpec((D_MODEL, ROUTER_PAD), const),
        pl.BlockSpec((1, ROUTER_PAD), const),
    ]
    args = [x, a_all, mh_all, z_all, z_all, wa, wm, wo, g_ffn.reshape(1, D_MODEL), wr_pad, br_pad]
    aliases = {}
    if prev is not None:
        in_specs += [pl.BlockSpec(memory_space=pl.ANY)] * 3
        args += list(prev)
        aliases = {11: 0, 12: 1, 13: 2}
    return pl.pallas_call(
        _merge_body,
        grid=(t // tile_m,),
        in_specs=in_specs,
        out_specs=[
            pl.BlockSpec((tile_m, D_MODEL), lambda i: (rb + i, 0)),
            pl.BlockSpec((tile_m, D_MODEL), lambda i: (rb + i, 0)),
            pl.BlockSpec((tile_m, ROUTER_PAD), lambda i: (rb + i, 0)),
        ],
        out_shape=[
            jax.ShapeDtypeStruct((t_all, D_MODEL), F32),
            jax.ShapeDtypeStruct((t_all, D_MODEL), BF16),
            jax.ShapeDtypeStruct((t_all, ROUTER_PAD), F32),
        ],
        input_output_aliases=aliases,
        compiler_params=_params("arbitrary"),
        name="merge",
    )(*args)


def _expert_body(be_ref, nb_ref, xs_ref, w1g_ref, w1u_ref, b1g_ref, b1u_ref, w2_ref, b2_ref, y_ref):
    i = pl.program_id(0)
    c = pl.program_id(1)
    used = i < nb_ref[0]

    @pl.when(used)
    def _():
        x = xs_ref[...]
        zg = jnp.dot(x, w1g_ref[...], preferred_element_type=F32) + b1g_ref[...]
        zu = jnp.dot(x, w1u_ref[...], preferred_element_type=F32) + b1u_ref[...]
        gate = jnp.minimum(zg, SWIGLU_LIMIT)
        up = jnp.clip(zu, -SWIGLU_LIMIT, SWIGLU_LIMIT)
        act = gate * jax.nn.sigmoid(SWIGLU_ALPHA * gate) * (up + 1.0)
        y = jnp.dot(act.astype(BF16), w2_ref[...], preferred_element_type=F32)

        @pl.when(c == 0)
        def _():
            y_ref[...] = y + b2_ref[...]

        @pl.when(c != 0)
        def _():
            y_ref[...] += y

    @pl.when(jnp.logical_not(used) & (c == 0))
    def _():
        y_ref[...] = jnp.zeros_like(y_ref)


def _experts(block_e, n_used, xs, w1, b1, w2, b2):
    n_slots = xs.shape[0]
    n_blocks = n_slots // MOE_BLOCK_M
    n_chunks = D_FF // MOE_FF_CHUNK

    def chunk(i, c, nb):
        return jnp.where(i < nb[0], c, n_chunks - 1)

    grid_spec = pltpu.PrefetchScalarGridSpec(
        num_scalar_prefetch=2,
        grid=(n_blocks, n_chunks),
        in_specs=[
            pl.BlockSpec((MOE_BLOCK_M, D_MODEL), lambda i, c, be, nb: (i, 0)),
            pl.BlockSpec((None, D_MODEL, MOE_FF_CHUNK), lambda i, c, be, nb: (be[i], 0, chunk(i, c, nb))),
            pl.BlockSpec((None, D_MODEL, MOE_FF_CHUNK),
                         lambda i, c, be, nb: (be[i], 0, n_chunks + chunk(i, c, nb))),
            pl.BlockSpec((None, 1, MOE_FF_CHUNK), lambda i, c, be, nb: (be[i], 0, chunk(i, c, nb))),
            pl.BlockSpec((None, 1, MOE_FF_CHUNK), lambda i, c, be, nb: (be[i], 0, n_chunks + chunk(i, c, nb))),
            pl.BlockSpec((None, MOE_FF_CHUNK, D_MODEL), lambda i, c, be, nb: (be[i], chunk(i, c, nb), 0)),
            pl.BlockSpec((None, 1, D_MODEL), lambda i, c, be, nb: (be[i], 0, 0)),
        ],
        out_specs=pl.BlockSpec((MOE_BLOCK_M, D_MODEL), lambda i, c, be, nb: (i, 0)),
    )
    return pl.pallas_call(
        _expert_body,
        grid_spec=grid_spec,
        out_shape=jax.ShapeDtypeStruct((n_slots, D_MODEL), F32),
        compiler_params=_params("arbitrary", "arbitrary"),
        name="experts",
    )(block_e, n_used, xs, w1, w1, b1, b1, w2, b2)


def _route(logits):
    top_v, top_e = lax.top_k(logits[:, :N_EXPERTS], TOP_K)
    return jax.nn.softmax(top_v, axis=-1), top_e


def _dispatch(top_e):
    t = top_e.shape[0]
    n_assign = t * TOP_K
    flat_e = top_e.reshape(-1)
    order = jnp.argsort(flat_e)
    se = flat_e[order]
    counts = jnp.bincount(flat_e, length=N_EXPERTS)
    starts = jnp.cumsum(counts) - counts
    padded = (counts + MOE_BLOCK_M - 1) // MOE_BLOCK_M * MOE_BLOCK_M
    pends = jnp.cumsum(padded)
    pstarts = pends - padded
    dest_sorted = pstarts[se] + jnp.arange(n_assign) - starts[se]
    n_blocks = -(-n_assign // MOE_BLOCK_M) + N_EXPERTS
    n_slots = n_blocks * MOE_BLOCK_M
    slot_tok = jnp.zeros((n_slots,), jnp.int32).at[dest_sorted].set((order // TOP_K).astype(jnp.int32))
    dest = jnp.zeros((n_assign,), jnp.int32).at[order].set(dest_sorted.astype(jnp.int32))
    block_e = jnp.minimum(jnp.searchsorted(pends, jnp.arange(n_blocks) * MOE_BLOCK_M, side='right'),
                          N_EXPERTS - 1).astype(jnp.int32)
    n_used = (pends[-1] // MOE_BLOCK_M).astype(jnp.int32).reshape(1)
    return slot_tok, dest.reshape(t, TOP_K), block_e, n_used


def _rms_norm(x, g):
    return x * lax.rsqrt(jnp.mean(x * x, axis=-1, keepdims=True) + RMS_EPS) * g


def _lane_row(v):
    return jnp.pad(v.astype(F32), (0, LANES - v.shape[0])).reshape(1, LANES)


def kernel(x_prompt, x_sample, cache_swa_k, cache_swa_v, state_mlstm_C, state_mlstm_n, state_mlstm_m,
           g_mix, w_in, b_if, attn_sinks, mlstm_norm_w, w_attn_proj, w_mlstm_proj, w_out,
           g_ffn, w_router, b_router, w_e1, b_e1, w_e2, b_e2, g_final):
    bp, sp = x_prompt.shape[0], x_prompt.shape[1]
    bs, ls = x_sample.shape[0], x_sample.shape[1]
    tp = bp * sp
    ts = bs * SAMPLE_ROWS
    t_all = tp + ts
    l = 0
    w_in_p = _permute_w_in(w_in[l])
    wa, wm, wo = w_attn_proj[l].astype(BF16), w_mlstm_proj[l].astype(BF16), w_out[l].astype(BF16)
    wr_pad = jnp.pad(w_router[l], ((0, 0), (0, ROUTER_PAD - N_EXPERTS)))
    br_pad = _lane_row(b_router[l])
    w1 = _cast_bf16(w_e1[l], 512)
    w2 = _cast_bf16(w_e2[l], 512)
    b1, b2 = b_e1[l].reshape(N_EXPERTS, 1, 2 * D_FF), b_e2[l].reshape(N_EXPERTS, 1, D_MODEL)
    sinks_row = _lane_row(attn_sinks[l])
    bias_row = _lane_row(b_if[l])
    norm_w_row = mlstm_norm_w[l].reshape(1, M_V_WIDTH)

    xp = x_prompt.reshape(tp, D_MODEL)
    xs = jnp.pad(x_sample, ((0, 0), (0, SAMPLE_ROWS - ls), (0, 0))).reshape(ts, D_MODEL)
    z_all = _inproj(xp, g_mix[l], w_in_p, 1024, t_all, 0)
    z_all = _inproj(xs, g_mix[l], w_in_p, 1024, t_all, tp, z_all)

    a_all, k1, v1 = _attn_prompt(z_all, sinks_row, bp, sp)
    a_all, k2, v2 = _attn_sample(z_all, sinks_row, cache_swa_k[l].reshape(bs, WINDOW, KV_WIDTH),
                                 cache_swa_v[l].reshape(bs, WINDOW, KV_WIDTH), a_all, tp, ls)
    mh_all, c1, n1, m1 = _mlstm_prompt(z_all, bias_row, norm_w_row, bp, sp)
    mh_all, c2, n2, m2 = _mlstm_sample(z_all, bias_row, norm_w_row, state_mlstm_C[l], state_mlstm_n[l],
                                       state_mlstm_m[l], mh_all, tp, ls)

    merged = _merge(xp, a_all, mh_all, z_all, wa, wm, wo, g_ffn[l], wr_pad, br_pad, 256, 0)
    x1, h2, logits = _merge(xs, a_all, mh_all, z_all, wa, wm, wo, g_ffn[l], wr_pad, br_pad, 256, tp, merged)

    gates, top_e = _route(logits)
    slot_tok, dest, block_e, n_used = _dispatch(top_e)
    ys = _experts(block_e, n_used, h2[slot_tok], w1, b1, w2, b2)
    moe = jnp.sum(ys[dest] * gates[..., None], axis=1)
    y = _rms_norm(x1 + moe, g_final)

    y_prompt = y[:tp].reshape(bp, sp, D_MODEL)
    y_sample = y[tp:].reshape(bs, SAMPLE_ROWS, D_MODEL)[:, :ls]
    k1 = k1.reshape(bp, WINDOW, N_KV_HEADS, HEAD_DIM)
    v1 = v1.reshape(bp, WINDOW, N_KV_HEADS, HEAD_DIM)
    k2 = k2.reshape(bs, WINDOW, N_KV_HEADS, HEAD_DIM)
    v2 = v2.reshape(bs, WINDOW, N_KV_HEADS, HEAD_DIM)
    m1 = m1[:, :, 0]
    st = lambda t: t[None]
    return (y_prompt, y_sample, st(k1), st(v1), st(c1), st(n1), st(m1), st(k2), st(v2), st(c2), st(n2), st(m2))
```

```python
import functools

import jax
import jax.numpy as jnp
from jax import lax
from jax.experimental import pallas as pl
from jax.experimental.pallas import tpu as pltpu

D_MODEL = 2048
N_HEADS = 16
N_KV_HEADS = 2
HEAD_DIM = 64
Q_PER_KV = N_HEADS // N_KV_HEADS
WINDOW = 128
ATTN_WIDTH = N_HEADS * HEAD_DIM
KV_WIDTH = N_KV_HEADS * HEAD_DIM
M_HEADS = 4
M_QK_DIM = 128
M_V_DIM = 256
M_QK_WIDTH = M_HEADS * M_QK_DIM
M_V_WIDTH = M_HEADS * M_V_DIM
M_CHUNK = 64
N_EXPERTS = 32
TOP_K = 4
D_FF = D_MODEL
SWIGLU_LIMIT = 7.0
SWIGLU_ALPHA = 1.702
RMS_EPS = 1e-5
NEG = -1e30

F32 = jnp.float32
BF16 = jnp.bfloat16

LANES = 128
SUBLANES = 8
VMEM_LIMIT_BYTES = 56 * 1024 * 1024

COL_GZ = 0
COL_AQ = COL_GZ + 2 * D_MODEL
COL_MV = COL_AQ + ATTN_WIDTH
COL_MO = COL_MV + M_V_WIDTH
COL_MQ = COL_MO + M_V_WIDTH
COL_MK = COL_MQ + M_QK_WIDTH
COL_AK = COL_MK + M_QK_WIDTH
COL_IF = COL_AK + 2 * KV_WIDTH
IN_TILE_N = 512
IN_WIDTH_PAD = -(-(COL_IF + LANES) // IN_TILE_N) * IN_TILE_N

SAMPLE_ROWS = SUBLANES
MOE_BLOCK_M = 512
MOE_FF_CHUNK = 1024
ROUTER_PAD = LANES


def _permute_w_in(w_in):
    splits = (ATTN_WIDTH, KV_WIDTH, KV_WIDTH, M_QK_WIDTH, M_QK_WIDTH, M_V_WIDTH, M_V_WIDTH, M_HEADS, M_HEADS,
              2 * D_MODEL)
    parts, off = [], 0
    for width in splits:
        parts.append(w_in[:, off:off + width])
        off += width
    aq, ak, av, mq, mk, mv, mo, mi, mf, gz = parts
    cols = jnp.concatenate([gz, aq, mv, mo, mq, mk, ak, av, mi, mf], axis=1)
    cols = jnp.pad(cols, ((0, 0), (0, IN_WIDTH_PAD - cols.shape[1])))
    return cols.astype(BF16)


def _params(*semantics):
    return pltpu.CompilerParams(dimension_semantics=semantics, vmem_limit_bytes=VMEM_LIMIT_BYTES)


def _log_sigmoid(x):
    return jnp.minimum(x, 0.0) - jnp.log1p(jnp.exp(-jnp.abs(x)))


def _cast_body(x_ref, o_ref):
    o_ref[...] = x_ref[...].astype(BF16)


def _cast_bf16(w, block_rows):
    e, r, c = w.shape
    return pl.pallas_call(
        _cast_body,
        grid=(e, r // block_rows),
        in_specs=[pl.BlockSpec((None, block_rows, c), lambda i, j: (i, j, 0))],
        out_specs=pl.BlockSpec((None, block_rows, c), lambda i, j: (i, j, 0)),
        out_shape=jax.ShapeDtypeStruct(w.shape, BF16),
        compiler_params=_params("arbitrary", "arbitrary"),
        name="cast_bf16",
    )(w)


def _inproj_body(tiles_a, xa_ref, xb_ref, g_ref, w_ref, z_ref, h_scr):
    def normalise(x_ref):
        x = x_ref[...]
        y = x * lax.rsqrt(jnp.mean(x * x, axis=-1, keepdims=True) + RMS_EPS)
        h_scr[...] = (y * g_ref[...]).astype(BF16)

    first = pl.program_id(1) == 0
    in_a = pl.program_id(0) < tiles_a

    @pl.when(first & in_a)
    def _():
        normalise(xa_ref)

    @pl.when(first & jnp.logical_not(in_a))
    def _():
        normalise(xb_ref)

    z_ref[...] = jnp.dot(h_scr[...], w_ref[...], preferred_element_type=F32)


def _inproj(x_a, x_b, g, w_bf16, tile_m):
    tiles_a, tiles_b = x_a.shape[0] // tile_m, x_b.shape[0] // tile_m
    n = w_bf16.shape[1]
    return pl.pallas_call(
        functools.partial(_inproj_body, tiles_a),
        grid=(tiles_a + tiles_b, n // IN_TILE_N),
        in_specs=[
            pl.BlockSpec((tile_m, D_MODEL), lambda i, j: (jnp.minimum(i, tiles_a - 1), 0)),
            pl.BlockSpec((tile_m, D_MODEL), lambda i, j: (jnp.maximum(i - tiles_a, 0), 0)),
            pl.BlockSpec((1, D_MODEL), lambda i, j: (0, 0)),
            pl.BlockSpec((D_MODEL, IN_TILE_N), lambda i, j: (0, j)),
        ],
        out_specs=pl.BlockSpec((tile_m, IN_TILE_N), lambda i, j: (i, j)),
        out_shape=jax.ShapeDtypeStruct(((tiles_a + tiles_b) * tile_m, n), F32),
        scratch_shapes=[pltpu.VMEM((tile_m, D_MODEL), BF16)],
        compiler_params=_params("arbitrary", "arbitrary"),
        name="inproj",
    )(x_a, x_b, g.reshape(1, D_MODEL), w_bf16)


assert KV_WIDTH == LANES and N_KV_HEADS == 2 and LANES == 2 * HEAD_DIM
Q_TILES = ATTN_WIDTH // LANES
TILES_PER_KV = Q_TILES // N_KV_HEADS
_NT = (((1,), (1,)), ((), ()))


def _kv_placements(x):
    lo = lax.broadcasted_iota(jnp.int32, x.shape, 1) < HEAD_DIM
    swapped = pltpu.roll(x, HEAD_DIM, axis=1)
    return ((jnp.where(lo, x, 0.0).astype(BF16), jnp.where(lo, 0.0, swapped).astype(BF16)),
            (jnp.where(lo, swapped, 0.0).astype(BF16), jnp.where(lo, 0.0, x).astype(BF16)))


def _slab_head(g, p, j):
    return g * Q_PER_KV + 2 * j + p


def _attn_prompt_body(q_ref, kvc_ref, kvp_ref, sink_ref, a_ref, k_ref, v_ref):
    n = pl.program_id(1)
    kc, vc = kvc_ref[:, :KV_WIDTH], kvc_ref[:, KV_WIDTH:]
    k_ref[...] = kc
    v_ref[...] = vc
    kk = _kv_placements(jnp.concatenate([kvp_ref[:, :KV_WIDTH], kc], axis=0))
    vv = _kv_placements(jnp.concatenate([kvp_ref[:, KV_WIDTH:], vc], axis=0))
    q = q_ref[...].astype(BF16)
    row = lax.broadcasted_iota(jnp.int32, (WINDOW, WINDOW), 0)
    col = lax.broadcasted_iota(jnp.int32, (WINDOW, WINDOW), 1)
    mask = jnp.concatenate([(col > row) & (n > 0), col <= row], axis=1)
    slabs, sinks = [], []
    for g in range(N_KV_HEADS):
        qt = jnp.concatenate([q[:, (TILES_PER_KV * g + j) * LANES:(TILES_PER_KV * g + j + 1) * LANES]
                              for j in range(TILES_PER_KV)], axis=0)
        for p in range(2):
            s = lax.dot_general(qt, kk[g][p], _NT, preferred_element_type=F32) * (HEAD_DIM ** -0.5)
            for j in range(TILES_PER_KV):
                h = _slab_head(g, p, j)
                slabs.append(jnp.where(mask, s[j * WINDOW:(j + 1) * WINDOW], NEG))
                sinks.append(jnp.broadcast_to(sink_ref[:, h:h + 1], (WINDOW, 1)))
    s_all = jnp.concatenate(slabs, axis=0)
    sink = jnp.concatenate(sinks, axis=0)
    m = jnp.maximum(jnp.max(s_all, axis=1, keepdims=True), sink)
    p_all = jnp.exp(s_all - m)
    denom = jnp.sum(p_all, axis=1, keepdims=True) + jnp.exp(sink - m)
    pn = (p_all * (1.0 / denom)).astype(BF16)
    half = TILES_PER_KV * WINDOW
    for g in range(N_KV_HEADS):
        base = 2 * half * g
        o = (jnp.dot(pn[base:base + half], vv[g][0], preferred_element_type=F32)
             + jnp.dot(pn[base + half:base + 2 * half], vv[g][1], preferred_element_type=F32))
        for j in range(TILES_PER_KV):
            t = TILES_PER_KV * g + j
            a_ref[:, t * LANES:(t + 1) * LANES] = o[j * WINDOW:(j + 1) * WINDOW].astype(BF16)


def _attn_prompt(z_all, sinks_row, batch, seq):
    nb = seq // WINDOW
    return pl.pallas_call(
        _attn_prompt_body,
        grid=(batch, nb),
        in_specs=[
            pl.BlockSpec((WINDOW, ATTN_WIDTH), lambda b, n: (b * nb + n, COL_AQ // ATTN_WIDTH)),
            pl.BlockSpec((WINDOW, 2 * KV_WIDTH), lambda b, n: (b * nb + n, COL_AK // (2 * KV_WIDTH))),
            pl.BlockSpec((WINDOW, 2 * KV_WIDTH),
                         lambda b, n: (b * nb + jnp.maximum(n - 1, 0), COL_AK // (2 * KV_WIDTH))),
            pl.BlockSpec((1, LANES), lambda b, n: (0, 0)),
        ],
        out_specs=[
            pl.BlockSpec((WINDOW, ATTN_WIDTH), lambda b, n: (b * nb + n, 0)),
            pl.BlockSpec((None, WINDOW, KV_WIDTH), lambda b, n: (b, 0, 0)),
            pl.BlockSpec((None, WINDOW, KV_WIDTH), lambda b, n: (b, 0, 0)),
        ],
        out_shape=[
            jax.ShapeDtypeStruct((batch * seq, ATTN_WIDTH), BF16),
            jax.ShapeDtypeStruct((batch, WINDOW, KV_WIDTH), F32),
            jax.ShapeDtypeStruct((batch, WINDOW, KV_WIDTH), F32),
        ],
        compiler_params=_params("arbitrary", "arbitrary"),
        name="attn_prompt",
    )(z_all, z_all, z_all, sinks_row)


ATTN_SAMPLE_GROUP = 8


def _attn_sample_body(n_valid, q_ref, kv_ref, ck_ref, cv_ref, sink_ref, a_ref, nk_ref, nv_ref, o_scr):
    rs = SAMPLE_ROWS
    slab = TILES_PER_KV * rs
    r8 = lax.broadcasted_iota(jnp.int32, (rs, KV_WIDTH), 0)
    qi_c = jnp.bitwise_and(lax.broadcasted_iota(jnp.int32, (slab, WINDOW), 0), rs - 1)
    mask_c = lax.broadcasted_iota(jnp.int32, (slab, WINDOW), 1) > qi_c
    qi_n = jnp.bitwise_and(lax.broadcasted_iota(jnp.int32, (slab, rs), 0), rs - 1)
    mask_n = lax.broadcasted_iota(jnp.int32, (slab, rs), 1) <= qi_n
    sc_list, sn_list, sink_list, values = [], [], [], []
    for b in range(ATTN_SAMPLE_GROUP):
        rows = slice(b * rs, (b + 1) * rs)
        q = q_ref[rows, :]
        kn, vn = kv_ref[rows, :KV_WIDTH], kv_ref[rows, KV_WIDTH:]
        ck, cv = ck_ref[b], cv_ref[b]
        for src, new, dst in ((ck, kn, nk_ref), (cv, vn, nv_ref)):
            rolled = pltpu.roll(src, WINDOW - n_valid, axis=0)
            dst[b] = rolled
            tail = jnp.where(r8 >= rs - n_valid, pltpu.roll(new, rs - n_valid, axis=0), rolled[WINDOW - rs:, :])
            dst[b, WINDOW - rs:, :] = tail
        kk_c, kk_n = _kv_placements(ck), _kv_placements(kn)
        values.append((_kv_placements(cv), _kv_placements(vn)))
        for g in range(N_KV_HEADS):
            qt = jnp.concatenate([q[:, (TILES_PER_KV * g + j) * LANES:(TILES_PER_KV * g + j + 1) * LANES]
                                  for j in range(TILES_PER_KV)], axis=0).astype(BF16)
            for p in range(2):
                s_c = lax.dot_general(qt, kk_c[g][p], _NT, preferred_element_type=F32) * (HEAD_DIM ** -0.5)
                s_n = lax.dot_general(qt, kk_n[g][p], _NT, preferred_element_type=F32) * (HEAD_DIM ** -0.5)
                sc_list.append(jnp.where(mask_c, s_c, NEG))
                sn_list.append(jnp.where(mask_n, s_n, NEG))
                for j in range(TILES_PER_KV):
                    h = _slab_head(g, p, j)
                    sink_list.append(jnp.broadcast_to(sink_ref[:, h:h + 1], (rs, 1)))
    s_c = jnp.concatenate(sc_list, axis=0)
    s_n = jnp.concatenate(sn_list, axis=0)
    sink = jnp.concatenate(sink_list, axis=0)
    m = jnp.maximum(jnp.maximum(jnp.max(s_c, axis=1, keepdims=True), jnp.max(s_n, axis=1, keepdims=True)), sink)
    p_c = jnp.exp(s_c - m)
    p_n = jnp.exp(s_n - m)
    inv = 1.0 / (jnp.sum(p_c, axis=1, keepdims=True) + jnp.sum(p_n, axis=1, keepdims=True) + jnp.exp(sink - m))
    p_c = (p_c * inv).astype(BF16)
    p_n = (p_n * inv).astype(BF16)
    for b in range(ATTN_SAMPLE_GROUP):
        vv_c, vv_n = values[b]
        for g in range(N_KV_HEADS):
            base = (b * N_KV_HEADS + g) * 2 * slab
            o = (jnp.dot(p_c[base:base + slab], vv_c[g][0], preferred_element_type=F32)
                 + jnp.dot(p_c[base + slab:base + 2 * slab], vv_c[g][1], preferred_element_type=F32)
                 + jnp.dot(p_n[base:base + slab], vv_n[g][0], preferred_element_type=F32)
                 + jnp.dot(p_n[base + slab:base + 2 * slab], vv_n[g][1], preferred_element_type=F32))
            for j in range(TILES_PER_KV):
                t = TILES_PER_KV * g + j
                o_scr[b * rs:(b + 1) * rs, t * LANES:(t + 1) * LANES] = o[j * rs:(j + 1) * rs]
    a_ref[...] = o_scr[...].astype(BF16)


def _attn_sample(z_all, sinks_row, cache_k, cache_v, row_off, n_valid):
    b = cache_k.shape[0]
    gr = ATTN_SAMPLE_GROUP
    rows = gr * SAMPLE_ROWS
    rb = row_off // rows
    return pl.pallas_call(
        functools.partial(_attn_sample_body, n_valid),
        grid=(b // gr,),
        in_specs=[
            pl.BlockSpec((rows, ATTN_WIDTH), lambda i: (rb + i, COL_AQ // ATTN_WIDTH)),
            pl.BlockSpec((rows, 2 * KV_WIDTH), lambda i: (rb + i, COL_AK // (2 * KV_WIDTH))),
            pl.BlockSpec((gr, WINDOW, KV_WIDTH), lambda i: (i, 0, 0)),
            pl.BlockSpec((gr, WINDOW, KV_WIDTH), lambda i: (i, 0, 0)),
            pl.BlockSpec((1, LANES), lambda i: (0, 0)),
        ],
        out_specs=[
            pl.BlockSpec((rows, ATTN_WIDTH), lambda i: (i, 0)),
            pl.BlockSpec((gr, WINDOW, KV_WIDTH), lambda i: (i, 0, 0)),
            pl.BlockSpec((gr, WINDOW, KV_WIDTH), lambda i: (i, 0, 0)),
        ],
        out_shape=[
            jax.ShapeDtypeStruct((b * SAMPLE_ROWS, ATTN_WIDTH), BF16),
            jax.ShapeDtypeStruct(cache_k.shape, F32),
            jax.ShapeDtypeStruct(cache_v.shape, F32),
        ],
        scratch_shapes=[pltpu.VMEM((rows, ATTN_WIDTH), F32)],
        compiler_params=_params("arbitrary"),
        name="attn_sample",
    )(z_all, z_all, cache_k, cache_v, sinks_row)


def _mlstm_gates(if_blk, bias_row, n_valid):
    length = if_blk.shape[0]
    g = if_blk + bias_row
    lane = lax.broadcasted_iota(jnp.int32, g.shape, 1)
    g = jnp.where(lane < M_HEADS, g, _log_sigmoid(g))
    if n_valid < length:
        row = lax.broadcasted_iota(jnp.int32, g.shape, 0)
        g = jnp.where(row < n_valid, g, jnp.where(lane < M_HEADS, NEG, 0.0))
    return g


def _mlstm_chunk(q, k, v, gates, gates_t, h, c, n, m):
    length = q.shape[0]
    i_col = gates[:, h:h + 1]
    f_col = gates[:, M_HEADS + h:M_HEADS + h + 1]
    i_row = gates_t[h:h + 1, :]
    f_row = gates_t[M_HEADS + h:M_HEADS + h + 1, :]
    t_idx = lax.broadcasted_iota(jnp.int32, (length, length), 0)
    s_idx = lax.broadcasted_iota(jnp.int32, (length, length), 1)
    causal = s_idx <= t_idx
    bcum_col = jnp.sum(jnp.where(causal, f_row, 0.0), axis=1, keepdims=True)
    bcum_row = jnp.sum(jnp.where(t_idx <= s_idx, f_col, 0.0), axis=0, keepdims=True)
    dlog = jnp.where(causal, bcum_col - bcum_row + i_row, NEG)
    inter = bcum_col + m
    mt = jnp.maximum(inter, jnp.max(dlog, axis=1, keepdims=True))
    w_inter = jnp.exp(inter - mt)
    qb = q.astype(BF16)
    ks = k * (M_QK_DIM ** -0.5)
    vb = v.astype(BF16)
    qk = lax.dot_general(qb, ks.astype(BF16), (((1,), (1,)), ((), ())), preferred_element_type=F32)
    qk = qk * jnp.exp(dlog - mt)
    num = (w_inter * jnp.dot(qb, c.astype(BF16), preferred_element_type=F32)
           + jnp.dot(qk.astype(BF16), vb, preferred_element_type=F32))
    den = w_inter * jnp.sum(q * n, axis=1, keepdims=True) + jnp.sum(qk, axis=1, keepdims=True)
    h_t = num / jnp.maximum(jnp.abs(den), jnp.exp(-mt))
    m_new = mt[length - 1:length, :]
    w_state = jnp.exp(bcum_col[length - 1:length, :] - bcum_col + i_col - m_new)
    decay = jnp.exp(inter[length - 1:length, :] - m_new)
    kw = ks * w_state
    c_new = decay * c + jnp.dot(kw.T.astype(BF16), vb, preferred_element_type=F32)
    n_new = decay * n + jnp.sum(kw, axis=0, keepdims=True)
    return h_t, c_new, n_new, m_new


def _mlstm_head_out(h_t, norm_w, o_gate):
    hs = h_t * lax.rsqrt(jnp.mean(h_t * h_t, axis=-1, keepdims=True) + RMS_EPS)
    return hs * norm_w * jax.nn.sigmoid(o_gate)


MLSTM_SEG = 512


def _mlstm_prompt_body(q_ref, k_ref, v_ref, o_ref, if_ref, bias_ref, nw_ref, mh_ref, c_out, n_out, m_out,
                       c_scr, n_scr, m_scr):
    seg = pl.program_id(1)

    @pl.when(seg == 0)
    def _():
        c_scr[...] = jnp.zeros_like(c_scr)
        n_scr[...] = jnp.zeros_like(n_scr)
        m_scr[...] = jnp.full_like(m_scr, NEG)

    def chunk_body(ci, carry):
        r = pl.multiple_of(ci * M_CHUNK, M_CHUNK)
        rows = pl.ds(r, M_CHUNK)
        gates = _mlstm_gates(if_ref[rows, :], bias_ref[...], M_CHUNK)
        gates_t = gates.T
        for h in range(M_HEADS):
            qs = slice(h * M_QK_DIM, (h + 1) * M_QK_DIM)
            vs = slice(h * M_V_DIM, (h + 1) * M_V_DIM)
            h_t, c_new, n_new, m_new = _mlstm_chunk(
                q_ref[rows, qs], k_ref[rows, qs], v_ref[rows, vs], gates, gates_t, h,
                c_scr[h], n_scr[h:h + 1, :], m_scr[h:h + 1, 0:1])
            c_scr[h] = c_new
            n_scr[h:h + 1, :] = n_new
            m_scr[h:h + 1, :] = jnp.broadcast_to(m_new, (1, LANES))
            mh_ref[rows, vs] = _mlstm_head_out(h_t, nw_ref[:, vs], o_ref[rows, vs]).astype(BF16)
        return carry

    lax.fori_loop(0, MLSTM_SEG // M_CHUNK, chunk_body, 0)

    @pl.when(seg == pl.num_programs(1) - 1)
    def _():
        c_out[...] = c_scr[...]
        n_out[...] = n_scr[0:M_HEADS, :]
        m_out[...] = m_scr[0:M_HEADS, :]


def _mlstm_prompt(z_all, bias_row, norm_w_row, batch, seq):
    ns = seq // MLSTM_SEG
    row = lambda b, s: b * ns + s
    return pl.pallas_call(
        _mlstm_prompt_body,
        grid=(batch, ns),
        in_specs=[
            pl.BlockSpec((MLSTM_SEG, M_QK_WIDTH), lambda b, s: (row(b, s), COL_MQ // M_QK_WIDTH)),
            pl.BlockSpec((MLSTM_SEG, M_QK_WIDTH), lambda b, s: (row(b, s), COL_MK // M_QK_WIDTH)),
            pl.BlockSpec((MLSTM_SEG, M_V_WIDTH), lambda b, s: (row(b, s), COL_MV // M_V_WIDTH)),
            pl.BlockSpec((MLSTM_SEG, M_V_WIDTH), lambda b, s: (row(b, s), COL_MO // M_V_WIDTH)),
            pl.BlockSpec((MLSTM_SEG, LANES), lambda b, s: (row(b, s), COL_IF // LANES)),
            pl.BlockSpec((1, LANES), lambda b, s: (0, 0)),
            pl.BlockSpec((1, M_V_WIDTH), lambda b, s: (0, 0)),
        ],
        out_specs=[
            pl.BlockSpec((MLSTM_SEG, M_V_WIDTH), lambda b, s: (row(b, s), 0)),
            pl.BlockSpec((None, M_HEADS, M_QK_DIM, M_V_DIM), lambda b, s: (b, 0, 0, 0)),
            pl.BlockSpec((None, M_HEADS, M_QK_DIM), lambda b, s: (b, 0, 0)),
            pl.BlockSpec((None, M_HEADS, LANES), lambda b, s: (b, 0, 0)),
        ],
        out_shape=[
            jax.ShapeDtypeStruct((batch * seq, M_V_WIDTH), BF16),
            jax.ShapeDtypeStruct((batch, M_HEADS, M_QK_DIM, M_V_DIM), F32),
            jax.ShapeDtypeStruct((batch, M_HEADS, M_QK_DIM), F32),
            jax.ShapeDtypeStruct((batch, M_HEADS, LANES), F32),
        ],
        scratch_shapes=[
            pltpu.VMEM((M_HEADS, M_QK_DIM, M_V_DIM), F32),
            pltpu.VMEM((SUBLANES, M_QK_DIM), F32),
            pltpu.VMEM((SUBLANES, LANES), F32),
        ],
        compiler_params=_params("arbitrary", "arbitrary"),
        name="mlstm_prompt",
    )(z_all, z_all, z_all, z_all, z_all, bias_row, norm_w_row)


MLSTM_SAMPLE_GROUP = 2


def _mlstm_sample_body(n_valid, q_ref, k_ref, v_ref, o_ref, if_ref, bias_ref, nw_ref, c0_ref, n0_ref, m0_ref,
                       mh_ref, c_out, n_out, m_out, h_scr):
    for g in range(MLSTM_SAMPLE_GROUP):
        rows = slice(g * SAMPLE_ROWS, (g + 1) * SAMPLE_ROWS)
        gates = _mlstm_gates(if_ref[rows, :], bias_ref[...], n_valid)
        gates_t = jnp.transpose(jnp.concatenate([gates] * (LANES // SAMPLE_ROWS), axis=0))[:, :SAMPLE_ROWS]
        for h in range(M_HEADS):
            qs = slice(h * M_QK_DIM, (h + 1) * M_QK_DIM)
            vs = slice(h * M_V_DIM, (h + 1) * M_V_DIM)
            h_t, c_new, n_new, m_new = _mlstm_chunk(
                q_ref[rows, qs], k_ref[rows, qs], v_ref[rows, vs], gates, gates_t, h,
                c0_ref[g, h], n0_ref[g, h:h + 1, :], m0_ref[g:g + 1, h:h + 1])
            c_out[g, h] = c_new
            n_out[g, h:h + 1, :] = n_new
            m_out[g:g + 1, h:h + 1] = m_new
            h_scr[rows, vs] = _mlstm_head_out(h_t, nw_ref[:, vs], o_ref[rows, vs])
    mh_ref[...] = h_scr[...].astype(BF16)


def _mlstm_sample(z_all, bias_row, norm_w_row, c0, n0, m0, row_off, n_valid):
    b = c0.shape[0]
    gr = MLSTM_SAMPLE_GROUP
    rows = gr * SAMPLE_ROWS
    rb = row_off // rows
    m0g = m0.reshape(b // gr, gr, M_HEADS)
    outs = pl.pallas_call(
        functools.partial(_mlstm_sample_body, n_valid),
        grid=(b // gr,),
        in_specs=[
            pl.BlockSpec((rows, M_QK_WIDTH), lambda i: (rb + i, COL_MQ // M_QK_WIDTH)),
            pl.BlockSpec((rows, M_QK_WIDTH), lambda i: (rb + i, COL_MK // M_QK_WIDTH)),
            pl.BlockSpec((rows, M_V_WIDTH), lambda i: (rb + i, COL_MV // M_V_WIDTH)),
            pl.BlockSpec((rows, M_V_WIDTH), lambda i: (rb + i, COL_MO // M_V_WIDTH)),
            pl.BlockSpec((rows, LANES), lambda i: (rb + i, COL_IF // LANES)),
            pl.BlockSpec((1, LANES), lambda i: (0, 0)),
            pl.BlockSpec((1, M_V_WIDTH), lambda i: (0, 0)),
            pl.BlockSpec((gr, M_HEADS, M_QK_DIM, M_V_DIM), lambda i: (i, 0, 0, 0)),
            pl.BlockSpec((gr, M_HEADS, M_QK_DIM), lambda i: (i, 0, 0)),
            pl.BlockSpec((None, gr, M_HEADS), lambda i: (i, 0, 0)),
        ],
        out_specs=[
            pl.BlockSpec((rows, M_V_WIDTH), lambda i: (i, 0)),
            pl.BlockSpec((gr, M_HEADS, M_QK_DIM, M_V_DIM), lambda i: (i, 0, 0, 0)),
            pl.BlockSpec((gr, M_HEADS, M_QK_DIM), lambda i: (i, 0, 0)),
            pl.BlockSpec((None, gr, M_HEADS), lambda i: (i, 0, 0)),
        ],
        out_shape=[
            jax.ShapeDtypeStruct((b * SAMPLE_ROWS, M_V_WIDTH), BF16),
            jax.ShapeDtypeStruct(c0.shape, F32),
            jax.ShapeDtypeStruct(n0.shape, F32),
            jax.ShapeDtypeStruct(m0g.shape, F32),
        ],
        scratch_shapes=[pltpu.VMEM((rows, M_V_WIDTH), F32)],
        compiler_params=_params("arbitrary"),
        name="mlstm_sample",
    )(z_all, z_all, z_all, z_all, z_all, bias_row, norm_w_row, c0, n0, m0g)
    mh, c, n, m = outs
    return mh, c, n, m.reshape(b, M_HEADS)


def _merge_body(tiles_a, xa_ref, xb_ref, aa_ref, ab_ref, ma_ref, mb_ref, gza_ref, gzm_ref, wa_ref, wm_ref, wo_ref,
                gf_ref, wr_ref, br_ref, x1_ref, h2_ref, lg_ref):
    in_a = pl.program_id(0) < tiles_a
    x = jnp.where(in_a, xa_ref[...], xb_ref[...])
    a = jnp.where(in_a, aa_ref[...], ab_ref[...])
    mh = jnp.where(in_a, ma_ref[...], mb_ref[...])
    pa = jnp.dot(a, wa_ref[...], preferred_element_type=F32)
    pm = jnp.dot(mh, wm_ref[...], preferred_element_type=F32)
    merged = jax.nn.sigmoid(gza_ref[...]) * pa + jax.nn.sigmoid(gzm_ref[...]) * pm
    x1 = x + jnp.dot(merged.astype(BF16), wo_ref[...], preferred_element_type=F32)
    x1_ref[...] = x1
    h2 = x1 * lax.rsqrt(jnp.mean(x1 * x1, axis=-1, keepdims=True) + RMS_EPS) * gf_ref[...]
    h2_ref[...] = h2
    lg_ref[...] = jnp.dot(h2, wr_ref[...], preferred_element_type=F32,
                          precision=lax.Precision.HIGHEST) + br_ref[...]


def _merge(x_a, x_b, a_a, a_b, mh_a, mh_b, z_all, wa, wm, wo, g_ffn, wr_pad, br_pad, tile_m):
    tiles_a, tiles_b = x_a.shape[0] // tile_m, x_b.shape[0] // tile_m
    t_all = z_all.shape[0]
    const = lambda i: (0, 0)
    from_a = lambda i: (jnp.minimum(i, tiles_a - 1), 0)
    from_b = lambda i: (jnp.maximum(i - tiles_a, 0), 0)
    return pl.pallas_call(
        functools.partial(_merge_body, tiles_a),
        grid=(tiles_a + tiles_b,),
        in_specs=[
            pl.BlockSpec((tile_m, D_MODEL), from_a),
            pl.BlockSpec((tile_m, D_MODEL), from_b),
            pl.BlockSpec((tile_m, ATTN_WIDTH), from_a),
            pl.BlockSpec((tile_m, ATTN_WIDTH), from_b),
            pl.BlockSpec((tile_m, M_V_WIDTH), from_a),
            pl.BlockSpec((tile_m, M_V_WIDTH), from_b),
            pl.BlockSpec((tile_m, D_MODEL), lambda i: (i, COL_GZ // D_MODEL)),
            pl.BlockSpec((tile_m, D_MODEL), lambda i: (i, COL_GZ // D_MODEL + 1)),
            pl.BlockSpec((ATTN_WIDTH, D_MODEL), const, pipeline_mode=pl.Buffered(1)),
            pl.BlockSpec((M_V_WIDTH, D_MODEL), const, pipeline_mode=pl.Buffered(1)),
            pl.BlockSpec((D_MODEL, D_MODEL), const, pipeline_mode=pl.Buffered(1)),
            pl.BlockSpec((1, D_MODEL), const),
            pl.BlockSpec((D_MODEL, ROUTER_PAD), const),
            pl.BlockSpec((1, ROUTER_PAD), const),
        ],
        out_specs=[
            pl.BlockSpec((tile_m, D_MODEL), lambda i: (i, 0)),
            pl.BlockSpec((tile_m, D_MODEL), lambda i: (i, 0)),
            pl.BlockSpec((tile_m, ROUTER_PAD), lambda i: (i, 0)),
        ],
        out_shape=[
            jax.ShapeDtypeStruct((t_all, D_MODEL), F32),
            jax.ShapeDtypeStruct((t_all, D_MODEL), F32),
            jax.ShapeDtypeStruct((t_all, ROUTER_PAD), F32),
        ],
        compiler_params=_params("arbitrary"),
        name="merge",
    )(x_a, x_b, a_a, a_b, mh_a, mh_b, z_all, z_all, wa, wm, wo, g_ffn.reshape(1, D_MODEL), wr_pad, br_pad)


def _expert_body(be_ref, nb_ref, xs_ref, w1g_ref, w1u_ref, b1g_ref, b1u_ref, w2_ref, b2_ref, y_ref):
    i = pl.program_id(0)
    c = pl.program_id(1)
    used = i < nb_ref[0]

    @pl.when(used)
    def _():
        x = xs_ref[...].astype(BF16)
        zg = jnp.dot(x, w1g_ref[...], preferred_element_type=F32) + b1g_ref[...]
        zu = jnp.dot(x, w1u_ref[...], preferred_element_type=F32) + b1u_ref[...]
        gate = jnp.minimum(zg, SWIGLU_LIMIT)
        up = jnp.clip(zu, -SWIGLU_LIMIT, SWIGLU_LIMIT)
        act = gate * jax.nn.sigmoid(SWIGLU_ALPHA * gate) * (up + 1.0)
        y = jnp.dot(act.astype(BF16), w2_ref[...], preferred_element_type=F32)

        @pl.when(c == 0)
        def _():
            y_ref[...] = y + b2_ref[...]

        @pl.when(c != 0)
        def _():
            y_ref[...] += y

    @pl.when(jnp.logical_not(used) & (c == 0))
    def _():
        y_ref[...] = jnp.zeros_like(y_ref)


def _experts(block_e, n_used, xs, w1, b1, w2, b2):
    n_slots = xs.shape[0]
    n_blocks = n_slots // MOE_BLOCK_M
    n_chunks = D_FF // MOE_FF_CHUNK

    def chunk(i, c, nb):
        return jnp.where(i < nb[0], c, n_chunks - 1)

    grid_spec = pltpu.PrefetchScalarGridSpec(
        num_scalar_prefetch=2,
        grid=(n_blocks, n_chunks),
        in_specs=[
            pl.BlockSpec((MOE_BLOCK_M, D_MODEL), lambda i, c, be, nb: (i, 0)),
            pl.BlockSpec((None, D_MODEL, MOE_FF_CHUNK), lambda i, c, be, nb: (be[i], 0, chunk(i, c, nb))),
            pl.BlockSpec((None, D_MODEL, MOE_FF_CHUNK),
                         lambda i, c, be, nb: (be[i], 0, n_chunks + chunk(i, c, nb))),
            pl.BlockSpec((None, 1, MOE_FF_CHUNK), lambda i, c, be, nb: (be[i], 0, chunk(i, c, nb))),
            pl.BlockSpec((None, 1, MOE_FF_CHUNK), lambda i, c, be, nb: (be[i], 0, n_chunks + chunk(i, c, nb))),
            pl.BlockSpec((None, MOE_FF_CHUNK, D_MODEL), lambda i, c, be, nb: (be[i], chunk(i, c, nb), 0)),
            pl.BlockSpec((None, 1, D_MODEL), lambda i, c, be, nb: (be[i], 0, 0)),
        ],
        out_specs=pl.BlockSpec((MOE_BLOCK_M, D_MODEL), lambda i, c, be, nb: (i, 0)),
    )
    return pl.pallas_call(
        _expert_body,
        grid_spec=grid_spec,
        out_shape=jax.ShapeDtypeStruct((n_slots, D_MODEL), F32),
        compiler_params=_params("arbitrary", "arbitrary"),
        name="experts",
    )(block_e, n_used, xs, w1, w1, b1, b1, w2, b2)


ROUTE_TILE = 1024
LANE_E = 0
LANE_RANK = TOP_K


def _route_body(lg_ref, meta_ref, gate_ref, cnt_ref, tri_scr, carry_scr):
    @pl.when(pl.program_id(0) == 0)
    def _():
        r = lax.broadcasted_iota(jnp.int32, (ROUTE_TILE, ROUTE_TILE), 0)
        c = lax.broadcasted_iota(jnp.int32, (ROUTE_TILE, ROUTE_TILE), 1)
        tri_scr[...] = jnp.where(c < r, 1.0, 0.0).astype(BF16)
        carry_scr[...] = jnp.zeros_like(carry_scr)

    lane_i = lax.broadcasted_iota(jnp.int32, (ROUTE_TILE, LANES), 1)
    lane = lane_i.astype(F32)
    x = jnp.where(lane_i < N_EXPERTS, lg_ref[...], -jnp.inf)
    vals, idxs, sels = [], [], []
    for _ in range(TOP_K):
        mx = jnp.max(x, axis=1, keepdims=True)
        idx = jnp.min(jnp.where(x == mx, lane, float(LANES)), axis=1, keepdims=True)
        sel = lane == idx
        x = jnp.where(sel, -jnp.inf, x)
        vals.append(mx)
        idxs.append(idx)
        sels.append(sel)
    weights = [jnp.exp(v - vals[0]) for v in vals]
    inv = 1.0 / (weights[0] + weights[1] + weights[2] + weights[3])
    onehot = sum(jnp.where(s, 1.0, 0.0) for s in sels)
    ranks = jnp.dot(tri_scr[...], onehot.astype(BF16), preferred_element_type=F32) + carry_scr[0:1, :]
    meta = jnp.zeros((ROUTE_TILE, LANES), F32)
    gate = jnp.zeros((ROUTE_TILE, LANES), F32)
    for k in range(TOP_K):
        rank_k = jnp.sum(jnp.where(sels[k], ranks, 0.0), axis=1, keepdims=True)
        meta = jnp.where(lane_i == LANE_E + k, idxs[k], meta)
        meta = jnp.where(lane_i == LANE_RANK + k, rank_k, meta)
        gate = jnp.where(lane_i == k, weights[k] * inv, gate)
    meta_ref[...] = meta.astype(jnp.int32)
    gate_ref[...] = gate
    carry_scr[0:1, :] = carry_scr[0:1, :] + jnp.sum(onehot, axis=0, keepdims=True)
    cnt_ref[...] = carry_scr[...]


def _route(logits):
    t = logits.shape[0]
    return pl.pallas_call(
        _route_body,
        grid=(t // ROUTE_TILE,),
        in_specs=[pl.BlockSpec((ROUTE_TILE, LANES), lambda i: (i, 0))],
        out_specs=[
            pl.BlockSpec((ROUTE_TILE, LANES), lambda i: (i, 0)),
            pl.BlockSpec((ROUTE_TILE, LANES), lambda i: (i, 0)),
            pl.BlockSpec((SUBLANES, LANES), lambda i: (0, 0)),
        ],
        out_shape=[
            jax.ShapeDtypeStruct((t, LANES), jnp.int32),
            jax.ShapeDtypeStruct((t, LANES), F32),
            jax.ShapeDtypeStruct((SUBLANES, LANES), F32),
        ],
        scratch_shapes=[pltpu.VMEM((ROUTE_TILE, ROUTE_TILE), BF16), pltpu.VMEM((SUBLANES, LANES), F32)],
        compiler_params=_params("arbitrary"),
        name="route",
    )(logits)


def _slot_tables(counts, n_blocks):
    padded = (counts + MOE_BLOCK_M - 1) // MOE_BLOCK_M * MOE_BLOCK_M
    pends = jnp.cumsum(padded)
    block_e = jnp.minimum(jnp.searchsorted(pends, jnp.arange(n_blocks) * MOE_BLOCK_M, side='right'),
                          N_EXPERTS - 1).astype(jnp.int32)
    n_used = (pends[-1] // MOE_BLOCK_M).astype(jnp.int32).reshape(1)
    return (pends - padded).astype(jnp.int32), pends.astype(jnp.int32), padded.astype(jnp.int32), block_e, n_used


DISPATCH_TILE = 512
IDX_SLOTS = 3


def _dispatch_body(pend_ref, padded_ref, dest_hbm, h_ref, xs_hbm, idx_smem, zero_scr, idx_sem, row_sem, zero_sem):
    i = pl.program_id(0)
    n = pl.num_programs(0)

    def idx_copy(tile):
        s = lax.rem(tile, IDX_SLOTS)
        return pltpu.make_async_copy(dest_hbm.at[pl.ds(tile, 1), :], idx_smem.at[pl.ds(s, 1), :], idx_sem.at[s])

    def zero_copy(e):
        start = pl.multiple_of(pend_ref[e] - MOE_BLOCK_M, MOE_BLOCK_M)
        return pltpu.make_async_copy(zero_scr, xs_hbm.at[pl.ds(start, MOE_BLOCK_M), :], zero_sem)

    def row_copy(t, d):
        return pltpu.make_async_copy(h_ref.at[pl.ds(t, 1), :], xs_hbm.at[pl.ds(d, 1), :], row_sem)

    @pl.when(i == 0)
    def _():
        zero_scr[...] = jnp.zeros_like(zero_scr)
        for e in range(N_EXPERTS):
            @pl.when(padded_ref[e] > 0)
            def _():
                zero_copy(e).start()
        for e in range(N_EXPERTS):
            @pl.when(padded_ref[e] > 0)
            def _():
                zero_copy(e).wait()
        idx_copy(0).start()

    @pl.when(i + 1 < n)
    def _():
        idx_copy(i + 1).start()

    idx_copy(i).wait()
    s = lax.rem(i, IDX_SLOTS)

    def issue(t, c):
        for k in range(TOP_K):
            row_copy(t, idx_smem[s, t * TOP_K + k]).start()
        return c

    lax.fori_loop(0, DISPATCH_TILE, issue, 0)

    def drain(t, c):
        for k in range(TOP_K):
            row_copy(0, 0).wait()
        return c

    lax.fori_loop(0, DISPATCH_TILE, drain, 0)


def _scatter_rows(h, dest, pends, padded, n_slots):
    t = h.shape[0]
    tiles = t // DISPATCH_TILE
    grid_spec = pltpu.PrefetchScalarGridSpec(
        num_scalar_prefetch=2,
        grid=(tiles,),
        in_specs=[
            pl.BlockSpec(memory_space=pl.ANY),
            pl.BlockSpec((DISPATCH_TILE, D_MODEL), lambda i, pe, pd: (i, 0)),
        ],
        out_specs=pl.BlockSpec(memory_space=pl.ANY),
        scratch_shapes=[
            pltpu.SMEM((IDX_SLOTS, DISPATCH_TILE * TOP_K), jnp.int32),
            pltpu.VMEM((MOE_BLOCK_M, D_MODEL), F32),
            pltpu.SemaphoreType.DMA((IDX_SLOTS,)),
            pltpu.SemaphoreType.DMA(()),
            pltpu.SemaphoreType.DMA(()),
        ],
    )
    return pl.pallas_call(
        _dispatch_body,
        grid_spec=grid_spec,
        out_shape=jax.ShapeDtypeStruct((n_slots, D_MODEL), F32),
        compiler_params=_params("arbitrary"),
        name="dispatch",
    )(pends, padded, dest.reshape(tiles, DISPATCH_TILE * TOP_K), h)


COMBINE_TILE = 256


def _combine_body(tile_off, dest_hbm, x1_ref, gate_ref, gf_ref, ys_hbm, y_ref, idx_smem, buf, idx_sem, row_sem):
    i = pl.program_id(0)
    n = pl.num_programs(0)

    def idx_copy(tile):
        s = lax.rem(tile, IDX_SLOTS)
        return pltpu.make_async_copy(dest_hbm.at[pl.ds(tile_off + tile, 1), :], idx_smem.at[pl.ds(s, 1), :],
                                     idx_sem.at[s])

    def row_copy(d, b, k, t):
        return pltpu.make_async_copy(ys_hbm.at[pl.ds(d, 1), :], buf.at[b, k, pl.ds(t, 1), :], row_sem.at[b])

    def issue_rows(tile):
        s = lax.rem(tile, IDX_SLOTS)
        b = lax.rem(tile, 2)

        def body(t, c):
            for k in range(TOP_K):
                row_copy(idx_smem[s, t * TOP_K + k], b, k, t).start()
            return c

        lax.fori_loop(0, COMBINE_TILE, body, 0)

    @pl.when(i == 0)
    def _():
        idx_copy(0).start()
        idx_copy(0).wait()
        issue_rows(0)

        @pl.when(n > 1)
        def _():
            idx_copy(1).start()

    @pl.when(i + 1 < n)
    def _():
        idx_copy(i + 1).wait()
        issue_rows(i + 1)

    @pl.when(i + 2 < n)
    def _():
        idx_copy(i + 2).start()

    b = lax.rem(i, 2)

    def drain(t, c):
        for k in range(TOP_K):
            row_copy(0, b, k, 0).wait()
        return c

    lax.fori_loop(0, COMBINE_TILE, drain, 0)
    gate = gate_ref[...]
    acc = x1_ref[...]
    for k in range(TOP_K):
        acc = acc + buf[b, k] * gate[:, k:k + 1]
    y_ref[...] = acc * lax.rsqrt(jnp.mean(acc * acc, axis=-1, keepdims=True) + RMS_EPS) * gf_ref[...]


def _combine(x1, gate, dest, ys, g_final, row_off, rows):
    t = x1.shape[0]
    tile_off = row_off // COMBINE_TILE
    return pl.pallas_call(
        functools.partial(_combine_body, tile_off),
        grid=(rows // COMBINE_TILE,),
        in_specs=[
            pl.BlockSpec(memory_space=pl.ANY),
            pl.BlockSpec((COMBINE_TILE, D_MODEL), lambda i: (tile_off + i, 0)),
            pl.BlockSpec((COMBINE_TILE, LANES), lambda i: (tile_off + i, 0)),
            pl.BlockSpec((1, D_MODEL), lambda i: (0, 0)),
            pl.BlockSpec(memory_space=pl.ANY),
        ],
        out_specs=pl.BlockSpec((COMBINE_TILE, D_MODEL), lambda i: (i, 0)),
        out_shape=jax.ShapeDtypeStruct((rows, D_MODEL), F32),
        scratch_shapes=[
            pltpu.SMEM((IDX_SLOTS, COMBINE_TILE * TOP_K), jnp.int32),
            pltpu.VMEM((2, TOP_K, COMBINE_TILE, D_MODEL), F32),
            pltpu.SemaphoreType.DMA((IDX_SLOTS,)),
            pltpu.SemaphoreType.DMA((2,)),
        ],
        compiler_params=_params("arbitrary"),
        name="combine",
    )(dest.reshape(t // COMBINE_TILE, COMBINE_TILE * TOP_K), x1, gate, g_final.reshape(1, D_MODEL), ys)


def _lane_row(v):
    return jnp.pad(v.astype(F32), (0, LANES - v.shape[0])).reshape(1, LANES)


def kernel(x_prompt, x_sample, cache_swa_k, cache_swa_v, state_mlstm_C, state_mlstm_n, state_mlstm_m,
           g_mix, w_in, b_if, attn_sinks, mlstm_norm_w, w_attn_proj, w_mlstm_proj, w_out,
           g_ffn, w_router, b_router, w_e1, b_e1, w_e2, b_e2, g_final):
    bp, sp = x_prompt.shape[0], x_prompt.shape[1]
    bs, ls = x_sample.shape[0], x_sample.shape[1]
    tp = bp * sp
    ts = bs * SAMPLE_ROWS
    t_all = tp + ts
    l = 0
    w_in_p = _permute_w_in(w_in[l])
    wa, wm, wo = w_attn_proj[l].astype(BF16), w_mlstm_proj[l].astype(BF16), w_out[l].astype(BF16)
    wr_pad = jnp.pad(w_router[l], ((0, 0), (0, ROUTER_PAD - N_EXPERTS)))
    br_pad = _lane_row(b_router[l])
    w1 = _cast_bf16(w_e1[l], 512)
    w2 = _cast_bf16(w_e2[l], 512)
    b1, b2 = b_e1[l].reshape(N_EXPERTS, 1, 2 * D_FF), b_e2[l].reshape(N_EXPERTS, 1, D_MODEL)
    sinks_row = _lane_row(attn_sinks[l])
    bias_row = _lane_row(b_if[l])
    norm_w_row = mlstm_norm_w[l].reshape(1, M_V_WIDTH)

    xp = x_prompt.reshape(tp, D_MODEL)
    xs = jnp.pad(x_sample, ((0, 0), (0, SAMPLE_ROWS - ls), (0, 0))).reshape(ts, D_MODEL)
    z_all = _inproj(xp, xs, g_mix[l], w_in_p, 1024)

    a_p, k1, v1 = _attn_prompt(z_all, sinks_row, bp, sp)
    a_s, k2, v2 = _attn_sample(z_all, sinks_row, cache_swa_k[l].reshape(bs, WINDOW, KV_WIDTH),
                               cache_swa_v[l].reshape(bs, WINDOW, KV_WIDTH), tp, ls)
    mh_p, c1, n1, m1 = _mlstm_prompt(z_all, bias_row, norm_w_row, bp, sp)
    mh_s, c2, n2, m2 = _mlstm_sample(z_all, bias_row, norm_w_row, state_mlstm_C[l], state_mlstm_n[l],
                                     state_mlstm_m[l], tp, ls)

    x1, h2, logits = _merge(xp, xs, a_p, a_s, mh_p, mh_s, z_all, wa, wm, wo, g_ffn[l], wr_pad, br_pad, 256)

    meta, gate, counts = _route(logits)
    n_blocks = -(-t_all * TOP_K // MOE_BLOCK_M) + N_EXPERTS
    pstarts, pends, padded, block_e, n_used = _slot_tables(counts[0, :N_EXPERTS].astype(jnp.int32), n_blocks)
    dest = pstarts[meta[:, LANE_E:LANE_E + TOP_K]] + meta[:, LANE_RANK:LANE_RANK + TOP_K]
    xs_sorted = _scatter_rows(h2, dest, pends, padded, n_blocks * MOE_BLOCK_M)
    ys = _experts(block_e, n_used, xs_sorted, w1, b1, w2, b2)
    y_prompt = _combine(x1, gate, dest, ys, g_final, 0, tp).reshape(bp, sp, D_MODEL)
    y_sample = _combine(x1, gate, dest, ys, g_final, tp, ts).reshape(bs, SAMPLE_ROWS, D_MODEL)[:, :ls]
    k1 = k1.reshape(bp, WINDOW, N_KV_HEADS, HEAD_DIM)
    v1 = v1.reshape(bp, WINDOW, N_KV_HEADS, HEAD_DIM)
    k2 = k2.reshape(bs, WINDOW, N_KV_HEADS, HEAD_DIM)
    v2 = v2.reshape(bs, WINDOW, N_KV_HEADS, HEAD_DIM)
    m1 = m1[:, :, 0]
    st = lambda t: t[None]
    return (y_prompt, y_sample, st(k1), st(v1), st(c1), st(n1), st(m1), st(k2), st(v2), st(c2), st(n2), st(m2))
```

```python
import functools

import jax
import jax.numpy as jnp
from jax import lax
from jax.experimental import pallas as pl
from jax.experimental.pallas import tpu as pltpu

D_MODEL = 2048
N_HEADS = 16
N_KV_HEADS = 2
HEAD_DIM = 64
Q_PER_KV = N_HEADS // N_KV_HEADS
WINDOW = 128
ATTN_WIDTH = N_HEADS * HEAD_DIM
KV_WIDTH = N_KV_HEADS * HEAD_DIM
M_HEADS = 4
M_QK_DIM = 128
M_V_DIM = 256
M_QK_WIDTH = M_HEADS * M_QK_DIM
M_V_WIDTH = M_HEADS * M_V_DIM
M_CHUNK = 64
N_EXPERTS = 32
TOP_K = 4
D_FF = D_MODEL
SWIGLU_LIMIT = 7.0
SWIGLU_ALPHA = 1.702
RMS_EPS = 1e-5
NEG = -1e30

F32 = jnp.float32
BF16 = jnp.bfloat16

LANES = 128
SUBLANES = 8
VMEM_LIMIT_BYTES = 56 * 1024 * 1024

COL_GZ = 0
COL_AQ = COL_GZ + 2 * D_MODEL
COL_MV = COL_AQ + ATTN_WIDTH
COL_MO = COL_MV + M_V_WIDTH
COL_MQ = COL_MO + M_V_WIDTH
COL_MK = COL_MQ + M_QK_WIDTH
COL_AK = COL_MK + M_QK_WIDTH
COL_IF = COL_AK + 2 * KV_WIDTH
IN_TILE_N = 2176
IN_WIDTH_PAD = -(-(COL_IF + LANES) // IN_TILE_N) * IN_TILE_N

SAMPLE_ROWS = SUBLANES
MOE_BLOCK_M = 512
MOE_FF_CHUNK = 1024
ROUTER_PAD = LANES


def _permute_w_in(w_in):
    splits = (ATTN_WIDTH, KV_WIDTH, KV_WIDTH, M_QK_WIDTH, M_QK_WIDTH, M_V_WIDTH, M_V_WIDTH, M_HEADS, M_HEADS,
              2 * D_MODEL)
    parts, off = [], 0
    for width in splits:
        parts.append(w_in[:, off:off + width])
        off += width
    aq, ak, av, mq, mk, mv, mo, mi, mf, gz = parts
    cols = jnp.concatenate([gz, aq, mv, mo, mq, mk, ak, av, mi, mf], axis=1)
    cols = jnp.pad(cols, ((0, 0), (0, IN_WIDTH_PAD - cols.shape[1])))
    return cols.astype(BF16)


def _params(*semantics):
    return pltpu.CompilerParams(dimension_semantics=semantics, vmem_limit_bytes=VMEM_LIMIT_BYTES)


def _log_sigmoid(x):
    return jnp.minimum(x, 0.0) - jnp.log1p(jnp.exp(-jnp.abs(x)))


def _cast_body(x_ref, o_ref):
    o_ref[...] = x_ref[...].astype(BF16)


def _cast_bf16(w, block_rows):
    e, r, c = w.shape
    return pl.pallas_call(
        _cast_body,
        grid=(e, r // block_rows),
        in_specs=[pl.BlockSpec((None, block_rows, c), lambda i, j: (i, j, 0))],
        out_specs=pl.BlockSpec((None, block_rows, c), lambda i, j: (i, j, 0)),
        out_shape=jax.ShapeDtypeStruct(w.shape, BF16),
        compiler_params=_params("arbitrary", "arbitrary"),
        name="cast_bf16",
    )(w)


def _inproj_body(tiles_a, xa_ref, xb_ref, g_ref, w_ref, z_ref, h_scr):
    def normalise(x_ref):
        x = x_ref[...]
        y = x * lax.rsqrt(jnp.mean(x * x, axis=-1, keepdims=True) + RMS_EPS)
        h_scr[...] = (y * g_ref[...]).astype(BF16)

    first = pl.program_id(1) == 0
    in_a = pl.program_id(0) < tiles_a

    @pl.when(first & in_a)
    def _():
        normalise(xa_ref)

    @pl.when(first & jnp.logical_not(in_a))
    def _():
        normalise(xb_ref)

    z_ref[...] = jnp.dot(h_scr[...], w_ref[...], preferred_element_type=F32)


def _inproj(x_a, x_b, g, w_bf16, tile_m):
    tiles_a, tiles_b = x_a.shape[0] // tile_m, x_b.shape[0] // tile_m
    n = w_bf16.shape[1]
    return pl.pallas_call(
        functools.partial(_inproj_body, tiles_a),
        grid=(tiles_a + tiles_b, n // IN_TILE_N),
        in_specs=[
            pl.BlockSpec((tile_m, D_MODEL), lambda i, j: (jnp.minimum(i, tiles_a - 1), 0)),
            pl.BlockSpec((tile_m, D_MODEL), lambda i, j: (jnp.maximum(i - tiles_a, 0), 0)),
            pl.BlockSpec((1, D_MODEL), lambda i, j: (0, 0)),
            pl.BlockSpec((D_MODEL, IN_TILE_N), lambda i, j: (0, j)),
        ],
        out_specs=pl.BlockSpec((tile_m, IN_TILE_N), lambda i, j: (i, j)),
        out_shape=jax.ShapeDtypeStruct(((tiles_a + tiles_b) * tile_m, n), F32),
        scratch_shapes=[pltpu.VMEM((tile_m, D_MODEL), BF16)],
        compiler_params=_params("arbitrary", "arbitrary"),
        name="inproj",
    )(x_a, x_b, g.reshape(1, D_MODEL), w_bf16)


assert KV_WIDTH == LANES and N_KV_HEADS == 2 and LANES == 2 * HEAD_DIM
Q_TILES = ATTN_WIDTH // LANES
TILES_PER_KV = Q_TILES // N_KV_HEADS
_NT = (((1,), (1,)), ((), ()))


def _kv_placements(x):
    lo = lax.broadcasted_iota(jnp.int32, x.shape, 1) < HEAD_DIM
    swapped = pltpu.roll(x, HEAD_DIM, axis=1)
    return ((jnp.where(lo, x, 0.0).astype(BF16), jnp.where(lo, 0.0, swapped).astype(BF16)),
            (jnp.where(lo, swapped, 0.0).astype(BF16), jnp.where(lo, 0.0, x).astype(BF16)))


def _slab_head(g, p, j):
    return g * Q_PER_KV + 2 * j + p


def _attn_prompt_body(q_ref, kvc_ref, kvp_ref, sink_ref, a_ref, k_ref, v_ref):
    n = pl.program_id(1)
    kc, vc = kvc_ref[:, :KV_WIDTH], kvc_ref[:, KV_WIDTH:]
    k_ref[...] = kc
    v_ref[...] = vc
    kk = _kv_placements(jnp.concatenate([kvp_ref[:, :KV_WIDTH], kc], axis=0))
    vv = _kv_placements(jnp.concatenate([kvp_ref[:, KV_WIDTH:], vc], axis=0))
    q = q_ref[...].astype(BF16)
    row = lax.broadcasted_iota(jnp.int32, (WINDOW, WINDOW), 0)
    col = lax.broadcasted_iota(jnp.int32, (WINDOW, WINDOW), 1)
    mask = jnp.concatenate([(col > row) & (n > 0), col <= row], axis=1)
    slabs, sinks = [], []
    for g in range(N_KV_HEADS):
        qt = jnp.concatenate([q[:, (TILES_PER_KV * g + j) * LANES:(TILES_PER_KV * g + j + 1) * LANES]
                              for j in range(TILES_PER_KV)], axis=0)
        for p in range(2):
            s = lax.dot_general(qt, kk[g][p], _NT, preferred_element_type=F32) * (HEAD_DIM ** -0.5)
            for j in range(TILES_PER_KV):
                h = _slab_head(g, p, j)
                slabs.append(jnp.where(mask, s[j * WINDOW:(j + 1) * WINDOW], NEG))
                sinks.append(jnp.broadcast_to(sink_ref[:, h:h + 1], (WINDOW, 1)))
    s_all = jnp.concatenate(slabs, axis=0)
    sink = jnp.concatenate(sinks, axis=0)
    m = jnp.maximum(jnp.max(s_all, axis=1, keepdims=True), sink)
    p_all = jnp.exp(s_all - m)
    denom = jnp.sum(p_all, axis=1, keepdims=True) + jnp.exp(sink - m)
    pn = (p_all * (1.0 / denom)).astype(BF16)
    half = TILES_PER_KV * WINDOW
    for g in range(N_KV_HEADS):
        base = 2 * half * g
        o = (jnp.dot(pn[base:base + half], vv[g][0], preferred_element_type=F32)
             + jnp.dot(pn[base + half:base + 2 * half], vv[g][1], preferred_element_type=F32))
        for j in range(TILES_PER_KV):
            t = TILES_PER_KV * g + j
            a_ref[:, t * LANES:(t + 1) * LANES] = o[j * WINDOW:(j + 1) * WINDOW].astype(BF16)


def _attn_prompt(z_all, sinks_row, batch, seq):
    nb = seq // WINDOW
    return pl.pallas_call(
        _attn_prompt_body,
        grid=(batch, nb),
        in_specs=[
            pl.BlockSpec((WINDOW, ATTN_WIDTH), lambda b, n: (b * nb + n, COL_AQ // ATTN_WIDTH)),
            pl.BlockSpec((WINDOW, 2 * KV_WIDTH), lambda b, n: (b * nb + n, COL_AK // (2 * KV_WIDTH))),
            pl.BlockSpec((WINDOW, 2 * KV_WIDTH),
                         lambda b, n: (b * nb + jnp.maximum(n - 1, 0), COL_AK // (2 * KV_WIDTH))),
            pl.BlockSpec((1, LANES), lambda b, n: (0, 0)),
        ],
        out_specs=[
            pl.BlockSpec((WINDOW, ATTN_WIDTH), lambda b, n: (b * nb + n, 0)),
            pl.BlockSpec((None, WINDOW, KV_WIDTH), lambda b, n: (b, 0, 0)),
            pl.BlockSpec((None, WINDOW, KV_WIDTH), lambda b, n: (b, 0, 0)),
        ],
        out_shape=[
            jax.ShapeDtypeStruct((batch * seq, ATTN_WIDTH), BF16),
            jax.ShapeDtypeStruct((batch, WINDOW, KV_WIDTH), F32),
            jax.ShapeDtypeStruct((batch, WINDOW, KV_WIDTH), F32),
        ],
        compiler_params=_params("arbitrary", "arbitrary"),
        name="attn_prompt",
    )(z_all, z_all, z_all, sinks_row)


ATTN_SAMPLE_GROUP = 8


def _attn_sample_body(n_valid, q_ref, kv_ref, ck_ref, cv_ref, sink_ref, a_ref, nk_ref, nv_ref, o_scr):
    rs = SAMPLE_ROWS
    slab = TILES_PER_KV * rs
    r8 = lax.broadcasted_iota(jnp.int32, (rs, KV_WIDTH), 0)
    qi_c = jnp.bitwise_and(lax.broadcasted_iota(jnp.int32, (slab, WINDOW), 0), rs - 1)
    mask_c = lax.broadcasted_iota(jnp.int32, (slab, WINDOW), 1) > qi_c
    qi_n = jnp.bitwise_and(lax.broadcasted_iota(jnp.int32, (slab, rs), 0), rs - 1)
    mask_n = lax.broadcasted_iota(jnp.int32, (slab, rs), 1) <= qi_n
    sc_list, sn_list, sink_list, values = [], [], [], []
    for b in range(ATTN_SAMPLE_GROUP):
        rows = slice(b * rs, (b + 1) * rs)
        q = q_ref[rows, :]
        kn, vn = kv_ref[rows, :KV_WIDTH], kv_ref[rows, KV_WIDTH:]
        ck, cv = ck_ref[b], cv_ref[b]
        for src, new, dst in ((ck, kn, nk_ref), (cv, vn, nv_ref)):
            rolled = pltpu.roll(src, WINDOW - n_valid, axis=0)
            dst[b] = rolled
            tail = jnp.where(r8 >= rs - n_valid, pltpu.roll(new, rs - n_valid, axis=0), rolled[WINDOW - rs:, :])
            dst[b, WINDOW - rs:, :] = tail
        kk_c, kk_n = _kv_placements(ck), _kv_placements(kn)
        values.append((_kv_placements(cv), _kv_placements(vn)))
        for g in range(N_KV_HEADS):
            qt = jnp.concatenate([q[:, (TILES_PER_KV * g + j) * LANES:(TILES_PER_KV * g + j + 1) * LANES]
                                  for j in range(TILES_PER_KV)], axis=0).astype(BF16)
            for p in range(2):
                s_c = lax.dot_general(qt, kk_c[g][p], _NT, preferred_element_type=F32) * (HEAD_DIM ** -0.5)
                s_n = lax.dot_general(qt, kk_n[g][p], _NT, preferred_element_type=F32) * (HEAD_DIM ** -0.5)
                sc_list.append(jnp.where(mask_c, s_c, NEG))
                sn_list.append(jnp.where(mask_n, s_n, NEG))
                for j in range(TILES_PER_KV):
                    h = _slab_head(g, p, j)
                    sink_list.append(jnp.broadcast_to(sink_ref[:, h:h + 1], (rs, 1)))
    s_c = jnp.concatenate(sc_list, axis=0)
    s_n = jnp.concatenate(sn_list, axis=0)
    sink = jnp.concatenate(sink_list, axis=0)
    m = jnp.maximum(jnp.maximum(jnp.max(s_c, axis=1, keepdims=True), jnp.max(s_n, axis=1, keepdims=True)), sink)
    p_c = jnp.exp(s_c - m)
    p_n = jnp.exp(s_n - m)
    inv = 1.0 / (jnp.sum(p_c, axis=1, keepdims=True) + jnp.sum(p_n, axis=1, keepdims=True) + jnp.exp(sink - m))
    p_c = (p_c * inv).astype(BF16)
    p_n = (p_n * inv).astype(BF16)
    for b in range(ATTN_SAMPLE_GROUP):
        vv_c, vv_n = values[b]
        for g in range(N_KV_HEADS):
            base = (b * N_KV_HEADS + g) * 2 * slab
            o = (jnp.dot(p_c[base:base + slab], vv_c[g][0], preferred_element_type=F32)
                 + jnp.dot(p_c[base + slab:base + 2 * slab], vv_c[g][1], preferred_element_type=F32)
                 + jnp.dot(p_n[base:base + slab], vv_n[g][0], preferred_element_type=F32)
                 + jnp.dot(p_n[base + slab:base + 2 * slab], vv_n[g][1], preferred_element_type=F32))
            for j in range(TILES_PER_KV):
                t = TILES_PER_KV * g + j
                o_scr[b * rs:(b + 1) * rs, t * LANES:(t + 1) * LANES] = o[j * rs:(j + 1) * rs]
    a_ref[...] = o_scr[...].astype(BF16)


def _attn_sample(z_all, sinks_row, cache_k, cache_v, row_off, n_valid):
    b = cache_k.shape[0]
    gr = ATTN_SAMPLE_GROUP
    rows = gr * SAMPLE_ROWS
    rb = row_off // rows
    return pl.pallas_call(
        functools.partial(_attn_sample_body, n_valid),
        grid=(b // gr,),
        in_specs=[
            pl.BlockSpec((rows, ATTN_WIDTH), lambda i: (rb + i, COL_AQ // ATTN_WIDTH)),
            pl.BlockSpec((rows, 2 * KV_WIDTH), lambda i: (rb + i, COL_AK // (2 * KV_WIDTH))),
            pl.BlockSpec((gr, WINDOW, KV_WIDTH), lambda i: (i, 0, 0)),
            pl.BlockSpec((gr, WINDOW, KV_WIDTH), lambda i: (i, 0, 0)),
            pl.BlockSpec((1, LANES), lambda i: (0, 0)),
        ],
        out_specs=[
            pl.BlockSpec((rows, ATTN_WIDTH), lambda i: (i, 0)),
            pl.BlockSpec((gr, WINDOW, KV_WIDTH), lambda i: (i, 0, 0)),
            pl.BlockSpec((gr, WINDOW, KV_WIDTH), lambda i: (i, 0, 0)),
        ],
        out_shape=[
            jax.ShapeDtypeStruct((b * SAMPLE_ROWS, ATTN_WIDTH), BF16),
            jax.ShapeDtypeStruct(cache_k.shape, F32),
            jax.ShapeDtypeStruct(cache_v.shape, F32),
        ],
        scratch_shapes=[pltpu.VMEM((rows, ATTN_WIDTH), F32)],
        compiler_params=_params("arbitrary"),
        name="attn_sample",
    )(z_all, z_all, cache_k, cache_v, sinks_row)


def _mlstm_gates(if_blk, bias_row, n_valid):
    length = if_blk.shape[0]
    g = if_blk + bias_row
    lane = lax.broadcasted_iota(jnp.int32, g.shape, 1)
    g = jnp.where(lane < M_HEADS, g, _log_sigmoid(g))
    if n_valid < length:
        row = lax.broadcasted_iota(jnp.int32, g.shape, 0)
        g = jnp.where(row < n_valid, g, jnp.where(lane < M_HEADS, NEG, 0.0))
    return g


def _mlstm_chunk(q, k, v, gates, gates_t, h, c, n, m):
    length = q.shape[0]
    i_col = gates[:, h:h + 1]
    f_col = gates[:, M_HEADS + h:M_HEADS + h + 1]
    i_row = gates_t[h:h + 1, :]
    f_row = gates_t[M_HEADS + h:M_HEADS + h + 1, :]
    t_idx = lax.broadcasted_iota(jnp.int32, (length, length), 0)
    s_idx = lax.broadcasted_iota(jnp.int32, (length, length), 1)
    causal = s_idx <= t_idx
    bcum_col = jnp.sum(jnp.where(causal, f_row, 0.0), axis=1, keepdims=True)
    bcum_row = jnp.sum(jnp.where(t_idx <= s_idx, f_col, 0.0), axis=0, keepdims=True)
    dlog = jnp.where(causal, bcum_col - bcum_row + i_row, NEG)
    inter = bcum_col + m
    mt = jnp.maximum(inter, jnp.max(dlog, axis=1, keepdims=True))
    w_inter = jnp.exp(inter - mt)
    qb = q.astype(BF16)
    ks = k * (M_QK_DIM ** -0.5)
    vb = v.astype(BF16)
    qk = lax.dot_general(qb, ks.astype(BF16), (((1,), (1,)), ((), ())), preferred_element_type=F32)
    qk = qk * jnp.exp(dlog - mt)
    num = (w_inter * jnp.dot(qb, c.astype(BF16), preferred_element_type=F32)
           + jnp.dot(qk.astype(BF16), vb, preferred_element_type=F32))
    den = w_inter * jnp.sum(q * n, axis=1, keepdims=True) + jnp.sum(qk, axis=1, keepdims=True)
    h_t = num / jnp.maximum(jnp.abs(den), jnp.exp(-mt))
    m_new = mt[length - 1:length, :]
    w_state = jnp.exp(bcum_col[length - 1:length, :] - bcum_col + i_col - m_new)
    decay = jnp.exp(inter[length - 1:length, :] - m_new)
    kw = ks * w_state
    c_new = decay * c + jnp.dot(kw.T.astype(BF16), vb, preferred_element_type=F32)
    n_new = decay * n + jnp.sum(kw, axis=0, keepdims=True)
    return h_t, c_new, n_new, m_new


def _mlstm_head_out(h_t, norm_w, o_gate):
    hs = h_t * lax.rsqrt(jnp.mean(h_t * h_t, axis=-1, keepdims=True) + RMS_EPS)
    return hs * norm_w * jax.nn.sigmoid(o_gate)


MLSTM_SEG = 512


def _mlstm_prompt_body(q_ref, k_ref, v_ref, o_ref, if_ref, bias_ref, nw_ref, mh_ref, c_out, n_out, m_out,
                       c_scr, n_scr, m_scr):
    seg = pl.program_id(1)

    @pl.when(seg == 0)
    def _():
        c_scr[...] = jnp.zeros_like(c_scr)
        n_scr[...] = jnp.zeros_like(n_scr)
        m_scr[...] = jnp.full_like(m_scr, NEG)

    def chunk_body(ci, carry):
        r = pl.multiple_of(ci * M_CHUNK, M_CHUNK)
        rows = pl.ds(r, M_CHUNK)
        gates = _mlstm_gates(if_ref[rows, :], bias_ref[...], M_CHUNK)
        gates_t = gates.T
        for h in range(M_HEADS):
            qs = slice(h * M_QK_DIM, (h + 1) * M_QK_DIM)
            vs = slice(h * M_V_DIM, (h + 1) * M_V_DIM)
            h_t, c_new, n_new, m_new = _mlstm_chunk(
                q_ref[rows, qs], k_ref[rows, qs], v_ref[rows, vs], gates, gates_t, h,
                c_scr[h], n_scr[h:h + 1, :], m_scr[h:h + 1, 0:1])
            c_scr[h] = c_new
            n_scr[h:h + 1, :] = n_new
            m_scr[h:h + 1, :] = jnp.broadcast_to(m_new, (1, LANES))
            mh_ref[rows, vs] = _mlstm_head_out(h_t, nw_ref[:, vs], o_ref[rows, vs]).astype(BF16)
        return carry

    lax.fori_loop(0, MLSTM_SEG // M_CHUNK, chunk_body, 0)

    @pl.when(seg == pl.num_programs(1) - 1)
    def _():
        c_out[...] = c_scr[...]
        n_out[...] = n_scr[0:M_HEADS, :]
        m_out[...] = m_scr[0:M_HEADS, :]


def _mlstm_prompt(z_all, bias_row, norm_w_row, batch, seq):
    ns = seq // MLSTM_SEG
    row = lambda b, s: b * ns + s
    return pl.pallas_call(
        _mlstm_prompt_body,
        grid=(batch, ns),
        in_specs=[
            pl.BlockSpec((MLSTM_SEG, M_QK_WIDTH), lambda b, s: (row(b, s), COL_MQ // M_QK_WIDTH)),
            pl.BlockSpec((MLSTM_SEG, M_QK_WIDTH), lambda b, s: (row(b, s), COL_MK // M_QK_WIDTH)),
            pl.BlockSpec((MLSTM_SEG, M_V_WIDTH), lambda b, s: (row(b, s), COL_MV // M_V_WIDTH)),
            pl.BlockSpec((MLSTM_SEG, M_V_WIDTH), lambda b, s: (row(b, s), COL_MO // M_V_WIDTH)),
            pl.BlockSpec((MLSTM_SEG, LANES), lambda b, s: (row(b, s), COL_IF // LANES)),
            pl.BlockSpec((1, LANES), lambda b, s: (0, 0)),
            pl.BlockSpec((1, M_V_WIDTH), lambda b, s: (0, 0)),
        ],
        out_specs=[
            pl.BlockSpec((MLSTM_SEG, M_V_WIDTH), lambda b, s: (row(b, s), 0)),
            pl.BlockSpec((None, M_HEADS, M_QK_DIM, M_V_DIM), lambda b, s: (b, 0, 0, 0)),
            pl.BlockSpec((None, M_HEADS, M_QK_DIM), lambda b, s: (b, 0, 0)),
            pl.BlockSpec((None, M_HEADS, LANES), lambda b, s: (b, 0, 0)),
        ],
        out_shape=[
            jax.ShapeDtypeStruct((batch * seq, M_V_WIDTH), BF16),
            jax.ShapeDtypeStruct((batch, M_HEADS, M_QK_DIM, M_V_DIM), F32),
            jax.ShapeDtypeStruct((batch, M_HEADS, M_QK_DIM), F32),
            jax.ShapeDtypeStruct((batch, M_HEADS, LANES), F32),
        ],
        scratch_shapes=[
            pltpu.VMEM((M_HEADS, M_QK_DIM, M_V_DIM), F32),
            pltpu.VMEM((SUBLANES, M_QK_DIM), F32),
            pltpu.VMEM((SUBLANES, LANES), F32),
        ],
        compiler_params=_params("arbitrary", "arbitrary"),
        name="mlstm_prompt",
    )(z_all, z_all, z_all, z_all, z_all, bias_row, norm_w_row)


MLSTM_SAMPLE_GROUP = 2


def _mlstm_sample_body(n_valid, q_ref, k_ref, v_ref, o_ref, if_ref, bias_ref, nw_ref, c0_ref, n0_ref, m0_ref,
                       mh_ref, c_out, n_out, m_out, h_scr):
    for g in range(MLSTM_SAMPLE_GROUP):
        rows = slice(g * SAMPLE_ROWS, (g + 1) * SAMPLE_ROWS)
        gates = _mlstm_gates(if_ref[rows, :], bias_ref[...], n_valid)
        gates_t = jnp.transpose(jnp.concatenate([gates] * (LANES // SAMPLE_ROWS), axis=0))[:, :SAMPLE_ROWS]
        for h in range(M_HEADS):
            qs = slice(h * M_QK_DIM, (h + 1) * M_QK_DIM)
            vs = slice(h * M_V_DIM, (h + 1) * M_V_DIM)
            h_t, c_new, n_new, m_new = _mlstm_chunk(
                q_ref[rows, qs], k_ref[rows, qs], v_ref[rows, vs], gates, gates_t, h,
                c0_ref[g, h], n0_ref[g, h:h + 1, :], m0_ref[g:g + 1, h:h + 1])
            c_out[g, h] = c_new
            n_out[g, h:h + 1, :] = n_new
            m_out[g:g + 1, h:h + 1] = m_new
            h_scr[rows, vs] = _mlstm_head_out(h_t, nw_ref[:, vs], o_ref[rows, vs])
    mh_ref[...] = h_scr[...].astype(BF16)


def _mlstm_sample(z_all, bias_row, norm_w_row, c0, n0, m0, row_off, n_valid):
    b = c0.shape[0]
    gr = MLSTM_SAMPLE_GROUP
    rows = gr * SAMPLE_ROWS
    rb = row_off // rows
    m0g = m0.reshape(b // gr, gr, M_HEADS)
    outs = pl.pallas_call(
        functools.partial(_mlstm_sample_body, n_valid),
        grid=(b // gr,),
        in_specs=[
            pl.BlockSpec((rows, M_QK_WIDTH), lambda i: (rb + i, COL_MQ // M_QK_WIDTH)),
            pl.BlockSpec((rows, M_QK_WIDTH), lambda i: (rb + i, COL_MK // M_QK_WIDTH)),
            pl.BlockSpec((rows, M_V_WIDTH), lambda i: (rb + i, COL_MV // M_V_WIDTH)),
            pl.BlockSpec((rows, M_V_WIDTH), lambda i: (rb + i, COL_MO // M_V_WIDTH)),
            pl.BlockSpec((rows, LANES), lambda i: (rb + i, COL_IF // LANES)),
            pl.BlockSpec((1, LANES), lambda i: (0, 0)),
            pl.BlockSpec((1, M_V_WIDTH), lambda i: (0, 0)),
            pl.BlockSpec((gr, M_HEADS, M_QK_DIM, M_V_DIM), lambda i: (i, 0, 0, 0)),
            pl.BlockSpec((gr, M_HEADS, M_QK_DIM), lambda i: (i, 0, 0)),
            pl.BlockSpec((None, gr, M_HEADS), lambda i: (i, 0, 0)),
        ],
        out_specs=[
            pl.BlockSpec((rows, M_V_WIDTH), lambda i: (i, 0)),
            pl.BlockSpec((gr, M_HEADS, M_QK_DIM, M_V_DIM), lambda i: (i, 0, 0, 0)),
            pl.BlockSpec((gr, M_HEADS, M_QK_DIM), lambda i: (i, 0, 0)),
            pl.BlockSpec((None, gr, M_HEADS), lambda i: (i, 0, 0)),
        ],
        out_shape=[
            jax.ShapeDtypeStruct((b * SAMPLE_ROWS, M_V_WIDTH), BF16),
            jax.ShapeDtypeStruct(c0.shape, F32),
            jax.ShapeDtypeStruct(n0.shape, F32),
            jax.ShapeDtypeStruct(m0g.shape, F32),
        ],
        scratch_shapes=[pltpu.VMEM((rows, M_V_WIDTH), F32)],
        compiler_params=_params("arbitrary"),
        name="mlstm_sample",
    )(z_all, z_all, z_all, z_all, z_all, bias_row, norm_w_row, c0, n0, m0g)
    mh, c, n, m = outs
    return mh, c, n, m.reshape(b, M_HEADS)


def _merge_body(tiles_a, xa_ref, xb_ref, aa_ref, ab_ref, ma_ref, mb_ref, gza_ref, gzm_ref, wa_ref, wm_ref, wo_ref,
                gf_ref, wr_ref, br_ref, x1_ref, h2_ref, lg_ref):
    in_a = pl.program_id(0) < tiles_a
    x = jnp.where(in_a, xa_ref[...], xb_ref[...])
    a = jnp.where(in_a, aa_ref[...], ab_ref[...])
    mh = jnp.where(in_a, ma_ref[...], mb_ref[...])
    pa = jnp.dot(a, wa_ref[...], preferred_element_type=F32)
    pm = jnp.dot(mh, wm_ref[...], preferred_element_type=F32)
    merged = jax.nn.sigmoid(gza_ref[...]) * pa + jax.nn.sigmoid(gzm_ref[...]) * pm
    x1 = x + jnp.dot(merged.astype(BF16), wo_ref[...], preferred_element_type=F32)
    x1_ref[...] = x1
    h2 = x1 * lax.rsqrt(jnp.mean(x1 * x1, axis=-1, keepdims=True) + RMS_EPS) * gf_ref[...]
    h2_ref[...] = h2
    hi = h2.astype(BF16)
    lo = (h2 - hi.astype(F32)).astype(BF16)
    both = jnp.dot(hi, wr_ref[...], preferred_element_type=F32)
    lg_ref[...] = (both[:, :ROUTER_PAD] + both[:, ROUTER_PAD:]
                   + jnp.dot(lo, wr_ref[:, :ROUTER_PAD], preferred_element_type=F32) + br_ref[...])


def _merge(x_a, x_b, a_a, a_b, mh_a, mh_b, z_all, wa, wm, wo, g_ffn, wr_pad, br_pad, tile_m):
    tiles_a, tiles_b = x_a.shape[0] // tile_m, x_b.shape[0] // tile_m
    t_all = z_all.shape[0]
    const = lambda i: (0, 0)
    from_a = lambda i: (jnp.minimum(i, tiles_a - 1), 0)
    from_b = lambda i: (jnp.maximum(i - tiles_a, 0), 0)
    return pl.pallas_call(
        functools.partial(_merge_body, tiles_a),
        grid=(tiles_a + tiles_b,),
        in_specs=[
            pl.BlockSpec((tile_m, D_MODEL), from_a),
            pl.BlockSpec((tile_m, D_MODEL), from_b),
            pl.BlockSpec((tile_m, ATTN_WIDTH), from_a),
            pl.BlockSpec((tile_m, ATTN_WIDTH), from_b),
            pl.BlockSpec((tile_m, M_V_WIDTH), from_a),
            pl.BlockSpec((tile_m, M_V_WIDTH), from_b),
            pl.BlockSpec((tile_m, D_MODEL), lambda i: (i, COL_GZ // D_MODEL)),
            pl.BlockSpec((tile_m, D_MODEL), lambda i: (i, COL_GZ // D_MODEL + 1)),
            pl.BlockSpec((ATTN_WIDTH, D_MODEL), const, pipeline_mode=pl.Buffered(1)),
            pl.BlockSpec((M_V_WIDTH, D_MODEL), const, pipeline_mode=pl.Buffered(1)),
            pl.BlockSpec((D_MODEL, D_MODEL), const, pipeline_mode=pl.Buffered(1)),
            pl.BlockSpec((1, D_MODEL), const),
            pl.BlockSpec((D_MODEL, 2 * ROUTER_PAD), const),
            pl.BlockSpec((1, ROUTER_PAD), const),
        ],
        out_specs=[
            pl.BlockSpec((tile_m, D_MODEL), lambda i: (i, 0)),
            pl.BlockSpec((tile_m, D_MODEL), lambda i: (i, 0)),
            pl.BlockSpec((tile_m, ROUTER_PAD), lambda i: (i, 0)),
        ],
        out_shape=[
            jax.ShapeDtypeStruct((t_all, D_MODEL), F32),
            jax.ShapeDtypeStruct((t_all, D_MODEL), F32),
            jax.ShapeDtypeStruct((t_all, ROUTER_PAD), F32),
        ],
        compiler_params=_params("arbitrary"),
        name="merge",
    )(x_a, x_b, a_a, a_b, mh_a, mh_b, z_all, z_all, wa, wm, wo, g_ffn.reshape(1, D_MODEL), wr_pad, br_pad)


def _expert_body(be_ref, nb_ref, xs_ref, w1g_ref, w1u_ref, b1g_ref, b1u_ref, w2_ref, b2_ref, y_ref):
    i = pl.program_id(0)
    c = pl.program_id(1)
    used = i < nb_ref[0]

    @pl.when(used)
    def _():
        x = xs_ref[...].astype(BF16)
        zg = jnp.dot(x, w1g_ref[...], preferred_element_type=F32) + b1g_ref[...]
        zu = jnp.dot(x, w1u_ref[...], preferred_element_type=F32) + b1u_ref[...]
        gate = jnp.minimum(zg, SWIGLU_LIMIT)
        up = jnp.clip(zu, -SWIGLU_LIMIT, SWIGLU_LIMIT)
        act = gate * jax.nn.sigmoid(SWIGLU_ALPHA * gate) * (up + 1.0)
        y = jnp.dot(act.astype(BF16), w2_ref[...], preferred_element_type=F32)

        @pl.when(c == 0)
        def _():
            y_ref[...] = y + b2_ref[...]

        @pl.when(c != 0)
        def _():
            y_ref[...] += y

    @pl.when(jnp.logical_not(used) & (c == 0))
    def _():
        y_ref[...] = jnp.zeros_like(y_ref)


def _experts(block_e, n_used, xs, w1, b1, w2, b2):
    n_slots = xs.shape[0]
    n_blocks = n_slots // MOE_BLOCK_M
    n_chunks = D_FF // MOE_FF_CHUNK

    def chunk(i, c, nb):
        block = jnp.minimum(i, nb[0] - 1)
        step = jnp.where(i < nb[0], c, n_chunks - 1)
        return jnp.where(block % 2 == 0, step, n_chunks - 1 - step)

    grid_spec = pltpu.PrefetchScalarGridSpec(
        num_scalar_prefetch=2,
        grid=(n_blocks, n_chunks),
        in_specs=[
            pl.BlockSpec((MOE_BLOCK_M, D_MODEL), lambda i, c, be, nb: (i, 0)),
            pl.BlockSpec((None, D_MODEL, MOE_FF_CHUNK), lambda i, c, be, nb: (be[i], 0, chunk(i, c, nb))),
            pl.BlockSpec((None, D_MODEL, MOE_FF_CHUNK),
                         lambda i, c, be, nb: (be[i], 0, n_chunks + chunk(i, c, nb))),
            pl.BlockSpec((None, 1, MOE_FF_CHUNK), lambda i, c, be, nb: (be[i], 0, chunk(i, c, nb))),
            pl.BlockSpec((None, 1, MOE_FF_CHUNK), lambda i, c, be, nb: (be[i], 0, n_chunks + chunk(i, c, nb))),
            pl.BlockSpec((None, MOE_FF_CHUNK, D_MODEL), lambda i, c, be, nb: (be[i], chunk(i, c, nb), 0)),
            pl.BlockSpec((None, 1, D_MODEL), lambda i, c, be, nb: (be[i], 0, 0)),
        ],
        out_specs=pl.BlockSpec((MOE_BLOCK_M, D_MODEL), lambda i, c, be, nb: (i, 0)),
    )
    return pl.pallas_call(
        _expert_body,
        grid_spec=grid_spec,
        out_shape=jax.ShapeDtypeStruct((n_slots, D_MODEL), F32),
        compiler_params=_params("arbitrary", "arbitrary"),
        name="experts",
    )(block_e, n_used, xs, w1, w1, b1, b1, w2, b2)


ROUTE_TILE = 1024
LANE_E = 0
LANE_RANK = TOP_K


def _route_body(lg_ref, meta_ref, gate_ref, cnt_ref, tri_scr, carry_scr):
    @pl.when(pl.program_id(0) == 0)
    def _():
        r = lax.broadcasted_iota(jnp.int32, (ROUTE_TILE, ROUTE_TILE), 0)
        c = lax.broadcasted_iota(jnp.int32, (ROUTE_TILE, ROUTE_TILE), 1)
        tri_scr[...] = jnp.where(c < r, 1.0, 0.0).astype(BF16)
        carry_scr[...] = jnp.zeros_like(carry_scr)

    lane_i = lax.broadcasted_iota(jnp.int32, (ROUTE_TILE, LANES), 1)
    lane = lane_i.astype(F32)
    x = jnp.where(lane_i < N_EXPERTS, lg_ref[...], -jnp.inf)
    vals, idxs, sels = [], [], []
    for _ in range(TOP_K):
        mx = jnp.max(x, axis=1, keepdims=True)
        idx = jnp.min(jnp.where(x == mx, lane, float(LANES)), axis=1, keepdims=True)
        sel = lane == idx
        x = jnp.where(sel, -jnp.inf, x)
        vals.append(mx)
        idxs.append(idx)
        sels.append(sel)
    weights = [jnp.exp(v - vals[0]) for v in vals]
    inv = 1.0 / (weights[0] + weights[1] + weights[2] + weights[3])
    onehot = sum(jnp.where(s, 1.0, 0.0) for s in sels)
    ranks = jnp.dot(tri_scr[...], onehot.astype(BF16), preferred_element_type=F32) + carry_scr[0:1, :]
    meta = jnp.zeros((ROUTE_TILE, LANES), F32)
    gate = jnp.zeros((ROUTE_TILE, LANES), F32)
    for k in range(TOP_K):
        rank_k = jnp.sum(jnp.where(sels[k], ranks, 0.0), axis=1, keepdims=True)
        meta = jnp.where(lane_i == LANE_E + k, idxs[k], meta)
        meta = jnp.where(lane_i == LANE_RANK + k, rank_k, meta)
        gate = jnp.where(lane_i == k, weights[k] * inv, gate)
    meta_ref[...] = meta.astype(jnp.int32)
    gate_ref[...] = gate
    carry_scr[0:1, :] = carry_scr[0:1, :] + jnp.sum(onehot, axis=0, keepdims=True)
    cnt_ref[...] = carry_scr[...]


def _route(logits):
    t = logits.shape[0]
    return pl.pallas_call(
        _route_body,
        grid=(t // ROUTE_TILE,),
        in_specs=[pl.BlockSpec((ROUTE_TILE, LANES), lambda i: (i, 0))],
        out_specs=[
            pl.BlockSpec((ROUTE_TILE, LANES), lambda i: (i, 0)),
            pl.BlockSpec((ROUTE_TILE, LANES), lambda i: (i, 0)),
            pl.BlockSpec((SUBLANES, LANES), lambda i: (0, 0)),
        ],
        out_shape=[
            jax.ShapeDtypeStruct((t, LANES), jnp.int32),
            jax.ShapeDtypeStruct((t, LANES), F32),
            jax.ShapeDtypeStruct((SUBLANES, LANES), F32),
        ],
        scratch_shapes=[pltpu.VMEM((ROUTE_TILE, ROUTE_TILE), BF16), pltpu.VMEM((SUBLANES, LANES), F32)],
        compiler_params=_params("arbitrary"),
        name="route",
    )(logits)


def _slot_tables(counts, n_blocks):
    padded = (counts + MOE_BLOCK_M - 1) // MOE_BLOCK_M * MOE_BLOCK_M
    pends = jnp.cumsum(padded)
    first_slot = jnp.arange(n_blocks, dtype=pends.dtype) * MOE_BLOCK_M
    block_e = jnp.minimum(jnp.sum(pends[None, :] <= first_slot[:, None], axis=1), N_EXPERTS - 1).astype(jnp.int32)
    n_used = (pends[-1] // MOE_BLOCK_M).astype(jnp.int32).reshape(1)
    return (pends - padded).astype(jnp.int32), pends.astype(jnp.int32), padded.astype(jnp.int32), block_e, n_used


DISPATCH_TILE = 512
IDX_SLOTS = 3


def _dispatch_body(pend_ref, padded_ref, dest_hbm, h_ref, xs_hbm, idx_smem, zero_scr, idx_sem, row_sem, zero_sem):
    i = pl.program_id(0)
    n = pl.num_programs(0)

    def idx_copy(tile):
        s = lax.rem(tile, IDX_SLOTS)
        return pltpu.make_async_copy(dest_hbm.at[pl.ds(tile, 1), :], idx_smem.at[pl.ds(s, 1), :], idx_sem.at[s])

    def zero_copy(e):
        start = pl.multiple_of(pend_ref[e] - MOE_BLOCK_M, MOE_BLOCK_M)
        return pltpu.make_async_copy(zero_scr, xs_hbm.at[pl.ds(start, MOE_BLOCK_M), :], zero_sem)

    def row_copy(t, d):
        return pltpu.make_async_copy(h_ref.at[pl.ds(t, 1), :], xs_hbm.at[pl.ds(d, 1), :], row_sem)

    @pl.when(i == 0)
    def _():
        zero_scr[...] = jnp.zeros_like(zero_scr)
        for e in range(N_EXPERTS):
            @pl.when(padded_ref[e] > 0)
            def _():
                zero_copy(e).start()
        for e in range(N_EXPERTS):
            @pl.when(padded_ref[e] > 0)
            def _():
                zero_copy(e).wait()
        idx_copy(0).start()

    @pl.when(i + 1 < n)
    def _():
        idx_copy(i + 1).start()

    idx_copy(i).wait()
    s = lax.rem(i, IDX_SLOTS)

    def issue(t, c):
        for k in range(TOP_K):
            row_copy(t, idx_smem[s, t * TOP_K + k]).start(priority=k % 2)
        return c

    lax.fori_loop(0, DISPATCH_TILE, issue, 0, unroll=8)
    for _ in range(TOP_K):
        pltpu.make_async_copy(h_ref, h_ref, row_sem).wait()


def _scatter_rows(h, dest, pends, padded, n_slots):
    t = h.shape[0]
    tiles = t // DISPATCH_TILE
    grid_spec = pltpu.PrefetchScalarGridSpec(
        num_scalar_prefetch=2,
        grid=(tiles,),
        in_specs=[
            pl.BlockSpec(memory_space=pl.ANY),
            pl.BlockSpec((DISPATCH_TILE, D_MODEL), lambda i, pe, pd: (i, 0)),
        ],
        out_specs=pl.BlockSpec(memory_space=pl.ANY),
        scratch_shapes=[
            pltpu.SMEM((IDX_SLOTS, DISPATCH_TILE * TOP_K), jnp.int32),
            pltpu.VMEM((MOE_BLOCK_M, D_MODEL), F32),
            pltpu.SemaphoreType.DMA((IDX_SLOTS,)),
            pltpu.SemaphoreType.DMA(()),
            pltpu.SemaphoreType.DMA(()),
        ],
    )
    return pl.pallas_call(
        _dispatch_body,
        grid_spec=grid_spec,
        out_shape=jax.ShapeDtypeStruct((n_slots, D_MODEL), F32),
        compiler_params=_params("arbitrary"),
        name="dispatch",
    )(pends, padded, dest.reshape(tiles, DISPATCH_TILE * TOP_K), h)


COMBINE_TILE = 256


def _combine_body(tile_off, dest_hbm, x1_ref, gate_ref, gf_ref, ys_hbm, y_ref, idx_smem, buf, idx_sem, row_sem):
    i = pl.program_id(0)
    n = pl.num_programs(0)

    def idx_copy(tile):
        s = lax.rem(tile, IDX_SLOTS)
        return pltpu.make_async_copy(dest_hbm.at[pl.ds(tile_off + tile, 1), :], idx_smem.at[pl.ds(s, 1), :],
                                     idx_sem.at[s])

    def row_copy(d, b, k, t):
        return pltpu.make_async_copy(ys_hbm.at[pl.ds(d, 1), :], buf.at[b, k, pl.ds(t, 1), :], row_sem.at[b])

    def issue_rows(tile):
        s = lax.rem(tile, IDX_SLOTS)
        b = lax.rem(tile, 2)

        def body(t, c):
            for k in range(TOP_K):
                row_copy(idx_smem[s, t * TOP_K + k], b, k, t).start(priority=k % 2)
            return c

        lax.fori_loop(0, COMBINE_TILE, body, 0, unroll=8)

    @pl.when(i == 0)
    def _():
        idx_copy(0).start()
        idx_copy(0).wait()
        issue_rows(0)

        @pl.when(n > 1)
        def _():
            idx_copy(1).start()

    @pl.when(i + 1 < n)
    def _():
        idx_copy(i + 1).wait()
        issue_rows(i + 1)

    @pl.when(i + 2 < n)
    def _():
        idx_copy(i + 2).start()

    b = lax.rem(i, 2)

    pltpu.make_async_copy(buf.at[b], buf.at[b], row_sem.at[b]).wait()
    gate = gate_ref[...]
    acc = x1_ref[...]
    for k in range(TOP_K):
        acc = acc + buf[b, k] * gate[:, k:k + 1]
    y_ref[...] = acc * lax.rsqrt(jnp.mean(acc * acc, axis=-1, keepdims=True) + RMS_EPS) * gf_ref[...]


def _combine(x1, gate, dest, ys, g_final, row_off, rows):
    t = x1.shape[0]
    tile_off = row_off // COMBINE_TILE
    return pl.pallas_call(
        functools.partial(_combine_body, tile_off),
        grid=(rows // COMBINE_TILE,),
        in_specs=[
            pl.BlockSpec(memory_space=pl.ANY),
            pl.BlockSpec((COMBINE_TILE, D_MODEL), lambda i: (tile_off + i, 0)),
            pl.BlockSpec((COMBINE_TILE, LANES), lambda i: (tile_off + i, 0)),
            pl.BlockSpec((1, D_MODEL), lambda i: (0, 0)),
            pl.BlockSpec(memory_space=pl.ANY),
        ],
        out_specs=pl.BlockSpec((COMBINE_TILE, D_MODEL), lambda i: (i, 0)),
        out_shape=jax.ShapeDtypeStruct((rows, D_MODEL), F32),
        scratch_shapes=[
            pltpu.SMEM((IDX_SLOTS, COMBINE_TILE * TOP_K), jnp.int32),
            pltpu.VMEM((2, TOP_K, COMBINE_TILE, D_MODEL), F32),
            pltpu.SemaphoreType.DMA((IDX_SLOTS,)),
            pltpu.SemaphoreType.DMA((2,)),
        ],
        compiler_params=_params("arbitrary"),
        name="combine",
    )(dest.reshape(t // COMBINE_TILE, COMBINE_TILE * TOP_K), x1, gate, g_final.reshape(1, D_MODEL), ys)


def _lane_row(v):
    return jnp.pad(v.astype(F32), (0, LANES - v.shape[0])).reshape(1, LANES)


def kernel(x_prompt, x_sample, cache_swa_k, cache_swa_v, state_mlstm_C, state_mlstm_n, state_mlstm_m,
           g_mix, w_in, b_if, attn_sinks, mlstm_norm_w, w_attn_proj, w_mlstm_proj, w_out,
           g_ffn, w_router, b_router, w_e1, b_e1, w_e2, b_e2, g_final):
    bp, sp = x_prompt.shape[0], x_prompt.shape[1]
    bs, ls = x_sample.shape[0], x_sample.shape[1]
    tp = bp * sp
    ts = bs * SAMPLE_ROWS
    t_all = tp + ts
    l = 0
    w_in_p = _permute_w_in(w_in[l])
    wa, wm, wo = w_attn_proj[l].astype(BF16), w_mlstm_proj[l].astype(BF16), w_out[l].astype(BF16)
    wr_f32 = jnp.pad(w_router[l], ((0, 0), (0, ROUTER_PAD - N_EXPERTS)))
    wr_hi = wr_f32.astype(BF16)
    wr_pad = jnp.concatenate([wr_hi, (wr_f32 - wr_hi.astype(F32)).astype(BF16)], axis=1)
    br_pad = _lane_row(b_router[l])
    w1 = _cast_bf16(w_e1[l], 512)
    w2 = _cast_bf16(w_e2[l], 512)
    b1, b2 = b_e1[l].reshape(N_EXPERTS, 1, 2 * D_FF), b_e2[l].reshape(N_EXPERTS, 1, D_MODEL)
    sinks_row = _lane_row(attn_sinks[l])
    bias_row = _lane_row(b_if[l])
    norm_w_row = mlstm_norm_w[l].reshape(1, M_V_WIDTH)

    xp = x_prompt.reshape(tp, D_MODEL)
    xs = jnp.pad(x_sample, ((0, 0), (0, SAMPLE_ROWS - ls), (0, 0))).reshape(ts, D_MODEL)
    z_all = _inproj(xp, xs, g_mix[l], w_in_p, 512)

    a_p, k1, v1 = _attn_prompt(z_all, sinks_row, bp, sp)
    a_s, k2, v2 = _attn_sample(z_all, sinks_row, cache_swa_k[l].reshape(bs, WINDOW, KV_WIDTH),
                               cache_swa_v[l].reshape(bs, WINDOW, KV_WIDTH), tp, ls)
    mh_p, c1, n1, m1 = _mlstm_prompt(z_all, bias_row, norm_w_row, bp, sp)
    mh_s, c2, n2, m2 = _mlstm_sample(z_all, bias_row, norm_w_row, state_mlstm_C[l], state_mlstm_n[l],
                                     state_mlstm_m[l], tp, ls)

    x1, h2, logits = _merge(xp, xs, a_p, a_s, mh_p, mh_s, z_all, wa, wm, wo, g_ffn[l], wr_pad, br_pad, 256)

    meta, gate, counts = _route(logits)
    n_blocks = -(-t_all * TOP_K // MOE_BLOCK_M) + N_EXPERTS
    pstarts, pends, padded, block_e, n_used = _slot_tables(counts[0, :N_EXPERTS].astype(jnp.int32), n_blocks)
    dest = pstarts[meta[:, LANE_E:LANE_E + TOP_K]] + meta[:, LANE_RANK:LANE_RANK + TOP_K]
    xs_sorted = _scatter_rows(h2, dest, pends, padded, n_blocks * MOE_BLOCK_M)
    ys = _experts(block_e, n_used, xs_sorted, w1, b1, w2, b2)
    y_prompt = _combine(x1, gate, dest, ys, g_final, 0, tp).reshape(bp, sp, D_MODEL)
    y_sample = _combine(x1, gate, dest, ys, g_final, tp, ts).reshape(bs, SAMPLE_ROWS, D_MODEL)[:, :ls]
    k1 = k1.reshape(bp, WINDOW, N_KV_HEADS, HEAD_DIM)
    v1 = v1.reshape(bp, WINDOW, N_KV_HEADS, HEAD_DIM)
    k2 = k2.reshape(bs, WINDOW, N_KV_HEADS, HEAD_DIM)
    v2 = v2.reshape(bs, WINDOW, N_KV_HEADS, HEAD_DIM)
    m1 = m1[:, :, 0]
    st = lambda t: t[None]
    return (y_prompt, y_sample, st(k1), st(v1), st(c1), st(n1), st(m1), st(k2), st(v2), st(c2), st(n2), st(m2))
```

```python
import functools

import jax
import jax.numpy as jnp
from jax import lax
from jax.experimental import pallas as pl
from jax.experimental.pallas import tpu as pltpu

D_MODEL = 2048
N_HEADS = 16
N_KV_HEADS = 2
HEAD_DIM = 64
Q_PER_KV = N_HEADS // N_KV_HEADS
WINDOW = 128
ATTN_WIDTH = N_HEADS * HEAD_DIM
KV_WIDTH = N_KV_HEADS * HEAD_DIM
M_HEADS = 4
M_QK_DIM = 128
M_V_DIM = 256
M_QK_WIDTH = M_HEADS * M_QK_DIM
M_V_WIDTH = M_HEADS * M_V_DIM
M_CHUNK = 64
N_EXPERTS = 32
TOP_K = 4
D_FF = D_MODEL
SWIGLU_LIMIT = 7.0
SWIGLU_ALPHA = 1.702
RMS_EPS = 1e-5
NEG = -1e30

F32 = jnp.float32
BF16 = jnp.bfloat16

LANES = 128
SUBLANES = 8
VMEM_LIMIT_BYTES = 56 * 1024 * 1024

COL_GZ = 0
COL_AQ = COL_GZ + 2 * D_MODEL
COL_MV = COL_AQ + ATTN_WIDTH
COL_MO = COL_MV + M_V_WIDTH
COL_MQ = COL_MO + M_V_WIDTH
COL_MK = COL_MQ + M_QK_WIDTH
COL_AK = COL_MK + M_QK_WIDTH
COL_IF = COL_AK + 2 * KV_WIDTH
IN_TILE_N = 2176
IN_WIDTH_PAD = -(-(COL_IF + LANES) // IN_TILE_N) * IN_TILE_N

SAMPLE_ROWS = SUBLANES
MOE_BLOCK_M = 512
MOE_FF_CHUNK = 1024
ROUTER_PAD = LANES


def _permute_w_in(w_in):
    splits = (ATTN_WIDTH, KV_WIDTH, KV_WIDTH, M_QK_WIDTH, M_QK_WIDTH, M_V_WIDTH, M_V_WIDTH, M_HEADS, M_HEADS,
              2 * D_MODEL)
    parts, off = [], 0
    for width in splits:
        parts.append(w_in[:, off:off + width])
        off += width
    aq, ak, av, mq, mk, mv, mo, mi, mf, gz = parts
    cols = jnp.concatenate([gz, aq, mv, mo, mq, mk, ak, av, mi, mf], axis=1)
    cols = jnp.pad(cols, ((0, 0), (0, IN_WIDTH_PAD - cols.shape[1])))
    return cols.astype(BF16)


def _params(*semantics):
    return pltpu.CompilerParams(dimension_semantics=semantics, vmem_limit_bytes=VMEM_LIMIT_BYTES)


def _log_sigmoid(x):
    return jnp.minimum(x, 0.0) - jnp.log1p(jnp.exp(-jnp.abs(x)))


def _inproj_body(tiles_a, xa_ref, xb_ref, g_ref, w_ref, z_ref, h_scr):
    def normalise(x_ref):
        x = x_ref[...]
        y = x * lax.rsqrt(jnp.mean(x * x, axis=-1, keepdims=True) + RMS_EPS)
        h_scr[...] = (y * g_ref[...]).astype(BF16)

    first = pl.program_id(1) == 0
    in_a = pl.program_id(0) < tiles_a

    @pl.when(first & in_a)
    def _():
        normalise(xa_ref)

    @pl.when(first & jnp.logical_not(in_a))
    def _():
        normalise(xb_ref)

    z_ref[...] = jnp.dot(h_scr[...], w_ref[...], preferred_element_type=F32)


def _inproj(x_a, x_b, g, w_bf16, tile_m):
    tiles_a, tiles_b = x_a.shape[0] // tile_m, x_b.shape[0] // tile_m
    n = w_bf16.shape[1]
    return pl.pallas_call(
        functools.partial(_inproj_body, tiles_a),
        grid=(tiles_a + tiles_b, n // IN_TILE_N),
        in_specs=[
            pl.BlockSpec((tile_m, D_MODEL), lambda i, j: (jnp.minimum(i, tiles_a - 1), 0)),
            pl.BlockSpec((tile_m, D_MODEL), lambda i, j: (jnp.maximum(i - tiles_a, 0), 0)),
            pl.BlockSpec((1, D_MODEL), lambda i, j: (0, 0)),
            pl.BlockSpec((D_MODEL, IN_TILE_N), lambda i, j: (0, j)),
        ],
        out_specs=pl.BlockSpec((tile_m, IN_TILE_N), lambda i, j: (i, j)),
        out_shape=jax.ShapeDtypeStruct(((tiles_a + tiles_b) * tile_m, n), F32),
        scratch_shapes=[pltpu.VMEM((tile_m, D_MODEL), BF16)],
        compiler_params=_params("arbitrary", "arbitrary"),
        name="inproj",
    )(x_a, x_b, g.reshape(1, D_MODEL), w_bf16)


assert KV_WIDTH == LANES and N_KV_HEADS == 2 and LANES == 2 * HEAD_DIM
Q_TILES = ATTN_WIDTH // LANES
TILES_PER_KV = Q_TILES // N_KV_HEADS
_NT = (((1,), (1,)), ((), ()))


def _kv_placements(x):
    lo = lax.broadcasted_iota(jnp.int32, x.shape, 1) < HEAD_DIM
    swapped = pltpu.roll(x, HEAD_DIM, axis=1)
    return ((jnp.where(lo, x, 0.0).astype(BF16), jnp.where(lo, 0.0, swapped).astype(BF16)),
            (jnp.where(lo, swapped, 0.0).astype(BF16), jnp.where(lo, 0.0, x).astype(BF16)))


def _slab_head(g, p, j):
    return g * Q_PER_KV + 2 * j + p


def _attn_prompt_body(q_ref, kvc_ref, kvp_ref, sink_ref, a_ref, k_ref, v_ref):
    n = pl.program_id(1)
    kc, vc = kvc_ref[:, :KV_WIDTH], kvc_ref[:, KV_WIDTH:]
    k_ref[...] = kc
    v_ref[...] = vc
    kk = _kv_placements(jnp.concatenate([kvp_ref[:, :KV_WIDTH], kc], axis=0))
    vv = _kv_placements(jnp.concatenate([kvp_ref[:, KV_WIDTH:], vc], axis=0))
    q = q_ref[...].astype(BF16)
    row = lax.broadcasted_iota(jnp.int32, (WINDOW, WINDOW), 0)
    col = lax.broadcasted_iota(jnp.int32, (WINDOW, WINDOW), 1)
    mask = jnp.concatenate([(col > row) & (n > 0), col <= row], axis=1)
    slabs, sinks = [], []
    for g in range(N_KV_HEADS):
        qt = jnp.concatenate([q[:, (TILES_PER_KV * g + j) * LANES:(TILES_PER_KV * g + j + 1) * LANES]
                              for j in range(TILES_PER_KV)], axis=0)
        for p in range(2):
            s = lax.dot_general(qt, kk[g][p], _NT, preferred_element_type=F32) * (HEAD_DIM ** -0.5)
            for j in range(TILES_PER_KV):
                h = _slab_head(g, p, j)
                slabs.append(jnp.where(mask, s[j * WINDOW:(j + 1) * WINDOW], NEG))
                sinks.append(jnp.broadcast_to(sink_ref[:, h:h + 1], (WINDOW, 1)))
    s_all = jnp.concatenate(slabs, axis=0)
    sink = jnp.concatenate(sinks, axis=0)
    m = jnp.maximum(jnp.max(s_all, axis=1, keepdims=True), sink)
    p_all = jnp.exp(s_all - m)
    denom = jnp.sum(p_all, axis=1, keepdims=True) + jnp.exp(sink - m)
    pn = (p_all * (1.0 / denom)).astype(BF16)
    half = TILES_PER_KV * WINDOW
    for g in range(N_KV_HEADS):
        base = 2 * half * g
        o = (jnp.dot(pn[base:base + half], vv[g][0], preferred_element_type=F32)
             + jnp.dot(pn[base + half:base + 2 * half], vv[g][1], preferred_element_type=F32))
        for j in range(TILES_PER_KV):
            t = TILES_PER_KV * g + j
            a_ref[:, t * LANES:(t + 1) * LANES] = o[j * WINDOW:(j + 1) * WINDOW].astype(BF16)


def _attn_prompt(z_all, sinks_row, batch, seq):
    nb = seq // WINDOW
    return pl.pallas_call(
        _attn_prompt_body,
        grid=(batch, nb),
        in_specs=[
            pl.BlockSpec((WINDOW, ATTN_WIDTH), lambda b, n: (b * nb + n, COL_AQ // ATTN_WIDTH)),
            pl.BlockSpec((WINDOW, 2 * KV_WIDTH), lambda b, n: (b * nb + n, COL_AK // (2 * KV_WIDTH))),
            pl.BlockSpec((WINDOW, 2 * KV_WIDTH),
                         lambda b, n: (b * nb + jnp.maximum(n - 1, 0), COL_AK // (2 * KV_WIDTH))),
            pl.BlockSpec((1, LANES), lambda b, n: (0, 0)),
        ],
        out_specs=[
            pl.BlockSpec((WINDOW, ATTN_WIDTH), lambda b, n: (b * nb + n, 0)),
            pl.BlockSpec((None, WINDOW, KV_WIDTH), lambda b, n: (b, 0, 0)),
            pl.BlockSpec((None, WINDOW, KV_WIDTH), lambda b, n: (b, 0, 0)),
        ],
        out_shape=[
            jax.ShapeDtypeStruct((batch * seq, ATTN_WIDTH), BF16),
            jax.ShapeDtypeStruct((batch, WINDOW, KV_WIDTH), F32),
            jax.ShapeDtypeStruct((batch, WINDOW, KV_WIDTH), F32),
        ],
        compiler_params=_params("arbitrary", "arbitrary"),
        name="attn_prompt",
    )(z_all, z_all, z_all, sinks_row)


ATTN_SAMPLE_GROUP = 8


def _attn_sample_body(n_valid, q_ref, kv_ref, ck_ref, cv_ref, sink_ref, a_ref, nk_ref, nv_ref, o_scr):
    rs = SAMPLE_ROWS
    slab = TILES_PER_KV * rs
    r8 = lax.broadcasted_iota(jnp.int32, (rs, KV_WIDTH), 0)
    qi_c = jnp.bitwise_and(lax.broadcasted_iota(jnp.int32, (slab, WINDOW), 0), rs - 1)
    mask_c = lax.broadcasted_iota(jnp.int32, (slab, WINDOW), 1) > qi_c
    qi_n = jnp.bitwise_and(lax.broadcasted_iota(jnp.int32, (slab, rs), 0), rs - 1)
    mask_n = lax.broadcasted_iota(jnp.int32, (slab, rs), 1) <= qi_n
    sc_list, sn_list, sink_list, values = [], [], [], []
    for b in range(ATTN_SAMPLE_GROUP):
        rows = slice(b * rs, (b + 1) * rs)
        q = q_ref[rows, :]
        kn, vn = kv_ref[rows, :KV_WIDTH], kv_ref[rows, KV_WIDTH:]
        ck, cv = ck_ref[b], cv_ref[b]
        for src, new, dst in ((ck, kn, nk_ref), (cv, vn, nv_ref)):
            rolled = pltpu.roll(src, WINDOW - n_valid, axis=0)
            dst[b] = rolled
            tail = jnp.where(r8 >= rs - n_valid, pltpu.roll(new, rs - n_valid, axis=0), rolled[WINDOW - rs:, :])
            dst[b, WINDOW - rs:, :] = tail
        kk_c, kk_n = _kv_placements(ck), _kv_placements(kn)
        values.append((_kv_placements(cv), _kv_placements(vn)))
        for g in range(N_KV_HEADS):
            qt = jnp.concatenate([q[:, (TILES_PER_KV * g + j) * LANES:(TILES_PER_KV * g + j + 1) * LANES]
                                  for j in range(TILES_PER_KV)], axis=0).astype(BF16)
            for p in range(2):
                s_c = lax.dot_general(qt, kk_c[g][p], _NT, preferred_element_type=F32) * (HEAD_DIM ** -0.5)
                s_n = lax.dot_general(qt, kk_n[g][p], _NT, preferred_element_type=F32) * (HEAD_DIM ** -0.5)
                sc_list.append(jnp.where(mask_c, s_c, NEG))
                sn_list.append(jnp.where(mask_n, s_n, NEG))
                for j in range(TILES_PER_KV):
                    h = _slab_head(g, p, j)
                    sink_list.append(jnp.broadcast_to(sink_ref[:, h:h + 1], (rs, 1)))
    s_c = jnp.concatenate(sc_list, axis=0)
    s_n = jnp.concatenate(sn_list, axis=0)
    sink = jnp.concatenate(sink_list, axis=0)
    m = jnp.maximum(jnp.maximum(jnp.max(s_c, axis=1, keepdims=True), jnp.max(s_n, axis=1, keepdims=True)), sink)
    p_c = jnp.exp(s_c - m)
    p_n = jnp.exp(s_n - m)
    inv = 1.0 / (jnp.sum(p_c, axis=1, keepdims=True) + jnp.sum(p_n, axis=1, keepdims=True) + jnp.exp(sink - m))
    p_c = (p_c * inv).astype(BF16)
    p_n = (p_n * inv).astype(BF16)
    for b in range(ATTN_SAMPLE_GROUP):
        vv_c, vv_n = values[b]
        for g in range(N_KV_HEADS):
            base = (b * N_KV_HEADS + g) * 2 * slab
            o = (jnp.dot(p_c[base:base + slab], vv_c[g][0], preferred_element_type=F32)
                 + jnp.dot(p_c[base + slab:base + 2 * slab], vv_c[g][1], preferred_element_type=F32)
                 + jnp.dot(p_n[base:base + slab], vv_n[g][0], preferred_element_type=F32)
                 + jnp.dot(p_n[base + slab:base + 2 * slab], vv_n[g][1], preferred_element_type=F32))
            for j in range(TILES_PER_KV):
                t = TILES_PER_KV * g + j
                o_scr[b * rs:(b + 1) * rs, t * LANES:(t + 1) * LANES] = o[j * rs:(j + 1) * rs]
    a_ref[...] = o_scr[...].astype(BF16)


def _attn_sample(z_all, sinks_row, cache_k, cache_v, row_off, n_valid):
    b = cache_k.shape[0]
    gr = ATTN_SAMPLE_GROUP
    rows = gr * SAMPLE_ROWS
    rb = row_off // rows
    return pl.pallas_call(
        functools.partial(_attn_sample_body, n_valid),
        grid=(b // gr,),
        in_specs=[
            pl.BlockSpec((rows, ATTN_WIDTH), lambda i: (rb + i, COL_AQ // ATTN_WIDTH)),
            pl.BlockSpec((rows, 2 * KV_WIDTH), lambda i: (rb + i, COL_AK // (2 * KV_WIDTH))),
            pl.BlockSpec((gr, WINDOW, KV_WIDTH), lambda i: (i, 0, 0)),
            pl.BlockSpec((gr, WINDOW, KV_WIDTH), lambda i: (i, 0, 0)),
            pl.BlockSpec((1, LANES), lambda i: (0, 0)),
        ],
        out_specs=[
            pl.BlockSpec((rows, ATTN_WIDTH), lambda i: (i, 0)),
            pl.BlockSpec((gr, WINDOW, KV_WIDTH), lambda i: (i, 0, 0)),
            pl.BlockSpec((gr, WINDOW, KV_WIDTH), lambda i: (i, 0, 0)),
        ],
        out_shape=[
            jax.ShapeDtypeStruct((b * SAMPLE_ROWS, ATTN_WIDTH), BF16),
            jax.ShapeDtypeStruct(cache_k.shape, F32),
            jax.ShapeDtypeStruct(cache_v.shape, F32),
        ],
        scratch_shapes=[pltpu.VMEM((rows, ATTN_WIDTH), F32)],
        compiler_params=_params("arbitrary"),
        name="attn_sample",
    )(z_all, z_all, cache_k, cache_v, sinks_row)


def _mlstm_gates(if_blk, bias_row, n_valid):
    length = if_blk.shape[0]
    g = if_blk + bias_row
    lane = lax.broadcasted_iota(jnp.int32, g.shape, 1)
    g = jnp.where(lane < M_HEADS, g, _log_sigmoid(g))
    if n_valid < length:
        row = lax.broadcasted_iota(jnp.int32, g.shape, 0)
        g = jnp.where(row < n_valid, g, jnp.where(lane < M_HEADS, NEG, 0.0))
    return g


def _mlstm_chunk(q, k, v, gates, gates_t, h, c, n, m):
    length = q.shape[0]
    i_col = gates[:, h:h + 1]
    f_col = gates[:, M_HEADS + h:M_HEADS + h + 1]
    i_row = gates_t[h:h + 1, :]
    f_row = gates_t[M_HEADS + h:M_HEADS + h + 1, :]
    t_idx = lax.broadcasted_iota(jnp.int32, (length, length), 0)
    s_idx = lax.broadcasted_iota(jnp.int32, (length, length), 1)
    causal = s_idx <= t_idx
    bcum_col = jnp.sum(jnp.where(causal, f_row, 0.0), axis=1, keepdims=True)
    bcum_row = jnp.sum(jnp.where(t_idx <= s_idx, f_col, 0.0), axis=0, keepdims=True)
    dlog = jnp.where(causal, bcum_col - bcum_row + i_row, NEG)
    inter = bcum_col + m
    mt = jnp.maximum(inter, jnp.max(dlog, axis=1, keepdims=True))
    w_inter = jnp.exp(inter - mt)
    qb = q.astype(BF16)
    ks = k * (M_QK_DIM ** -0.5)
    vb = v.astype(BF16)
    qk = lax.dot_general(qb, ks.astype(BF16), (((1,), (1,)), ((), ())), preferred_element_type=F32)
    qk = qk * jnp.exp(dlog - mt)
    num = (w_inter * jnp.dot(qb, c.astype(BF16), preferred_element_type=F32)
           + jnp.dot(qk.astype(BF16), vb, preferred_element_type=F32))
    den = w_inter * jnp.sum(q * n, axis=1, keepdims=True) + jnp.sum(qk, axis=1, keepdims=True)
    h_t = num / jnp.maximum(jnp.abs(den), jnp.exp(-mt))
    m_new = mt[length - 1:length, :]
    w_state = jnp.exp(bcum_col[length - 1:length, :] - bcum_col + i_col - m_new)
    decay = jnp.exp(inter[length - 1:length, :] - m_new)
    kw = ks * w_state
    c_new = decay * c + jnp.dot(kw.T.astype(BF16), vb, preferred_element_type=F32)
    n_new = decay * n + jnp.sum(kw, axis=0, keepdims=True)
    return h_t, c_new, n_new, m_new


def _mlstm_head_out(h_t, norm_w, o_gate):
    hs = h_t * lax.rsqrt(jnp.mean(h_t * h_t, axis=-1, keepdims=True) + RMS_EPS)
    return hs * norm_w * jax.nn.sigmoid(o_gate)


MLSTM_SEG = 512


MLSTM_BATCH = 2
MLSTM_PROMPT_CHUNK = 256


def _mlstm_prompt_body(*refs):
    nb = MLSTM_BATCH
    ins, rest = refs[:5 * nb], refs[5 * nb:]
    bias_ref, nw_ref, mh_ref, c_out, n_out, m_out, c_scr, n_scr, m_scr = rest
    seg = pl.program_id(1)

    @pl.when(seg == 0)
    def _():
        c_scr[...] = jnp.zeros_like(c_scr)
        n_scr[...] = jnp.zeros_like(n_scr)
        m_scr[...] = jnp.full_like(m_scr, NEG)

    def chunk_body(ci, carry):
        r = pl.multiple_of(ci * MLSTM_PROMPT_CHUNK, MLSTM_PROMPT_CHUNK)
        rows = pl.ds(r, MLSTM_PROMPT_CHUNK)
        state = [[(c_scr[u, h], n_scr[u, h:h + 1, :], m_scr[u, h:h + 1, 0:1]) for h in range(M_HEADS)]
                 for u in range(nb)]
        results = []
        for u in range(nb):
            q_ref, k_ref, v_ref, o_ref, if_ref = ins[5 * u:5 * u + 5]
            gates = _mlstm_gates(if_ref[rows, :], bias_ref[...], MLSTM_PROMPT_CHUNK)
            gates_t = gates.T
            for h in range(M_HEADS):
                qs = slice(h * M_QK_DIM, (h + 1) * M_QK_DIM)
                vs = slice(h * M_V_DIM, (h + 1) * M_V_DIM)
                h_t, c_new, n_new, m_new = _mlstm_chunk(
                    q_ref[rows, qs], k_ref[rows, qs], v_ref[rows, vs], gates, gates_t, h, *state[u][h])
                out = _mlstm_head_out(h_t, nw_ref[:, vs], o_ref[rows, vs]).astype(BF16)
                results.append((u, h, vs, out, c_new, n_new, m_new))
        for u, h, vs, out, c_new, n_new, m_new in results:
            c_scr[u, h] = c_new
            n_scr[u, h:h + 1, :] = n_new
            m_scr[u, h:h + 1, :] = jnp.broadcast_to(m_new, (1, LANES))
            mh_ref[u, rows, vs] = out
        return carry

    lax.fori_loop(0, MLSTM_SEG // MLSTM_PROMPT_CHUNK, chunk_body, 0)

    @pl.when(seg == pl.num_programs(1) - 1)
    def _():
        c_out[...] = c_scr[...]
        n_out[...] = n_scr[:, 0:M_HEADS, :]
        m_out[...] = m_scr[:, 0:M_HEADS, :]


def _mlstm_prompt(z_all, bias_row, norm_w_row, batch, seq):
    ns = seq // MLSTM_SEG
    nb = MLSTM_BATCH
    in_specs, args = [], []
    for u in range(nb):
        row = lambda p, s, u=u: (p * nb + u) * ns + s
        in_specs += [
            pl.BlockSpec((MLSTM_SEG, M_QK_WIDTH), lambda p, s, row=row: (row(p, s), COL_MQ // M_QK_WIDTH)),
            pl.BlockSpec((MLSTM_SEG, M_QK_WIDTH), lambda p, s, row=row: (row(p, s), COL_MK // M_QK_WIDTH)),
            pl.BlockSpec((MLSTM_SEG, M_V_WIDTH), lambda p, s, row=row: (row(p, s), COL_MV // M_V_WIDTH)),
            pl.BlockSpec((MLSTM_SEG, M_V_WIDTH), lambda p, s, row=row: (row(p, s), COL_MO // M_V_WIDTH)),
            pl.BlockSpec((MLSTM_SEG, LANES), lambda p, s, row=row: (row(p, s), COL_IF // LANES)),
        ]
        args += [z_all] * 5
    in_specs += [pl.BlockSpec((1, LANES), lambda p, s: (0, 0)), pl.BlockSpec((1, M_V_WIDTH), lambda p, s: (0, 0))]
    mh, c, n, m = pl.pallas_call(
        _mlstm_prompt_body,
        grid=(batch // nb, ns),
        in_specs=in_specs,
        out_specs=[
            pl.BlockSpec((None, nb, MLSTM_SEG, M_V_WIDTH), lambda p, s: (p, 0, s, 0)),
            pl.BlockSpec((nb, M_HEADS, M_QK_DIM, M_V_DIM), lambda p, s: (p, 0, 0, 0)),
            pl.BlockSpec((nb, M_HEADS, M_QK_DIM), lambda p, s: (p, 0, 0)),
            pl.BlockSpec((nb, M_HEADS, LANES), lambda p, s: (p, 0, 0)),
        ],
        out_shape=[
            jax.ShapeDtypeStruct((batch // nb, nb, seq, M_V_WIDTH), BF16),
            jax.ShapeDtypeStruct((batch, M_HEADS, M_QK_DIM, M_V_DIM), F32),
            jax.ShapeDtypeStruct((batch, M_HEADS, M_QK_DIM), F32),
            jax.ShapeDtypeStruct((batch, M_HEADS, LANES), F32),
        ],
        scratch_shapes=[
            pltpu.VMEM((nb, M_HEADS, M_QK_DIM, M_V_DIM), F32),
            pltpu.VMEM((nb, SUBLANES, M_QK_DIM), F32),
            pltpu.VMEM((nb, SUBLANES, LANES), F32),
        ],
        compiler_params=_params("arbitrary", "arbitrary"),
        name="mlstm_prompt",
    )(*args, bias_row, norm_w_row)
    return mh.reshape(batch * seq, M_V_WIDTH), c, n, m


MLSTM_SAMPLE_GROUP = 2


def _mlstm_sample_body(n_valid, q_ref, k_ref, v_ref, o_ref, if_ref, bias_ref, nw_ref, c0_ref, n0_ref, m0_ref,
                       mh_ref, c_out, n_out, m_out, h_scr):
    for g in range(MLSTM_SAMPLE_GROUP):
        rows = slice(g * SAMPLE_ROWS, (g + 1) * SAMPLE_ROWS)
        gates = _mlstm_gates(if_ref[rows, :], bias_ref[...], n_valid)
        gates_t = jnp.transpose(jnp.concatenate([gates] * (LANES // SAMPLE_ROWS), axis=0))[:, :SAMPLE_ROWS]
        for h in range(M_HEADS):
            qs = slice(h * M_QK_DIM, (h + 1) * M_QK_DIM)
            vs = slice(h * M_V_DIM, (h + 1) * M_V_DIM)
            h_t, c_new, n_new, m_new = _mlstm_chunk(
                q_ref[rows, qs], k_ref[rows, qs], v_ref[rows, vs], gates, gates_t, h,
                c0_ref[g, h], n0_ref[g, h:h + 1, :], m0_ref[g:g + 1, h:h + 1])
            c_out[g, h] = c_new
            n_out[g, h:h + 1, :] = n_new
            m_out[g:g + 1, h:h + 1] = m_new
            h_scr[rows, vs] = _mlstm_head_out(h_t, nw_ref[:, vs], o_ref[rows, vs])
    mh_ref[...] = h_scr[...].astype(BF16)


def _mlstm_sample(z_all, bias_row, norm_w_row, c0, n0, m0, row_off, n_valid):
    b = c0.shape[0]
    gr = MLSTM_SAMPLE_GROUP
    rows = gr * SAMPLE_ROWS
    rb = row_off // rows
    m0g = m0.reshape(b // gr, gr, M_HEADS)
    outs = pl.pallas_call(
        functools.partial(_mlstm_sample_body, n_valid),
        grid=(b // gr,),
        in_specs=[
            pl.BlockSpec((rows, M_QK_WIDTH), lambda i: (rb + i, COL_MQ // M_QK_WIDTH)),
            pl.BlockSpec((rows, M_QK_WIDTH), lambda i: (rb + i, COL_MK // M_QK_WIDTH)),
            pl.BlockSpec((rows, M_V_WIDTH), lambda i: (rb + i, COL_MV // M_V_WIDTH)),
            pl.BlockSpec((rows, M_V_WIDTH), lambda i: (rb + i, COL_MO // M_V_WIDTH)),
            pl.BlockSpec((rows, LANES), lambda i: (rb + i, COL_IF // LANES)),
            pl.BlockSpec((1, LANES), lambda i: (0, 0)),
            pl.BlockSpec((1, M_V_WIDTH), lambda i: (0, 0)),
            pl.BlockSpec((gr, M_HEADS, M_QK_DIM, M_V_DIM), lambda i: (i, 0, 0, 0)),
            pl.BlockSpec((gr, M_HEADS, M_QK_DIM), lambda i: (i, 0, 0)),
            pl.BlockSpec((None, gr, M_HEADS), lambda i: (i, 0, 0)),
        ],
        out_specs=[
            pl.BlockSpec((rows, M_V_WIDTH), lambda i: (i, 0)),
            pl.BlockSpec((gr, M_HEADS, M_QK_DIM, M_V_DIM), lambda i: (i, 0, 0, 0)),
            pl.BlockSpec((gr, M_HEADS, M_QK_DIM), lambda i: (i, 0, 0)),
            pl.BlockSpec((None, gr, M_HEADS), lambda i: (i, 0, 0)),
        ],
        out_shape=[
            jax.ShapeDtypeStruct((b * SAMPLE_ROWS, M_V_WIDTH), BF16),
            jax.ShapeDtypeStruct(c0.shape, F32),
            jax.ShapeDtypeStruct(n0.shape, F32),
            jax.ShapeDtypeStruct(m0g.shape, F32),
        ],
        scratch_shapes=[pltpu.VMEM((rows, M_V_WIDTH), F32)],
        compiler_params=_params("arbitrary"),
        name="mlstm_sample",
    )(z_all, z_all, z_all, z_all, z_all, bias_row, norm_w_row, c0, n0, m0g)
    mh, c, n, m = outs
    return mh, c, n, m.reshape(b, M_HEADS)


def _merge_body(tiles_a, xa_ref, xb_ref, aa_ref, ab_ref, ma_ref, mb_ref, gza_ref, gzm_ref, wa_ref, wm_ref, wo_ref,
                gf_ref, wr_ref, br_ref, x1_ref, h2_ref, lg_ref):
    in_a = pl.program_id(0) < tiles_a
    x = jnp.where(in_a, xa_ref[...], xb_ref[...])
    a = jnp.where(in_a, aa_ref[...], ab_ref[...])
    mh = jnp.where(in_a, ma_ref[...], mb_ref[...])
    pa = jnp.dot(a, wa_ref[...], preferred_element_type=F32)
    pm = jnp.dot(mh, wm_ref[...], preferred_element_type=F32)
    merged = jax.nn.sigmoid(gza_ref[...]) * pa + jax.nn.sigmoid(gzm_ref[...]) * pm
    x1 = x + jnp.dot(merged.astype(BF16), wo_ref[...], preferred_element_type=F32)
    x1_ref[...] = x1
    h2 = x1 * lax.rsqrt(jnp.mean(x1 * x1, axis=-1, keepdims=True) + RMS_EPS) * gf_ref[...]
    h2_ref[...] = h2
    hi = h2.astype(BF16)
    lo = (h2 - hi.astype(F32)).astype(BF16)
    both = jnp.dot(hi, wr_ref[...], preferred_element_type=F32)
    lg_ref[...] = (both[:, :ROUTER_PAD] + both[:, ROUTER_PAD:]
                   + jnp.dot(lo, wr_ref[:, :ROUTER_PAD], preferred_element_type=F32) + br_ref[...])


def _merge(x_a, x_b, a_a, a_b, mh_a, mh_b, z_all, wa, wm, wo, g_ffn, wr_pad, br_pad, tile_m):
    tiles_a, tiles_b = x_a.shape[0] // tile_m, x_b.shape[0] // tile_m
    t_all = z_all.shape[0]
    const = lambda i: (0, 0)
    from_a = lambda i: (jnp.minimum(i, tiles_a - 1), 0)
    from_b = lambda i: (jnp.maximum(i - tiles_a, 0), 0)
    return pl.pallas_call(
        functools.partial(_merge_body, tiles_a),
        grid=(tiles_a + tiles_b,),
        in_specs=[
            pl.BlockSpec((tile_m, D_MODEL), from_a),
            pl.BlockSpec((tile_m, D_MODEL), from_b),
            pl.BlockSpec((tile_m, ATTN_WIDTH), from_a),
            pl.BlockSpec((tile_m, ATTN_WIDTH), from_b),
            pl.BlockSpec((tile_m, M_V_WIDTH), from_a),
            pl.BlockSpec((tile_m, M_V_WIDTH), from_b),
            pl.BlockSpec((tile_m, D_MODEL), lambda i: (i, COL_GZ // D_MODEL)),
            pl.BlockSpec((tile_m, D_MODEL), lambda i: (i, COL_GZ // D_MODEL + 1)),
            pl.BlockSpec((ATTN_WIDTH, D_MODEL), const, pipeline_mode=pl.Buffered(1)),
            pl.BlockSpec((M_V_WIDTH, D_MODEL), const, pipeline_mode=pl.Buffered(1)),
            pl.BlockSpec((D_MODEL, D_MODEL), const, pipeline_mode=pl.Buffered(1)),
            pl.BlockSpec((1, D_MODEL), const),
            pl.BlockSpec((D_MODEL, 2 * ROUTER_PAD), const),
            pl.BlockSpec((1, ROUTER_PAD), const),
        ],
        out_specs=[
            pl.BlockSpec((tile_m, D_MODEL), lambda i: (i, 0)),
            pl.BlockSpec((tile_m, D_MODEL), lambda i: (i, 0)),
            pl.BlockSpec((tile_m, ROUTER_PAD), lambda i: (i, 0)),
        ],
        out_shape=[
            jax.ShapeDtypeStruct((t_all, D_MODEL), F32),
            jax.ShapeDtypeStruct((t_all, D_MODEL), F32),
            jax.ShapeDtypeStruct((t_all, ROUTER_PAD), F32),
        ],
        compiler_params=_params("arbitrary"),
        name="merge",
    )(x_a, x_b, a_a, a_b, mh_a, mh_b, z_all, z_all, wa, wm, wo, g_ffn.reshape(1, D_MODEL), wr_pad, br_pad)


def _expert_body(be_ref, nb_ref, xs_ref, w1g_ref, w1u_ref, b1g_ref, b1u_ref, w2_ref, b2_ref, y_ref):
    i = pl.program_id(0)
    c = pl.program_id(1)
    used = i < nb_ref[0]

    @pl.when(used)
    def _():
        x = xs_ref[...].astype(BF16)
        zg = jnp.dot(x, w1g_ref[...], preferred_element_type=F32) + b1g_ref[...]
        zu = jnp.dot(x, w1u_ref[...], preferred_element_type=F32) + b1u_ref[...]
        gate = jnp.minimum(zg, SWIGLU_LIMIT)
        up = jnp.clip(zu, -SWIGLU_LIMIT, SWIGLU_LIMIT)
        act = gate * jax.nn.sigmoid(SWIGLU_ALPHA * gate) * (up + 1.0)
        y = jnp.dot(act.astype(BF16), w2_ref[...], preferred_element_type=F32)

        @pl.when(c == 0)
        def _():
            y_ref[...] = y + b2_ref[...]

        @pl.when(c != 0)
        def _():
            y_ref[...] += y

    @pl.when(jnp.logical_not(used) & (c == 0))
    def _():
        y_ref[...] = jnp.zeros_like(y_ref)


def _experts(block_e, n_used, xs, w1, b1, w2, b2):
    n_slots = xs.shape[0]
    n_blocks = n_slots // MOE_BLOCK_M
    n_chunks = D_FF // MOE_FF_CHUNK

    def chunk(i, c, nb):
        block = jnp.minimum(i, nb[0] - 1)
        step = jnp.where(i < nb[0], c, n_chunks - 1)
        return jnp.where(block % 2 == 0, step, n_chunks - 1 - step)

    grid_spec = pltpu.PrefetchScalarGridSpec(
        num_scalar_prefetch=2,
        grid=(n_blocks, n_chunks),
        in_specs=[
            pl.BlockSpec((MOE_BLOCK_M, D_MODEL), lambda i, c, be, nb: (i, 0)),
            pl.BlockSpec((None, D_MODEL, MOE_FF_CHUNK), lambda i, c, be, nb: (be[i], 0, chunk(i, c, nb))),
            pl.BlockSpec((None, D_MODEL, MOE_FF_CHUNK),
                         lambda i, c, be, nb: (be[i], 0, n_chunks + chunk(i, c, nb))),
            pl.BlockSpec((None, 1, MOE_FF_CHUNK), lambda i, c, be, nb: (be[i], 0, chunk(i, c, nb))),
            pl.BlockSpec((None, 1, MOE_FF_CHUNK), lambda i, c, be, nb: (be[i], 0, n_chunks + chunk(i, c, nb))),
            pl.BlockSpec((None, MOE_FF_CHUNK, D_MODEL), lambda i, c, be, nb: (be[i], chunk(i, c, nb), 0)),
            pl.BlockSpec((None, 1, D_MODEL), lambda i, c, be, nb: (be[i], 0, 0)),
        ],
        out_specs=pl.BlockSpec((MOE_BLOCK_M, D_MODEL), lambda i, c, be, nb: (i, 0)),
    )
    return pl.pallas_call(
        _expert_body,
        grid_spec=grid_spec,
        out_shape=jax.ShapeDtypeStruct((n_slots, D_MODEL), F32),
        compiler_params=_params("arbitrary", "arbitrary"),
        name="experts",
    )(block_e, n_used, xs, w1, w1, b1, b1, w2, b2)


ROUTE_TILE = 1024
LANE_E = 0
LANE_RANK = TOP_K


def _route_body(lg_ref, meta_ref, gate_ref, cnt_ref, tri_scr, carry_scr):
    @pl.when(pl.program_id(0) == 0)
    def _():
        r = lax.broadcasted_iota(jnp.int32, (ROUTE_TILE, ROUTE_TILE), 0)
        c = lax.broadcasted_iota(jnp.int32, (ROUTE_TILE, ROUTE_TILE), 1)
        tri_scr[...] = jnp.where(c < r, 1.0, 0.0).astype(BF16)
        carry_scr[...] = jnp.zeros_like(carry_scr)

    lane_i = lax.broadcasted_iota(jnp.int32, (ROUTE_TILE, LANES), 1)
    lane = lane_i.astype(F32)
    x = jnp.where(lane_i < N_EXPERTS, lg_ref[...], -jnp.inf)
    vals, idxs, sels = [], [], []
    for _ in range(TOP_K):
        mx = jnp.max(x, axis=1, keepdims=True)
        idx = jnp.min(jnp.where(x == mx, lane, float(LANES)), axis=1, keepdims=True)
        sel = lane == idx
        x = jnp.where(sel, -jnp.inf, x)
        vals.append(mx)
        idxs.append(idx)
        sels.append(sel)
    weights = [jnp.exp(v - vals[0]) for v in vals]
    inv = 1.0 / (weights[0] + weights[1] + weights[2] + weights[3])
    onehot = sum(jnp.where(s, 1.0, 0.0) for s in sels)
    ranks = jnp.dot(tri_scr[...], onehot.astype(BF16), preferred_element_type=F32) + carry_scr[0:1, :]
    meta = jnp.zeros((ROUTE_TILE, LANES), F32)
    gate = jnp.zeros((ROUTE_TILE, LANES), F32)
    for k in range(TOP_K):
        rank_k = jnp.sum(jnp.where(sels[k], ranks, 0.0), axis=1, keepdims=True)
        meta = jnp.where(lane_i == LANE_E + k, idxs[k], meta)
        meta = jnp.where(lane_i == LANE_RANK + k, rank_k, meta)
        gate = jnp.where(lane_i == k, weights[k] * inv, gate)
    meta_ref[...] = meta.astype(jnp.int32)
    gate_ref[...] = gate
    carry_scr[0:1, :] = carry_scr[0:1, :] + jnp.sum(onehot, axis=0, keepdims=True)
    cnt_ref[...] = carry_scr[...]


def _route(logits):
    t = logits.shape[0]
    return pl.pallas_call(
        _route_body,
        grid=(t // ROUTE_TILE,),
        in_specs=[pl.BlockSpec((ROUTE_TILE, LANES), lambda i: (i, 0))],
        out_specs=[
            pl.BlockSpec((ROUTE_TILE, LANES), lambda i: (i, 0)),
            pl.BlockSpec((ROUTE_TILE, LANES), lambda i: (i, 0)),
            pl.BlockSpec((SUBLANES, LANES), lambda i: (0, 0)),
        ],
        out_shape=[
            jax.ShapeDtypeStruct((t, LANES), jnp.int32),
            jax.ShapeDtypeStruct((t, LANES), F32),
            jax.ShapeDtypeStruct((SUBLANES, LANES), F32),
        ],
        scratch_shapes=[pltpu.VMEM((ROUTE_TILE, ROUTE_TILE), BF16), pltpu.VMEM((SUBLANES, LANES), F32)],
        compiler_params=_params("arbitrary"),
        name="route",
    )(logits)


def _slot_tables(counts, n_blocks):
    padded = (counts + MOE_BLOCK_M - 1) // MOE_BLOCK_M * MOE_BLOCK_M
    pends = jnp.cumsum(padded)
    first_slot = jnp.arange(n_blocks, dtype=pends.dtype) * MOE_BLOCK_M
    block_e = jnp.minimum(jnp.sum(pends[None, :] <= first_slot[:, None], axis=1), N_EXPERTS - 1).astype(jnp.int32)
    n_used = (pends[-1] // MOE_BLOCK_M).astype(jnp.int32).reshape(1)
    return (pends - padded).astype(jnp.int32), pends.astype(jnp.int32), padded.astype(jnp.int32), block_e, n_used


DISPATCH_TILE = 256
CAST_ROWS = 512
IDX_SLOTS = 3


def _dispatch_body(n_cast, pend_ref, padded_ref, dest_hbm, h_ref, w1_ref, w2_ref, xs_hbm, w1b_ref, w2b_ref,
                   idx_smem, zero_scr, idx_sem, row_sem, zero_sem):
    i = pl.program_id(0)
    n = pl.num_programs(0)

    def idx_copy(tile):
        s = lax.rem(tile, IDX_SLOTS)
        return pltpu.make_async_copy(dest_hbm.at[pl.ds(tile, 1), :], idx_smem.at[pl.ds(s, 1), :], idx_sem.at[s])

    def zero_copy(e):
        start = pl.multiple_of(pend_ref[e] - MOE_BLOCK_M, MOE_BLOCK_M)
        return pltpu.make_async_copy(zero_scr, xs_hbm.at[pl.ds(start, MOE_BLOCK_M), :], zero_sem)

    def row_copy(t, d):
        return pltpu.make_async_copy(h_ref.at[pl.ds(t, 1), :], xs_hbm.at[pl.ds(d, 1), :], row_sem)

    @pl.when(i == 0)
    def _():
        zero_scr[...] = jnp.zeros_like(zero_scr)
        for e in range(N_EXPERTS):
            @pl.when(padded_ref[e] > 0)
            def _():
                zero_copy(e).start()
        for e in range(N_EXPERTS):
            @pl.when(padded_ref[e] > 0)
            def _():
                zero_copy(e).wait()
        idx_copy(0).start()

    @pl.when(i + 1 < n)
    def _():
        idx_copy(i + 1).start()

    idx_copy(i).wait()
    s = lax.rem(i, IDX_SLOTS)

    def issue(t, c):
        for k in range(TOP_K):
            row_copy(t, idx_smem[s, t * TOP_K + k]).start(priority=k % 2)
        return c

    lax.fori_loop(0, DISPATCH_TILE, issue, 0, unroll=8)

    @pl.when(i < n_cast)
    def _():
        w1b_ref[...] = w1_ref[...].astype(BF16)
        w2b_ref[...] = w2_ref[...].astype(BF16)

    for _ in range(TOP_K):
        pltpu.make_async_copy(h_ref, h_ref, row_sem).wait()


def _scatter_rows(h, dest, pends, padded, n_slots, w1, w2):
    t = h.shape[0]
    tiles = t // DISPATCH_TILE
    w1_rows = w1.reshape(-1, w1.shape[-1])
    w2_rows = w2.reshape(-1, w2.shape[-1])
    n_cast = w1_rows.shape[0] // CAST_ROWS
    assert w2_rows.shape[0] == w1_rows.shape[0] and n_cast * CAST_ROWS == w1_rows.shape[0] and n_cast <= tiles
    cast_block = lambda i, pe, pd: (jnp.minimum(i, n_cast - 1), 0)
    grid_spec = pltpu.PrefetchScalarGridSpec(
        num_scalar_prefetch=2,
        grid=(tiles,),
        in_specs=[
            pl.BlockSpec(memory_space=pl.ANY),
            pl.BlockSpec((DISPATCH_TILE, D_MODEL), lambda i, pe, pd: (i, 0)),
            pl.BlockSpec((CAST_ROWS, w1_rows.shape[1]), cast_block),
            pl.BlockSpec((CAST_ROWS, w2_rows.shape[1]), cast_block),
        ],
        out_specs=[
            pl.BlockSpec(memory_space=pl.ANY),
            pl.BlockSpec((CAST_ROWS, w1_rows.shape[1]), cast_block),
            pl.BlockSpec((CAST_ROWS, w2_rows.shape[1]), cast_block),
        ],
        scratch_shapes=[
            pltpu.SMEM((IDX_SLOTS, DISPATCH_TILE * TOP_K), jnp.int32),
            pltpu.VMEM((MOE_BLOCK_M, D_MODEL), F32),
            pltpu.SemaphoreType.DMA((IDX_SLOTS,)),
            pltpu.SemaphoreType.DMA(()),
            pltpu.SemaphoreType.DMA(()),
        ],
    )
    xs, w1b, w2b = pl.pallas_call(
        functools.partial(_dispatch_body, n_cast),
        grid_spec=grid_spec,
        out_shape=[
            jax.ShapeDtypeStruct((n_slots, D_MODEL), F32),
            jax.ShapeDtypeStruct(w1_rows.shape, BF16),
            jax.ShapeDtypeStruct(w2_rows.shape, BF16),
        ],
        compiler_params=_params("arbitrary"),
        name="dispatch",
    )(pends, padded, dest.reshape(tiles, DISPATCH_TILE * TOP_K), h, w1_rows, w2_rows)
    return xs, w1b.reshape(w1.shape), w2b.reshape(w2.shape)


COMBINE_TILE = 256


def _combine_body(tile_off, dest_hbm, x1_ref, gate_ref, gf_ref, ys_hbm, y_ref, idx_smem, buf, idx_sem, row_sem):
    i = pl.program_id(0)
    n = pl.num_programs(0)

    def idx_copy(tile):
        s = lax.rem(tile, IDX_SLOTS)
        return pltpu.make_async_copy(dest_hbm.at[pl.ds(tile_off + tile, 1), :], idx_smem.at[pl.ds(s, 1), :],
                                     idx_sem.at[s])

    def row_copy(d, b, k, t):
        return pltpu.make_async_copy(ys_hbm.at[pl.ds(d, 1), :], buf.at[b, k, pl.ds(t, 1), :], row_sem.at[b])

    def issue_rows(tile):
        s = lax.rem(tile, IDX_SLOTS)
        b = lax.rem(tile, 2)

        def body(t, c):
            for k in range(TOP_K):
                row_copy(idx_smem[s, t * TOP_K + k], b, k, t).start(priority=k % 2)
            return c

        lax.fori_loop(0, COMBINE_TILE, body, 0, unroll=8)

    @pl.when(i == 0)
    def _():
        idx_copy(0).start()
        idx_copy(0).wait()
        issue_rows(0)

        @pl.when(n > 1)
        def _():
            idx_copy(1).start()

    @pl.when(i + 1 < n)
    def _():
        idx_copy(i + 1).wait()
        issue_rows(i + 1)

    @pl.when(i + 2 < n)
    def _():
        idx_copy(i + 2).start()

    b = lax.rem(i, 2)

    pltpu.make_async_copy(buf.at[b], buf.at[b], row_sem.at[b]).wait()
    gate = gate_ref[...]
    acc = x1_ref[...]
    for k in range(TOP_K):
        acc = acc + buf[b, k] * gate[:, k:k + 1]
    y_ref[...] = acc * lax.rsqrt(jnp.mean(acc * acc, axis=-1, keepdims=True) + RMS_EPS) * gf_ref[...]


def _combine(x1, gate, dest, ys, g_final, row_off, rows):
    t = x1.shape[0]
    tile_off = row_off // COMBINE_TILE
    return pl.pallas_call(
        functools.partial(_combine_body, tile_off),
        grid=(rows // COMBINE_TILE,),
        in_specs=[
            pl.BlockSpec(memory_space=pl.ANY),
            pl.BlockSpec((COMBINE_TILE, D_MODEL), lambda i: (tile_off + i, 0)),
            pl.BlockSpec((COMBINE_TILE, LANES), lambda i: (tile_off + i, 0)),
            pl.BlockSpec((1, D_MODEL), lambda i: (0, 0)),
            pl.BlockSpec(memory_space=pl.ANY),
        ],
        out_specs=pl.BlockSpec((COMBINE_TILE, D_MODEL), lambda i: (i, 0)),
        out_shape=jax.ShapeDtypeStruct((rows, D_MODEL), F32),
        scratch_shapes=[
            pltpu.SMEM((IDX_SLOTS, COMBINE_TILE * TOP_K), jnp.int32),
            pltpu.VMEM((2, TOP_K, COMBINE_TILE, D_MODEL), F32),
            pltpu.SemaphoreType.DMA((IDX_SLOTS,)),
            pltpu.SemaphoreType.DMA((2,)),
        ],
        compiler_params=_params("arbitrary"),
        name="combine",
    )(dest.reshape(t // COMBINE_TILE, COMBINE_TILE * TOP_K), x1, gate, g_final.reshape(1, D_MODEL), ys)


def _lane_row(v):
    return jnp.pad(v.astype(F32), (0, LANES - v.shape[0])).reshape(1, LANES)


def kernel(x_prompt, x_sample, cache_swa_k, cache_swa_v, state_mlstm_C, state_mlstm_n, state_mlstm_m,
           g_mix, w_in, b_if, attn_sinks, mlstm_norm_w, w_attn_proj, w_mlstm_proj, w_out,
           g_ffn, w_router, b_router, w_e1, b_e1, w_e2, b_e2, g_final):
    bp, sp = x_prompt.shape[0], x_prompt.shape[1]
    bs, ls = x_sample.shape[0], x_sample.shape[1]
    tp = bp * sp
    ts = bs * SAMPLE_ROWS
    t_all = tp + ts
    l = 0
    w_in_p = _permute_w_in(w_in[l])
    wa, wm, wo = w_attn_proj[l].astype(BF16), w_mlstm_proj[l].astype(BF16), w_out[l].astype(BF16)
    wr_f32 = jnp.pad(w_router[l], ((0, 0), (0, ROUTER_PAD - N_EXPERTS)))
    wr_hi = wr_f32.astype(BF16)
    wr_pad = jnp.concatenate([wr_hi, (wr_f32 - wr_hi.astype(F32)).astype(BF16)], axis=1)
    br_pad = _lane_row(b_router[l])
    b1, b2 = b_e1[l].reshape(N_EXPERTS, 1, 2 * D_FF), b_e2[l].reshape(N_EXPERTS, 1, D_MODEL)
    sinks_row = _lane_row(attn_sinks[l])
    bias_row = _lane_row(b_if[l])
    norm_w_row = mlstm_norm_w[l].reshape(1, M_V_WIDTH)

    xp = x_prompt.reshape(tp, D_MODEL)
    xs = jnp.pad(x_sample, ((0, 0), (0, SAMPLE_ROWS - ls), (0, 0))).reshape(ts, D_MODEL)
    z_all = _inproj(xp, xs, g_mix[l], w_in_p, 512)

    a_p, k1, v1 = _attn_prompt(z_all, sinks_row, bp, sp)
    a_s, k2, v2 = _attn_sample(z_all, sinks_row, cache_swa_k[l].reshape(bs, WINDOW, KV_WIDTH),
                               cache_swa_v[l].reshape(bs, WINDOW, KV_WIDTH), tp, ls)
    mh_p, c1, n1, m1 = _mlstm_prompt(z_all, bias_row, norm_w_row, bp, sp)
    mh_s, c2, n2, m2 = _mlstm_sample(z_all, bias_row, norm_w_row, state_mlstm_C[l], state_mlstm_n[l],
                                     state_mlstm_m[l], tp, ls)

    x1, h2, logits = _merge(xp, xs, a_p, a_s, mh_p, mh_s, z_all, wa, wm, wo, g_ffn[l], wr_pad, br_pad, 256)

    meta, gate, counts = _route(logits)
    n_blocks = -(-t_all * TOP_K // MOE_BLOCK_M) + N_EXPERTS
    pstarts, pends, padded, block_e, n_used = _slot_tables(counts[0, :N_EXPERTS].astype(jnp.int32), n_blocks)
    dest = pstarts[meta[:, LANE_E:LANE_E + TOP_K]] + meta[:, LANE_RANK:LANE_RANK + TOP_K]
    xs_sorted, w1, w2 = _scatter_rows(h2, dest, pends, padded, n_blocks * MOE_BLOCK_M, w_e1[l], w_e2[l])
    ys = _experts(block_e, n_used, xs_sorted, w1, b1, w2, b2)
    y_prompt = _combine(x1, gate, dest, ys, g_final, 0, tp).reshape(bp, sp, D_MODEL)
    y_sample = _combine(x1, gate, dest, ys, g_final, tp, ts).reshape(bs, SAMPLE_ROWS, D_MODEL)[:, :ls]
    k1 = k1.reshape(bp, WINDOW, N_KV_HEADS, HEAD_DIM)
    v1 = v1.reshape(bp, WINDOW, N_KV_HEADS, HEAD_DIM)
    k2 = k2.reshape(bs, WINDOW, N_KV_HEADS, HEAD_DIM)
    v2 = v2.reshape(bs, WINDOW, N_KV_HEADS, HEAD_DIM)
    m1 = m1[:, :, 0]
    st = lambda t: t[None]
    return (y_prompt, y_sample, st(k1), st(v1), st(c1), st(n1), st(m1), st(k2), st(v2), st(c2), st(n2), st(m2))
```

```python
import functools

import jax
import jax.numpy as jnp
from jax import lax
from jax.experimental import pallas as pl
from jax.experimental.pallas import tpu as pltpu

D_MODEL = 2048
N_HEADS = 16
N_KV_HEADS = 2
HEAD_DIM = 64
Q_PER_KV = N_HEADS // N_KV_HEADS
WINDOW = 128
ATTN_WIDTH = N_HEADS * HEAD_DIM
KV_WIDTH = N_KV_HEADS * HEAD_DIM
M_HEADS = 4
M_QK_DIM = 128
M_V_DIM = 256
M_QK_WIDTH = M_HEADS * M_QK_DIM
M_V_WIDTH = M_HEADS * M_V_DIM
M_CHUNK = 64
N_EXPERTS = 32
TOP_K = 4
D_FF = D_MODEL
SWIGLU_LIMIT = 7.0
SWIGLU_ALPHA = 1.702
RMS_EPS = 1e-5
NEG = -1e30

F32 = jnp.float32
BF16 = jnp.bfloat16

LANES = 128
SUBLANES = 8
VMEM_LIMIT_BYTES = 56 * 1024 * 1024

COL_GZ = 0
COL_AQ = COL_GZ + 2 * D_MODEL
COL_MV = COL_AQ + ATTN_WIDTH
COL_MO = COL_MV + M_V_WIDTH
COL_MQ = COL_MO + M_V_WIDTH
COL_MK = COL_MQ + M_QK_WIDTH
COL_AK = COL_MK + M_QK_WIDTH
COL_IF = COL_AK + 2 * KV_WIDTH
IN_TILE_N = 2176
IN_WIDTH_PAD = -(-(COL_IF + LANES) // IN_TILE_N) * IN_TILE_N

SAMPLE_ROWS = SUBLANES
MOE_BLOCK_M = 512
MOE_FF_CHUNK = 1024
ROUTER_PAD = LANES


def _permute_w_in(w_in):
    splits = (ATTN_WIDTH, KV_WIDTH, KV_WIDTH, M_QK_WIDTH, M_QK_WIDTH, M_V_WIDTH, M_V_WIDTH, M_HEADS, M_HEADS,
              2 * D_MODEL)
    parts, off = [], 0
    for width in splits:
        parts.append(w_in[:, off:off + width])
        off += width
    aq, ak, av, mq, mk, mv, mo, mi, mf, gz = parts
    cols = jnp.concatenate([gz, aq, mv, mo, mq, mk, ak, av, mi, mf], axis=1)
    cols = jnp.pad(cols, ((0, 0), (0, IN_WIDTH_PAD - cols.shape[1])))
    return cols.astype(BF16)


def _params(*semantics):
    return pltpu.CompilerParams(dimension_semantics=semantics, vmem_limit_bytes=VMEM_LIMIT_BYTES)


def _log_sigmoid(x):
    return jnp.minimum(x, 0.0) - jnp.log1p(jnp.exp(-jnp.abs(x)))


def _inproj_body(tiles_a, xa_ref, xb_ref, g_ref, w_ref, z_ref, h_scr):
    def normalise(x_ref):
        x = x_ref[...]
        y = x * lax.rsqrt(jnp.mean(x * x, axis=-1, keepdims=True) + RMS_EPS)
        h_scr[...] = (y * g_ref[...]).astype(BF16)

    first = pl.program_id(1) == 0
    in_a = pl.program_id(0) < tiles_a

    @pl.when(first & in_a)
    def _():
        normalise(xa_ref)

    @pl.when(first & jnp.logical_not(in_a))
    def _():
        normalise(xb_ref)

    z_ref[...] = jnp.dot(h_scr[...], w_ref[...], preferred_element_type=F32)


def _inproj(x_a, x_b, g, w_bf16, tile_m):
    tiles_a, tiles_b = x_a.shape[0] // tile_m, x_b.shape[0] // tile_m
    n = w_bf16.shape[1]
    return pl.pallas_call(
        functools.partial(_inproj_body, tiles_a),
        grid=(tiles_a + tiles_b, n // IN_TILE_N),
        in_specs=[
            pl.BlockSpec((tile_m, D_MODEL), lambda i, j: (jnp.minimum(i, tiles_a - 1), 0)),
            pl.BlockSpec((tile_m, D_MODEL), lambda i, j: (jnp.maximum(i - tiles_a, 0), 0)),
            pl.BlockSpec((1, D_MODEL), lambda i, j: (0, 0)),
            pl.BlockSpec((D_MODEL, IN_TILE_N), lambda i, j: (0, j)),
        ],
        out_specs=pl.BlockSpec((tile_m, IN_TILE_N), lambda i, j: (i, j)),
        out_shape=jax.ShapeDtypeStruct(((tiles_a + tiles_b) * tile_m, n), F32),
        scratch_shapes=[pltpu.VMEM((tile_m, D_MODEL), BF16)],
        compiler_params=_params("arbitrary", "arbitrary"),
        name="inproj",
    )(x_a, x_b, g.reshape(1, D_MODEL), w_bf16)


assert KV_WIDTH == LANES and N_KV_HEADS == 2 and LANES == 2 * HEAD_DIM
Q_TILES = ATTN_WIDTH // LANES
TILES_PER_KV = Q_TILES // N_KV_HEADS
_NT = (((1,), (1,)), ((), ()))


def _kv_placements(x):
    lo = lax.broadcasted_iota(jnp.int32, x.shape, 1) < HEAD_DIM
    swapped = pltpu.roll(x, HEAD_DIM, axis=1)
    return ((jnp.where(lo, x, 0.0).astype(BF16), jnp.where(lo, 0.0, swapped).astype(BF16)),
            (jnp.where(lo, swapped, 0.0).astype(BF16), jnp.where(lo, 0.0, x).astype(BF16)))


def _slab_head(g, p, j):
    return g * Q_PER_KV + 2 * j + p


def _attn_prompt_body(q_ref, kvc_ref, kvp_ref, sink_ref, a_ref, k_ref, v_ref):
    n = pl.program_id(1)
    kc, vc = kvc_ref[:, :KV_WIDTH], kvc_ref[:, KV_WIDTH:]
    k_ref[...] = kc
    v_ref[...] = vc
    kk = _kv_placements(jnp.concatenate([kvp_ref[:, :KV_WIDTH], kc], axis=0))
    vv = _kv_placements(jnp.concatenate([kvp_ref[:, KV_WIDTH:], vc], axis=0))
    q = q_ref[...].astype(BF16)
    row = lax.broadcasted_iota(jnp.int32, (WINDOW, WINDOW), 0)
    col = lax.broadcasted_iota(jnp.int32, (WINDOW, WINDOW), 1)
    mask = jnp.concatenate([(col > row) & (n > 0), col <= row], axis=1)
    slabs, sinks = [], []
    for g in range(N_KV_HEADS):
        qt = jnp.concatenate([q[:, (TILES_PER_KV * g + j) * LANES:(TILES_PER_KV * g + j + 1) * LANES]
                              for j in range(TILES_PER_KV)], axis=0)
        for p in range(2):
            s = lax.dot_general(qt, kk[g][p], _NT, preferred_element_type=F32) * (HEAD_DIM ** -0.5)
            for j in range(TILES_PER_KV):
                h = _slab_head(g, p, j)
                slabs.append(jnp.where(mask, s[j * WINDOW:(j + 1) * WINDOW], NEG))
                sinks.append(jnp.broadcast_to(sink_ref[:, h:h + 1], (WINDOW, LANES)))
    s_all = jnp.concatenate(slabs, axis=0)
    sink = jnp.concatenate(sinks, axis=0)
    m = jnp.maximum(jnp.broadcast_to(jnp.max(s_all, axis=1, keepdims=True), sink.shape), sink)
    p_all = jnp.exp(s_all - jnp.concatenate([m, m], axis=1))
    p_hi = p_all.astype(BF16)
    p_lo = (p_all - p_hi.astype(F32)).astype(BF16)
    ones = jnp.ones((2 * WINDOW, LANES), BF16)
    denom = (jnp.dot(p_hi, ones, preferred_element_type=F32) + jnp.dot(p_lo, ones, preferred_element_type=F32)
             + jnp.exp(sink - m))
    inv = 1.0 / denom
    pn = (p_all * jnp.concatenate([inv, inv], axis=1)).astype(BF16)
    half = TILES_PER_KV * WINDOW
    for g in range(N_KV_HEADS):
        base = 2 * half * g
        o = (jnp.dot(pn[base:base + half], vv[g][0], preferred_element_type=F32)
             + jnp.dot(pn[base + half:base + 2 * half], vv[g][1], preferred_element_type=F32))
        for j in range(TILES_PER_KV):
            t = TILES_PER_KV * g + j
            a_ref[:, t * LANES:(t + 1) * LANES] = o[j * WINDOW:(j + 1) * WINDOW].astype(BF16)


def _attn_prompt(z_all, sinks_row, batch, seq):
    nb = seq // WINDOW
    return pl.pallas_call(
        _attn_prompt_body,
        grid=(batch, nb),
        in_specs=[
            pl.BlockSpec((WINDOW, ATTN_WIDTH), lambda b, n: (b * nb + n, COL_AQ // ATTN_WIDTH)),
            pl.BlockSpec((WINDOW, 2 * KV_WIDTH), lambda b, n: (b * nb + n, COL_AK // (2 * KV_WIDTH))),
            pl.BlockSpec((WINDOW, 2 * KV_WIDTH),
                         lambda b, n: (b * nb + jnp.maximum(n - 1, 0), COL_AK // (2 * KV_WIDTH))),
            pl.BlockSpec((1, LANES), lambda b, n: (0, 0)),
        ],
        out_specs=[
            pl.BlockSpec((WINDOW, ATTN_WIDTH), lambda b, n: (b * nb + n, 0)),
            pl.BlockSpec((None, WINDOW, KV_WIDTH), lambda b, n: (b, 0, 0)),
            pl.BlockSpec((None, WINDOW, KV_WIDTH), lambda b, n: (b, 0, 0)),
        ],
        out_shape=[
            jax.ShapeDtypeStruct((batch * seq, ATTN_WIDTH), BF16),
            jax.ShapeDtypeStruct((batch, WINDOW, KV_WIDTH), F32),
            jax.ShapeDtypeStruct((batch, WINDOW, KV_WIDTH), F32),
        ],
        compiler_params=_params("arbitrary", "arbitrary"),
        name="attn_prompt",
    )(z_all, z_all, z_all, sinks_row)


ATTN_SAMPLE_GROUP = 8


def _attn_sample_body(n_valid, q_ref, kv_ref, ck_ref, cv_ref, sink_ref, a_ref, nk_ref, nv_ref, o_scr):
    rs = SAMPLE_ROWS
    slab = TILES_PER_KV * rs
    r8 = lax.broadcasted_iota(jnp.int32, (rs, KV_WIDTH), 0)
    qi_c = jnp.bitwise_and(lax.broadcasted_iota(jnp.int32, (slab, WINDOW), 0), rs - 1)
    mask_c = lax.broadcasted_iota(jnp.int32, (slab, WINDOW), 1) > qi_c
    qi_n = jnp.bitwise_and(lax.broadcasted_iota(jnp.int32, (slab, rs), 0), rs - 1)
    mask_n = lax.broadcasted_iota(jnp.int32, (slab, rs), 1) <= qi_n
    sc_list, sn_list, sink_list, values = [], [], [], []
    for b in range(ATTN_SAMPLE_GROUP):
        rows = slice(b * rs, (b + 1) * rs)
        q = q_ref[rows, :]
        kn, vn = kv_ref[rows, :KV_WIDTH], kv_ref[rows, KV_WIDTH:]
        ck, cv = ck_ref[b], cv_ref[b]
        for src, new, dst in ((ck, kn, nk_ref), (cv, vn, nv_ref)):
            rolled = pltpu.roll(src, WINDOW - n_valid, axis=0)
            dst[b] = rolled
            tail = jnp.where(r8 >= rs - n_valid, pltpu.roll(new, rs - n_valid, axis=0), rolled[WINDOW - rs:, :])
            dst[b, WINDOW - rs:, :] = tail
        kk_c, kk_n = _kv_placements(ck), _kv_placements(kn)
        values.append((_kv_placements(cv), _kv_placements(vn)))
        for g in range(N_KV_HEADS):
            qt = jnp.concatenate([q[:, (TILES_PER_KV * g + j) * LANES:(TILES_PER_KV * g + j + 1) * LANES]
                                  for j in range(TILES_PER_KV)], axis=0).astype(BF16)
            for p in range(2):
                s_c = lax.dot_general(qt, kk_c[g][p], _NT, preferred_element_type=F32) * (HEAD_DIM ** -0.5)
                s_n = lax.dot_general(qt, kk_n[g][p], _NT, preferred_element_type=F32) * (HEAD_DIM ** -0.5)
                sc_list.append(jnp.where(mask_c, s_c, NEG))
                sn_list.append(jnp.where(mask_n, s_n, NEG))
                for j in range(TILES_PER_KV):
                    h = _slab_head(g, p, j)
                    sink_list.append(jnp.broadcast_to(sink_ref[:, h:h + 1], (rs, 1)))
    s_c = jnp.concatenate(sc_list, axis=0)
    s_n = jnp.concatenate(sn_list, axis=0)
    sink = jnp.concatenate(sink_list, axis=0)
    m = jnp.maximum(jnp.maximum(jnp.max(s_c, axis=1, keepdims=True), jnp.max(s_n, axis=1, keepdims=True)), sink)
    p_c = jnp.exp(s_c - m)
    p_n = jnp.exp(s_n - m)
    inv = 1.0 / (jnp.sum(p_c, axis=1, keepdims=True) + jnp.sum(p_n, axis=1, keepdims=True) + jnp.exp(sink - m))
    p_c = (p_c * inv).astype(BF16)
    p_n = (p_n * inv).astype(BF16)
    for b in range(ATTN_SAMPLE_GROUP):
        vv_c, vv_n = values[b]
        for g in range(N_KV_HEADS):
            base = (b * N_KV_HEADS + g) * 2 * slab
            o = (jnp.dot(p_c[base:base + slab], vv_c[g][0], preferred_element_type=F32)
                 + jnp.dot(p_c[base + slab:base + 2 * slab], vv_c[g][1], preferred_element_type=F32)
                 + jnp.dot(p_n[base:base + slab], vv_n[g][0], preferred_element_type=F32)
                 + jnp.dot(p_n[base + slab:base + 2 * slab], vv_n[g][1], preferred_element_type=F32))
            for j in range(TILES_PER_KV):
                t = TILES_PER_KV * g + j
                o_scr[b * rs:(b + 1) * rs, t * LANES:(t + 1) * LANES] = o[j * rs:(j + 1) * rs]
    a_ref[...] = o_scr[...].astype(BF16)


def _attn_sample(z_all, sinks_row, cache_k, cache_v, row_off, n_valid):
    b = cache_k.shape[0]
    gr = ATTN_SAMPLE_GROUP
    rows = gr * SAMPLE_ROWS
    rb = row_off // rows
    return pl.pallas_call(
        functools.partial(_attn_sample_body, n_valid),
        grid=(b // gr,),
        in_specs=[
            pl.BlockSpec((rows, ATTN_WIDTH), lambda i: (rb + i, COL_AQ // ATTN_WIDTH)),
            pl.BlockSpec((rows, 2 * KV_WIDTH), lambda i: (rb + i, COL_AK // (2 * KV_WIDTH))),
            pl.BlockSpec((gr, WINDOW, KV_WIDTH), lambda i: (i, 0, 0)),
            pl.BlockSpec((gr, WINDOW, KV_WIDTH), lambda i: (i, 0, 0)),
            pl.BlockSpec((1, LANES), lambda i: (0, 0)),
        ],
        out_specs=[
            pl.BlockSpec((rows, ATTN_WIDTH), lambda i: (i, 0)),
            pl.BlockSpec((gr, WINDOW, KV_WIDTH), lambda i: (i, 0, 0)),
            pl.BlockSpec((gr, WINDOW, KV_WIDTH), lambda i: (i, 0, 0)),
        ],
        out_shape=[
            jax.ShapeDtypeStruct((b * SAMPLE_ROWS, ATTN_WIDTH), BF16),
            jax.ShapeDtypeStruct(cache_k.shape, F32),
            jax.ShapeDtypeStruct(cache_v.shape, F32),
        ],
        scratch_shapes=[pltpu.VMEM((rows, ATTN_WIDTH), F32)],
        compiler_params=_params("arbitrary"),
        name="attn_sample",
    )(z_all, z_all, cache_k, cache_v, sinks_row)


def _mlstm_gates(if_blk, bias_row, n_valid):
    length = if_blk.shape[0]
    g = if_blk + bias_row
    lane = lax.broadcasted_iota(jnp.int32, g.shape, 1)
    g = jnp.where(lane < M_HEADS, g, _log_sigmoid(g))
    if n_valid < length:
        row = lax.broadcasted_iota(jnp.int32, g.shape, 0)
        g = jnp.where(row < n_valid, g, jnp.where(lane < M_HEADS, NEG, 0.0))
    return g


def _mlstm_chunk(q, k, v, gates, gates_t, h, c, n, m):
    length = q.shape[0]
    i_col = gates[:, h:h + 1]
    f_col = gates[:, M_HEADS + h:M_HEADS + h + 1]
    i_row = gates_t[h:h + 1, :]
    f_row = gates_t[M_HEADS + h:M_HEADS + h + 1, :]
    t_idx = lax.broadcasted_iota(jnp.int32, (length, length), 0)
    s_idx = lax.broadcasted_iota(jnp.int32, (length, length), 1)
    causal = s_idx <= t_idx
    bcum_col = jnp.sum(jnp.where(causal, f_row, 0.0), axis=1, keepdims=True)
    bcum_row = jnp.sum(jnp.where(t_idx <= s_idx, f_col, 0.0), axis=0, keepdims=True)
    dlog = jnp.where(causal, bcum_col - bcum_row + i_row, NEG)
    inter = bcum_col + m
    mt = jnp.maximum(inter, jnp.max(dlog, axis=1, keepdims=True))
    w_inter = jnp.exp(inter - mt)
    qb = q.astype(BF16)
    ks = k * (M_QK_DIM ** -0.5)
    vb = v.astype(BF16)
    qk = lax.dot_general(qb, ks.astype(BF16), (((1,), (1,)), ((), ())), preferred_element_type=F32)
    qk = qk * jnp.exp(dlog - mt)
    num = (w_inter * jnp.dot(qb, c.astype(BF16), preferred_element_type=F32)
           + jnp.dot(qk.astype(BF16), vb, preferred_element_type=F32))
    den = w_inter * jnp.sum(q * n, axis=1, keepdims=True) + jnp.sum(qk, axis=1, keepdims=True)
    h_t = num / jnp.maximum(jnp.abs(den), jnp.exp(-mt))
    m_new = mt[length - 1:length, :]
    w_state = jnp.exp(bcum_col[length - 1:length, :] - bcum_col + i_col - m_new)
    decay = jnp.exp(inter[length - 1:length, :] - m_new)
    kw = ks * w_state
    c_new = decay * c + jnp.dot(kw.T.astype(BF16), vb, preferred_element_type=F32)
    n_new = decay * n + jnp.sum(kw, axis=0, keepdims=True)
    return h_t, c_new, n_new, m_new


def _mlstm_head_out(h_t, norm_w, o_gate):
    hs = h_t * lax.rsqrt(jnp.mean(h_t * h_t, axis=-1, keepdims=True) + RMS_EPS)
    return hs * norm_w * jax.nn.sigmoid(o_gate)


MLSTM_SEG = 512


MLSTM_BATCH = 2
MLSTM_PROMPT_CHUNK = 256


def _mlstm_prompt_body(*refs):
    nb = MLSTM_BATCH
    ins, rest = refs[:5 * nb], refs[5 * nb:]
    bias_ref, nw_ref, mh_ref, c_out, n_out, m_out, c_scr, n_scr, m_scr = rest
    seg = pl.program_id(1)

    @pl.when(seg == 0)
    def _():
        c_scr[...] = jnp.zeros_like(c_scr)
        n_scr[...] = jnp.zeros_like(n_scr)
        m_scr[...] = jnp.full_like(m_scr, NEG)

    def chunk_body(ci, carry):
        r = pl.multiple_of(ci * MLSTM_PROMPT_CHUNK, MLSTM_PROMPT_CHUNK)
        rows = pl.ds(r, MLSTM_PROMPT_CHUNK)
        state = [[(c_scr[u, h], n_scr[u, h:h + 1, :], m_scr[u, h:h + 1, 0:1]) for h in range(M_HEADS)]
                 for u in range(nb)]
        results = []
        for u in range(nb):
            q_ref, k_ref, v_ref, o_ref, if_ref = ins[5 * u:5 * u + 5]
            gates = _mlstm_gates(if_ref[rows, :], bias_ref[...], MLSTM_PROMPT_CHUNK)
            gates_t = gates.T
            for h in range(M_HEADS):
                qs = slice(h * M_QK_DIM, (h + 1) * M_QK_DIM)
                vs = slice(h * M_V_DIM, (h + 1) * M_V_DIM)
                h_t, c_new, n_new, m_new = _mlstm_chunk(
                    q_ref[rows, qs], k_ref[rows, qs], v_ref[rows, vs], gates, gates_t, h, *state[u][h])
                out = _mlstm_head_out(h_t, nw_ref[:, vs], o_ref[rows, vs]).astype(BF16)
                results.append((u, h, vs, out, c_new, n_new, m_new))
        for u, h, vs, out, c_new, n_new, m_new in results:
            c_scr[u, h] = c_new
            n_scr[u, h:h + 1, :] = n_new
            m_scr[u, h:h + 1, :] = jnp.broadcast_to(m_new, (1, LANES))
            mh_ref[u, rows, vs] = out
        return carry

    lax.fori_loop(0, MLSTM_SEG // MLSTM_PROMPT_CHUNK, chunk_body, 0)

    @pl.when(seg == pl.num_programs(1) - 1)
    def _():
        c_out[...] = c_scr[...]
        n_out[...] = n_scr[:, 0:M_HEADS, :]
        m_out[...] = m_scr[:, 0:M_HEADS, :]


def _mlstm_prompt(z_all, bias_row, norm_w_row, batch, seq):
    ns = seq // MLSTM_SEG
    nb = MLSTM_BATCH
    in_specs, args = [], []
    for u in range(nb):
        row = lambda p, s, u=u: (p * nb + u) * ns + s
        in_specs += [
            pl.BlockSpec((MLSTM_SEG, M_QK_WIDTH), lambda p, s, row=row: (row(p, s), COL_MQ // M_QK_WIDTH)),
            pl.BlockSpec((MLSTM_SEG, M_QK_WIDTH), lambda p, s, row=row: (row(p, s), COL_MK // M_QK_WIDTH)),
            pl.BlockSpec((MLSTM_SEG, M_V_WIDTH), lambda p, s, row=row: (row(p, s), COL_MV // M_V_WIDTH)),
            pl.BlockSpec((MLSTM_SEG, M_V_WIDTH), lambda p, s, row=row: (row(p, s), COL_MO // M_V_WIDTH)),
            pl.BlockSpec((MLSTM_SEG, LANES), lambda p, s, row=row: (row(p, s), COL_IF // LANES)),
        ]
        args += [z_all] * 5
    in_specs += [pl.BlockSpec((1, LANES), lambda p, s: (0, 0)), pl.BlockSpec((1, M_V_WIDTH), lambda p, s: (0, 0))]
    mh, c, n, m = pl.pallas_call(
        _mlstm_prompt_body,
        grid=(batch // nb, ns),
        in_specs=in_specs,
        out_specs=[
            pl.BlockSpec((None, nb, MLSTM_SEG, M_V_WIDTH), lambda p, s: (p, 0, s, 0)),
            pl.BlockSpec((nb, M_HEADS, M_QK_DIM, M_V_DIM), lambda p, s: (p, 0, 0, 0)),
            pl.BlockSpec((nb, M_HEADS, M_QK_DIM), lambda p, s: (p, 0, 0)),
            pl.BlockSpec((nb, M_HEADS, LANES), lambda p, s: (p, 0, 0)),
        ],
        out_shape=[
            jax.ShapeDtypeStruct((batch // nb, nb, seq, M_V_WIDTH), BF16),
            jax.ShapeDtypeStruct((batch, M_HEADS, M_QK_DIM, M_V_DIM), F32),
            jax.ShapeDtypeStruct((batch, M_HEADS, M_QK_DIM), F32),
            jax.ShapeDtypeStruct((batch, M_HEADS, LANES), F32),
        ],
        scratch_shapes=[
            pltpu.VMEM((nb, M_HEADS, M_QK_DIM, M_V_DIM), F32),
            pltpu.VMEM((nb, SUBLANES, M_QK_DIM), F32),
            pltpu.VMEM((nb, SUBLANES, LANES), F32),
        ],
        compiler_params=_params("arbitrary", "arbitrary"),
        name="mlstm_prompt",
    )(*args, bias_row, norm_w_row)
    return mh.reshape(batch * seq, M_V_WIDTH), c, n, m


MLSTM_SAMPLE_GROUP = 2


def _mlstm_sample_body(n_valid, q_ref, k_ref, v_ref, o_ref, if_ref, bias_ref, nw_ref, c0_ref, n0_ref, m0_ref,
                       mh_ref, c_out, n_out, m_out, h_scr):
    for g in range(MLSTM_SAMPLE_GROUP):
        rows = slice(g * SAMPLE_ROWS, (g + 1) * SAMPLE_ROWS)
        gates = _mlstm_gates(if_ref[rows, :], bias_ref[...], n_valid)
        gates_t = jnp.transpose(jnp.concatenate([gates] * (LANES // SAMPLE_ROWS), axis=0))[:, :SAMPLE_ROWS]
        for h in range(M_HEADS):
            qs = slice(h * M_QK_DIM, (h + 1) * M_QK_DIM)
            vs = slice(h * M_V_DIM, (h + 1) * M_V_DIM)
            h_t, c_new, n_new, m_new = _mlstm_chunk(
                q_ref[rows, qs], k_ref[rows, qs], v_ref[rows, vs], gates, gates_t, h,
                c0_ref[g, h], n0_ref[g, h:h + 1, :], m0_ref[g:g + 1, h:h + 1])
            c_out[g, h] = c_new
            n_out[g, h:h + 1, :] = n_new
            m_out[g:g + 1, h:h + 1] = m_new
            h_scr[rows, vs] = _mlstm_head_out(h_t, nw_ref[:, vs], o_ref[rows, vs])
    mh_ref[...] = h_scr[...].astype(BF16)


def _mlstm_sample(z_all, bias_row, norm_w_row, c0, n0, m0, row_off, n_valid):
    b = c0.shape[0]
    gr = MLSTM_SAMPLE_GROUP
    rows = gr * SAMPLE_ROWS
    rb = row_off // rows
    m0g = m0.reshape(b // gr, gr, M_HEADS)
    outs = pl.pallas_call(
        functools.partial(_mlstm_sample_body, n_valid),
        grid=(b // gr,),
        in_specs=[
            pl.BlockSpec((rows, M_QK_WIDTH), lambda i: (rb + i, COL_MQ // M_QK_WIDTH)),
            pl.BlockSpec((rows, M_QK_WIDTH), lambda i: (rb + i, COL_MK // M_QK_WIDTH)),
            pl.BlockSpec((rows, M_V_WIDTH), lambda i: (rb + i, COL_MV // M_V_WIDTH)),
            pl.BlockSpec((rows, M_V_WIDTH), lambda i: (rb + i, COL_MO // M_V_WIDTH)),
            pl.BlockSpec((rows, LANES), lambda i: (rb + i, COL_IF // LANES)),
            pl.BlockSpec((1, LANES), lambda i: (0, 0)),
            pl.BlockSpec((1, M_V_WIDTH), lambda i: (0, 0)),
            pl.BlockSpec((gr, M_HEADS, M_QK_DIM, M_V_DIM), lambda i: (i, 0, 0, 0)),
            pl.BlockSpec((gr, M_HEADS, M_QK_DIM), lambda i: (i, 0, 0)),
            pl.BlockSpec((None, gr, M_HEADS), lambda i: (i, 0, 0)),
        ],
        out_specs=[
            pl.BlockSpec((rows, M_V_WIDTH), lambda i: (i, 0)),
            pl.BlockSpec((gr, M_HEADS, M_QK_DIM, M_V_DIM), lambda i: (i, 0, 0, 0)),
            pl.BlockSpec((gr, M_HEADS, M_QK_DIM), lambda i: (i, 0, 0)),
            pl.BlockSpec((None, gr, M_HEADS), lambda i: (i, 0, 0)),
        ],
        out_shape=[
            jax.ShapeDtypeStruct((b * SAMPLE_ROWS, M_V_WIDTH), BF16),
            jax.ShapeDtypeStruct(c0.shape, F32),
            jax.ShapeDtypeStruct(n0.shape, F32),
            jax.ShapeDtypeStruct(m0g.shape, F32),
        ],
        scratch_shapes=[pltpu.VMEM((rows, M_V_WIDTH), F32)],
        compiler_params=_params("arbitrary"),
        name="mlstm_sample",
    )(z_all, z_all, z_all, z_all, z_all, bias_row, norm_w_row, c0, n0, m0g)
    mh, c, n, m = outs
    return mh, c, n, m.reshape(b, M_HEADS)


def _merge_body(tiles_a, xa_ref, xb_ref, aa_ref, ab_ref, ma_ref, mb_ref, gza_ref, gzm_ref, wa_ref, wm_ref, wo_ref,
                gf_ref, wr_ref, br_ref, x1_ref, h2_ref, lg_ref):
    in_a = pl.program_id(0) < tiles_a
    x = jnp.where(in_a, xa_ref[...], xb_ref[...])
    a = jnp.where(in_a, aa_ref[...], ab_ref[...])
    mh = jnp.where(in_a, ma_ref[...], mb_ref[...])
    pa = jnp.dot(a, wa_ref[...], preferred_element_type=F32)
    pm = jnp.dot(mh, wm_ref[...], preferred_element_type=F32)
    merged = jax.nn.sigmoid(gza_ref[...]) * pa + jax.nn.sigmoid(gzm_ref[...]) * pm
    x1 = x + jnp.dot(merged.astype(BF16), wo_ref[...], preferred_element_type=F32)
    x1_ref[...] = x1
    h2 = x1 * lax.rsqrt(jnp.mean(x1 * x1, axis=-1, keepdims=True) + RMS_EPS) * gf_ref[...]
    h2_ref[...] = h2
    hi = h2.astype(BF16)
    lo = (h2 - hi.astype(F32)).astype(BF16)
    both = jnp.dot(hi, wr_ref[...], preferred_element_type=F32)
    lg_ref[...] = (both[:, :ROUTER_PAD] + both[:, ROUTER_PAD:]
                   + jnp.dot(lo, wr_ref[:, :ROUTER_PAD], preferred_element_type=F32) + br_ref[...])


def _merge(x_a, x_b, a_a, a_b, mh_a, mh_b, z_all, wa, wm, wo, g_ffn, wr_pad, br_pad, tile_m):
    tiles_a, tiles_b = x_a.shape[0] // tile_m, x_b.shape[0] // tile_m
    t_all = z_all.shape[0]
    const = lambda i: (0, 0)
    from_a = lambda i: (jnp.minimum(i, tiles_a - 1), 0)
    from_b = lambda i: (jnp.maximum(i - tiles_a, 0), 0)
    return pl.pallas_call(
        functools.partial(_merge_body, tiles_a),
        grid=(tiles_a + tiles_b,),
        in_specs=[
            pl.BlockSpec((tile_m, D_MODEL), from_a),
            pl.BlockSpec((tile_m, D_MODEL), from_b),
            pl.BlockSpec((tile_m, ATTN_WIDTH), from_a),
            pl.BlockSpec((tile_m, ATTN_WIDTH), from_b),
            pl.BlockSpec((tile_m, M_V_WIDTH), from_a),
            pl.BlockSpec((tile_m, M_V_WIDTH), from_b),
            pl.BlockSpec((tile_m, D_MODEL), lambda i: (i, COL_GZ // D_MODEL)),
            pl.BlockSpec((tile_m, D_MODEL), lambda i: (i, COL_GZ // D_MODEL + 1)),
            pl.BlockSpec((ATTN_WIDTH, D_MODEL), const, pipeline_mode=pl.Buffered(1)),
            pl.BlockSpec((M_V_WIDTH, D_MODEL), const, pipeline_mode=pl.Buffered(1)),
            pl.BlockSpec((D_MODEL, D_MODEL), const, pipeline_mode=pl.Buffered(1)),
            pl.BlockSpec((1, D_MODEL), const),
            pl.BlockSpec((D_MODEL, 2 * ROUTER_PAD), const),
            pl.BlockSpec((1, ROUTER_PAD), const),
        ],
        out_specs=[
            pl.BlockSpec((tile_m, D_MODEL), lambda i: (i, 0)),
            pl.BlockSpec((tile_m, D_MODEL), lambda i: (i, 0)),
            pl.BlockSpec((tile_m, ROUTER_PAD), lambda i: (i, 0)),
        ],
        out_shape=[
            jax.ShapeDtypeStruct((t_all, D_MODEL), F32),
            jax.ShapeDtypeStruct((t_all, D_MODEL), F32),
            jax.ShapeDtypeStruct((t_all, ROUTER_PAD), F32),
        ],
        compiler_params=_params("arbitrary"),
        name="merge",
    )(x_a, x_b, a_a, a_b, mh_a, mh_b, z_all, z_all, wa, wm, wo, g_ffn.reshape(1, D_MODEL), wr_pad, br_pad)


def _expert_body(be_ref, nb_ref, xs_ref, w1g_ref, w1u_ref, b1g_ref, b1u_ref, w2_ref, b2_ref, y_ref):
    i = pl.program_id(0)
    c = pl.program_id(1)
    used = i < nb_ref[0]

    @pl.when(used)
    def _():
        x = xs_ref[...].astype(BF16)
        zg = jnp.dot(x, w1g_ref[...], preferred_element_type=F32) + b1g_ref[...]
        zu = jnp.dot(x, w1u_ref[...], preferred_element_type=F32) + b1u_ref[...]
        gate = jnp.minimum(zg, SWIGLU_LIMIT)
        up = jnp.clip(zu, -SWIGLU_LIMIT, SWIGLU_LIMIT)
        act = gate * jax.nn.sigmoid(SWIGLU_ALPHA * gate) * (up + 1.0)
        y = jnp.dot(act.astype(BF16), w2_ref[...], preferred_element_type=F32)

        @pl.when(c == 0)
        def _():
            y_ref[...] = y + b2_ref[...]

        @pl.when(c != 0)
        def _():
            y_ref[...] += y

    @pl.when(jnp.logical_not(used) & (c == 0))
    def _():
        y_ref[...] = jnp.zeros_like(y_ref)


def _experts(block_e, n_used, xs, w1, b1, w2, b2):
    n_slots = xs.shape[0]
    n_blocks = n_slots // MOE_BLOCK_M
    n_chunks = D_FF // MOE_FF_CHUNK

    def chunk(i, c, nb):
        block = jnp.minimum(i, nb[0] - 1)
        step = jnp.where(i < nb[0], c, n_chunks - 1)
        return jnp.where(block % 2 == 0, step, n_chunks - 1 - step)

    grid_spec = pltpu.PrefetchScalarGridSpec(
        num_scalar_prefetch=2,
        grid=(n_blocks, n_chunks),
        in_specs=[
            pl.BlockSpec((MOE_BLOCK_M, D_MODEL), lambda i, c, be, nb: (i, 0)),
            pl.BlockSpec((None, D_MODEL, MOE_FF_CHUNK), lambda i, c, be, nb: (be[i], 0, chunk(i, c, nb))),
            pl.BlockSpec((None, D_MODEL, MOE_FF_CHUNK),
                         lambda i, c, be, nb: (be[i], 0, n_chunks + chunk(i, c, nb))),
            pl.BlockSpec((None, 1, MOE_FF_CHUNK), lambda i, c, be, nb: (be[i], 0, chunk(i, c, nb))),
            pl.BlockSpec((None, 1, MOE_FF_CHUNK), lambda i, c, be, nb: (be[i], 0, n_chunks + chunk(i, c, nb))),
            pl.BlockSpec((None, MOE_FF_CHUNK, D_MODEL), lambda i, c, be, nb: (be[i], chunk(i, c, nb), 0)),
            pl.BlockSpec((None, 1, D_MODEL), lambda i, c, be, nb: (be[i], 0, 0)),
        ],
        out_specs=pl.BlockSpec((MOE_BLOCK_M, D_MODEL), lambda i, c, be, nb: (i, 0)),
    )
    return pl.pallas_call(
        _expert_body,
        grid_spec=grid_spec,
        out_shape=jax.ShapeDtypeStruct((n_slots, D_MODEL), F32),
        compiler_params=_params("arbitrary", "arbitrary"),
        name="experts",
    )(block_e, n_used, xs, w1, w1, b1, b1, w2, b2)


ROUTE_TILE = 1024
LANE_E = 0
LANE_RANK = TOP_K


def _route_body(lg_ref, meta_ref, gate_ref, cnt_ref, tri_scr, carry_scr):
    @pl.when(pl.program_id(0) == 0)
    def _():
        r = lax.broadcasted_iota(jnp.int32, (ROUTE_TILE, ROUTE_TILE), 0)
        c = lax.broadcasted_iota(jnp.int32, (ROUTE_TILE, ROUTE_TILE), 1)
        tri_scr[...] = jnp.where(c < r, 1.0, 0.0).astype(BF16)
        carry_scr[...] = jnp.zeros_like(carry_scr)

    lane_i = lax.broadcasted_iota(jnp.int32, (ROUTE_TILE, LANES), 1)
    lane = lane_i.astype(F32)
    x = jnp.where(lane_i < N_EXPERTS, lg_ref[...], -jnp.inf)
    vals, idxs, sels = [], [], []
    for _ in range(TOP_K):
        mx = jnp.max(x, axis=1, keepdims=True)
        idx = jnp.min(jnp.where(x == mx, lane, float(LANES)), axis=1, keepdims=True)
        sel = lane == idx
        x = jnp.where(sel, -jnp.inf, x)
        vals.append(mx)
        idxs.append(idx)
        sels.append(sel)
    weights = [jnp.exp(v - vals[0]) for v in vals]
    inv = 1.0 / (weights[0] + weights[1] + weights[2] + weights[3])
    onehot = sum(jnp.where(s, 1.0, 0.0) for s in sels)
    ranks = jnp.dot(tri_scr[...], onehot.astype(BF16), preferred_element_type=F32) + carry_scr[0:1, :]
    meta = jnp.zeros((ROUTE_TILE, LANES), F32)
    gate = jnp.zeros((ROUTE_TILE, LANES), F32)
    for k in range(TOP_K):
        rank_k = jnp.sum(jnp.where(sels[k], ranks, 0.0), axis=1, keepdims=True)
        meta = jnp.where(lane_i == LANE_E + k, idxs[k], meta)
        meta = jnp.where(lane_i == LANE_RANK + k, rank_k, meta)
        gate = jnp.where(lane_i == k, weights[k] * inv, gate)
    meta_ref[...] = meta.astype(jnp.int32)
    gate_ref[...] = gate
    carry_scr[0:1, :] = carry_scr[0:1, :] + jnp.sum(onehot, axis=0, keepdims=True)
    cnt_ref[...] = carry_scr[...]


def _route(logits):
    t = logits.shape[0]
    return pl.pallas_call(
        _route_body,
        grid=(t // ROUTE_TILE,),
        in_specs=[pl.BlockSpec((ROUTE_TILE, LANES), lambda i: (i, 0))],
        out_specs=[
            pl.BlockSpec((ROUTE_TILE, LANES), lambda i: (i, 0)),
            pl.BlockSpec((ROUTE_TILE, LANES), lambda i: (i, 0)),
            pl.BlockSpec((SUBLANES, LANES), lambda i: (0, 0)),
        ],
        out_shape=[
            jax.ShapeDtypeStruct((t, LANES), jnp.int32),
            jax.ShapeDtypeStruct((t, LANES), F32),
            jax.ShapeDtypeStruct((SUBLANES, LANES), F32),
        ],
        scratch_shapes=[pltpu.VMEM((ROUTE_TILE, ROUTE_TILE), BF16), pltpu.VMEM((SUBLANES, LANES), F32)],
        compiler_params=_params("arbitrary"),
        name="route",
    )(logits)


def _slot_tables(counts, n_blocks):
    padded = (counts + MOE_BLOCK_M - 1) // MOE_BLOCK_M * MOE_BLOCK_M
    pends = jnp.cumsum(padded)
    first_slot = jnp.arange(n_blocks, dtype=pends.dtype) * MOE_BLOCK_M
    block_e = jnp.minimum(jnp.sum(pends[None, :] <= first_slot[:, None], axis=1), N_EXPERTS - 1).astype(jnp.int32)
    n_used = (pends[-1] // MOE_BLOCK_M).astype(jnp.int32).reshape(1)
    return (pends - padded).astype(jnp.int32), pends.astype(jnp.int32), padded.astype(jnp.int32), block_e, n_used


DISPATCH_TILE = 256
CAST_ROWS = 512
IDX_SLOTS = 3


def _dispatch_body(n_cast, pend_ref, padded_ref, dest_hbm, h_ref, w1_ref, w2_ref, xs_hbm, w1b_ref, w2b_ref,
                   idx_smem, zero_scr, idx_sem, row_sem, zero_sem):
    i = pl.program_id(0)
    n = pl.num_programs(0)

    width = DISPATCH_TILE * TOP_K

    def idx_copy(tile):
        s = lax.rem(tile, IDX_SLOTS)
        return pltpu.make_async_copy(dest_hbm.at[pl.ds(pl.multiple_of(tile * width, width), width)],
                                     idx_smem.at[pl.ds(pl.multiple_of(s * width, width), width)], idx_sem.at[s])

    def zero_copy(e):
        start = pl.multiple_of(pend_ref[e] - MOE_BLOCK_M, MOE_BLOCK_M)
        return pltpu.make_async_copy(zero_scr, xs_hbm.at[pl.ds(start, MOE_BLOCK_M), :], zero_sem)

    def row_copy(t, d):
        return pltpu.make_async_copy(h_ref.at[pl.ds(t, 1), :], xs_hbm.at[pl.ds(d, 1), :], row_sem)

    @pl.when(i == 0)
    def _():
        zero_scr[...] = jnp.zeros_like(zero_scr)
        for e in range(N_EXPERTS):
            @pl.when(padded_ref[e] > 0)
            def _():
                zero_copy(e).start()
        for e in range(N_EXPERTS):
            @pl.when(padded_ref[e] > 0)
            def _():
                zero_copy(e).wait()
        idx_copy(0).start()

    @pl.when(i + 1 < n)
    def _():
        idx_copy(i + 1).start()

    idx_copy(i).wait()
    base = lax.rem(i, IDX_SLOTS) * width

    def issue(t, c):
        for k in range(TOP_K):
            row_copy(t, idx_smem[base + t * TOP_K + k]).start(priority=k % 2)
        return c

    lax.fori_loop(0, DISPATCH_TILE, issue, 0, unroll=8)

    @pl.when(i < n_cast)
    def _():
        w1b_ref[...] = w1_ref[...].astype(BF16)
        w2b_ref[...] = w2_ref[...].astype(BF16)

    for _ in range(TOP_K):
        pltpu.make_async_copy(h_ref, h_ref, row_sem).wait()


def _scatter_rows(h, dest, pends, padded, n_slots, w1, w2):
    t = h.shape[0]
    tiles = t // DISPATCH_TILE
    w1_rows = w1.reshape(-1, w1.shape[-1])
    w2_rows = w2.reshape(-1, w2.shape[-1])
    n_cast = w1_rows.shape[0] // CAST_ROWS
    assert w2_rows.shape[0] == w1_rows.shape[0] and n_cast * CAST_ROWS == w1_rows.shape[0] and n_cast <= tiles
    cast_block = lambda i, pe, pd: (jnp.minimum(i, n_cast - 1), 0)
    grid_spec = pltpu.PrefetchScalarGridSpec(
        num_scalar_prefetch=2,
        grid=(tiles,),
        in_specs=[
            pl.BlockSpec(memory_space=pl.ANY),
            pl.BlockSpec((DISPATCH_TILE, D_MODEL), lambda i, pe, pd: (i, 0)),
            pl.BlockSpec((CAST_ROWS, w1_rows.shape[1]), cast_block),
            pl.BlockSpec((CAST_ROWS, w2_rows.shape[1]), cast_block),
        ],
        out_specs=[
            pl.BlockSpec(memory_space=pl.ANY),
            pl.BlockSpec((CAST_ROWS, w1_rows.shape[1]), cast_block),
            pl.BlockSpec((CAST_ROWS, w2_rows.shape[1]), cast_block),
        ],
        scratch_shapes=[
            pltpu.SMEM((IDX_SLOTS * DISPATCH_TILE * TOP_K,), jnp.int32),
            pltpu.VMEM((MOE_BLOCK_M, D_MODEL), F32),
            pltpu.SemaphoreType.DMA((IDX_SLOTS,)),
            pltpu.SemaphoreType.DMA(()),
            pltpu.SemaphoreType.DMA(()),
        ],
    )
    xs, w1b, w2b = pl.pallas_call(
        functools.partial(_dispatch_body, n_cast),
        grid_spec=grid_spec,
        out_shape=[
            jax.ShapeDtypeStruct((n_slots, D_MODEL), F32),
            jax.ShapeDtypeStruct(w1_rows.shape, BF16),
            jax.ShapeDtypeStruct(w2_rows.shape, BF16),
        ],
        compiler_params=_params("arbitrary"),
        name="dispatch",
    )(pends, padded, dest.reshape(-1), h, w1_rows, w2_rows)
    return xs, w1b.reshape(w1.shape), w2b.reshape(w2.shape)


COMBINE_TILE = 256


def _combine_body(tile_off, dest_hbm, x1_ref, gate_ref, gf_ref, ys_hbm, y_ref, idx_smem, buf, idx_sem, row_sem):
    i = pl.program_id(0)
    n = pl.num_programs(0)

    width = COMBINE_TILE * TOP_K

    def idx_copy(tile):
        s = lax.rem(tile, IDX_SLOTS)
        return pltpu.make_async_copy(dest_hbm.at[pl.ds(pl.multiple_of((tile_off + tile) * width, width), width)],
                                     idx_smem.at[pl.ds(pl.multiple_of(s * width, width), width)], idx_sem.at[s])

    def row_copy(d, b, k, t):
        return pltpu.make_async_copy(ys_hbm.at[pl.ds(d, 1), :], buf.at[b, k, pl.ds(t, 1), :], row_sem.at[b])

    def issue_rows(tile):
        base = lax.rem(tile, IDX_SLOTS) * width
        b = lax.rem(tile, 2)

        def body(t, c):
            for k in range(TOP_K):
                row_copy(idx_smem[base + t * TOP_K + k], b, k, t).start(priority=k % 2)
            return c

        lax.fori_loop(0, COMBINE_TILE, body, 0, unroll=8)

    @pl.when(i == 0)
    def _():
        idx_copy(0).start()
        idx_copy(0).wait()
        issue_rows(0)

        @pl.when(n > 1)
        def _():
            idx_copy(1).start()

    @pl.when(i + 1 < n)
    def _():
        idx_copy(i + 1).wait()
        issue_rows(i + 1)

    @pl.when(i + 2 < n)
    def _():
        idx_copy(i + 2).start()

    b = lax.rem(i, 2)

    pltpu.make_async_copy(buf.at[b], buf.at[b], row_sem.at[b]).wait()
    gate = gate_ref[...]
    acc = x1_ref[...]
    for k in range(TOP_K):
        acc = acc + buf[b, k] * gate[:, k:k + 1]
    y_ref[...] = acc * lax.rsqrt(jnp.mean(acc * acc, axis=-1, keepdims=True) + RMS_EPS) * gf_ref[...]


def _combine(x1, gate, dest, ys, g_final, row_off, rows):
    t = x1.shape[0]
    tile_off = row_off // COMBINE_TILE
    return pl.pallas_call(
        functools.partial(_combine_body, tile_off),
        grid=(rows // COMBINE_TILE,),
        in_specs=[
            pl.BlockSpec(memory_space=pl.ANY),
            pl.BlockSpec((COMBINE_TILE, D_MODEL), lambda i: (tile_off + i, 0)),
            pl.BlockSpec((COMBINE_TILE, LANES), lambda i: (tile_off + i, 0)),
            pl.BlockSpec((1, D_MODEL), lambda i: (0, 0)),
            pl.BlockSpec(memory_space=pl.ANY),
        ],
        out_specs=pl.BlockSpec((COMBINE_TILE, D_MODEL), lambda i: (i, 0)),
        out_shape=jax.ShapeDtypeStruct((rows, D_MODEL), F32),
        scratch_shapes=[
            pltpu.SMEM((IDX_SLOTS * COMBINE_TILE * TOP_K,), jnp.int32),
            pltpu.VMEM((2, TOP_K, COMBINE_TILE, D_MODEL), F32),
            pltpu.SemaphoreType.DMA((IDX_SLOTS,)),
            pltpu.SemaphoreType.DMA((2,)),
        ],
        compiler_params=_params("arbitrary"),
        name="combine",
    )(dest.reshape(-1), x1, gate, g_final.reshape(1, D_MODEL), ys)


def _lane_row(v):
    return jnp.pad(v.astype(F32), (0, LANES - v.shape[0])).reshape(1, LANES)


def kernel(x_prompt, x_sample, cache_swa_k, cache_swa_v, state_mlstm_C, state_mlstm_n, state_mlstm_m,
           g_mix, w_in, b_if, attn_sinks, mlstm_norm_w, w_attn_proj, w_mlstm_proj, w_out,
           g_ffn, w_router, b_router, w_e1, b_e1, w_e2, b_e2, g_final):
    bp, sp = x_prompt.shape[0], x_prompt.shape[1]
    bs, ls = x_sample.shape[0], x_sample.shape[1]
    tp = bp * sp
    ts = bs * SAMPLE_ROWS
    t_all = tp + ts
    l = 0
    w_in_p = _permute_w_in(w_in[l])
    wa, wm, wo = w_attn_proj[l].astype(BF16), w_mlstm_proj[l].astype(BF16), w_out[l].astype(BF16)
    wr_f32 = jnp.pad(w_router[l], ((0, 0), (0, ROUTER_PAD - N_EXPERTS)))
    wr_hi = wr_f32.astype(BF16)
    wr_pad = jnp.concatenate([wr_hi, (wr_f32 - wr_hi.astype(F32)).astype(BF16)], axis=1)
    br_pad = _lane_row(b_router[l])
    b1, b2 = b_e1[l].reshape(N_EXPERTS, 1, 2 * D_FF), b_e2[l].reshape(N_EXPERTS, 1, D_MODEL)
    sinks_row = _lane_row(attn_sinks[l])
    bias_row = _lane_row(b_if[l])
    norm_w_row = mlstm_norm_w[l].reshape(1, M_V_WIDTH)

    xp = x_prompt.reshape(tp, D_MODEL)
    xs = jnp.pad(x_sample, ((0, 0), (0, SAMPLE_ROWS - ls), (0, 0))).reshape(ts, D_MODEL)
    z_all = _inproj(xp, xs, g_mix[l], w_in_p, 512)

    a_p, k1, v1 = _attn_prompt(z_all, sinks_row, bp, sp)
    a_s, k2, v2 = _attn_sample(z_all, sinks_row, cache_swa_k[l].reshape(bs, WINDOW, KV_WIDTH),
                               cache_swa_v[l].reshape(bs, WINDOW, KV_WIDTH), tp, ls)
    mh_p, c1, n1, m1 = _mlstm_prompt(z_all, bias_row, norm_w_row, bp, sp)
    mh_s, c2, n2, m2 = _mlstm_sample(z_all, bias_row, norm_w_row, state_mlstm_C[l], state_mlstm_n[l],
                                     state_mlstm_m[l], tp, ls)

    x1, h2, logits = _merge(xp, xs, a_p, a_s, mh_p, mh_s, z_all, wa, wm, wo, g_ffn[l], wr_pad, br_pad, 256)

    meta, gate, counts = _route(logits)
    n_blocks = -(-t_all * TOP_K // MOE_BLOCK_M) + N_EXPERTS
    pstarts, pends, padded, block_e, n_used = _slot_tables(counts[0, :N_EXPERTS].astype(jnp.int32), n_blocks)
    dest = pstarts[meta[:, LANE_E:LANE_E + TOP_K]] + meta[:, LANE_RANK:LANE_RANK + TOP_K]
    xs_sorted, w1, w2 = _scatter_rows(h2, dest, pends, padded, n_blocks * MOE_BLOCK_M, w_e1[l], w_e2[l])
    ys = _experts(block_e, n_used, xs_sorted, w1, b1, w2, b2)
    y_prompt = _combine(x1, gate, dest, ys, g_final, 0, tp).reshape(bp, sp, D_MODEL)
    y_sample = _combine(x1, gate, dest, ys, g_final, tp, ts).reshape(bs, SAMPLE_ROWS, D_MODEL)[:, :ls]
    k1 = k1.reshape(bp, WINDOW, N_KV_HEADS, HEAD_DIM)
    v1 = v1.reshape(bp, WINDOW, N_KV_HEADS, HEAD_DIM)
    k2 = k2.reshape(bs, WINDOW, N_KV_HEADS, HEAD_DIM)
    v2 = v2.reshape(bs, WINDOW, N_KV_HEADS, HEAD_DIM)
    m1 = m1[:, :, 0]
    st = lambda t: t[None]
    return (y_prompt, y_sample, st(k1), st(v1), st(c1), st(n1), st(m1), st(k2), st(v2), st(c2), st(n2), st(m2))
```

```python
import functools

import jax
import jax.numpy as jnp
from jax import lax
from jax.experimental import pallas as pl
from jax.experimental.pallas import tpu as pltpu

D_MODEL = 2048
N_HEADS = 16
N_KV_HEADS = 2
HEAD_DIM = 64
Q_PER_KV = N_HEADS // N_KV_HEADS
WINDOW = 128
ATTN_WIDTH = N_HEADS * HEAD_DIM
KV_WIDTH = N_KV_HEADS * HEAD_DIM
M_HEADS = 4
M_QK_DIM = 128
M_V_DIM = 256
M_QK_WIDTH = M_HEADS * M_QK_DIM
M_V_WIDTH = M_HEADS * M_V_DIM
M_CHUNK = 64
N_EXPERTS = 32
TOP_K = 4
D_FF = D_MODEL
SWIGLU_LIMIT = 7.0
SWIGLU_ALPHA = 1.702
RMS_EPS = 1e-5
NEG = -1e30

F32 = jnp.float32
BF16 = jnp.bfloat16

LANES = 128
SUBLANES = 8
VMEM_LIMIT_BYTES = 56 * 1024 * 1024

COL_GZ = 0
COL_AQ = COL_GZ + 2 * D_MODEL
COL_MV = COL_AQ + ATTN_WIDTH
COL_MO = COL_MV + M_V_WIDTH
COL_MQ = COL_MO + M_V_WIDTH
COL_MK = COL_MQ + M_QK_WIDTH
COL_AK = COL_MK + M_QK_WIDTH
COL_IF = COL_AK + 2 * KV_WIDTH
IN_TILE_N = 2176
IN_WIDTH_PAD = -(-(COL_IF + LANES) // IN_TILE_N) * IN_TILE_N

SAMPLE_ROWS = SUBLANES
MOE_BLOCK_M = 512
MOE_FF_CHUNK = 1024
ROUTER_PAD = LANES


def _permute_w_in(w_in):
    splits = (ATTN_WIDTH, KV_WIDTH, KV_WIDTH, M_QK_WIDTH, M_QK_WIDTH, M_V_WIDTH, M_V_WIDTH, M_HEADS, M_HEADS,
              2 * D_MODEL)
    parts, off = [], 0
    for width in splits:
        parts.append(w_in[:, off:off + width])
        off += width
    aq, ak, av, mq, mk, mv, mo, mi, mf, gz = parts
    cols = jnp.concatenate([gz, aq, mv, mo, mq, mk, ak, av, mi, mf], axis=1)
    cols = jnp.pad(cols, ((0, 0), (0, IN_WIDTH_PAD - cols.shape[1])))
    return cols.astype(BF16)


def _params(*semantics):
    return pltpu.CompilerParams(dimension_semantics=semantics, vmem_limit_bytes=VMEM_LIMIT_BYTES)


def _log_sigmoid(x):
    return jnp.minimum(x, 0.0) - jnp.log1p(jnp.exp(-jnp.abs(x)))


def _inproj_body(tiles_a, xa_ref, xb_ref, g_ref, w_ref, z_ref, h_scr):
    def normalise(x_ref):
        x = x_ref[...]
        y = x * lax.rsqrt(jnp.mean(x * x, axis=-1, keepdims=True) + RMS_EPS)
        h_scr[...] = (y * g_ref[...]).astype(BF16)

    first = pl.program_id(1) == 0
    in_a = pl.program_id(0) < tiles_a

    @pl.when(first & in_a)
    def _():
        normalise(xa_ref)

    @pl.when(first & jnp.logical_not(in_a))
    def _():
        normalise(xb_ref)

    z_ref[...] = jnp.dot(h_scr[...], w_ref[...], preferred_element_type=F32)


def _inproj(x_a, x_b, g, w_bf16, tile_m):
    tiles_a, tiles_b = x_a.shape[0] // tile_m, x_b.shape[0] // tile_m
    n = w_bf16.shape[1]
    return pl.pallas_call(
        functools.partial(_inproj_body, tiles_a),
        grid=(tiles_a + tiles_b, n // IN_TILE_N),
        in_specs=[
            pl.BlockSpec((tile_m, D_MODEL), lambda i, j: (jnp.minimum(i, tiles_a - 1), 0)),
            pl.BlockSpec((tile_m, D_MODEL), lambda i, j: (jnp.maximum(i - tiles_a, 0), 0)),
            pl.BlockSpec((1, D_MODEL), lambda i, j: (0, 0)),
            pl.BlockSpec((D_MODEL, IN_TILE_N), lambda i, j: (0, j)),
        ],
        out_specs=pl.BlockSpec((tile_m, IN_TILE_N), lambda i, j: (i, j)),
        out_shape=jax.ShapeDtypeStruct(((tiles_a + tiles_b) * tile_m, n), F32),
        scratch_shapes=[pltpu.VMEM((tile_m, D_MODEL), BF16)],
        compiler_params=_params("arbitrary", "arbitrary"),
        name="inproj",
    )(x_a, x_b, g.reshape(1, D_MODEL), w_bf16)


assert KV_WIDTH == LANES and N_KV_HEADS == 2 and LANES == 2 * HEAD_DIM
Q_TILES = ATTN_WIDTH // LANES
TILES_PER_KV = Q_TILES // N_KV_HEADS
_NT = (((1,), (1,)), ((), ()))


def _kv_placements(x):
    lo = lax.broadcasted_iota(jnp.int32, x.shape, 1) < HEAD_DIM
    swapped = pltpu.roll(x, HEAD_DIM, axis=1)
    return ((jnp.where(lo, x, 0.0).astype(BF16), jnp.where(lo, 0.0, swapped).astype(BF16)),
            (jnp.where(lo, swapped, 0.0).astype(BF16), jnp.where(lo, 0.0, x).astype(BF16)))


def _slab_head(g, p, j):
    return g * Q_PER_KV + 2 * j + p


def _attn_prompt_body(q_ref, kvc_ref, kvp_ref, sink_ref, w1_ref, w2_ref, a_ref, k_ref, v_ref, w1b_ref, w2b_ref):
    w1b_ref[...] = w1_ref[...].astype(BF16)
    w2b_ref[...] = w2_ref[...].astype(BF16)
    n = pl.program_id(1)
    kc, vc = kvc_ref[:, :KV_WIDTH], kvc_ref[:, KV_WIDTH:]
    k_ref[...] = kc
    v_ref[...] = vc
    kk = _kv_placements(jnp.concatenate([kvp_ref[:, :KV_WIDTH], kc], axis=0))
    vv = _kv_placements(jnp.concatenate([kvp_ref[:, KV_WIDTH:], vc], axis=0))
    q = q_ref[...].astype(BF16)
    row = lax.broadcasted_iota(jnp.int32, (WINDOW, WINDOW), 0)
    col = lax.broadcasted_iota(jnp.int32, (WINDOW, WINDOW), 1)
    mask = jnp.concatenate([(col > row) & (n > 0), col <= row], axis=1)
    slabs, sinks = [], []
    for g in range(N_KV_HEADS):
        qt = jnp.concatenate([q[:, (TILES_PER_KV * g + j) * LANES:(TILES_PER_KV * g + j + 1) * LANES]
                              for j in range(TILES_PER_KV)], axis=0)
        for p in range(2):
            s = lax.dot_general(qt, kk[g][p], _NT, preferred_element_type=F32) * (HEAD_DIM ** -0.5)
            for j in range(TILES_PER_KV):
                h = _slab_head(g, p, j)
                slabs.append(jnp.where(mask, s[j * WINDOW:(j + 1) * WINDOW], NEG))
                sinks.append(jnp.broadcast_to(sink_ref[:, h:h + 1], (WINDOW, LANES)))
    s_all = jnp.concatenate(slabs, axis=0)
    sink = jnp.concatenate(sinks, axis=0)
    m = jnp.maximum(jnp.broadcast_to(jnp.max(s_all, axis=1, keepdims=True), sink.shape), sink)
    p_all = jnp.exp(s_all - jnp.concatenate([m, m], axis=1))
    p_hi = p_all.astype(BF16)
    p_lo = (p_all - p_hi.astype(F32)).astype(BF16)
    ones = jnp.ones((2 * WINDOW, LANES), BF16)
    denom = (jnp.dot(p_hi, ones, preferred_element_type=F32) + jnp.dot(p_lo, ones, preferred_element_type=F32)
             + jnp.exp(sink - m))
    inv = 1.0 / denom
    pn = (p_all * jnp.concatenate([inv, inv], axis=1)).astype(BF16)
    half = TILES_PER_KV * WINDOW
    for g in range(N_KV_HEADS):
        base = 2 * half * g
        o = (jnp.dot(pn[base:base + half], vv[g][0], preferred_element_type=F32)
             + jnp.dot(pn[base + half:base + 2 * half], vv[g][1], preferred_element_type=F32))
        for j in range(TILES_PER_KV):
            t = TILES_PER_KV * g + j
            a_ref[:, t * LANES:(t + 1) * LANES] = o[j * WINDOW:(j + 1) * WINDOW].astype(BF16)


def _attn_prompt(z_all, sinks_row, batch, seq, w1, w2):
    nb = seq // WINDOW
    steps = batch * nb
    w1_rows = w1.reshape(-1, w1.shape[-1])
    w2_rows = w2.reshape(-1, w2.shape[-1])
    r1, r2 = w1_rows.shape[0] // steps, w2_rows.shape[0] // steps
    assert r1 * steps == w1_rows.shape[0] and r2 * steps == w2_rows.shape[0] and r1 % 16 == 0 and r2 % 16 == 0
    step = lambda b, n: (b * nb + n, 0)
    a, k, v, w1b, w2b = pl.pallas_call(
        _attn_prompt_body,
        grid=(batch, nb),
        in_specs=[
            pl.BlockSpec((WINDOW, ATTN_WIDTH), lambda b, n: (b * nb + n, COL_AQ // ATTN_WIDTH)),
            pl.BlockSpec((WINDOW, 2 * KV_WIDTH), lambda b, n: (b * nb + n, COL_AK // (2 * KV_WIDTH))),
            pl.BlockSpec((WINDOW, 2 * KV_WIDTH),
                         lambda b, n: (b * nb + jnp.maximum(n - 1, 0), COL_AK // (2 * KV_WIDTH))),
            pl.BlockSpec((1, LANES), lambda b, n: (0, 0)),
            pl.BlockSpec((r1, w1_rows.shape[1]), step),
            pl.BlockSpec((r2, w2_rows.shape[1]), step),
        ],
        out_specs=[
            pl.BlockSpec((WINDOW, ATTN_WIDTH), lambda b, n: (b * nb + n, 0)),
            pl.BlockSpec((None, WINDOW, KV_WIDTH), lambda b, n: (b, 0, 0)),
            pl.BlockSpec((None, WINDOW, KV_WIDTH), lambda b, n: (b, 0, 0)),
            pl.BlockSpec((r1, w1_rows.shape[1]), step),
            pl.BlockSpec((r2, w2_rows.shape[1]), step),
        ],
        out_shape=[
            jax.ShapeDtypeStruct((batch * seq, ATTN_WIDTH), BF16),
            jax.ShapeDtypeStruct((batch, WINDOW, KV_WIDTH), F32),
            jax.ShapeDtypeStruct((batch, WINDOW, KV_WIDTH), F32),
            jax.ShapeDtypeStruct(w1_rows.shape, BF16),
            jax.ShapeDtypeStruct(w2_rows.shape, BF16),
        ],
        compiler_params=_params("arbitrary", "arbitrary"),
        name="attn_prompt",
    )(z_all, z_all, z_all, sinks_row, w1_rows, w2_rows)
    return a, k, v, w1b.reshape(w1.shape), w2b.reshape(w2.shape)


ATTN_SAMPLE_GROUP = 8


def _attn_sample_body(n_valid, q_ref, kv_ref, ck_ref, cv_ref, sink_ref, a_ref, nk_ref, nv_ref, o_scr):
    rs = SAMPLE_ROWS
    slab = TILES_PER_KV * rs
    r8 = lax.broadcasted_iota(jnp.int32, (rs, KV_WIDTH), 0)
    qi_c = jnp.bitwise_and(lax.broadcasted_iota(jnp.int32, (slab, WINDOW), 0), rs - 1)
    mask_c = lax.broadcasted_iota(jnp.int32, (slab, WINDOW), 1) > qi_c
    qi_n = jnp.bitwise_and(lax.broadcasted_iota(jnp.int32, (slab, rs), 0), rs - 1)
    mask_n = lax.broadcasted_iota(jnp.int32, (slab, rs), 1) <= qi_n
    sc_list, sn_list, sink_list, values = [], [], [], []
    for b in range(ATTN_SAMPLE_GROUP):
        rows = slice(b * rs, (b + 1) * rs)
        q = q_ref[rows, :]
        kn, vn = kv_ref[rows, :KV_WIDTH], kv_ref[rows, KV_WIDTH:]
        ck, cv = ck_ref[b], cv_ref[b]
        for src, new, dst in ((ck, kn, nk_ref), (cv, vn, nv_ref)):
            rolled = pltpu.roll(src, WINDOW - n_valid, axis=0)
            dst[b] = rolled
            tail = jnp.where(r8 >= rs - n_valid, pltpu.roll(new, rs - n_valid, axis=0), rolled[WINDOW - rs:, :])
            dst[b, WINDOW - rs:, :] = tail
        kk_c, kk_n = _kv_placements(ck), _kv_placements(kn)
        values.append((_kv_placements(cv), _kv_placements(vn)))
        for g in range(N_KV_HEADS):
            qt = jnp.concatenate([q[:, (TILES_PER_KV * g + j) * LANES:(TILES_PER_KV * g + j + 1) * LANES]
                                  for j in range(TILES_PER_KV)], axis=0).astype(BF16)
            for p in range(2):
                s_c = lax.dot_general(qt, kk_c[g][p], _NT, preferred_element_type=F32) * (HEAD_DIM ** -0.5)
                s_n = lax.dot_general(qt, kk_n[g][p], _NT, preferred_element_type=F32) * (HEAD_DIM ** -0.5)
                sc_list.append(jnp.where(mask_c, s_c, NEG))
                sn_list.append(jnp.where(mask_n, s_n, NEG))
                for j in range(TILES_PER_KV):
                    h = _slab_head(g, p, j)
                    sink_list.append(jnp.broadcast_to(sink_ref[:, h:h + 1], (rs, 1)))
    s_c = jnp.concatenate(sc_list, axis=0)
    s_n = jnp.concatenate(sn_list, axis=0)
    sink = jnp.concatenate(sink_list, axis=0)
    m = jnp.maximum(jnp.maximum(jnp.max(s_c, axis=1, keepdims=True), jnp.max(s_n, axis=1, keepdims=True)), sink)
    p_c = jnp.exp(s_c - m)
    p_n = jnp.exp(s_n - m)
    inv = 1.0 / (jnp.sum(p_c, axis=1, keepdims=True) + jnp.sum(p_n, axis=1, keepdims=True) + jnp.exp(sink - m))
    p_c = (p_c * inv).astype(BF16)
    p_n = (p_n * inv).astype(BF16)
    for b in range(ATTN_SAMPLE_GROUP):
        vv_c, vv_n = values[b]
        for g in range(N_KV_HEADS):
            base = (b * N_KV_HEADS + g) * 2 * slab
            o = (jnp.dot(p_c[base:base + slab], vv_c[g][0], preferred_element_type=F32)
                 + jnp.dot(p_c[base + slab:base + 2 * slab], vv_c[g][1], preferred_element_type=F32)
                 + jnp.dot(p_n[base:base + slab], vv_n[g][0], preferred_element_type=F32)
                 + jnp.dot(p_n[base + slab:base + 2 * slab], vv_n[g][1], preferred_element_type=F32))
            for j in range(TILES_PER_KV):
                t = TILES_PER_KV * g + j
                o_scr[b * rs:(b + 1) * rs, t * LANES:(t + 1) * LANES] = o[j * rs:(j + 1) * rs]
    a_ref[...] = o_scr[...].astype(BF16)


def _attn_sample(z_all, sinks_row, cache_k, cache_v, row_off, n_valid):
    b = cache_k.shape[0]
    gr = ATTN_SAMPLE_GROUP
    rows = gr * SAMPLE_ROWS
    rb = row_off // rows
    return pl.pallas_call(
        functools.partial(_attn_sample_body, n_valid),
        grid=(b // gr,),
        in_specs=[
            pl.BlockSpec((rows, ATTN_WIDTH), lambda i: (rb + i, COL_AQ // ATTN_WIDTH)),
            pl.BlockSpec((rows, 2 * KV_WIDTH), lambda i: (rb + i, COL_AK // (2 * KV_WIDTH))),
            pl.BlockSpec((gr, WINDOW, KV_WIDTH), lambda i: (i, 0, 0)),
            pl.BlockSpec((gr, WINDOW, KV_WIDTH), lambda i: (i, 0, 0)),
            pl.BlockSpec((1, LANES), lambda i: (0, 0)),
        ],
        out_specs=[
            pl.BlockSpec((rows, ATTN_WIDTH), lambda i: (i, 0)),
            pl.BlockSpec((gr, WINDOW, KV_WIDTH), lambda i: (i, 0, 0)),
            pl.BlockSpec((gr, WINDOW, KV_WIDTH), lambda i: (i, 0, 0)),
        ],
        out_shape=[
            jax.ShapeDtypeStruct((b * SAMPLE_ROWS, ATTN_WIDTH), BF16),
            jax.ShapeDtypeStruct(cache_k.shape, F32),
            jax.ShapeDtypeStruct(cache_v.shape, F32),
        ],
        scratch_shapes=[pltpu.VMEM((rows, ATTN_WIDTH), F32)],
        compiler_params=_params("arbitrary"),
        name="attn_sample",
    )(z_all, z_all, cache_k, cache_v, sinks_row)


def _mlstm_gates(if_blk, bias_row, n_valid):
    length = if_blk.shape[0]
    g = if_blk + bias_row
    lane = lax.broadcasted_iota(jnp.int32, g.shape, 1)
    g = jnp.where(lane < M_HEADS, g, _log_sigmoid(g))
    if n_valid < length:
        row = lax.broadcasted_iota(jnp.int32, g.shape, 0)
        g = jnp.where(row < n_valid, g, jnp.where(lane < M_HEADS, NEG, 0.0))
    return g


def _mlstm_chunk(q, k, v, gates, gates_t, h, c, n, m):
    length = q.shape[0]
    i_col = gates[:, h:h + 1]
    f_col = gates[:, M_HEADS + h:M_HEADS + h + 1]
    i_row = gates_t[h:h + 1, :]
    f_row = gates_t[M_HEADS + h:M_HEADS + h + 1, :]
    t_idx = lax.broadcasted_iota(jnp.int32, (length, length), 0)
    s_idx = lax.broadcasted_iota(jnp.int32, (length, length), 1)
    causal = s_idx <= t_idx
    bcum_col = jnp.sum(jnp.where(causal, f_row, 0.0), axis=1, keepdims=True)
    bcum_row = jnp.sum(jnp.where(t_idx <= s_idx, f_col, 0.0), axis=0, keepdims=True)
    dlog = jnp.where(causal, bcum_col - bcum_row + i_row, NEG)
    inter = bcum_col + m
    mt = jnp.maximum(inter, jnp.max(dlog, axis=1, keepdims=True))
    w_inter = jnp.exp(inter - mt)
    qb = q.astype(BF16)
    ks = k * (M_QK_DIM ** -0.5)
    vb = v.astype(BF16)
    qk = lax.dot_general(qb, ks.astype(BF16), (((1,), (1,)), ((), ())), preferred_element_type=F32)
    qk = qk * jnp.exp(dlog - mt)
    num = (w_inter * jnp.dot(qb, c.astype(BF16), preferred_element_type=F32)
           + jnp.dot(qk.astype(BF16), vb, preferred_element_type=F32))
    den = w_inter * jnp.sum(q * n, axis=1, keepdims=True) + jnp.sum(qk, axis=1, keepdims=True)
    h_t = num / jnp.maximum(jnp.abs(den), jnp.exp(-mt))
    m_new = mt[length - 1:length, :]
    w_state = jnp.exp(bcum_col[length - 1:length, :] - bcum_col + i_col - m_new)
    decay = jnp.exp(inter[length - 1:length, :] - m_new)
    kw = ks * w_state
    c_new = decay * c + jnp.dot(kw.T.astype(BF16), vb, preferred_element_type=F32)
    n_new = decay * n + jnp.sum(kw, axis=0, keepdims=True)
    return h_t, c_new, n_new, m_new


def _mlstm_head_out(h_t, norm_w, o_gate):
    hs = h_t * lax.rsqrt(jnp.mean(h_t * h_t, axis=-1, keepdims=True) + RMS_EPS)
    return hs * norm_w * jax.nn.sigmoid(o_gate)


MLSTM_SEG = 512


MLSTM_BATCH = 2
MLSTM_PROMPT_CHUNK = 256


def _mlstm_prompt_body(*refs):
    nb = MLSTM_BATCH
    ins, rest = refs[:5 * nb], refs[5 * nb:]
    bias_ref, nw_ref, mh_ref, c_out, n_out, m_out, c_scr, n_scr, m_scr = rest
    seg = pl.program_id(1)

    @pl.when(seg == 0)
    def _():
        c_scr[...] = jnp.zeros_like(c_scr)
        n_scr[...] = jnp.zeros_like(n_scr)
        m_scr[...] = jnp.full_like(m_scr, NEG)

    def chunk_body(ci, carry):
        r = pl.multiple_of(ci * MLSTM_PROMPT_CHUNK, MLSTM_PROMPT_CHUNK)
        rows = pl.ds(r, MLSTM_PROMPT_CHUNK)
        state = [[(c_scr[u, h], n_scr[u, h:h + 1, :], m_scr[u, h:h + 1, 0:1]) for h in range(M_HEADS)]
                 for u in range(nb)]
        results = []
        for u in range(nb):
            q_ref, k_ref, v_ref, o_ref, if_ref = ins[5 * u:5 * u + 5]
            gates = _mlstm_gates(if_ref[rows, :], bias_ref[...], MLSTM_PROMPT_CHUNK)
            gates_t = gates.T
            for h in range(M_HEADS):
                qs = slice(h * M_QK_DIM, (h + 1) * M_QK_DIM)
                vs = slice(h * M_V_DIM, (h + 1) * M_V_DIM)
                h_t, c_new, n_new, m_new = _mlstm_chunk(
                    q_ref[rows, qs], k_ref[rows, qs], v_ref[rows, vs], gates, gates_t, h, *state[u][h])
                out = _mlstm_head_out(h_t, nw_ref[:, vs], o_ref[rows, vs]).astype(BF16)
                results.append((u, h, vs, out, c_new, n_new, m_new))
        for u, h, vs, out, c_new, n_new, m_new in results:
            c_scr[u, h] = c_new
            n_scr[u, h:h + 1, :] = n_new
            m_scr[u, h:h + 1, :] = jnp.broadcast_to(m_new, (1, LANES))
            mh_ref[u, rows, vs] = out
        return carry

    lax.fori_loop(0, MLSTM_SEG // MLSTM_PROMPT_CHUNK, chunk_body, 0)

    @pl.when(seg == pl.num_programs(1) - 1)
    def _():
        c_out[...] = c_scr[...]
        n_out[...] = n_scr[:, 0:M_HEADS, :]
        m_out[...] = m_scr[:, 0:M_HEADS, :]


def _mlstm_prompt(z_all, bias_row, norm_w_row, batch, seq):
    ns = seq // MLSTM_SEG
    nb = MLSTM_BATCH
    in_specs, args = [], []
    for u in range(nb):
        row = lambda p, s, u=u: (p * nb + u) * ns + s
        in_specs += [
            pl.BlockSpec((MLSTM_SEG, M_QK_WIDTH), lambda p, s, row=row: (row(p, s), COL_MQ // M_QK_WIDTH)),
            pl.BlockSpec((MLSTM_SEG, M_QK_WIDTH), lambda p, s, row=row: (row(p, s), COL_MK // M_QK_WIDTH)),
            pl.BlockSpec((MLSTM_SEG, M_V_WIDTH), lambda p, s, row=row: (row(p, s), COL_MV // M_V_WIDTH)),
            pl.BlockSpec((MLSTM_SEG, M_V_WIDTH), lambda p, s, row=row: (row(p, s), COL_MO // M_V_WIDTH)),
            pl.BlockSpec((MLSTM_SEG, LANES), lambda p, s, row=row: (row(p, s), COL_IF // LANES)),
        ]
        args += [z_all] * 5
    in_specs += [pl.BlockSpec((1, LANES), lambda p, s: (0, 0)), pl.BlockSpec((1, M_V_WIDTH), lambda p, s: (0, 0))]
    mh, c, n, m = pl.pallas_call(
        _mlstm_prompt_body,
        grid=(batch // nb, ns),
        in_specs=in_specs,
        out_specs=[
            pl.BlockSpec((None, nb, MLSTM_SEG, M_V_WIDTH), lambda p, s: (p, 0, s, 0)),
            pl.BlockSpec((nb, M_HEADS, M_QK_DIM, M_V_DIM), lambda p, s: (p, 0, 0, 0)),
            pl.BlockSpec((nb, M_HEADS, M_QK_DIM), lambda p, s: (p, 0, 0)),
            pl.BlockSpec((nb, M_HEADS, LANES), lambda p, s: (p, 0, 0)),
        ],
        out_shape=[
            jax.ShapeDtypeStruct((batch // nb, nb, seq, M_V_WIDTH), BF16),
            jax.ShapeDtypeStruct((batch, M_HEADS, M_QK_DIM, M_V_DIM), F32),
            jax.ShapeDtypeStruct((batch, M_HEADS, M_QK_DIM), F32),
            jax.ShapeDtypeStruct((batch, M_HEADS, LANES), F32),
        ],
        scratch_shapes=[
            pltpu.VMEM((nb, M_HEADS, M_QK_DIM, M_V_DIM), F32),
            pltpu.VMEM((nb, SUBLANES, M_QK_DIM), F32),
            pltpu.VMEM((nb, SUBLANES, LANES), F32),
        ],
        compiler_params=_params("arbitrary", "arbitrary"),
        name="mlstm_prompt",
    )(*args, bias_row, norm_w_row)
    return mh.reshape(batch * seq, M_V_WIDTH), c, n, m


MLSTM_SAMPLE_GROUP = 8


def _mlstm_sample_body(n_valid, q_ref, k_ref, v_ref, o_ref, if_ref, bias_ref, nw_ref, c0_ref, n0_ref, m0_ref,
                       mh_ref, c_out, n_out, m_out, h_scr):
    for g in range(MLSTM_SAMPLE_GROUP):
        rows = slice(g * SAMPLE_ROWS, (g + 1) * SAMPLE_ROWS)
        gates = _mlstm_gates(if_ref[rows, :], bias_ref[...], n_valid)
        gates_t = jnp.transpose(jnp.concatenate([gates] * (LANES // SAMPLE_ROWS), axis=0))[:, :SAMPLE_ROWS]
        for h in range(M_HEADS):
            qs = slice(h * M_QK_DIM, (h + 1) * M_QK_DIM)
            vs = slice(h * M_V_DIM, (h + 1) * M_V_DIM)
            h_t, c_new, n_new, m_new = _mlstm_chunk(
                q_ref[rows, qs], k_ref[rows, qs], v_ref[rows, vs], gates, gates_t, h,
                c0_ref[g, h], n0_ref[g, h:h + 1, :], m0_ref[g:g + 1, h:h + 1])
            c_out[g, h] = c_new
            n_out[g, h:h + 1, :] = n_new
            m_out[g:g + 1, h:h + 1] = m_new
            h_scr[rows, vs] = _mlstm_head_out(h_t, nw_ref[:, vs], o_ref[rows, vs])
    mh_ref[...] = h_scr[...].astype(BF16)


def _mlstm_sample(z_all, bias_row, norm_w_row, c0, n0, m0, row_off, n_valid):
    b = c0.shape[0]
    gr = MLSTM_SAMPLE_GROUP
    rows = gr * SAMPLE_ROWS
    rb = row_off // rows
    m0g = m0.reshape(b // gr, gr, M_HEADS)
    outs = pl.pallas_call(
        functools.partial(_mlstm_sample_body, n_valid),
        grid=(b // gr,),
        in_specs=[
            pl.BlockSpec((rows, M_QK_WIDTH), lambda i: (rb + i, COL_MQ // M_QK_WIDTH)),
            pl.BlockSpec((rows, M_QK_WIDTH), lambda i: (rb + i, COL_MK // M_QK_WIDTH)),
            pl.BlockSpec((rows, M_V_WIDTH), lambda i: (rb + i, COL_MV // M_V_WIDTH)),
            pl.BlockSpec((rows, M_V_WIDTH), lambda i: (rb + i, COL_MO // M_V_WIDTH)),
            pl.BlockSpec((rows, LANES), lambda i: (rb + i, COL_IF // LANES)),
            pl.BlockSpec((1, LANES), lambda i: (0, 0)),
            pl.BlockSpec((1, M_V_WIDTH), lambda i: (0, 0)),
            pl.BlockSpec((gr, M_HEADS, M_QK_DIM, M_V_DIM), lambda i: (i, 0, 0, 0)),
            pl.BlockSpec((gr, M_HEADS, M_QK_DIM), lambda i: (i, 0, 0)),
            pl.BlockSpec((None, gr, M_HEADS), lambda i: (i, 0, 0)),
        ],
        out_specs=[
            pl.BlockSpec((rows, M_V_WIDTH), lambda i: (i, 0)),
            pl.BlockSpec((gr, M_HEADS, M_QK_DIM, M_V_DIM), lambda i: (i, 0, 0, 0)),
            pl.BlockSpec((gr, M_HEADS, M_QK_DIM), lambda i: (i, 0, 0)),
            pl.BlockSpec((None, gr, M_HEADS), lambda i: (i, 0, 0)),
        ],
        out_shape=[
            jax.ShapeDtypeStruct((b * SAMPLE_ROWS, M_V_WIDTH), BF16),
            jax.ShapeDtypeStruct(c0.shape, F32),
            jax.ShapeDtypeStruct(n0.shape, F32),
            jax.ShapeDtypeStruct(m0g.shape, F32),
        ],
        scratch_shapes=[pltpu.VMEM((rows, M_V_WIDTH), F32)],
        compiler_params=_params("arbitrary"),
        name="mlstm_sample",
    )(z_all, z_all, z_all, z_all, z_all, bias_row, norm_w_row, c0, n0, m0g)
    mh, c, n, m = outs
    return mh, c, n, m.reshape(b, M_HEADS)


def _merge_body(tiles_a, xa_ref, xb_ref, aa_ref, ab_ref, ma_ref, mb_ref, gza_ref, gzm_ref, wa_ref, wm_ref, wo_ref,
                gf_ref, wr_ref, br_ref, x1_ref, h2_ref, lg_ref):
    in_a = pl.program_id(0) < tiles_a
    x = jnp.where(in_a, xa_ref[...], xb_ref[...])
    a = jnp.where(in_a, aa_ref[...], ab_ref[...])
    mh = jnp.where(in_a, ma_ref[...], mb_ref[...])
    pa = jnp.dot(a, wa_ref[...], preferred_element_type=F32)
    pm = jnp.dot(mh, wm_ref[...], preferred_element_type=F32)
    merged = jax.nn.sigmoid(gza_ref[...]) * pa + jax.nn.sigmoid(gzm_ref[...]) * pm
    x1 = x + jnp.dot(merged.astype(BF16), wo_ref[...], preferred_element_type=F32)
    x1_ref[...] = x1
    h2 = x1 * lax.rsqrt(jnp.mean(x1 * x1, axis=-1, keepdims=True) + RMS_EPS) * gf_ref[...]
    h2_ref[...] = h2
    hi = h2.astype(BF16)
    lo = (h2 - hi.astype(F32)).astype(BF16)
    both = jnp.dot(hi, wr_ref[...], preferred_element_type=F32)
    lg_ref[...] = (both[:, :ROUTER_PAD] + both[:, ROUTER_PAD:]
                   + jnp.dot(lo, wr_ref[:, :ROUTER_PAD], preferred_element_type=F32) + br_ref[...])


def _merge(x_a, x_b, a_a, a_b, mh_a, mh_b, z_all, wa, wm, wo, g_ffn, wr_pad, br_pad, tile_m):
    tiles_a, tiles_b = x_a.shape[0] // tile_m, x_b.shape[0] // tile_m
    t_all = z_all.shape[0]
    const = lambda i: (0, 0)
    from_a = lambda i: (jnp.minimum(i, tiles_a - 1), 0)
    from_b = lambda i: (jnp.maximum(i - tiles_a, 0), 0)
    return pl.pallas_call(
        functools.partial(_merge_body, tiles_a),
        grid=(tiles_a + tiles_b,),
        in_specs=[
            pl.BlockSpec((tile_m, D_MODEL), from_a),
            pl.BlockSpec((tile_m, D_MODEL), from_b),
            pl.BlockSpec((tile_m, ATTN_WIDTH), from_a),
            pl.BlockSpec((tile_m, ATTN_WIDTH), from_b),
            pl.BlockSpec((tile_m, M_V_WIDTH), from_a),
            pl.BlockSpec((tile_m, M_V_WIDTH), from_b),
            pl.BlockSpec((tile_m, D_MODEL), lambda i: (i, COL_GZ // D_MODEL)),
            pl.BlockSpec((tile_m, D_MODEL), lambda i: (i, COL_GZ // D_MODEL + 1)),
            pl.BlockSpec((ATTN_WIDTH, D_MODEL), const, pipeline_mode=pl.Buffered(1)),
            pl.BlockSpec((M_V_WIDTH, D_MODEL), const, pipeline_mode=pl.Buffered(1)),
            pl.BlockSpec((D_MODEL, D_MODEL), const, pipeline_mode=pl.Buffered(1)),
            pl.BlockSpec((1, D_MODEL), const),
            pl.BlockSpec((D_MODEL, 2 * ROUTER_PAD), const),
            pl.BlockSpec((1, ROUTER_PAD), const),
        ],
        out_specs=[
            pl.BlockSpec((tile_m, D_MODEL), lambda i: (i, 0)),
            pl.BlockSpec((tile_m, D_MODEL), lambda i: (i, 0)),
            pl.BlockSpec((tile_m, ROUTER_PAD), lambda i: (i, 0)),
        ],
        out_shape=[
            jax.ShapeDtypeStruct((t_all, D_MODEL), F32),
            jax.ShapeDtypeStruct((t_all, D_MODEL), F32),
            jax.ShapeDtypeStruct((t_all, ROUTER_PAD), F32),
        ],
        compiler_params=_params("arbitrary"),
        name="merge",
    )(x_a, x_b, a_a, a_b, mh_a, mh_b, z_all, z_all, wa, wm, wo, g_ffn.reshape(1, D_MODEL), wr_pad, br_pad)


def _expert_body(be_ref, nb_ref, xs_ref, w1g_ref, w1u_ref, b1g_ref, b1u_ref, w2_ref, b2_ref, y_ref):
    i = pl.program_id(0)
    c = pl.program_id(1)
    used = i < nb_ref[0]

    @pl.when(used)
    def _():
        x = xs_ref[...].astype(BF16)
        zg = jnp.dot(x, w1g_ref[...], preferred_element_type=F32) + b1g_ref[...]
        zu = jnp.dot(x, w1u_ref[...], preferred_element_type=F32) + b1u_ref[...]
        gate = jnp.minimum(zg, SWIGLU_LIMIT)
        up = jnp.clip(zu, -SWIGLU_LIMIT, SWIGLU_LIMIT)
        act = gate * jax.nn.sigmoid(SWIGLU_ALPHA * gate) * (up + 1.0)
        y = jnp.dot(act.astype(BF16), w2_ref[...], preferred_element_type=F32)

        @pl.when(c == 0)
        def _():
            y_ref[...] = y + b2_ref[...]

        @pl.when(c != 0)
        def _():
            y_ref[...] += y

    @pl.when(jnp.logical_not(used) & (c == 0))
    def _():
        y_ref[...] = jnp.zeros_like(y_ref)


def _experts(block_e, n_used, xs, w1, b1, w2, b2):
    n_slots = xs.shape[0]
    n_blocks = n_slots // MOE_BLOCK_M
    n_chunks = D_FF // MOE_FF_CHUNK

    def chunk(i, c, nb):
        block = jnp.minimum(i, nb[0] - 1)
        step = jnp.where(i < nb[0], c, n_chunks - 1)
        return jnp.where(block % 2 == 0, step, n_chunks - 1 - step)

    grid_spec = pltpu.PrefetchScalarGridSpec(
        num_scalar_prefetch=2,
        grid=(n_blocks, n_chunks),
        in_specs=[
            pl.BlockSpec((MOE_BLOCK_M, D_MODEL), lambda i, c, be, nb: (i, 0)),
            pl.BlockSpec((None, D_MODEL, MOE_FF_CHUNK), lambda i, c, be, nb: (be[i], 0, chunk(i, c, nb))),
            pl.BlockSpec((None, D_MODEL, MOE_FF_CHUNK),
                         lambda i, c, be, nb: (be[i], 0, n_chunks + chunk(i, c, nb))),
            pl.BlockSpec((None, 1, MOE_FF_CHUNK), lambda i, c, be, nb: (be[i], 0, chunk(i, c, nb))),
            pl.BlockSpec((None, 1, MOE_FF_CHUNK), lambda i, c, be, nb: (be[i], 0, n_chunks + chunk(i, c, nb))),
            pl.BlockSpec((None, MOE_FF_CHUNK, D_MODEL), lambda i, c, be, nb: (be[i], chunk(i, c, nb), 0)),
            pl.BlockSpec((None, 1, D_MODEL), lambda i, c, be, nb: (be[i], 0, 0)),
        ],
        out_specs=pl.BlockSpec((MOE_BLOCK_M, D_MODEL), lambda i, c, be, nb: (i, 0)),
    )
    return pl.pallas_call(
        _expert_body,
        grid_spec=grid_spec,
        out_shape=jax.ShapeDtypeStruct((n_slots, D_MODEL), F32),
        compiler_params=_params("arbitrary", "arbitrary"),
        name="experts",
    )(block_e, n_used, xs, w1, w1, b1, b1, w2, b2)


ROUTE_TILE = 1024
LANE_E = 0
LANE_RANK = TOP_K


def _route_body(lg_ref, meta_ref, gate_ref, cnt_ref, tri_scr, carry_scr):
    @pl.when(pl.program_id(0) == 0)
    def _():
        r = lax.broadcasted_iota(jnp.int32, (ROUTE_TILE, ROUTE_TILE), 0)
        c = lax.broadcasted_iota(jnp.int32, (ROUTE_TILE, ROUTE_TILE), 1)
        tri_scr[...] = jnp.where(c < r, 1.0, 0.0).astype(BF16)
        carry_scr[...] = jnp.zeros_like(carry_scr)

    lane_i = lax.broadcasted_iota(jnp.int32, (ROUTE_TILE, LANES), 1)
    lane = lane_i.astype(F32)
    x = jnp.where(lane_i < N_EXPERTS, lg_ref[...], -jnp.inf)
    vals, idxs, sels = [], [], []
    for _ in range(TOP_K):
        mx = jnp.max(x, axis=1, keepdims=True)
        idx = jnp.min(jnp.where(x == mx, lane, float(LANES)), axis=1, keepdims=True)
        sel = lane == idx
        x = jnp.where(sel, -jnp.inf, x)
        vals.append(mx)
        idxs.append(idx)
        sels.append(sel)
    weights = [jnp.exp(v - vals[0]) for v in vals]
    inv = 1.0 / (weights[0] + weights[1] + weights[2] + weights[3])
    onehot = sum(jnp.where(s, 1.0, 0.0) for s in sels)
    ranks = jnp.dot(tri_scr[...], onehot.astype(BF16), preferred_element_type=F32) + carry_scr[0:1, :]
    meta = jnp.zeros((ROUTE_TILE, LANES), F32)
    gate = jnp.zeros((ROUTE_TILE, LANES), F32)
    for k in range(TOP_K):
        rank_k = jnp.sum(jnp.where(sels[k], ranks, 0.0), axis=1, keepdims=True)
        meta = jnp.where(lane_i == LANE_E + k, idxs[k], meta)
        meta = jnp.where(lane_i == LANE_RANK + k, rank_k, meta)
        gate = jnp.where(lane_i == k, weights[k] * inv, gate)
    meta_ref[...] = meta.astype(jnp.int32)
    gate_ref[...] = gate
    carry_scr[0:1, :] = carry_scr[0:1, :] + jnp.sum(onehot, axis=0, keepdims=True)
    cnt_ref[...] = carry_scr[...]


def _route(logits):
    t = logits.shape[0]
    return pl.pallas_call(
        _route_body,
        grid=(t // ROUTE_TILE,),
        in_specs=[pl.BlockSpec((ROUTE_TILE, LANES), lambda i: (i, 0))],
        out_specs=[
            pl.BlockSpec((ROUTE_TILE, LANES), lambda i: (i, 0)),
            pl.BlockSpec((ROUTE_TILE, LANES), lambda i: (i, 0)),
            pl.BlockSpec((SUBLANES, LANES), lambda i: (0, 0)),
        ],
        out_shape=[
            jax.ShapeDtypeStruct((t, LANES), jnp.int32),
            jax.ShapeDtypeStruct((t, LANES), F32),
            jax.ShapeDtypeStruct((SUBLANES, LANES), F32),
        ],
        scratch_shapes=[pltpu.VMEM((ROUTE_TILE, ROUTE_TILE), BF16), pltpu.VMEM((SUBLANES, LANES), F32)],
        compiler_params=_params("arbitrary"),
        name="route",
    )(logits)


def _slot_tables(counts, n_blocks):
    padded = (counts + MOE_BLOCK_M - 1) // MOE_BLOCK_M * MOE_BLOCK_M
    pends = jnp.cumsum(padded)
    first_slot = jnp.arange(n_blocks, dtype=pends.dtype) * MOE_BLOCK_M
    block_e = jnp.minimum(jnp.sum(pends[None, :] <= first_slot[:, None], axis=1), N_EXPERTS - 1).astype(jnp.int32)
    n_used = (pends[-1] // MOE_BLOCK_M).astype(jnp.int32).reshape(1)
    return (pends - padded).astype(jnp.int32), pends.astype(jnp.int32), padded.astype(jnp.int32), block_e, n_used


DISPATCH_TILE = 512
IDX_SLOTS = 3


def _dispatch_body(pend_ref, padded_ref, dest_hbm, h_ref, xs_hbm, idx_smem, zero_scr, idx_sem, row_sem, zero_sem):
    i = pl.program_id(0)
    n = pl.num_programs(0)

    width = DISPATCH_TILE * TOP_K

    def idx_copy(tile):
        s = lax.rem(tile, IDX_SLOTS)
        return pltpu.make_async_copy(dest_hbm.at[pl.ds(pl.multiple_of(tile * width, width), width)],
                                     idx_smem.at[pl.ds(pl.multiple_of(s * width, width), width)], idx_sem.at[s])

    def zero_copy(e):
        start = pl.multiple_of(pend_ref[e] - MOE_BLOCK_M, MOE_BLOCK_M)
        return pltpu.make_async_copy(zero_scr, xs_hbm.at[pl.ds(start, MOE_BLOCK_M), :], zero_sem)

    def row_copy(g, j, d):
        return pltpu.make_async_copy(h_ref.at[g, pl.ds(j, 1), :], xs_hbm.at[pl.ds(d, 1), :], row_sem)

    @pl.when(i == 0)
    def _():
        zero_scr[...] = jnp.zeros_like(zero_scr)
        for e in range(N_EXPERTS):
            @pl.when(padded_ref[e] > 0)
            def _():
                zero_copy(e).start()
        for e in range(N_EXPERTS):
            @pl.when(padded_ref[e] > 0)
            def _():
                zero_copy(e).wait()
        idx_copy(0).start()

    @pl.when(i + 1 < n)
    def _():
        idx_copy(i + 1).start()

    idx_copy(i).wait()
    base = lax.rem(i, IDX_SLOTS) * width

    def issue(g, c):
        for j in range(SUBLANES):
            for k in range(TOP_K):
                d = idx_smem[base + g * (SUBLANES * TOP_K) + (j * TOP_K + k)]
                row_copy(g, j, d).start(priority=k % 2)
        return c

    lax.fori_loop(0, DISPATCH_TILE // SUBLANES, issue, 0)
    for _ in range(TOP_K):
        pltpu.make_async_copy(h_ref, h_ref, row_sem).wait()


def _scatter_rows(h, dest, pends, padded, n_slots):
    t = h.shape[0]
    tiles = t // DISPATCH_TILE
    grid_spec = pltpu.PrefetchScalarGridSpec(
        num_scalar_prefetch=2,
        grid=(tiles,),
        in_specs=[
            pl.BlockSpec(memory_space=pl.ANY),
            pl.BlockSpec((DISPATCH_TILE // SUBLANES, SUBLANES, D_MODEL), lambda i, pe, pd: (i, 0, 0)),
        ],
        out_specs=pl.BlockSpec(memory_space=pl.ANY),
        scratch_shapes=[
            pltpu.SMEM((IDX_SLOTS * DISPATCH_TILE * TOP_K,), jnp.int32),
            pltpu.VMEM((MOE_BLOCK_M, D_MODEL), F32),
            pltpu.SemaphoreType.DMA((IDX_SLOTS,)),
            pltpu.SemaphoreType.DMA(()),
            pltpu.SemaphoreType.DMA(()),
        ],
    )
    return pl.pallas_call(
        _dispatch_body,
        grid_spec=grid_spec,
        out_shape=jax.ShapeDtypeStruct((n_slots, D_MODEL), F32),
        compiler_params=_params("arbitrary"),
        name="dispatch",
    )(pends, padded, dest.reshape(-1), h.reshape(t // SUBLANES, SUBLANES, D_MODEL))


COMBINE_TILE = 256


def _combine_body(tile_off, dest_hbm, x1_ref, gate_ref, gf_ref, ys_hbm, y_ref, idx_smem, buf, idx_sem, row_sem):
    i = pl.program_id(0)
    n = pl.num_programs(0)

    width = COMBINE_TILE * TOP_K

    def idx_copy(tile):
        s = lax.rem(tile, IDX_SLOTS)
        return pltpu.make_async_copy(dest_hbm.at[pl.ds(pl.multiple_of((tile_off + tile) * width, width), width)],
                                     idx_smem.at[pl.ds(pl.multiple_of(s * width, width), width)], idx_sem.at[s])

    def row_copy(d, b, k, g, j):
        return pltpu.make_async_copy(ys_hbm.at[pl.ds(d, 1), :], buf.at[b, k, g, pl.ds(j, 1), :], row_sem.at[b])

    def issue_rows(tile):
        base = lax.rem(tile, IDX_SLOTS) * width
        b = lax.rem(tile, 2)

        def body(g, c):
            for j in range(SUBLANES):
                for k in range(TOP_K):
                    d = idx_smem[base + g * (SUBLANES * TOP_K) + (j * TOP_K + k)]
                    row_copy(d, b, k, g, j).start(priority=k % 2)
            return c

        lax.fori_loop(0, COMBINE_TILE // SUBLANES, body, 0)

    @pl.when(i == 0)
    def _():
        idx_copy(0).start()
        idx_copy(0).wait()
        issue_rows(0)

        @pl.when(n > 1)
        def _():
            idx_copy(1).start()

    @pl.when(i + 1 < n)
    def _():
        idx_copy(i + 1).wait()
        issue_rows(i + 1)

    @pl.when(i + 2 < n)
    def _():
        idx_copy(i + 2).start()

    b = lax.rem(i, 2)

    pltpu.make_async_copy(buf.at[b], buf.at[b], row_sem.at[b]).wait()
    gate = gate_ref[...]
    acc = x1_ref[...]
    for k in range(TOP_K):
        acc = acc + buf[b, k].reshape(COMBINE_TILE, D_MODEL) * gate[:, k:k + 1]
    y_ref[...] = acc * lax.rsqrt(jnp.mean(acc * acc, axis=-1, keepdims=True) + RMS_EPS) * gf_ref[...]


def _combine(x1, gate, dest, ys, g_final, row_off, rows):
    t = x1.shape[0]
    tile_off = row_off // COMBINE_TILE
    return pl.pallas_call(
        functools.partial(_combine_body, tile_off),
        grid=(rows // COMBINE_TILE,),
        in_specs=[
            pl.BlockSpec(memory_space=pl.ANY),
            pl.BlockSpec((COMBINE_TILE, D_MODEL), lambda i: (tile_off + i, 0)),
            pl.BlockSpec((COMBINE_TILE, LANES), lambda i: (tile_off + i, 0)),
            pl.BlockSpec((1, D_MODEL), lambda i: (0, 0)),
            pl.BlockSpec(memory_space=pl.ANY),
        ],
        out_specs=pl.BlockSpec((COMBINE_TILE, D_MODEL), lambda i: (i, 0)),
        out_shape=jax.ShapeDtypeStruct((rows, D_MODEL), F32),
        scratch_shapes=[
            pltpu.SMEM((IDX_SLOTS * COMBINE_TILE * TOP_K,), jnp.int32),
            pltpu.VMEM((2, TOP_K, COMBINE_TILE // SUBLANES, SUBLANES, D_MODEL), F32),
            pltpu.SemaphoreType.DMA((IDX_SLOTS,)),
            pltpu.SemaphoreType.DMA((2,)),
        ],
        compiler_params=_params("arbitrary"),
        name="combine",
    )(dest.reshape(-1), x1, gate, g_final.reshape(1, D_MODEL), ys)


def _lane_row(v):
    return jnp.pad(v.astype(F32), (0, LANES - v.shape[0])).reshape(1, LANES)


def kernel(x_prompt, x_sample, cache_swa_k, cache_swa_v, state_mlstm_C, state_mlstm_n, state_mlstm_m,
           g_mix, w_in, b_if, attn_sinks, mlstm_norm_w, w_attn_proj, w_mlstm_proj, w_out,
           g_ffn, w_router, b_router, w_e1, b_e1, w_e2, b_e2, g_final):
    bp, sp = x_prompt.shape[0], x_prompt.shape[1]
    bs, ls = x_sample.shape[0], x_sample.shape[1]
    tp = bp * sp
    ts = bs * SAMPLE_ROWS
    t_all = tp + ts
    l = 0
    w_in_p = _permute_w_in(w_in[l])
    wa, wm, wo = w_attn_proj[l].astype(BF16), w_mlstm_proj[l].astype(BF16), w_out[l].astype(BF16)
    wr_f32 = jnp.pad(w_router[l], ((0, 0), (0, ROUTER_PAD - N_EXPERTS)))
    wr_hi = wr_f32.astype(BF16)
    wr_pad = jnp.concatenate([wr_hi, (wr_f32 - wr_hi.astype(F32)).astype(BF16)], axis=1)
    br_pad = _lane_row(b_router[l])
    b1, b2 = b_e1[l].reshape(N_EXPERTS, 1, 2 * D_FF), b_e2[l].reshape(N_EXPERTS, 1, D_MODEL)
    sinks_row = _lane_row(attn_sinks[l])
    bias_row = _lane_row(b_if[l])
    norm_w_row = mlstm_norm_w[l].reshape(1, M_V_WIDTH)

    xp = x_prompt.reshape(tp, D_MODEL)
    xs = jnp.pad(x_sample, ((0, 0), (0, SAMPLE_ROWS - ls), (0, 0))).reshape(ts, D_MODEL)
    z_all = _inproj(xp, xs, g_mix[l], w_in_p, 512)

    a_p, k1, v1, w1, w2 = _attn_prompt(z_all, sinks_row, bp, sp, w_e1[l], w_e2[l])
    a_s, k2, v2 = _attn_sample(z_all, sinks_row, cache_swa_k[l].reshape(bs, WINDOW, KV_WIDTH),
                               cache_swa_v[l].reshape(bs, WINDOW, KV_WIDTH), tp, ls)
    mh_p, c1, n1, m1 = _mlstm_prompt(z_all, bias_row, norm_w_row, bp, sp)
    mh_s, c2, n2, m2 = _mlstm_sample(z_all, bias_row, norm_w_row, state_mlstm_C[l], state_mlstm_n[l],
                                     state_mlstm_m[l], tp, ls)

    x1, h2, logits = _merge(xp, xs, a_p, a_s, mh_p, mh_s, z_all, wa, wm, wo, g_ffn[l], wr_pad, br_pad, 256)

    meta, gate, counts = _route(logits)
    n_blocks = -(-t_all * TOP_K // MOE_BLOCK_M) + N_EXPERTS
    pstarts, pends, padded, block_e, n_used = _slot_tables(counts[0, :N_EXPERTS].astype(jnp.int32), n_blocks)
    dest = pstarts[meta[:, LANE_E:LANE_E + TOP_K]] + meta[:, LANE_RANK:LANE_RANK + TOP_K]
    xs_sorted = _scatter_rows(h2, dest, pends, padded, n_blocks * MOE_BLOCK_M)
    ys = _experts(block_e, n_used, xs_sorted, w1, b1, w2, b2)
    y_prompt = _combine(x1, gate, dest, ys, g_final, 0, tp).reshape(bp, sp, D_MODEL)
    y_sample = _combine(x1, gate, dest, ys, g_final, tp, ts).reshape(bs, SAMPLE_ROWS, D_MODEL)[:, :ls]
    k1 = k1.reshape(bp, WINDOW, N_KV_HEADS, HEAD_DIM)
    v1 = v1.reshape(bp, WINDOW, N_KV_HEADS, HEAD_DIM)
    k2 = k2.reshape(bs, WINDOW, N_KV_HEADS, HEAD_DIM)
    v2 = v2.reshape(bs, WINDOW, N_KV_HEADS, HEAD_DIM)
    m1 = m1[:, :, 0]
    st = lambda t: t[None]
    return (y_prompt, y_sample, st(k1), st(v1), st(c1), st(n1), st(m1), st(k2), st(v2), st(c2), st(n2), st(m2))
```

```python
import functools

import jax
import jax.numpy as jnp
from jax import lax
from jax.experimental import pallas as pl
from jax.experimental.pallas import tpu as pltpu

D_MODEL = 2048
N_HEADS = 16
N_KV_HEADS = 2
HEAD_DIM = 64
Q_PER_KV = N_HEADS // N_KV_HEADS
WINDOW = 128
ATTN_WIDTH = N_HEADS * HEAD_DIM
KV_WIDTH = N_KV_HEADS * HEAD_DIM
M_HEADS = 4
M_QK_DIM = 128
M_V_DIM = 256
M_QK_WIDTH = M_HEADS * M_QK_DIM
M_V_WIDTH = M_HEADS * M_V_DIM
M_CHUNK = 64
N_EXPERTS = 32
TOP_K = 4
D_FF = D_MODEL
SWIGLU_LIMIT = 7.0
SWIGLU_ALPHA = 1.702
RMS_EPS = 1e-5
NEG = -1e30

F32 = jnp.float32
BF16 = jnp.bfloat16

LANES = 128
SUBLANES = 8
VMEM_LIMIT_BYTES = 56 * 1024 * 1024

COL_GZ = 0
COL_AQ = COL_GZ + 2 * D_MODEL
COL_MV = COL_AQ + ATTN_WIDTH
COL_MO = COL_MV + M_V_WIDTH
COL_MQ = COL_MO + M_V_WIDTH
COL_MK = COL_MQ + M_QK_WIDTH
COL_AK = COL_MK + M_QK_WIDTH
COL_IF = COL_AK + 2 * KV_WIDTH
IN_TILE_N = 2176
IN_WIDTH_PAD = -(-(COL_IF + LANES) // IN_TILE_N) * IN_TILE_N

SAMPLE_ROWS = SUBLANES
MOE_BLOCK_M = 512
MOE_FF_CHUNK = 1024
ROUTER_PAD = LANES


def _permute_w_in(w_in):
    splits = (ATTN_WIDTH, KV_WIDTH, KV_WIDTH, M_QK_WIDTH, M_QK_WIDTH, M_V_WIDTH, M_V_WIDTH, M_HEADS, M_HEADS,
              2 * D_MODEL)
    parts, off = [], 0
    for width in splits:
        parts.append(w_in[:, off:off + width])
        off += width
    aq, ak, av, mq, mk, mv, mo, mi, mf, gz = parts
    cols = jnp.concatenate([gz, aq, mv, mo, mq, mk, ak, av, mi, mf], axis=1)
    cols = jnp.pad(cols, ((0, 0), (0, IN_WIDTH_PAD - cols.shape[1])))
    return cols.astype(BF16)


def _params(*semantics):
    return pltpu.CompilerParams(dimension_semantics=semantics, vmem_limit_bytes=VMEM_LIMIT_BYTES)


def _log_sigmoid(x):
    return jnp.minimum(x, 0.0) - jnp.log1p(jnp.exp(-jnp.abs(x)))


def _inproj_body(tiles_a, xa_ref, xb_ref, g_ref, w_ref, z_ref, h_scr):
    def normalise(x_ref):
        x = x_ref[...]
        y = x * lax.rsqrt(jnp.mean(x * x, axis=-1, keepdims=True) + RMS_EPS)
        h_scr[...] = (y * g_ref[...]).astype(BF16)

    first = pl.program_id(1) == 0
    in_a = pl.program_id(0) < tiles_a

    @pl.when(first & in_a)
    def _():
        normalise(xa_ref)

    @pl.when(first & jnp.logical_not(in_a))
    def _():
        normalise(xb_ref)

    z_ref[...] = jnp.dot(h_scr[...], w_ref[...], preferred_element_type=F32)


def _inproj(x_a, x_b, g, w_bf16, tile_m):
    tiles_a, tiles_b = x_a.shape[0] // tile_m, x_b.shape[0] // tile_m
    n = w_bf16.shape[1]
    return pl.pallas_call(
        functools.partial(_inproj_body, tiles_a),
        grid=(tiles_a + tiles_b, n // IN_TILE_N),
        in_specs=[
            pl.BlockSpec((tile_m, D_MODEL), lambda i, j: (jnp.minimum(i, tiles_a - 1), 0)),
            pl.BlockSpec((tile_m, D_MODEL), lambda i, j: (jnp.maximum(i - tiles_a, 0), 0)),
            pl.BlockSpec((1, D_MODEL), lambda i, j: (0, 0)),
            pl.BlockSpec((D_MODEL, IN_TILE_N), lambda i, j: (0, j)),
        ],
        out_specs=pl.BlockSpec((tile_m, IN_TILE_N), lambda i, j: (i, j)),
        out_shape=jax.ShapeDtypeStruct(((tiles_a + tiles_b) * tile_m, n), F32),
        scratch_shapes=[pltpu.VMEM((tile_m, D_MODEL), BF16)],
        compiler_params=_params("arbitrary", "arbitrary"),
        name="inproj",
    )(x_a, x_b, g.reshape(1, D_MODEL), w_bf16)


assert KV_WIDTH == LANES and N_KV_HEADS == 2 and LANES == 2 * HEAD_DIM
Q_TILES = ATTN_WIDTH // LANES
TILES_PER_KV = Q_TILES // N_KV_HEADS
_NT = (((1,), (1,)), ((), ()))


def _kv_placements(x):
    lo = lax.broadcasted_iota(jnp.int32, x.shape, 1) < HEAD_DIM
    swapped = pltpu.roll(x, HEAD_DIM, axis=1)
    return ((jnp.where(lo, x, 0.0).astype(BF16), jnp.where(lo, 0.0, swapped).astype(BF16)),
            (jnp.where(lo, swapped, 0.0).astype(BF16), jnp.where(lo, 0.0, x).astype(BF16)))


def _slab_head(g, p, j):
    return g * Q_PER_KV + 2 * j + p


def _attn_prompt_body(q_ref, kvc_ref, kvp_ref, sink_ref, w_ref, a_ref, k_ref, v_ref, wb_ref):
    wb_ref[...] = w_ref[...].astype(BF16)
    n = pl.program_id(1)
    kc, vc = kvc_ref[:, :KV_WIDTH], kvc_ref[:, KV_WIDTH:]
    k_ref[...] = kc
    v_ref[...] = vc
    kk = _kv_placements(jnp.concatenate([kvp_ref[:, :KV_WIDTH], kc], axis=0))
    vv = _kv_placements(jnp.concatenate([kvp_ref[:, KV_WIDTH:], vc], axis=0))
    q = q_ref[...].astype(BF16)
    row = lax.broadcasted_iota(jnp.int32, (WINDOW, WINDOW), 0)
    col = lax.broadcasted_iota(jnp.int32, (WINDOW, WINDOW), 1)
    mask = jnp.concatenate([(col > row) & (n > 0), col <= row], axis=1)
    slabs, sinks = [], []
    for g in range(N_KV_HEADS):
        qt = jnp.concatenate([q[:, (TILES_PER_KV * g + j) * LANES:(TILES_PER_KV * g + j + 1) * LANES]
                              for j in range(TILES_PER_KV)], axis=0)
        for p in range(2):
            s = lax.dot_general(qt, kk[g][p], _NT, preferred_element_type=F32) * (HEAD_DIM ** -0.5)
            for j in range(TILES_PER_KV):
                h = _slab_head(g, p, j)
                slabs.append(jnp.where(mask, s[j * WINDOW:(j + 1) * WINDOW], NEG))
                sinks.append(jnp.broadcast_to(sink_ref[:, h:h + 1], (WINDOW, LANES)))
    s_all = jnp.concatenate(slabs, axis=0)
    sink = jnp.concatenate(sinks, axis=0)
    m = jnp.maximum(jnp.broadcast_to(jnp.max(s_all, axis=1, keepdims=True), sink.shape), sink)
    p_all = jnp.exp(s_all - jnp.concatenate([m, m], axis=1))
    p_hi = p_all.astype(BF16)
    p_lo = (p_all - p_hi.astype(F32)).astype(BF16)
    ones = jnp.ones((2 * WINDOW, LANES), BF16)
    denom = (jnp.dot(p_hi, ones, preferred_element_type=F32) + jnp.dot(p_lo, ones, preferred_element_type=F32)
             + jnp.exp(sink - m))
    inv = 1.0 / denom
    pn = (p_all * jnp.concatenate([inv, inv], axis=1)).astype(BF16)
    half = TILES_PER_KV * WINDOW
    for g in range(N_KV_HEADS):
        base = 2 * half * g
        o = (jnp.dot(pn[base:base + half], vv[g][0], preferred_element_type=F32)
             + jnp.dot(pn[base + half:base + 2 * half], vv[g][1], preferred_element_type=F32))
        for j in range(TILES_PER_KV):
            t = TILES_PER_KV * g + j
            a_ref[:, t * LANES:(t + 1) * LANES] = o[j * WINDOW:(j + 1) * WINDOW].astype(BF16)


def _cast_slices(w, steps):
    rows = w.reshape(-1, w.shape[-1])
    per_step = rows.shape[0] // steps
    assert per_step * steps == rows.shape[0] and per_step % 16 == 0
    return rows, per_step


def _attn_prompt(z_all, sinks_row, batch, seq, w):
    nb = seq // WINDOW
    w_rows, r1 = _cast_slices(w, batch * nb)
    step = lambda b, n: (b * nb + n, 0)
    a, k, v, wb = pl.pallas_call(
        _attn_prompt_body,
        grid=(batch, nb),
        in_specs=[
            pl.BlockSpec((WINDOW, ATTN_WIDTH), lambda b, n: (b * nb + n, COL_AQ // ATTN_WIDTH)),
            pl.BlockSpec((WINDOW, 2 * KV_WIDTH), lambda b, n: (b * nb + n, COL_AK // (2 * KV_WIDTH))),
            pl.BlockSpec((WINDOW, 2 * KV_WIDTH),
                         lambda b, n: (b * nb + jnp.maximum(n - 1, 0), COL_AK // (2 * KV_WIDTH))),
            pl.BlockSpec((1, LANES), lambda b, n: (0, 0)),
            pl.BlockSpec((r1, w_rows.shape[1]), step),
        ],
        out_specs=[
            pl.BlockSpec((WINDOW, ATTN_WIDTH), lambda b, n: (b * nb + n, 0)),
            pl.BlockSpec((None, WINDOW, KV_WIDTH), lambda b, n: (b, 0, 0)),
            pl.BlockSpec((None, WINDOW, KV_WIDTH), lambda b, n: (b, 0, 0)),
            pl.BlockSpec((r1, w_rows.shape[1]), step),
        ],
        out_shape=[
            jax.ShapeDtypeStruct((batch * seq, ATTN_WIDTH), BF16),
            jax.ShapeDtypeStruct((batch, WINDOW, KV_WIDTH), F32),
            jax.ShapeDtypeStruct((batch, WINDOW, KV_WIDTH), F32),
            jax.ShapeDtypeStruct(w_rows.shape, BF16),
        ],
        compiler_params=_params("arbitrary", "arbitrary"),
        name="attn_prompt",
    )(z_all, z_all, z_all, sinks_row, w_rows)
    return a, k, v, wb.reshape(w.shape)


ATTN_SAMPLE_GROUP = 8


def _attn_sample_body(n_valid, q_ref, kv_ref, ck_ref, cv_ref, sink_ref, a_ref, nk_ref, nv_ref, o_scr):
    rs = SAMPLE_ROWS
    slab = TILES_PER_KV * rs
    r8 = lax.broadcasted_iota(jnp.int32, (rs, KV_WIDTH), 0)
    qi_c = jnp.bitwise_and(lax.broadcasted_iota(jnp.int32, (slab, WINDOW), 0), rs - 1)
    mask_c = lax.broadcasted_iota(jnp.int32, (slab, WINDOW), 1) > qi_c
    qi_n = jnp.bitwise_and(lax.broadcasted_iota(jnp.int32, (slab, rs), 0), rs - 1)
    mask_n = lax.broadcasted_iota(jnp.int32, (slab, rs), 1) <= qi_n
    sc_list, sn_list, sink_list, values = [], [], [], []
    for b in range(ATTN_SAMPLE_GROUP):
        rows = slice(b * rs, (b + 1) * rs)
        q = q_ref[rows, :]
        kn, vn = kv_ref[rows, :KV_WIDTH], kv_ref[rows, KV_WIDTH:]
        ck, cv = ck_ref[b], cv_ref[b]
        for src, new, dst in ((ck, kn, nk_ref), (cv, vn, nv_ref)):
            rolled = pltpu.roll(src, WINDOW - n_valid, axis=0)
            dst[b] = rolled
            tail = jnp.where(r8 >= rs - n_valid, pltpu.roll(new, rs - n_valid, axis=0), rolled[WINDOW - rs:, :])
            dst[b, WINDOW - rs:, :] = tail
        kk_c, kk_n = _kv_placements(ck), _kv_placements(kn)
        values.append((_kv_placements(cv), _kv_placements(vn)))
        for g in range(N_KV_HEADS):
            qt = jnp.concatenate([q[:, (TILES_PER_KV * g + j) * LANES:(TILES_PER_KV * g + j + 1) * LANES]
                                  for j in range(TILES_PER_KV)], axis=0).astype(BF16)
            for p in range(2):
                s_c = lax.dot_general(qt, kk_c[g][p], _NT, preferred_element_type=F32) * (HEAD_DIM ** -0.5)
                s_n = lax.dot_general(qt, kk_n[g][p], _NT, preferred_element_type=F32) * (HEAD_DIM ** -0.5)
                sc_list.append(jnp.where(mask_c, s_c, NEG))
                sn_list.append(jnp.where(mask_n, s_n, NEG))
                for j in range(TILES_PER_KV):
                    h = _slab_head(g, p, j)
                    sink_list.append(jnp.broadcast_to(sink_ref[:, h:h + 1], (rs, 1)))
    s_c = jnp.concatenate(sc_list, axis=0)
    s_n = jnp.concatenate(sn_list, axis=0)
    sink = jnp.concatenate(sink_list, axis=0)
    m = jnp.maximum(jnp.maximum(jnp.max(s_c, axis=1, keepdims=True), jnp.max(s_n, axis=1, keepdims=True)), sink)
    p_c = jnp.exp(s_c - m)
    p_n = jnp.exp(s_n - m)
    inv = 1.0 / (jnp.sum(p_c, axis=1, keepdims=True) + jnp.sum(p_n, axis=1, keepdims=True) + jnp.exp(sink - m))
    p_c = (p_c * inv).astype(BF16)
    p_n = (p_n * inv).astype(BF16)
    for b in range(ATTN_SAMPLE_GROUP):
        vv_c, vv_n = values[b]
        for g in range(N_KV_HEADS):
            base = (b * N_KV_HEADS + g) * 2 * slab
            o = (jnp.dot(p_c[base:base + slab], vv_c[g][0], preferred_element_type=F32)
                 + jnp.dot(p_c[base + slab:base + 2 * slab], vv_c[g][1], preferred_element_type=F32)
                 + jnp.dot(p_n[base:base + slab], vv_n[g][0], preferred_element_type=F32)
                 + jnp.dot(p_n[base + slab:base + 2 * slab], vv_n[g][1], preferred_element_type=F32))
            for j in range(TILES_PER_KV):
                t = TILES_PER_KV * g + j
                o_scr[b * rs:(b + 1) * rs, t * LANES:(t + 1) * LANES] = o[j * rs:(j + 1) * rs]
    a_ref[...] = o_scr[...].astype(BF16)


def _attn_sample(z_all, sinks_row, cache_k, cache_v, row_off, n_valid):
    b = cache_k.shape[0]
    gr = ATTN_SAMPLE_GROUP
    rows = gr * SAMPLE_ROWS
    rb = row_off // rows
    return pl.pallas_call(
        functools.partial(_attn_sample_body, n_valid),
        grid=(b // gr,),
        in_specs=[
            pl.BlockSpec((rows, ATTN_WIDTH), lambda i: (rb + i, COL_AQ // ATTN_WIDTH)),
            pl.BlockSpec((rows, 2 * KV_WIDTH), lambda i: (rb + i, COL_AK // (2 * KV_WIDTH))),
            pl.BlockSpec((gr, WINDOW, KV_WIDTH), lambda i: (i, 0, 0)),
            pl.BlockSpec((gr, WINDOW, KV_WIDTH), lambda i: (i, 0, 0)),
            pl.BlockSpec((1, LANES), lambda i: (0, 0)),
        ],
        out_specs=[
            pl.BlockSpec((rows, ATTN_WIDTH), lambda i: (i, 0)),
            pl.BlockSpec((gr, WINDOW, KV_WIDTH), lambda i: (i, 0, 0)),
            pl.BlockSpec((gr, WINDOW, KV_WIDTH), lambda i: (i, 0, 0)),
        ],
        out_shape=[
            jax.ShapeDtypeStruct((b * SAMPLE_ROWS, ATTN_WIDTH), BF16),
            jax.ShapeDtypeStruct(cache_k.shape, F32),
            jax.ShapeDtypeStruct(cache_v.shape, F32),
        ],
        scratch_shapes=[pltpu.VMEM((rows, ATTN_WIDTH), F32)],
        compiler_params=_params("arbitrary"),
        name="attn_sample",
    )(z_all, z_all, cache_k, cache_v, sinks_row)


def _mlstm_gates(if_blk, bias_row, n_valid):
    length = if_blk.shape[0]
    g = if_blk + bias_row
    lane = lax.broadcasted_iota(jnp.int32, g.shape, 1)
    g = jnp.where(lane < M_HEADS, g, _log_sigmoid(g))
    if n_valid < length:
        row = lax.broadcasted_iota(jnp.int32, g.shape, 0)
        g = jnp.where(row < n_valid, g, jnp.where(lane < M_HEADS, NEG, 0.0))
    return g


def _mlstm_chunk(q, k, v, gates, gates_t, h, c, n, m):
    length = q.shape[0]
    i_col = gates[:, h:h + 1]
    f_col = gates[:, M_HEADS + h:M_HEADS + h + 1]
    i_row = gates_t[h:h + 1, :]
    f_row = gates_t[M_HEADS + h:M_HEADS + h + 1, :]
    t_idx = lax.broadcasted_iota(jnp.int32, (length, length), 0)
    s_idx = lax.broadcasted_iota(jnp.int32, (length, length), 1)
    causal = s_idx <= t_idx
    bcum_col = jnp.sum(jnp.where(causal, f_row, 0.0), axis=1, keepdims=True)
    bcum_row = jnp.sum(jnp.where(t_idx <= s_idx, f_col, 0.0), axis=0, keepdims=True)
    dlog = jnp.where(causal, bcum_col - bcum_row + i_row, NEG)
    inter = bcum_col + m
    mt = jnp.maximum(inter, jnp.max(dlog, axis=1, keepdims=True))
    w_inter = jnp.exp(inter - mt)
    qb = q.astype(BF16)
    ks = k * (M_QK_DIM ** -0.5)
    vb = v.astype(BF16)
    qk = lax.dot_general(qb, ks.astype(BF16), (((1,), (1,)), ((), ())), preferred_element_type=F32)
    qk = qk * jnp.exp(dlog - mt)
    num = (w_inter * jnp.dot(qb, c.astype(BF16), preferred_element_type=F32)
           + jnp.dot(qk.astype(BF16), vb, preferred_element_type=F32))
    den = w_inter * jnp.sum(q * n, axis=1, keepdims=True) + jnp.sum(qk, axis=1, keepdims=True)
    h_t = num / jnp.maximum(jnp.abs(den), jnp.exp(-mt))
    m_new = mt[length - 1:length, :]
    w_state = jnp.exp(bcum_col[length - 1:length, :] - bcum_col + i_col - m_new)
    decay = jnp.exp(inter[length - 1:length, :] - m_new)
    kw = ks * w_state
    c_new = decay * c + jnp.dot(kw.T.astype(BF16), vb, preferred_element_type=F32)
    n_new = decay * n + jnp.sum(kw, axis=0, keepdims=True)
    return h_t, c_new, n_new, m_new


def _mlstm_head_out(h_t, norm_w, o_gate):
    hs = h_t * lax.rsqrt(jnp.mean(h_t * h_t, axis=-1, keepdims=True) + RMS_EPS)
    return hs * norm_w * jax.nn.sigmoid(o_gate)


MLSTM_SEG = 256


MLSTM_BATCH = 2
MLSTM_PROMPT_CHUNK = 256


def _mlstm_prompt_body(*refs):
    nb = MLSTM_BATCH
    ins, rest = refs[:5 * nb], refs[5 * nb:]
    bias_ref, nw_ref, w_ref, mh_ref, c_out, n_out, m_out, wb_ref, c_scr, n_scr, m_scr = rest
    seg = pl.program_id(1)
    wb_ref[...] = w_ref[...].astype(BF16)

    @pl.when(seg == 0)
    def _():
        c_scr[...] = jnp.zeros_like(c_scr)
        n_scr[...] = jnp.zeros_like(n_scr)
        m_scr[...] = jnp.full_like(m_scr, NEG)

    def chunk_body(ci, carry):
        r = pl.multiple_of(ci * MLSTM_PROMPT_CHUNK, MLSTM_PROMPT_CHUNK)
        rows = pl.ds(r, MLSTM_PROMPT_CHUNK)
        state = [[(c_scr[u, h], n_scr[u, h:h + 1, :], m_scr[u, h:h + 1, 0:1]) for h in range(M_HEADS)]
                 for u in range(nb)]
        results = []
        for u in range(nb):
            q_ref, k_ref, v_ref, o_ref, if_ref = ins[5 * u:5 * u + 5]
            gates = _mlstm_gates(if_ref[rows, :], bias_ref[...], MLSTM_PROMPT_CHUNK)
            gates_t = gates.T
            for h in range(M_HEADS):
                qs = slice(h * M_QK_DIM, (h + 1) * M_QK_DIM)
                vs = slice(h * M_V_DIM, (h + 1) * M_V_DIM)
                h_t, c_new, n_new, m_new = _mlstm_chunk(
                    q_ref[rows, qs], k_ref[rows, qs], v_ref[rows, vs], gates, gates_t, h, *state[u][h])
                out = _mlstm_head_out(h_t, nw_ref[:, vs], o_ref[rows, vs]).astype(BF16)
                results.append((u, h, vs, out, c_new, n_new, m_new))
        for u, h, vs, out, c_new, n_new, m_new in results:
            c_scr[u, h] = c_new
            n_scr[u, h:h + 1, :] = n_new
            m_scr[u, h:h + 1, :] = jnp.broadcast_to(m_new, (1, LANES))
            mh_ref[u, rows, vs] = out
        return carry

    lax.fori_loop(0, MLSTM_SEG // MLSTM_PROMPT_CHUNK, chunk_body, 0)

    @pl.when(seg == pl.num_programs(1) - 1)
    def _():
        c_out[...] = c_scr[...]
        n_out[...] = n_scr[:, 0:M_HEADS, :]
        m_out[...] = m_scr[:, 0:M_HEADS, :]


def _mlstm_prompt(z_all, bias_row, norm_w_row, batch, seq, w):
    ns = seq // MLSTM_SEG
    nb = MLSTM_BATCH
    w_rows, wr = _cast_slices(w, (batch // nb) * ns)
    in_specs, args = [], []
    for u in range(nb):
        row = lambda p, s, u=u: (p * nb + u) * ns + s
        in_specs += [
            pl.BlockSpec((MLSTM_SEG, M_QK_WIDTH), lambda p, s, row=row: (row(p, s), COL_MQ // M_QK_WIDTH)),
            pl.BlockSpec((MLSTM_SEG, M_QK_WIDTH), lambda p, s, row=row: (row(p, s), COL_MK // M_QK_WIDTH)),
            pl.BlockSpec((MLSTM_SEG, M_V_WIDTH), lambda p, s, row=row: (row(p, s), COL_MV // M_V_WIDTH)),
            pl.BlockSpec((MLSTM_SEG, M_V_WIDTH), lambda p, s, row=row: (row(p, s), COL_MO // M_V_WIDTH)),
            pl.BlockSpec((MLSTM_SEG, LANES), lambda p, s, row=row: (row(p, s), COL_IF // LANES)),
        ]
        args += [z_all] * 5
    in_specs += [pl.BlockSpec((1, LANES), lambda p, s: (0, 0)), pl.BlockSpec((1, M_V_WIDTH), lambda p, s: (0, 0)),
                 pl.BlockSpec((wr, w_rows.shape[1]), lambda p, s: (p * ns + s, 0))]
    mh, c, n, m, wb = pl.pallas_call(
        _mlstm_prompt_body,
        grid=(batch // nb, ns),
        in_specs=in_specs,
        out_specs=[
            pl.BlockSpec((None, nb, MLSTM_SEG, M_V_WIDTH), lambda p, s: (p, 0, s, 0)),
            pl.BlockSpec((nb, M_HEADS, M_QK_DIM, M_V_DIM), lambda p, s: (p, 0, 0, 0)),
            pl.BlockSpec((nb, M_HEADS, M_QK_DIM), lambda p, s: (p, 0, 0)),
            pl.BlockSpec((nb, M_HEADS, LANES), lambda p, s: (p, 0, 0)),
            pl.BlockSpec((wr, w_rows.shape[1]), lambda p, s: (p * ns + s, 0)),
        ],
        out_shape=[
            jax.ShapeDtypeStruct((batch // nb, nb, seq, M_V_WIDTH), BF16),
            jax.ShapeDtypeStruct((batch, M_HEADS, M_QK_DIM, M_V_DIM), F32),
            jax.ShapeDtypeStruct((batch, M_HEADS, M_QK_DIM), F32),
            jax.ShapeDtypeStruct((batch, M_HEADS, LANES), F32),
            jax.ShapeDtypeStruct(w_rows.shape, BF16),
        ],
        scratch_shapes=[
            pltpu.VMEM((nb, M_HEADS, M_QK_DIM, M_V_DIM), F32),
            pltpu.VMEM((nb, SUBLANES, M_QK_DIM), F32),
            pltpu.VMEM((nb, SUBLANES, LANES), F32),
        ],
        compiler_params=_params("arbitrary", "arbitrary"),
        name="mlstm_prompt",
    )(*args, bias_row, norm_w_row, w_rows)
    return mh.reshape(batch * seq, M_V_WIDTH), c, n, m, wb.reshape(w.shape)


MLSTM_SAMPLE_GROUP = 8


def _mlstm_sample_body(n_valid, q_ref, k_ref, v_ref, o_ref, if_ref, bias_ref, nw_ref, c0_ref, n0_ref, m0_ref,
                       mh_ref, c_out, n_out, m_out, h_scr):
    for g in range(MLSTM_SAMPLE_GROUP):
        rows = slice(g * SAMPLE_ROWS, (g + 1) * SAMPLE_ROWS)
        gates = _mlstm_gates(if_ref[rows, :], bias_ref[...], n_valid)
        gates_t = jnp.transpose(jnp.concatenate([gates] * (LANES // SAMPLE_ROWS), axis=0))[:, :SAMPLE_ROWS]
        for h in range(M_HEADS):
            qs = slice(h * M_QK_DIM, (h + 1) * M_QK_DIM)
            vs = slice(h * M_V_DIM, (h + 1) * M_V_DIM)
            h_t, c_new, n_new, m_new = _mlstm_chunk(
                q_ref[rows, qs], k_ref[rows, qs], v_ref[rows, vs], gates, gates_t, h,
                c0_ref[g, h], n0_ref[g, h:h + 1, :], m0_ref[g:g + 1, h:h + 1])
            c_out[g, h] = c_new
            n_out[g, h:h + 1, :] = n_new
            m_out[g:g + 1, h:h + 1] = m_new
            h_scr[rows, vs] = _mlstm_head_out(h_t, nw_ref[:, vs], o_ref[rows, vs])
    mh_ref[...] = h_scr[...].astype(BF16)


def _mlstm_sample(z_all, bias_row, norm_w_row, c0, n0, m0, row_off, n_valid):
    b = c0.shape[0]
    gr = MLSTM_SAMPLE_GROUP
    rows = gr * SAMPLE_ROWS
    rb = row_off // rows
    m0g = m0.reshape(b // gr, gr, M_HEADS)
    outs = pl.pallas_call(
        functools.partial(_mlstm_sample_body, n_valid),
        grid=(b // gr,),
        in_specs=[
            pl.BlockSpec((rows, M_QK_WIDTH), lambda i: (rb + i, COL_MQ // M_QK_WIDTH)),
            pl.BlockSpec((rows, M_QK_WIDTH), lambda i: (rb + i, COL_MK // M_QK_WIDTH)),
            pl.BlockSpec((rows, M_V_WIDTH), lambda i: (rb + i, COL_MV // M_V_WIDTH)),
            pl.BlockSpec((rows, M_V_WIDTH), lambda i: (rb + i, COL_MO // M_V_WIDTH)),
            pl.BlockSpec((rows, LANES), lambda i: (rb + i, COL_IF // LANES)),
            pl.BlockSpec((1, LANES), lambda i: (0, 0)),
            pl.BlockSpec((1, M_V_WIDTH), lambda i: (0, 0)),
            pl.BlockSpec((gr, M_HEADS, M_QK_DIM, M_V_DIM), lambda i: (i, 0, 0, 0)),
            pl.BlockSpec((gr, M_HEADS, M_QK_DIM), lambda i: (i, 0, 0)),
            pl.BlockSpec((None, gr, M_HEADS), lambda i: (i, 0, 0)),
        ],
        out_specs=[
            pl.BlockSpec((rows, M_V_WIDTH), lambda i: (i, 0)),
            pl.BlockSpec((gr, M_HEADS, M_QK_DIM, M_V_DIM), lambda i: (i, 0, 0, 0)),
            pl.BlockSpec((gr, M_HEADS, M_QK_DIM), lambda i: (i, 0, 0)),
            pl.BlockSpec((None, gr, M_HEADS), lambda i: (i, 0, 0)),
        ],
        out_shape=[
            jax.ShapeDtypeStruct((b * SAMPLE_ROWS, M_V_WIDTH), BF16),
            jax.ShapeDtypeStruct(c0.shape, F32),
            jax.ShapeDtypeStruct(n0.shape, F32),
            jax.ShapeDtypeStruct(m0g.shape, F32),
        ],
        scratch_shapes=[pltpu.VMEM((rows, M_V_WIDTH), F32)],
        compiler_params=_params("arbitrary"),
        name="mlstm_sample",
    )(z_all, z_all, z_all, z_all, z_all, bias_row, norm_w_row, c0, n0, m0g)
    mh, c, n, m = outs
    return mh, c, n, m.reshape(b, M_HEADS)


def _merge_body(tiles_a, xa_ref, xb_ref, aa_ref, ab_ref, ma_ref, mb_ref, gza_ref, gzm_ref, wa_ref, wm_ref, wo_ref,
                gf_ref, wr_ref, br_ref, x1_ref, h2_ref, lg_ref):
    in_a = pl.program_id(0) < tiles_a
    x = jnp.where(in_a, xa_ref[...], xb_ref[...])
    a = jnp.where(in_a, aa_ref[...], ab_ref[...])
    mh = jnp.where(in_a, ma_ref[...], mb_ref[...])
    pa = jnp.dot(a, wa_ref[...], preferred_element_type=F32)
    pm = jnp.dot(mh, wm_ref[...], preferred_element_type=F32)
    merged = jax.nn.sigmoid(gza_ref[...]) * pa + jax.nn.sigmoid(gzm_ref[...]) * pm
    x1 = x + jnp.dot(merged.astype(BF16), wo_ref[...], preferred_element_type=F32)
    x1_ref[...] = x1
    h2 = x1 * lax.rsqrt(jnp.mean(x1 * x1, axis=-1, keepdims=True) + RMS_EPS) * gf_ref[...]
    h2_ref[...] = h2
    hi = h2.astype(BF16)
    lo = (h2 - hi.astype(F32)).astype(BF16)
    both = jnp.dot(hi, wr_ref[...], preferred_element_type=F32)
    lg_ref[...] = (both[:, :ROUTER_PAD] + both[:, ROUTER_PAD:]
                   + jnp.dot(lo, wr_ref[:, :ROUTER_PAD], preferred_element_type=F32) + br_ref[...])


def _merge(x_a, x_b, a_a, a_b, mh_a, mh_b, z_all, wa, wm, wo, g_ffn, wr_pad, br_pad, tile_m):
    tiles_a, tiles_b = x_a.shape[0] // tile_m, x_b.shape[0] // tile_m
    t_all = z_all.shape[0]
    const = lambda i: (0, 0)
    from_a = lambda i: (jnp.minimum(i, tiles_a - 1), 0)
    from_b = lambda i: (jnp.maximum(i - tiles_a, 0), 0)
    return pl.pallas_call(
        functools.partial(_merge_body, tiles_a),
        grid=(tiles_a + tiles_b,),
        in_specs=[
            pl.BlockSpec((tile_m, D_MODEL), from_a),
            pl.BlockSpec((tile_m, D_MODEL), from_b),
            pl.BlockSpec((tile_m, ATTN_WIDTH), from_a),
            pl.BlockSpec((tile_m, ATTN_WIDTH), from_b),
            pl.BlockSpec((tile_m, M_V_WIDTH), from_a),
            pl.BlockSpec((tile_m, M_V_WIDTH), from_b),
            pl.BlockSpec((tile_m, D_MODEL), lambda i: (i, COL_GZ // D_MODEL)),
            pl.BlockSpec((tile_m, D_MODEL), lambda i: (i, COL_GZ // D_MODEL + 1)),
            pl.BlockSpec((ATTN_WIDTH, D_MODEL), const, pipeline_mode=pl.Buffered(1)),
            pl.BlockSpec((M_V_WIDTH, D_MODEL), const, pipeline_mode=pl.Buffered(1)),
            pl.BlockSpec((D_MODEL, D_MODEL), const, pipeline_mode=pl.Buffered(1)),
            pl.BlockSpec((1, D_MODEL), const),
            pl.BlockSpec((D_MODEL, 2 * ROUTER_PAD), const),
            pl.BlockSpec((1, ROUTER_PAD), const),
        ],
        out_specs=[
            pl.BlockSpec((tile_m, D_MODEL), lambda i: (i, 0)),
            pl.BlockSpec((tile_m, D_MODEL), lambda i: (i, 0)),
            pl.BlockSpec((tile_m, ROUTER_PAD), lambda i: (i, 0)),
        ],
        out_shape=[
            jax.ShapeDtypeStruct((t_all, D_MODEL), F32),
            jax.ShapeDtypeStruct((t_all, D_MODEL), F32),
            jax.ShapeDtypeStruct((t_all, ROUTER_PAD), F32),
        ],
        compiler_params=_params("arbitrary"),
        name="merge",
    )(x_a, x_b, a_a, a_b, mh_a, mh_b, z_all, z_all, wa, wm, wo, g_ffn.reshape(1, D_MODEL), wr_pad, br_pad)


def _expert_body(be_ref, nb_ref, xs_ref, w1g_ref, w1u_ref, b1g_ref, b1u_ref, w2_ref, b2_ref, y_ref):
    i = pl.program_id(0)
    c = pl.program_id(1)
    used = i < nb_ref[0]

    @pl.when(used)
    def _():
        x = xs_ref[...].astype(BF16)
        zg = jnp.dot(x, w1g_ref[...], preferred_element_type=F32) + b1g_ref[...]
        zu = jnp.dot(x, w1u_ref[...], preferred_element_type=F32) + b1u_ref[...]
        gate = jnp.minimum(zg, SWIGLU_LIMIT)
        up = jnp.clip(zu, -SWIGLU_LIMIT, SWIGLU_LIMIT)
        act = gate * jax.nn.sigmoid(SWIGLU_ALPHA * gate) * (up + 1.0)
        y = jnp.dot(act.astype(BF16), w2_ref[...], preferred_element_type=F32)

        @pl.when(c == 0)
        def _():
            y_ref[...] = y + b2_ref[...]

        @pl.when(c != 0)
        def _():
            y_ref[...] += y

    @pl.when(jnp.logical_not(used) & (c == 0))
    def _():
        y_ref[...] = jnp.zeros_like(y_ref)


def _experts(block_e, n_used, xs, w1, b1, w2, b2):
    n_slots = xs.shape[0]
    n_blocks = n_slots // MOE_BLOCK_M
    n_chunks = D_FF // MOE_FF_CHUNK

    def chunk(i, c, nb):
        block = jnp.minimum(i, nb[0] - 1)
        step = jnp.where(i < nb[0], c, n_chunks - 1)
        return jnp.where(block % 2 == 0, step, n_chunks - 1 - step)

    grid_spec = pltpu.PrefetchScalarGridSpec(
        num_scalar_prefetch=2,
        grid=(n_blocks, n_chunks),
        in_specs=[
            pl.BlockSpec((MOE_BLOCK_M, D_MODEL), lambda i, c, be, nb: (i, 0)),
            pl.BlockSpec((None, D_MODEL, MOE_FF_CHUNK), lambda i, c, be, nb: (be[i], 0, chunk(i, c, nb))),
            pl.BlockSpec((None, D_MODEL, MOE_FF_CHUNK),
                         lambda i, c, be, nb: (be[i], 0, n_chunks + chunk(i, c, nb))),
            pl.BlockSpec((None, 1, MOE_FF_CHUNK), lambda i, c, be, nb: (be[i], 0, chunk(i, c, nb))),
            pl.BlockSpec((None, 1, MOE_FF_CHUNK), lambda i, c, be, nb: (be[i], 0, n_chunks + chunk(i, c, nb))),
            pl.BlockSpec((None, MOE_FF_CHUNK, D_MODEL), lambda i, c, be, nb: (be[i], chunk(i, c, nb), 0)),
            pl.BlockSpec((None, 1, D_MODEL), lambda i, c, be, nb: (be[i], 0, 0)),
        ],
        out_specs=pl.BlockSpec((MOE_BLOCK_M, D_MODEL), lambda i, c, be, nb: (i, 0)),
    )
    return pl.pallas_call(
        _expert_body,
        grid_spec=grid_spec,
        out_shape=jax.ShapeDtypeStruct((n_slots, D_MODEL), F32),
        compiler_params=_params("arbitrary", "arbitrary"),
        name="experts",
    )(block_e, n_used, xs, w1, w1, b1, b1, w2, b2)


ROUTE_TILE = 1024
LANE_E = 0
LANE_RANK = TOP_K


def _route_body(lg_ref, meta_ref, gate_ref, cnt_ref, tri_scr, carry_scr):
    @pl.when(pl.program_id(0) == 0)
    def _():
        r = lax.broadcasted_iota(jnp.int32, (ROUTE_TILE, ROUTE_TILE), 0)
        c = lax.broadcasted_iota(jnp.int32, (ROUTE_TILE, ROUTE_TILE), 1)
        tri_scr[...] = jnp.where(c < r, 1.0, 0.0).astype(BF16)
        carry_scr[...] = jnp.zeros_like(carry_scr)

    lane_i = lax.broadcasted_iota(jnp.int32, (ROUTE_TILE, LANES), 1)
    lane = lane_i.astype(F32)
    x = jnp.where(lane_i < N_EXPERTS, lg_ref[...], -jnp.inf)
    vals, idxs, sels = [], [], []
    for _ in range(TOP_K):
        mx = jnp.max(x, axis=1, keepdims=True)
        idx = jnp.min(jnp.where(x == mx, lane, float(LANES)), axis=1, keepdims=True)
        sel = lane == idx
        x = jnp.where(sel, -jnp.inf, x)
        vals.append(mx)
        idxs.append(idx)
        sels.append(sel)
    weights = [jnp.exp(v - vals[0]) for v in vals]
    inv = 1.0 / (weights[0] + weights[1] + weights[2] + weights[3])
    onehot = sum(jnp.where(s, 1.0, 0.0) for s in sels)
    ranks = jnp.dot(tri_scr[...], onehot.astype(BF16), preferred_element_type=F32) + carry_scr[0:1, :]
    meta = jnp.zeros((ROUTE_TILE, LANES), F32)
    gate = jnp.zeros((ROUTE_TILE, LANES), F32)
    for k in range(TOP_K):
        rank_k = jnp.sum(jnp.where(sels[k], ranks, 0.0), axis=1, keepdims=True)
        meta = jnp.where(lane_i == LANE_E + k, idxs[k], meta)
        meta = jnp.where(lane_i == LANE_RANK + k, rank_k, meta)
        gate = jnp.where(lane_i == k, weights[k] * inv, gate)
    meta_ref[...] = meta.astype(jnp.int32)
    gate_ref[...] = gate
    carry_scr[0:1, :] = carry_scr[0:1, :] + jnp.sum(onehot, axis=0, keepdims=True)
    cnt_ref[...] = carry_scr[...]


def _route(logits):
    t = logits.shape[0]
    return pl.pallas_call(
        _route_body,
        grid=(t // ROUTE_TILE,),
        in_specs=[pl.BlockSpec((ROUTE_TILE, LANES), lambda i: (i, 0))],
        out_specs=[
            pl.BlockSpec((ROUTE_TILE, LANES), lambda i: (i, 0)),
            pl.BlockSpec((ROUTE_TILE, LANES), lambda i: (i, 0)),
            pl.BlockSpec((SUBLANES, LANES), lambda i: (0, 0)),
        ],
        out_shape=[
            jax.ShapeDtypeStruct((t, LANES), jnp.int32),
            jax.ShapeDtypeStruct((t, LANES), F32),
            jax.ShapeDtypeStruct((SUBLANES, LANES), F32),
        ],
        scratch_shapes=[pltpu.VMEM((ROUTE_TILE, ROUTE_TILE), BF16), pltpu.VMEM((SUBLANES, LANES), F32)],
        compiler_params=_params("arbitrary"),
        name="route",
    )(logits)


def _slot_tables(counts, n_blocks):
    padded = (counts + MOE_BLOCK_M - 1) // MOE_BLOCK_M * MOE_BLOCK_M
    pends = jnp.cumsum(padded)
    first_slot = jnp.arange(n_blocks, dtype=pends.dtype) * MOE_BLOCK_M
    block_e = jnp.minimum(jnp.sum(pends[None, :] <= first_slot[:, None], axis=1), N_EXPERTS - 1).astype(jnp.int32)
    n_used = (pends[-1] // MOE_BLOCK_M).astype(jnp.int32).reshape(1)
    return (pends - padded).astype(jnp.int32), pends.astype(jnp.int32), padded.astype(jnp.int32), block_e, n_used


DISPATCH_TILE = 512
IDX_SLOTS = 3


def _dispatch_body(pend_ref, padded_ref, dest_hbm, h_ref, xs_hbm, idx_smem, zero_scr, idx_sem, row_sem, zero_sem):
    i = pl.program_id(0)
    n = pl.num_programs(0)

    width = DISPATCH_TILE * TOP_K

    def idx_copy(tile):
        s = lax.rem(tile, IDX_SLOTS)
        return pltpu.make_async_copy(dest_hbm.at[pl.ds(pl.multiple_of(tile * width, width), width)],
                                     idx_smem.at[pl.ds(pl.multiple_of(s * width, width), width)], idx_sem.at[s])

    def zero_copy(e):
        start = pl.multiple_of(pend_ref[e] - MOE_BLOCK_M, MOE_BLOCK_M)
        return pltpu.make_async_copy(zero_scr, xs_hbm.at[pl.ds(start, MOE_BLOCK_M), :], zero_sem)

    def row_copy(g, j, d):
        return pltpu.make_async_copy(h_ref.at[g, pl.ds(j, 1), :], xs_hbm.at[pl.ds(d, 1), :], row_sem)

    @pl.when(i == 0)
    def _():
        zero_scr[...] = jnp.zeros_like(zero_scr)
        for e in range(N_EXPERTS):
            @pl.when(padded_ref[e] > 0)
            def _():
                zero_copy(e).start()
        for e in range(N_EXPERTS):
            @pl.when(padded_ref[e] > 0)
            def _():
                zero_copy(e).wait()
        idx_copy(0).start()

    @pl.when(i + 1 < n)
    def _():
        idx_copy(i + 1).start()

    idx_copy(i).wait()
    base = lax.rem(i, IDX_SLOTS) * width

    def issue(g, c):
        for j in range(SUBLANES):
            for k in range(TOP_K):
                d = idx_smem[base + g * (SUBLANES * TOP_K) + (j * TOP_K + k)]
                row_copy(g, j, d).start(priority=k % 2)
        return c

    lax.fori_loop(0, DISPATCH_TILE // SUBLANES, issue, 0)
    for _ in range(TOP_K):
        pltpu.make_async_copy(h_ref, h_ref, row_sem).wait()


def _scatter_rows(h, dest, pends, padded, n_slots):
    t = h.shape[0]
    tiles = t // DISPATCH_TILE
    grid_spec = pltpu.PrefetchScalarGridSpec(
        num_scalar_prefetch=2,
        grid=(tiles,),
        in_specs=[
            pl.BlockSpec(memory_space=pl.ANY),
            pl.BlockSpec((DISPATCH_TILE // SUBLANES, SUBLANES, D_MODEL), lambda i, pe, pd: (i, 0, 0)),
        ],
        out_specs=pl.BlockSpec(memory_space=pl.ANY),
        scratch_shapes=[
            pltpu.SMEM((IDX_SLOTS * DISPATCH_TILE * TOP_K,), jnp.int32),
            pltpu.VMEM((MOE_BLOCK_M, D_MODEL), F32),
            pltpu.SemaphoreType.DMA((IDX_SLOTS,)),
            pltpu.SemaphoreType.DMA(()),
            pltpu.SemaphoreType.DMA(()),
        ],
    )
    return pl.pallas_call(
        _dispatch_body,
        grid_spec=grid_spec,
        out_shape=jax.ShapeDtypeStruct((n_slots, D_MODEL), F32),
        compiler_params=_params("arbitrary"),
        name="dispatch",
    )(pends, padded, dest.reshape(-1), h.reshape(t // SUBLANES, SUBLANES, D_MODEL))


COMBINE_TILE = 256


def _combine_body(tile_off, dest_hbm, x1_ref, gate_ref, gf_ref, ys_hbm, y_ref, idx_smem, buf, idx_sem, row_sem):
    i = pl.program_id(0)
    n = pl.num_programs(0)

    width = COMBINE_TILE * TOP_K

    def idx_copy(tile):
        s = lax.rem(tile, IDX_SLOTS)
        return pltpu.make_async_copy(dest_hbm.at[pl.ds(pl.multiple_of((tile_off + tile) * width, width), width)],
                                     idx_smem.at[pl.ds(pl.multiple_of(s * width, width), width)], idx_sem.at[s])

    def row_copy(d, b, k, g, j):
        return pltpu.make_async_copy(ys_hbm.at[pl.ds(d, 1), :], buf.at[b, k, g, pl.ds(j, 1), :], row_sem.at[b])

    def issue_rows(tile):
        base = lax.rem(tile, IDX_SLOTS) * width
        b = lax.rem(tile, 2)

        def body(g, c):
            for j in range(SUBLANES):
                for k in range(TOP_K):
                    d = idx_smem[base + g * (SUBLANES * TOP_K) + (j * TOP_K + k)]
                    row_copy(d, b, k, g, j).start(priority=k % 2)
            return c

        lax.fori_loop(0, COMBINE_TILE // SUBLANES, body, 0)

    @pl.when(i == 0)
    def _():
        idx_copy(0).start()
        idx_copy(0).wait()
        issue_rows(0)

        @pl.when(n > 1)
        def _():
            idx_copy(1).start()

    @pl.when(i + 1 < n)
    def _():
        idx_copy(i + 1).wait()
        issue_rows(i + 1)

    @pl.when(i + 2 < n)
    def _():
        idx_copy(i + 2).start()

    b = lax.rem(i, 2)

    pltpu.make_async_copy(buf.at[b], buf.at[b], row_sem.at[b]).wait()
    gate = gate_ref[...]
    acc = x1_ref[...]
    for k in range(TOP_K):
        acc = acc + buf[b, k].reshape(COMBINE_TILE, D_MODEL) * gate[:, k:k + 1]
    y_ref[...] = acc * lax.rsqrt(jnp.mean(acc * acc, axis=-1, keepdims=True) + RMS_EPS) * gf_ref[...]


def _combine(x1, gate, dest, ys, g_final, row_off, rows):
    t = x1.shape[0]
    tile_off = row_off // COMBINE_TILE
    return pl.pallas_call(
        functools.partial(_combine_body, tile_off),
        grid=(rows // COMBINE_TILE,),
        in_specs=[
            pl.BlockSpec(memory_space=pl.ANY),
            pl.BlockSpec((COMBINE_TILE, D_MODEL), lambda i: (tile_off + i, 0)),
            pl.BlockSpec((COMBINE_TILE, LANES), lambda i: (tile_off + i, 0)),
            pl.BlockSpec((1, D_MODEL), lambda i: (0, 0)),
            pl.BlockSpec(memory_space=pl.ANY),
        ],
        out_specs=pl.BlockSpec((COMBINE_TILE, D_MODEL), lambda i: (i, 0)),
        out_shape=jax.ShapeDtypeStruct((rows, D_MODEL), F32),
        scratch_shapes=[
            pltpu.SMEM((IDX_SLOTS * COMBINE_TILE * TOP_K,), jnp.int32),
            pltpu.VMEM((2, TOP_K, COMBINE_TILE // SUBLANES, SUBLANES, D_MODEL), F32),
            pltpu.SemaphoreType.DMA((IDX_SLOTS,)),
            pltpu.SemaphoreType.DMA((2,)),
        ],
        compiler_params=_params("arbitrary"),
        name="combine",
    )(dest.reshape(-1), x1, gate, g_final.reshape(1, D_MODEL), ys)


def _lane_row(v):
    return jnp.pad(v.astype(F32), (0, LANES - v.shape[0])).reshape(1, LANES)


def kernel(x_prompt, x_sample, cache_swa_k, cache_swa_v, state_mlstm_C, state_mlstm_n, state_mlstm_m,
           g_mix, w_in, b_if, attn_sinks, mlstm_norm_w, w_attn_proj, w_mlstm_proj, w_out,
           g_ffn, w_router, b_router, w_e1, b_e1, w_e2, b_e2, g_final):
    bp, sp = x_prompt.shape[0], x_prompt.shape[1]
    bs, ls = x_sample.shape[0], x_sample.shape[1]
    tp = bp * sp
    ts = bs * SAMPLE_ROWS
    t_all = tp + ts
    l = 0
    w_in_p = _permute_w_in(w_in[l])
    wa, wm, wo = w_attn_proj[l].astype(BF16), w_mlstm_proj[l].astype(BF16), w_out[l].astype(BF16)
    wr_f32 = jnp.pad(w_router[l], ((0, 0), (0, ROUTER_PAD - N_EXPERTS)))
    wr_hi = wr_f32.astype(BF16)
    wr_pad = jnp.concatenate([wr_hi, (wr_f32 - wr_hi.astype(F32)).astype(BF16)], axis=1)
    br_pad = _lane_row(b_router[l])
    b1, b2 = b_e1[l].reshape(N_EXPERTS, 1, 2 * D_FF), b_e2[l].reshape(N_EXPERTS, 1, D_MODEL)
    sinks_row = _lane_row(attn_sinks[l])
    bias_row = _lane_row(b_if[l])
    norm_w_row = mlstm_norm_w[l].reshape(1, M_V_WIDTH)

    xp = x_prompt.reshape(tp, D_MODEL)
    xs = jnp.pad(x_sample, ((0, 0), (0, SAMPLE_ROWS - ls), (0, 0))).reshape(ts, D_MODEL)
    z_all = _inproj(xp, xs, g_mix[l], w_in_p, 512)

    a_p, k1, v1, w1 = _attn_prompt(z_all, sinks_row, bp, sp, w_e1[l])
    a_s, k2, v2 = _attn_sample(z_all, sinks_row, cache_swa_k[l].reshape(bs, WINDOW, KV_WIDTH),
                               cache_swa_v[l].reshape(bs, WINDOW, KV_WIDTH), tp, ls)
    mh_p, c1, n1, m1, w2 = _mlstm_prompt(z_all, bias_row, norm_w_row, bp, sp, w_e2[l])
    mh_s, c2, n2, m2 = _mlstm_sample(z_all, bias_row, norm_w_row, state_mlstm_C[l], state_mlstm_n[l],
                                     state_mlstm_m[l], tp, ls)

    x1, h2, logits = _merge(xp, xs, a_p, a_s, mh_p, mh_s, z_all, wa, wm, wo, g_ffn[l], wr_pad, br_pad, 256)

    meta, gate, counts = _route(logits)
    n_blocks = -(-t_all * TOP_K // MOE_BLOCK_M) + N_EXPERTS
    pstarts, pends, padded, block_e, n_used = _slot_tables(counts[0, :N_EXPERTS].astype(jnp.int32), n_blocks)
    dest = pstarts[meta[:, LANE_E:LANE_E + TOP_K]] + meta[:, LANE_RANK:LANE_RANK + TOP_K]
    xs_sorted = _scatter_rows(h2, dest, pends, padded, n_blocks * MOE_BLOCK_M)
    ys = _experts(block_e, n_used, xs_sorted, w1, b1, w2, b2)
    y_prompt = _combine(x1, gate, dest, ys, g_final, 0, tp).reshape(bp, sp, D_MODEL)
    y_sample = _combine(x1, gate, dest, ys, g_final, tp, ts).reshape(bs, SAMPLE_ROWS, D_MODEL)[:, :ls]
    k1 = k1.reshape(bp, WINDOW, N_KV_HEADS, HEAD_DIM)
    v1 = v1.reshape(bp, WINDOW, N_KV_HEADS, HEAD_DIM)
    k2 = k2.reshape(bs, WINDOW, N_KV_HEADS, HEAD_DIM)
    v2 = v2.reshape(bs, WINDOW, N_KV_HEADS, HEAD_DIM)
    m1 = m1[:, :, 0]
    st = lambda t: t[None]
    return (y_prompt, y_sample, st(k1), st(v1), st(c1), st(n1), st(m1), st(k2), st(v2), st(c2), st(n2), st(m2))
```

```python
import functools

import jax
import jax.numpy as jnp
from jax import lax
from jax.experimental import pallas as pl
from jax.experimental.pallas import tpu as pltpu

D_MODEL = 2048
N_HEADS = 16
N_KV_HEADS = 2
HEAD_DIM = 64
Q_PER_KV = N_HEADS // N_KV_HEADS
WINDOW = 128
ATTN_WIDTH = N_HEADS * HEAD_DIM
KV_WIDTH = N_KV_HEADS * HEAD_DIM
M_HEADS = 4
M_QK_DIM = 128
M_V_DIM = 256
M_QK_WIDTH = M_HEADS * M_QK_DIM
M_V_WIDTH = M_HEADS * M_V_DIM
M_CHUNK = 64
N_EXPERTS = 32
TOP_K = 4
D_FF = D_MODEL
SWIGLU_LIMIT = 7.0
SWIGLU_ALPHA = 1.702
RMS_EPS = 1e-5
NEG = -1e30

F32 = jnp.float32
BF16 = jnp.bfloat16

LANES = 128
SUBLANES = 8
VMEM_LIMIT_BYTES = 56 * 1024 * 1024

COL_GZ = 0
COL_AQ = COL_GZ + 2 * D_MODEL
COL_MV = COL_AQ + ATTN_WIDTH
COL_MO = COL_MV + M_V_WIDTH
COL_MQ = COL_MO + M_V_WIDTH
COL_MK = COL_MQ + M_QK_WIDTH
COL_AK = COL_MK + M_QK_WIDTH
COL_IF = COL_AK + 2 * KV_WIDTH
IN_TILE_N = 2176
IN_WIDTH_PAD = -(-(COL_IF + LANES) // IN_TILE_N) * IN_TILE_N

SAMPLE_ROWS = SUBLANES
MOE_BLOCK_M = 512
MOE_FF_CHUNK = 1024
ROUTER_PAD = LANES


def _permute_w_in(w_in):
    splits = (ATTN_WIDTH, KV_WIDTH, KV_WIDTH, M_QK_WIDTH, M_QK_WIDTH, M_V_WIDTH, M_V_WIDTH, M_HEADS, M_HEADS,
              2 * D_MODEL)
    parts, off = [], 0
    for width in splits:
        parts.append(w_in[:, off:off + width])
        off += width
    aq, ak, av, mq, mk, mv, mo, mi, mf, gz = parts
    cols = jnp.concatenate([gz, aq, mv, mo, mq, mk, ak, av, mi, mf], axis=1)
    cols = jnp.pad(cols, ((0, 0), (0, IN_WIDTH_PAD - cols.shape[1])))
    return cols.astype(BF16)


def _params(*semantics):
    return pltpu.CompilerParams(dimension_semantics=semantics, vmem_limit_bytes=VMEM_LIMIT_BYTES)


def _log_sigmoid(x):
    return jnp.minimum(x, 0.0) - jnp.log1p(jnp.exp(-jnp.abs(x)))


def _inproj_body(tiles_a, xa_ref, xb_ref, g_ref, w_ref, z_ref, h_scr):
    def normalise(x_ref):
        x = x_ref[...]
        y = x * lax.rsqrt(jnp.mean(x * x, axis=-1, keepdims=True) + RMS_EPS)
        h_scr[...] = (y * g_ref[...]).astype(BF16)

    first = pl.program_id(1) == 0
    in_a = pl.program_id(0) < tiles_a

    @pl.when(first & in_a)
    def _():
        normalise(xa_ref)

    @pl.when(first & jnp.logical_not(in_a))
    def _():
        normalise(xb_ref)

    z_ref[...] = jnp.dot(h_scr[...], w_ref[...], preferred_element_type=F32)


def _inproj(x_a, x_b, g, w_bf16, tile_m):
    tiles_a, tiles_b = x_a.shape[0] // tile_m, x_b.shape[0] // tile_m
    n = w_bf16.shape[1]
    return pl.pallas_call(
        functools.partial(_inproj_body, tiles_a),
        grid=(tiles_a + tiles_b, n // IN_TILE_N),
        in_specs=[
            pl.BlockSpec((tile_m, D_MODEL), lambda i, j: (jnp.minimum(i, tiles_a - 1), 0)),
            pl.BlockSpec((tile_m, D_MODEL), lambda i, j: (jnp.maximum(i - tiles_a, 0), 0)),
            pl.BlockSpec((1, D_MODEL), lambda i, j: (0, 0)),
            pl.BlockSpec((D_MODEL, IN_TILE_N), lambda i, j: (0, j)),
        ],
        out_specs=pl.BlockSpec((tile_m, IN_TILE_N), lambda i, j: (i, j)),
        out_shape=jax.ShapeDtypeStruct(((tiles_a + tiles_b) * tile_m, n), F32),
        scratch_shapes=[pltpu.VMEM((tile_m, D_MODEL), BF16)],
        compiler_params=_params("arbitrary", "arbitrary"),
        name="inproj",
    )(x_a, x_b, g.reshape(1, D_MODEL), w_bf16)


assert KV_WIDTH == LANES and N_KV_HEADS == 2 and LANES == 2 * HEAD_DIM
Q_TILES = ATTN_WIDTH // LANES
TILES_PER_KV = Q_TILES // N_KV_HEADS
_NT = (((1,), (1,)), ((), ()))


def _kv_placements(x):
    lo = lax.broadcasted_iota(jnp.int32, x.shape, 1) < HEAD_DIM
    swapped = pltpu.roll(x, HEAD_DIM, axis=1)
    return ((jnp.where(lo, x, 0.0).astype(BF16), jnp.where(lo, 0.0, swapped).astype(BF16)),
            (jnp.where(lo, swapped, 0.0).astype(BF16), jnp.where(lo, 0.0, x).astype(BF16)))


def _slab_head(g, p, j):
    return g * Q_PER_KV + 2 * j + p


def _attn_prompt_body(q_ref, kvc_ref, kvp_ref, sink_ref, w_ref, a_ref, k_ref, v_ref, wb_ref):
    wb_ref[...] = w_ref[...].astype(BF16)
    n = pl.program_id(1)
    kc, vc = kvc_ref[:, :KV_WIDTH], kvc_ref[:, KV_WIDTH:]
    k_ref[...] = kc
    v_ref[...] = vc
    kk = _kv_placements(jnp.concatenate([kvp_ref[:, :KV_WIDTH], kc], axis=0))
    vv = _kv_placements(jnp.concatenate([kvp_ref[:, KV_WIDTH:], vc], axis=0))
    q = q_ref[...].astype(BF16)
    row = lax.broadcasted_iota(jnp.int32, (WINDOW, WINDOW), 0)
    col = lax.broadcasted_iota(jnp.int32, (WINDOW, WINDOW), 1)
    mask = jnp.concatenate([(col > row) & (n > 0), col <= row], axis=1)
    slabs, sinks = [], []
    for g in range(N_KV_HEADS):
        qt = jnp.concatenate([q[:, (TILES_PER_KV * g + j) * LANES:(TILES_PER_KV * g + j + 1) * LANES]
                              for j in range(TILES_PER_KV)], axis=0)
        for p in range(2):
            s = lax.dot_general(qt, kk[g][p], _NT, preferred_element_type=F32) * (HEAD_DIM ** -0.5)
            for j in range(TILES_PER_KV):
                h = _slab_head(g, p, j)
                slabs.append(jnp.where(mask, s[j * WINDOW:(j + 1) * WINDOW], NEG))
                sinks.append(jnp.broadcast_to(sink_ref[:, h:h + 1], (WINDOW, LANES)))
    s_all = jnp.concatenate(slabs, axis=0)
    sink = jnp.concatenate(sinks, axis=0)
    m = jnp.maximum(jnp.broadcast_to(jnp.max(s_all, axis=1, keepdims=True), sink.shape), sink)
    p_all = jnp.exp(s_all - jnp.concatenate([m, m], axis=1))
    p_hi = p_all.astype(BF16)
    p_lo = (p_all - p_hi.astype(F32)).astype(BF16)
    ones = jnp.ones((2 * WINDOW, LANES), BF16)
    denom = (jnp.dot(p_hi, ones, preferred_element_type=F32) + jnp.dot(p_lo, ones, preferred_element_type=F32)
             + jnp.exp(sink - m))
    inv = 1.0 / denom
    pn = (p_all * jnp.concatenate([inv, inv], axis=1)).astype(BF16)
    half = TILES_PER_KV * WINDOW
    for g in range(N_KV_HEADS):
        base = 2 * half * g
        o = (jnp.dot(pn[base:base + half], vv[g][0], preferred_element_type=F32)
             + jnp.dot(pn[base + half:base + 2 * half], vv[g][1], preferred_element_type=F32))
        for j in range(TILES_PER_KV):
            t = TILES_PER_KV * g + j
            a_ref[:, t * LANES:(t + 1) * LANES] = o[j * WINDOW:(j + 1) * WINDOW].astype(BF16)


def _cast_slices(w, steps):
    rows = w.reshape(-1, w.shape[-1])
    per_step = rows.shape[0] // steps
    assert per_step * steps == rows.shape[0] and per_step % 16 == 0
    return rows, per_step


def _attn_prompt(z_all, sinks_row, batch, seq, w):
    nb = seq // WINDOW
    w_rows, r1 = _cast_slices(w, batch * nb)
    step = lambda b, n: (b * nb + n, 0)
    a, k, v, wb = pl.pallas_call(
        _attn_prompt_body,
        grid=(batch, nb),
        in_specs=[
            pl.BlockSpec((WINDOW, ATTN_WIDTH), lambda b, n: (b * nb + n, COL_AQ // ATTN_WIDTH)),
            pl.BlockSpec((WINDOW, 2 * KV_WIDTH), lambda b, n: (b * nb + n, COL_AK // (2 * KV_WIDTH))),
            pl.BlockSpec((WINDOW, 2 * KV_WIDTH),
                         lambda b, n: (b * nb + jnp.maximum(n - 1, 0), COL_AK // (2 * KV_WIDTH))),
            pl.BlockSpec((1, LANES), lambda b, n: (0, 0)),
            pl.BlockSpec((r1, w_rows.shape[1]), step),
        ],
        out_specs=[
            pl.BlockSpec((WINDOW, ATTN_WIDTH), lambda b, n: (b * nb + n, 0)),
            pl.BlockSpec((None, WINDOW, KV_WIDTH), lambda b, n: (b, 0, 0)),
            pl.BlockSpec((None, WINDOW, KV_WIDTH), lambda b, n: (b, 0, 0)),
            pl.BlockSpec((r1, w_rows.shape[1]), step),
        ],
        out_shape=[
            jax.ShapeDtypeStruct((batch * seq, ATTN_WIDTH), BF16),
            jax.ShapeDtypeStruct((batch, WINDOW, KV_WIDTH), F32),
            jax.ShapeDtypeStruct((batch, WINDOW, KV_WIDTH), F32),
            jax.ShapeDtypeStruct(w_rows.shape, BF16),
        ],
        compiler_params=_params("arbitrary", "arbitrary"),
        name="attn_prompt",
    )(z_all, z_all, z_all, sinks_row, w_rows)
    return a, k, v, wb.reshape(w.shape)


ATTN_SAMPLE_GROUP = 8


def _attn_sample_body(n_valid, q_ref, kv_ref, ck_ref, cv_ref, sink_ref, a_ref, nk_ref, nv_ref, o_scr):
    rs = SAMPLE_ROWS
    slab = TILES_PER_KV * rs
    r8 = lax.broadcasted_iota(jnp.int32, (rs, KV_WIDTH), 0)
    qi_c = jnp.bitwise_and(lax.broadcasted_iota(jnp.int32, (slab, WINDOW), 0), rs - 1)
    mask_c = lax.broadcasted_iota(jnp.int32, (slab, WINDOW), 1) > qi_c
    qi_n = jnp.bitwise_and(lax.broadcasted_iota(jnp.int32, (slab, rs), 0), rs - 1)
    mask_n = lax.broadcasted_iota(jnp.int32, (slab, rs), 1) <= qi_n
    sc_list, sn_list, sink_list, values = [], [], [], []
    for b in range(ATTN_SAMPLE_GROUP):
        rows = slice(b * rs, (b + 1) * rs)
        q = q_ref[rows, :]
        kn, vn = kv_ref[rows, :KV_WIDTH], kv_ref[rows, KV_WIDTH:]
        ck, cv = ck_ref[b], cv_ref[b]
        for src, new, dst in ((ck, kn, nk_ref), (cv, vn, nv_ref)):
            rolled = pltpu.roll(src, WINDOW - n_valid, axis=0)
            dst[b] = rolled
            tail = jnp.where(r8 >= rs - n_valid, pltpu.roll(new, rs - n_valid, axis=0), rolled[WINDOW - rs:, :])
            dst[b, WINDOW - rs:, :] = tail
        kk_c, kk_n = _kv_placements(ck), _kv_placements(kn)
        values.append((_kv_placements(cv), _kv_placements(vn)))
        for g in range(N_KV_HEADS):
            qt = jnp.concatenate([q[:, (TILES_PER_KV * g + j) * LANES:(TILES_PER_KV * g + j + 1) * LANES]
                                  for j in range(TILES_PER_KV)], axis=0).astype(BF16)
            for p in range(2):
                s_c = lax.dot_general(qt, kk_c[g][p], _NT, preferred_element_type=F32) * (HEAD_DIM ** -0.5)
                s_n = lax.dot_general(qt, kk_n[g][p], _NT, preferred_element_type=F32) * (HEAD_DIM ** -0.5)
                sc_list.append(jnp.where(mask_c, s_c, NEG))
                sn_list.append(jnp.where(mask_n, s_n, NEG))
                for j in range(TILES_PER_KV):
                    h = _slab_head(g, p, j)
                    sink_list.append(jnp.broadcast_to(sink_ref[:, h:h + 1], (rs, 1)))
    s_c = jnp.concatenate(sc_list, axis=0)
    s_n = jnp.concatenate(sn_list, axis=0)
    sink = jnp.concatenate(sink_list, axis=0)
    m = jnp.maximum(jnp.maximum(jnp.max(s_c, axis=1, keepdims=True), jnp.max(s_n, axis=1, keepdims=True)), sink)
    p_c = jnp.exp(s_c - m)
    p_n = jnp.exp(s_n - m)
    inv = 1.0 / (jnp.sum(p_c, axis=1, keepdims=True) + jnp.sum(p_n, axis=1, keepdims=True) + jnp.exp(sink - m))
    p_c = (p_c * inv).astype(BF16)
    p_n = (p_n * inv).astype(BF16)
    for b in range(ATTN_SAMPLE_GROUP):
        vv_c, vv_n = values[b]
        for g in range(N_KV_HEADS):
            base = (b * N_KV_HEADS + g) * 2 * slab
            o = (jnp.dot(p_c[base:base + slab], vv_c[g][0], preferred_element_type=F32)
                 + jnp.dot(p_c[base + slab:base + 2 * slab], vv_c[g][1], preferred_element_type=F32)
                 + jnp.dot(p_n[base:base + slab], vv_n[g][0], preferred_element_type=F32)
                 + jnp.dot(p_n[base + slab:base + 2 * slab], vv_n[g][1], preferred_element_type=F32))
            for j in range(TILES_PER_KV):
                t = TILES_PER_KV * g + j
                o_scr[b * rs:(b + 1) * rs, t * LANES:(t + 1) * LANES] = o[j * rs:(j + 1) * rs]
    a_ref[...] = o_scr[...].astype(BF16)


def _attn_sample(z_all, sinks_row, cache_k, cache_v, row_off, n_valid):
    b = cache_k.shape[0]
    gr = ATTN_SAMPLE_GROUP
    rows = gr * SAMPLE_ROWS
    rb = row_off // rows
    return pl.pallas_call(
        functools.partial(_attn_sample_body, n_valid),
        grid=(b // gr,),
        in_specs=[
            pl.BlockSpec((rows, ATTN_WIDTH), lambda i: (rb + i, COL_AQ // ATTN_WIDTH)),
            pl.BlockSpec((rows, 2 * KV_WIDTH), lambda i: (rb + i, COL_AK // (2 * KV_WIDTH))),
            pl.BlockSpec((gr, WINDOW, KV_WIDTH), lambda i: (i, 0, 0)),
            pl.BlockSpec((gr, WINDOW, KV_WIDTH), lambda i: (i, 0, 0)),
            pl.BlockSpec((1, LANES), lambda i: (0, 0)),
        ],
        out_specs=[
            pl.BlockSpec((rows, ATTN_WIDTH), lambda i: (i, 0)),
            pl.BlockSpec((gr, WINDOW, KV_WIDTH), lambda i: (i, 0, 0)),
            pl.BlockSpec((gr, WINDOW, KV_WIDTH), lambda i: (i, 0, 0)),
        ],
        out_shape=[
            jax.ShapeDtypeStruct((b * SAMPLE_ROWS, ATTN_WIDTH), BF16),
            jax.ShapeDtypeStruct(cache_k.shape, F32),
            jax.ShapeDtypeStruct(cache_v.shape, F32),
        ],
        scratch_shapes=[pltpu.VMEM((rows, ATTN_WIDTH), F32)],
        compiler_params=_params("arbitrary"),
        name="attn_sample",
    )(z_all, z_all, cache_k, cache_v, sinks_row)


def _mlstm_gates(if_blk, bias_row, n_valid):
    length = if_blk.shape[0]
    g = if_blk + bias_row
    lane = lax.broadcasted_iota(jnp.int32, g.shape, 1)
    g = jnp.where(lane < M_HEADS, g, _log_sigmoid(g))
    if n_valid < length:
        row = lax.broadcasted_iota(jnp.int32, g.shape, 0)
        g = jnp.where(row < n_valid, g, jnp.where(lane < M_HEADS, NEG, 0.0))
    return g


def _mlstm_chunk(q, k, v, gates, gates_t, h, c, n, m):
    length = q.shape[0]
    i_col = gates[:, h:h + 1]
    f_col = gates[:, M_HEADS + h:M_HEADS + h + 1]
    i_row = gates_t[h:h + 1, :]
    f_row = gates_t[M_HEADS + h:M_HEADS + h + 1, :]
    t_idx = lax.broadcasted_iota(jnp.int32, (length, length), 0)
    s_idx = lax.broadcasted_iota(jnp.int32, (length, length), 1)
    causal = s_idx <= t_idx
    bcum_col = jnp.sum(jnp.where(causal, f_row, 0.0), axis=1, keepdims=True)
    bcum_row = jnp.sum(jnp.where(t_idx <= s_idx, f_col, 0.0), axis=0, keepdims=True)
    dlog = jnp.where(causal, bcum_col - bcum_row + i_row, NEG)
    inter = bcum_col + m
    mt = jnp.maximum(inter, jnp.max(dlog, axis=1, keepdims=True))
    w_inter = jnp.exp(inter - mt)
    qb = q.astype(BF16)
    ks = k * (M_QK_DIM ** -0.5)
    vb = v.astype(BF16)
    qk = lax.dot_general(qb, ks.astype(BF16), (((1,), (1,)), ((), ())), preferred_element_type=F32)
    qk = qk * jnp.exp(dlog - mt)
    num = (w_inter * jnp.dot(qb, c.astype(BF16), preferred_element_type=F32)
           + jnp.dot(qk.astype(BF16), vb, preferred_element_type=F32))
    den = w_inter * jnp.sum(q * n, axis=1, keepdims=True) + jnp.sum(qk, axis=1, keepdims=True)
    h_t = num / jnp.maximum(jnp.abs(den), jnp.exp(-mt))
    m_new = mt[length - 1:length, :]
    w_state = jnp.exp(bcum_col[length - 1:length, :] - bcum_col + i_col - m_new)
    decay = jnp.exp(inter[length - 1:length, :] - m_new)
    kw = ks * w_state
    c_new = decay * c + jnp.dot(kw.T.astype(BF16), vb, preferred_element_type=F32)
    n_new = decay * n + jnp.sum(kw, axis=0, keepdims=True)
    return h_t, c_new, n_new, m_new


def _mlstm_head_out(h_t, norm_w, o_gate):
    hs = h_t * lax.rsqrt(jnp.mean(h_t * h_t, axis=-1, keepdims=True) + RMS_EPS)
    return hs * norm_w * jax.nn.sigmoid(o_gate)


MLSTM_SEG = 256


MLSTM_BATCH = 2
MLSTM_PROMPT_CHUNK = 256


def _mlstm_prompt_body(*refs):
    nb = MLSTM_BATCH
    ins, rest = refs[:5 * nb], refs[5 * nb:]
    bias_ref, nw_ref, w_ref, mh_ref, c_out, n_out, m_out, wb_ref, c_scr, n_scr, m_scr = rest
    seg = pl.program_id(1)
    wb_ref[...] = w_ref[...].astype(BF16)

    @pl.when(seg == 0)
    def _():
        c_scr[...] = jnp.zeros_like(c_scr)
        n_scr[...] = jnp.zeros_like(n_scr)
        m_scr[...] = jnp.full_like(m_scr, NEG)

    def chunk_body(ci, carry):
        r = pl.multiple_of(ci * MLSTM_PROMPT_CHUNK, MLSTM_PROMPT_CHUNK)
        rows = pl.ds(r, MLSTM_PROMPT_CHUNK)
        state = [[(c_scr[u, h], n_scr[u, h:h + 1, :], m_scr[u, h:h + 1, 0:1]) for h in range(M_HEADS)]
                 for u in range(nb)]
        results = []
        for u in range(nb):
            q_ref, k_ref, v_ref, o_ref, if_ref = ins[5 * u:5 * u + 5]
            gates = _mlstm_gates(if_ref[rows, :], bias_ref[...], MLSTM_PROMPT_CHUNK)
            gates_t = gates.T
            for h in range(M_HEADS):
                qs = slice(h * M_QK_DIM, (h + 1) * M_QK_DIM)
                vs = slice(h * M_V_DIM, (h + 1) * M_V_DIM)
                h_t, c_new, n_new, m_new = _mlstm_chunk(
                    q_ref[rows, qs], k_ref[rows, qs], v_ref[rows, vs], gates, gates_t, h, *state[u][h])
                out = _mlstm_head_out(h_t, nw_ref[:, vs], o_ref[rows, vs]).astype(BF16)
                results.append((u, h, vs, out, c_new, n_new, m_new))
        for u, h, vs, out, c_new, n_new, m_new in results:
            c_scr[u, h] = c_new
            n_scr[u, h:h + 1, :] = n_new
            m_scr[u, h:h + 1, :] = jnp.broadcast_to(m_new, (1, LANES))
            mh_ref[u, rows, vs] = out
        return carry

    lax.fori_loop(0, MLSTM_SEG // MLSTM_PROMPT_CHUNK, chunk_body, 0)

    @pl.when(seg == pl.num_programs(1) - 1)
    def _():
        c_out[...] = c_scr[...]
        n_out[...] = n_scr[:, 0:M_HEADS, :]
        m_out[...] = m_scr[:, 0:M_HEADS, :]


def _mlstm_prompt(z_all, bias_row, norm_w_row, batch, seq, w):
    ns = seq // MLSTM_SEG
    nb = MLSTM_BATCH
    w_rows, wr = _cast_slices(w, (batch // nb) * ns)
    in_specs, args = [], []
    for u in range(nb):
        row = lambda p, s, u=u: (p * nb + u) * ns + s
        in_specs += [
            pl.BlockSpec((MLSTM_SEG, M_QK_WIDTH), lambda p, s, row=row: (row(p, s), COL_MQ // M_QK_WIDTH)),
            pl.BlockSpec((MLSTM_SEG, M_QK_WIDTH), lambda p, s, row=row: (row(p, s), COL_MK // M_QK_WIDTH)),
            pl.BlockSpec((MLSTM_SEG, M_V_WIDTH), lambda p, s, row=row: (row(p, s), COL_MV // M_V_WIDTH)),
            pl.BlockSpec((MLSTM_SEG, M_V_WIDTH), lambda p, s, row=row: (row(p, s), COL_MO // M_V_WIDTH)),
            pl.BlockSpec((MLSTM_SEG, LANES), lambda p, s, row=row: (row(p, s), COL_IF // LANES)),
        ]
        args += [z_all] * 5
    in_specs += [pl.BlockSpec((1, LANES), lambda p, s: (0, 0)), pl.BlockSpec((1, M_V_WIDTH), lambda p, s: (0, 0)),
                 pl.BlockSpec((wr, w_rows.shape[1]), lambda p, s: (p * ns + s, 0))]
    mh, c, n, m, wb = pl.pallas_call(
        _mlstm_prompt_body,
        grid=(batch // nb, ns),
        in_specs=in_specs,
        out_specs=[
            pl.BlockSpec((None, nb, MLSTM_SEG, M_V_WIDTH), lambda p, s: (p, 0, s, 0)),
            pl.BlockSpec((nb, M_HEADS, M_QK_DIM, M_V_DIM), lambda p, s: (p, 0, 0, 0)),
            pl.BlockSpec((nb, M_HEADS, M_QK_DIM), lambda p, s: (p, 0, 0)),
            pl.BlockSpec((nb, M_HEADS, LANES), lambda p, s: (p, 0, 0)),
            pl.BlockSpec((wr, w_rows.shape[1]), lambda p, s: (p * ns + s, 0)),
        ],
        out_shape=[
            jax.ShapeDtypeStruct((batch // nb, nb, seq, M_V_WIDTH), BF16),
            jax.ShapeDtypeStruct((batch, M_HEADS, M_QK_DIM, M_V_DIM), F32),
            jax.ShapeDtypeStruct((batch, M_HEADS, M_QK_DIM), F32),
            jax.ShapeDtypeStruct((batch, M_HEADS, LANES), F32),
            jax.ShapeDtypeStruct(w_rows.shape, BF16),
        ],
        scratch_shapes=[
            pltpu.VMEM((nb, M_HEADS, M_QK_DIM, M_V_DIM), F32),
            pltpu.VMEM((nb, SUBLANES, M_QK_DIM), F32),
            pltpu.VMEM((nb, SUBLANES, LANES), F32),
        ],
        compiler_params=_params("arbitrary", "arbitrary"),
        name="mlstm_prompt",
    )(*args, bias_row, norm_w_row, w_rows)
    return mh.reshape(batch * seq, M_V_WIDTH), c, n, m, wb.reshape(w.shape)


MLSTM_SAMPLE_GROUP = 8


def _mlstm_sample_body(n_valid, q_ref, k_ref, v_ref, o_ref, if_ref, bias_ref, nw_ref, c0_ref, n0_ref, m0_ref,
                       mh_ref, c_out, n_out, m_out, h_scr):
    for g in range(MLSTM_SAMPLE_GROUP):
        rows = slice(g * SAMPLE_ROWS, (g + 1) * SAMPLE_ROWS)
        gates = _mlstm_gates(if_ref[rows, :], bias_ref[...], n_valid)
        gates_t = jnp.transpose(jnp.concatenate([gates] * (LANES // SAMPLE_ROWS), axis=0))[:, :SAMPLE_ROWS]
        for h in range(M_HEADS):
            qs = slice(h * M_QK_DIM, (h + 1) * M_QK_DIM)
            vs = slice(h * M_V_DIM, (h + 1) * M_V_DIM)
            h_t, c_new, n_new, m_new = _mlstm_chunk(
                q_ref[rows, qs], k_ref[rows, qs], v_ref[rows, vs], gates, gates_t, h,
                c0_ref[g, h], n0_ref[g, h:h + 1, :], m0_ref[g:g + 1, h:h + 1])
            c_out[g, h] = c_new
            n_out[g, h:h + 1, :] = n_new
            m_out[g:g + 1, h:h + 1] = m_new
            h_scr[rows, vs] = _mlstm_head_out(h_t, nw_ref[:, vs], o_ref[rows, vs])
    mh_ref[...] = h_scr[...].astype(BF16)


def _mlstm_sample(z_all, bias_row, norm_w_row, c0, n0, m0, row_off, n_valid):
    b = c0.shape[0]
    gr = MLSTM_SAMPLE_GROUP
    rows = gr * SAMPLE_ROWS
    rb = row_off // rows
    m0g = m0.reshape(b // gr, gr, M_HEADS)
    outs = pl.pallas_call(
        functools.partial(_mlstm_sample_body, n_valid),
        grid=(b // gr,),
        in_specs=[
            pl.BlockSpec((rows, M_QK_WIDTH), lambda i: (rb + i, COL_MQ // M_QK_WIDTH)),
            pl.BlockSpec((rows, M_QK_WIDTH), lambda i: (rb + i, COL_MK // M_QK_WIDTH)),
            pl.BlockSpec((rows, M_V_WIDTH), lambda i: (rb + i, COL_MV // M_V_WIDTH)),
            pl.BlockSpec((rows, M_V_WIDTH), lambda i: (rb + i, COL_MO // M_V_WIDTH)),
            pl.BlockSpec((rows, LANES), lambda i: (rb + i, COL_IF // LANES)),
            pl.BlockSpec((1, LANES), lambda i: (0, 0)),
            pl.BlockSpec((1, M_V_WIDTH), lambda i: (0, 0)),
            pl.BlockSpec((gr, M_HEADS, M_QK_DIM, M_V_DIM), lambda i: (i, 0, 0, 0)),
            pl.BlockSpec((gr, M_HEADS, M_QK_DIM), lambda i: (i, 0, 0)),
            pl.BlockSpec((None, gr, M_HEADS), lambda i: (i, 0, 0)),
        ],
        out_specs=[
            pl.BlockSpec((rows, M_V_WIDTH), lambda i: (i, 0)),
            pl.BlockSpec((gr, M_HEADS, M_QK_DIM, M_V_DIM), lambda i: (i, 0, 0, 0)),
            pl.BlockSpec((gr, M_HEADS, M_QK_DIM), lambda i: (i, 0, 0)),
            pl.BlockSpec((None, gr, M_HEADS), lambda i: (i, 0, 0)),
        ],
        out_shape=[
            jax.ShapeDtypeStruct((b * SAMPLE_ROWS, M_V_WIDTH), BF16),
            jax.ShapeDtypeStruct(c0.shape, F32),
            jax.ShapeDtypeStruct(n0.shape, F32),
            jax.ShapeDtypeStruct(m0g.shape, F32),
        ],
        scratch_shapes=[pltpu.VMEM((rows, M_V_WIDTH), F32)],
        compiler_params=_params("arbitrary"),
        name="mlstm_sample",
    )(z_all, z_all, z_all, z_all, z_all, bias_row, norm_w_row, c0, n0, m0g)
    mh, c, n, m = outs
    return mh, c, n, m.reshape(b, M_HEADS)


def _merge_body(tiles_a, xa_ref, xb_ref, aa_ref, ab_ref, ma_ref, mb_ref, gza_ref, gzm_ref, wa_ref, wm_ref, wo_ref,
                gf_ref, wr_ref, br_ref, x1_ref, h2_ref, lg_ref):
    in_a = pl.program_id(0) < tiles_a
    x = jnp.where(in_a, xa_ref[...], xb_ref[...])
    a = jnp.where(in_a, aa_ref[...], ab_ref[...])
    mh = jnp.where(in_a, ma_ref[...], mb_ref[...])
    pa = jnp.dot(a, wa_ref[...], preferred_element_type=F32)
    pm = jnp.dot(mh, wm_ref[...], preferred_element_type=F32)
    merged = jax.nn.sigmoid(gza_ref[...]) * pa + jax.nn.sigmoid(gzm_ref[...]) * pm
    x1 = x + jnp.dot(merged.astype(BF16), wo_ref[...], preferred_element_type=F32)
    x1_ref[...] = x1
    h2 = x1 * lax.rsqrt(jnp.mean(x1 * x1, axis=-1, keepdims=True) + RMS_EPS) * gf_ref[...]
    h2_ref[...] = h2
    hi = h2.astype(BF16)
    lo = (h2 - hi.astype(F32)).astype(BF16)
    both = jnp.dot(hi, wr_ref[...], preferred_element_type=F32)
    lg_ref[...] = (both[:, :ROUTER_PAD] + both[:, ROUTER_PAD:]
                   + jnp.dot(lo, wr_ref[:, :ROUTER_PAD], preferred_element_type=F32) + br_ref[...])


def _merge(x_a, x_b, a_a, a_b, mh_a, mh_b, z_all, wa, wm, wo, g_ffn, wr_pad, br_pad, tile_m):
    tiles_a, tiles_b = x_a.shape[0] // tile_m, x_b.shape[0] // tile_m
    t_all = z_all.shape[0]
    const = lambda i: (0, 0)
    from_a = lambda i: (jnp.minimum(i, tiles_a - 1), 0)
    from_b = lambda i: (jnp.maximum(i - tiles_a, 0), 0)
    return pl.pallas_call(
        functools.partial(_merge_body, tiles_a),
        grid=(tiles_a + tiles_b,),
        in_specs=[
            pl.BlockSpec((tile_m, D_MODEL), from_a),
            pl.BlockSpec((tile_m, D_MODEL), from_b),
            pl.BlockSpec((tile_m, ATTN_WIDTH), from_a),
            pl.BlockSpec((tile_m, ATTN_WIDTH), from_b),
            pl.BlockSpec((tile_m, M_V_WIDTH), from_a),
            pl.BlockSpec((tile_m, M_V_WIDTH), from_b),
            pl.BlockSpec((tile_m, D_MODEL), lambda i: (i, COL_GZ // D_MODEL)),
            pl.BlockSpec((tile_m, D_MODEL), lambda i: (i, COL_GZ // D_MODEL + 1)),
            pl.BlockSpec((ATTN_WIDTH, D_MODEL), const, pipeline_mode=pl.Buffered(1)),
            pl.BlockSpec((M_V_WIDTH, D_MODEL), const, pipeline_mode=pl.Buffered(1)),
            pl.BlockSpec((D_MODEL, D_MODEL), const, pipeline_mode=pl.Buffered(1)),
            pl.BlockSpec((1, D_MODEL), const),
            pl.BlockSpec((D_MODEL, 2 * ROUTER_PAD), const),
            pl.BlockSpec((1, ROUTER_PAD), const),
        ],
        out_specs=[
            pl.BlockSpec((tile_m, D_MODEL), lambda i: (i, 0)),
            pl.BlockSpec((tile_m, D_MODEL), lambda i: (i, 0)),
            pl.BlockSpec((tile_m, ROUTER_PAD), lambda i: (i, 0)),
        ],
        out_shape=[
            jax.ShapeDtypeStruct((t_all, D_MODEL), F32),
            jax.ShapeDtypeStruct((t_all, D_MODEL), F32),
            jax.ShapeDtypeStruct((t_all, ROUTER_PAD), F32),
        ],
        compiler_params=_params("arbitrary"),
        name="merge",
    )(x_a, x_b, a_a, a_b, mh_a, mh_b, z_all, z_all, wa, wm, wo, g_ffn.reshape(1, D_MODEL), wr_pad, br_pad)


def _expert_body(be_ref, nb_ref, xs_ref, w1g_ref, w1u_ref, b1g_ref, b1u_ref, w2_ref, b2_ref, y_ref):
    i = pl.program_id(0)
    c = pl.program_id(1)
    used = i < nb_ref[0]

    @pl.when(used)
    def _():
        x = xs_ref[...].astype(BF16)
        zg = jnp.dot(x, w1g_ref[...], preferred_element_type=F32) + b1g_ref[...]
        zu = jnp.dot(x, w1u_ref[...], preferred_element_type=F32) + b1u_ref[...]
        gate = jnp.minimum(zg, SWIGLU_LIMIT)
        up = jnp.clip(zu, -SWIGLU_LIMIT, SWIGLU_LIMIT)
        act = gate * jax.nn.sigmoid(SWIGLU_ALPHA * gate) * (up + 1.0)
        y = jnp.dot(act.astype(BF16), w2_ref[...], preferred_element_type=F32)

        @pl.when(c == 0)
        def _():
            y_ref[...] = y + b2_ref[...]

        @pl.when(c != 0)
        def _():
            y_ref[...] += y

    @pl.when(jnp.logical_not(used) & (c == 0))
    def _():
        y_ref[...] = jnp.zeros_like(y_ref)


def _experts(block_e, n_used, xs, w1, b1, w2, b2):
    n_slots = xs.shape[0]
    n_blocks = n_slots // MOE_BLOCK_M
    n_chunks = D_FF // MOE_FF_CHUNK

    def chunk(i, c, nb):
        block = jnp.minimum(i, nb[0] - 1)
        step = jnp.where(i < nb[0], c, n_chunks - 1)
        return jnp.where(block % 2 == 0, step, n_chunks - 1 - step)

    grid_spec = pltpu.PrefetchScalarGridSpec(
        num_scalar_prefetch=2,
        grid=(n_blocks, n_chunks),
        in_specs=[
            pl.BlockSpec((MOE_BLOCK_M, D_MODEL), lambda i, c, be, nb: (jnp.minimum(i, nb[0] - 1), 0)),
            pl.BlockSpec((None, D_MODEL, MOE_FF_CHUNK), lambda i, c, be, nb: (be[i], 0, chunk(i, c, nb))),
            pl.BlockSpec((None, D_MODEL, MOE_FF_CHUNK),
                         lambda i, c, be, nb: (be[i], 0, n_chunks + chunk(i, c, nb))),
            pl.BlockSpec((None, 1, MOE_FF_CHUNK), lambda i, c, be, nb: (be[i], 0, chunk(i, c, nb))),
            pl.BlockSpec((None, 1, MOE_FF_CHUNK), lambda i, c, be, nb: (be[i], 0, n_chunks + chunk(i, c, nb))),
            pl.BlockSpec((None, MOE_FF_CHUNK, D_MODEL), lambda i, c, be, nb: (be[i], chunk(i, c, nb), 0)),
            pl.BlockSpec((None, 1, D_MODEL), lambda i, c, be, nb: (be[i], 0, 0)),
        ],
        out_specs=pl.BlockSpec((MOE_BLOCK_M, D_MODEL), lambda i, c, be, nb: (i, 0)),
    )
    return pl.pallas_call(
        _expert_body,
        grid_spec=grid_spec,
        out_shape=jax.ShapeDtypeStruct((n_slots, D_MODEL), F32),
        compiler_params=_params("arbitrary", "arbitrary"),
        name="experts",
    )(block_e, n_used, xs, w1, w1, b1, b1, w2, b2)


ROUTE_TILE = 1024
LANE_E = 0
LANE_RANK = TOP_K


def _route_body(lg_ref, meta_ref, gate_ref, cnt_ref, tri_scr, carry_scr):
    @pl.when(pl.program_id(0) == 0)
    def _():
        r = lax.broadcasted_iota(jnp.int32, (ROUTE_TILE, ROUTE_TILE), 0)
        c = lax.broadcasted_iota(jnp.int32, (ROUTE_TILE, ROUTE_TILE), 1)
        tri_scr[...] = jnp.where(c < r, 1.0, 0.0).astype(BF16)
        carry_scr[...] = jnp.zeros_like(carry_scr)

    lane_i = lax.broadcasted_iota(jnp.int32, (ROUTE_TILE, LANES), 1)
    lane = lane_i.astype(F32)
    x = jnp.where(lane_i < N_EXPERTS, lg_ref[...], -jnp.inf)
    vals, idxs, sels = [], [], []
    for _ in range(TOP_K):
        mx = jnp.max(x, axis=1, keepdims=True)
        idx = jnp.min(jnp.where(x == mx, lane, float(LANES)), axis=1, keepdims=True)
        sel = lane == idx
        x = jnp.where(sel, -jnp.inf, x)
        vals.append(mx)
        idxs.append(idx)
        sels.append(sel)
    weights = [jnp.exp(v - vals[0]) for v in vals]
    inv = 1.0 / (weights[0] + weights[1] + weights[2] + weights[3])
    onehot = sum(jnp.where(s, 1.0, 0.0) for s in sels)
    ranks = jnp.dot(tri_scr[...], onehot.astype(BF16), preferred_element_type=F32) + carry_scr[0:1, :]
    meta = jnp.zeros((ROUTE_TILE, LANES), F32)
    gate = jnp.zeros((ROUTE_TILE, LANES), F32)
    for k in range(TOP_K):
        rank_k = jnp.sum(jnp.where(sels[k], ranks, 0.0), axis=1, keepdims=True)
        meta = jnp.where(lane_i == LANE_E + k, idxs[k], meta)
        meta = jnp.where(lane_i == LANE_RANK + k, rank_k, meta)
        gate = jnp.where(lane_i == k, weights[k] * inv, gate)
    meta_ref[...] = meta.astype(jnp.int32)
    gate_ref[...] = gate
    carry_scr[0:1, :] = carry_scr[0:1, :] + jnp.sum(onehot, axis=0, keepdims=True)
    cnt_ref[...] = carry_scr[...]


def _route(logits):
    t = logits.shape[0]
    return pl.pallas_call(
        _route_body,
        grid=(t // ROUTE_TILE,),
        in_specs=[pl.BlockSpec((ROUTE_TILE, LANES), lambda i: (i, 0))],
        out_specs=[
            pl.BlockSpec((ROUTE_TILE, LANES), lambda i: (i, 0)),
            pl.BlockSpec((ROUTE_TILE, LANES), lambda i: (i, 0)),
            pl.BlockSpec((SUBLANES, LANES), lambda i: (0, 0)),
        ],
        out_shape=[
            jax.ShapeDtypeStruct((t, LANES), jnp.int32),
            jax.ShapeDtypeStruct((t, LANES), F32),
            jax.ShapeDtypeStruct((SUBLANES, LANES), F32),
        ],
        scratch_shapes=[pltpu.VMEM((ROUTE_TILE, ROUTE_TILE), BF16), pltpu.VMEM((SUBLANES, LANES), F32)],
        compiler_params=_params("arbitrary"),
        name="route",
    )(logits)


def _slot_tables(counts, n_blocks):
    padded = (counts + MOE_BLOCK_M - 1) // MOE_BLOCK_M * MOE_BLOCK_M
    pends = jnp.cumsum(padded)
    first_slot = jnp.arange(n_blocks, dtype=pends.dtype) * MOE_BLOCK_M
    block_e = jnp.minimum(jnp.sum(pends[None, :] <= first_slot[:, None], axis=1), N_EXPERTS - 1).astype(jnp.int32)
    n_used = (pends[-1] // MOE_BLOCK_M).astype(jnp.int32).reshape(1)
    return (pends - padded).astype(jnp.int32), pends.astype(jnp.int32), padded.astype(jnp.int32), block_e, n_used


DISPATCH_TILE = 1024
IDX_SLOTS = 3


def _dispatch_body(pend_ref, padded_ref, dest_hbm, h_ref, xs_hbm, idx_smem, zero_scr, idx_sem, row_sem, zero_sem):
    i = pl.program_id(0)
    n = pl.num_programs(0)

    width = DISPATCH_TILE * TOP_K

    def idx_copy(tile):
        s = lax.rem(tile, IDX_SLOTS)
        return pltpu.make_async_copy(dest_hbm.at[pl.ds(pl.multiple_of(tile * width, width), width)],
                                     idx_smem.at[pl.ds(pl.multiple_of(s * width, width), width)], idx_sem.at[s])

    def zero_copy(e):
        start = pl.multiple_of(pend_ref[e] - MOE_BLOCK_M, MOE_BLOCK_M)
        return pltpu.make_async_copy(zero_scr, xs_hbm.at[pl.ds(start, MOE_BLOCK_M), :], zero_sem)

    def row_copy(g, j, d):
        return pltpu.make_async_copy(h_ref.at[g, pl.ds(j, 1), :], xs_hbm.at[pl.ds(d, 1), :], row_sem)

    @pl.when(i == 0)
    def _():
        zero_scr[...] = jnp.zeros_like(zero_scr)
        for e in range(N_EXPERTS):
            @pl.when(padded_ref[e] > 0)
            def _():
                zero_copy(e).start()
        for e in range(N_EXPERTS):
            @pl.when(padded_ref[e] > 0)
            def _():
                zero_copy(e).wait()
        idx_copy(0).start()

    @pl.when(i + 1 < n)
    def _():
        idx_copy(i + 1).start()

    idx_copy(i).wait()
    base = lax.rem(i, IDX_SLOTS) * width

    def issue(g, c):
        for j in range(SUBLANES):
            for k in range(TOP_K):
                d = idx_smem[base + g * (SUBLANES * TOP_K) + (j * TOP_K + k)]
                row_copy(g, j, d).start(priority=k % 2)
        return c

    lax.fori_loop(0, DISPATCH_TILE // SUBLANES, issue, 0)
    for _ in range(TOP_K):
        pltpu.make_async_copy(h_ref, h_ref, row_sem).wait()


def _scatter_rows(h, dest, pends, padded, n_slots):
    t = h.shape[0]
    tiles = t // DISPATCH_TILE
    grid_spec = pltpu.PrefetchScalarGridSpec(
        num_scalar_prefetch=2,
        grid=(tiles,),
        in_specs=[
            pl.BlockSpec(memory_space=pl.ANY),
            pl.BlockSpec((DISPATCH_TILE // SUBLANES, SUBLANES, D_MODEL), lambda i, pe, pd: (i, 0, 0)),
        ],
        out_specs=pl.BlockSpec(memory_space=pl.ANY),
        scratch_shapes=[
            pltpu.SMEM((IDX_SLOTS * DISPATCH_TILE * TOP_K,), jnp.int32),
            pltpu.VMEM((MOE_BLOCK_M, D_MODEL), F32),
            pltpu.SemaphoreType.DMA((IDX_SLOTS,)),
            pltpu.SemaphoreType.DMA(()),
            pltpu.SemaphoreType.DMA(()),
        ],
    )
    return pl.pallas_call(
        _dispatch_body,
        grid_spec=grid_spec,
        out_shape=jax.ShapeDtypeStruct((n_slots, D_MODEL), F32),
        compiler_params=_params("arbitrary"),
        name="dispatch",
    )(pends, padded, dest.reshape(-1), h.reshape(t // SUBLANES, SUBLANES, D_MODEL))


COMBINE_TILE = 512


def _combine_body(tile_off, dest_hbm, x1_ref, gate_ref, gf_ref, ys_hbm, y_ref, idx_smem, buf, idx_sem, row_sem):
    i = pl.program_id(0)
    n = pl.num_programs(0)

    width = COMBINE_TILE * TOP_K

    def idx_copy(tile):
        s = lax.rem(tile, IDX_SLOTS)
        return pltpu.make_async_copy(dest_hbm.at[pl.ds(pl.multiple_of((tile_off + tile) * width, width), width)],
                                     idx_smem.at[pl.ds(pl.multiple_of(s * width, width), width)], idx_sem.at[s])

    def row_copy(d, b, k, g, j):
        return pltpu.make_async_copy(ys_hbm.at[pl.ds(d, 1), :], buf.at[b, k, g, pl.ds(j, 1), :], row_sem.at[b])

    def issue_rows(tile):
        base = lax.rem(tile, IDX_SLOTS) * width
        b = lax.rem(tile, 2)

        def body(g, c):
            for j in range(SUBLANES):
                for k in range(TOP_K):
                    d = idx_smem[base + g * (SUBLANES * TOP_K) + (j * TOP_K + k)]
                    row_copy(d, b, k, g, j).start(priority=k % 2)
            return c

        lax.fori_loop(0, COMBINE_TILE // SUBLANES, body, 0)

    @pl.when(i == 0)
    def _():
        idx_copy(0).start()
        idx_copy(0).wait()
        issue_rows(0)

        @pl.when(n > 1)
        def _():
            idx_copy(1).start()

    @pl.when(i + 1 < n)
    def _():
        idx_copy(i + 1).wait()
        issue_rows(i + 1)

    @pl.when(i + 2 < n)
    def _():
        idx_copy(i + 2).start()

    b = lax.rem(i, 2)

    pltpu.make_async_copy(buf.at[b], buf.at[b], row_sem.at[b]).wait()
    gate = gate_ref[...]
    acc = x1_ref[...]
    for k in range(TOP_K):
        acc = acc + buf[b, k].reshape(COMBINE_TILE, D_MODEL) * gate[:, k:k + 1]
    y_ref[...] = acc * lax.rsqrt(jnp.mean(acc * acc, axis=-1, keepdims=True) + RMS_EPS) * gf_ref[...]


def _combine(x1, gate, dest, ys, g_final, row_off, rows):
    t = x1.shape[0]
    tile_off = row_off // COMBINE_TILE
    return pl.pallas_call(
        functools.partial(_combine_body, tile_off),
        grid=(rows // COMBINE_TILE,),
        in_specs=[
            pl.BlockSpec(memory_space=pl.ANY),
            pl.BlockSpec((COMBINE_TILE, D_MODEL), lambda i: (tile_off + i, 0)),
            pl.BlockSpec((COMBINE_TILE, LANES), lambda i: (tile_off + i, 0)),
            pl.BlockSpec((1, D_MODEL), lambda i: (0, 0)),
            pl.BlockSpec(memory_space=pl.ANY),
        ],
        out_specs=pl.BlockSpec((COMBINE_TILE, D_MODEL), lambda i: (i, 0)),
        out_shape=jax.ShapeDtypeStruct((rows, D_MODEL), F32),
        scratch_shapes=[
            pltpu.SMEM((IDX_SLOTS * COMBINE_TILE * TOP_K,), jnp.int32),
            pltpu.VMEM((2, TOP_K, COMBINE_TILE // SUBLANES, SUBLANES, D_MODEL), F32),
            pltpu.SemaphoreType.DMA((IDX_SLOTS,)),
            pltpu.SemaphoreType.DMA((2,)),
        ],
        compiler_params=_params("arbitrary"),
        name="combine",
    )(dest.reshape(-1), x1, gate, g_final.reshape(1, D_MODEL), ys)


def _lane_row(v):
    return jnp.pad(v.astype(F32), (0, LANES - v.shape[0])).reshape(1, LANES)


def kernel(x_prompt, x_sample, cache_swa_k, cache_swa_v, state_mlstm_C, state_mlstm_n, state_mlstm_m,
           g_mix, w_in, b_if, attn_sinks, mlstm_norm_w, w_attn_proj, w_mlstm_proj, w_out,
           g_ffn, w_router, b_router, w_e1, b_e1, w_e2, b_e2, g_final):
    bp, sp = x_prompt.shape[0], x_prompt.shape[1]
    bs, ls = x_sample.shape[0], x_sample.shape[1]
    tp = bp * sp
    ts = bs * SAMPLE_ROWS
    t_all = tp + ts
    l = 0
    w_in_p = _permute_w_in(w_in[l])
    wa, wm, wo = w_attn_proj[l].astype(BF16), w_mlstm_proj[l].astype(BF16), w_out[l].astype(BF16)
    wr_f32 = jnp.pad(w_router[l], ((0, 0), (0, ROUTER_PAD - N_EXPERTS)))
    wr_hi = wr_f32.astype(BF16)
    wr_pad = jnp.concatenate([wr_hi, (wr_f32 - wr_hi.astype(F32)).astype(BF16)], axis=1)
    br_pad = _lane_row(b_router[l])
    b1, b2 = b_e1[l].reshape(N_EXPERTS, 1, 2 * D_FF), b_e2[l].reshape(N_EXPERTS, 1, D_MODEL)
    sinks_row = _lane_row(attn_sinks[l])
    bias_row = _lane_row(b_if[l])
    norm_w_row = mlstm_norm_w[l].reshape(1, M_V_WIDTH)

    xp = x_prompt.reshape(tp, D_MODEL)
    xs = jnp.pad(x_sample, ((0, 0), (0, SAMPLE_ROWS - ls), (0, 0))).reshape(ts, D_MODEL)
    z_all = _inproj(xp, xs, g_mix[l], w_in_p, 512)

    a_p, k1, v1, w1 = _attn_prompt(z_all, sinks_row, bp, sp, w_e1[l])
    a_s, k2, v2 = _attn_sample(z_all, sinks_row, cache_swa_k[l].reshape(bs, WINDOW, KV_WIDTH),
                               cache_swa_v[l].reshape(bs, WINDOW, KV_WIDTH), tp, ls)
    mh_p, c1, n1, m1, w2 = _mlstm_prompt(z_all, bias_row, norm_w_row, bp, sp, w_e2[l])
    mh_s, c2, n2, m2 = _mlstm_sample(z_all, bias_row, norm_w_row, state_mlstm_C[l], state_mlstm_n[l],
                                     state_mlstm_m[l], tp, ls)

    x1, h2, logits = _merge(xp, xs, a_p, a_s, mh_p, mh_s, z_all, wa, wm, wo, g_ffn[l], wr_pad, br_pad, 256)

    meta, gate, counts = _route(logits)
    n_blocks = -(-t_all * TOP_K // MOE_BLOCK_M) + N_EXPERTS
    pstarts, pends, padded, block_e, n_used = _slot_tables(counts[0, :N_EXPERTS].astype(jnp.int32), n_blocks)
    dest = pstarts[meta[:, LANE_E:LANE_E + TOP_K]] + meta[:, LANE_RANK:LANE_RANK + TOP_K]
    xs_sorted = _scatter_rows(h2, dest, pends, padded, n_blocks * MOE_BLOCK_M)
    ys = _experts(block_e, n_used, xs_sorted, w1, b1, w2, b2)
    y_prompt = _combine(x1, gate, dest, ys, g_final, 0, tp).reshape(bp, sp, D_MODEL)
    y_sample = _combine(x1, gate, dest, ys, g_final, tp, ts).reshape(bs, SAMPLE_ROWS, D_MODEL)[:, :ls]
    k1 = k1.reshape(bp, WINDOW, N_KV_HEADS, HEAD_DIM)
    v1 = v1.reshape(bp, WINDOW, N_KV_HEADS, HEAD_DIM)
    k2 = k2.reshape(bs, WINDOW, N_KV_HEADS, HEAD_DIM)
    v2 = v2.reshape(bs, WINDOW, N_KV_HEADS, HEAD_DIM)
    m1 = m1[:, :, 0]
    st = lambda t: t[None]
    return (y_prompt, y_sample, st(k1), st(v1), st(c1), st(n1), st(m1), st(k2), st(v2), st(c2), st(n2), st(m2))
```

```python
import functools

import jax
import jax.numpy as jnp
from jax import lax
from jax.experimental import pallas as pl
from jax.experimental.pallas import tpu as pltpu

D_MODEL = 2048
N_HEADS = 16
N_KV_HEADS = 2
HEAD_DIM = 64
Q_PER_KV = N_HEADS // N_KV_HEADS
WINDOW = 128
ATTN_WIDTH = N_HEADS * HEAD_DIM
KV_WIDTH = N_KV_HEADS * HEAD_DIM
M_HEADS = 4
M_QK_DIM = 128
M_V_DIM = 256
M_QK_WIDTH = M_HEADS * M_QK_DIM
M_V_WIDTH = M_HEADS * M_V_DIM
M_CHUNK = 64
N_EXPERTS = 32
TOP_K = 4
D_FF = D_MODEL
SWIGLU_LIMIT = 7.0
SWIGLU_ALPHA = 1.702
RMS_EPS = 1e-5
NEG = -1e30

F32 = jnp.float32
BF16 = jnp.bfloat16

LANES = 128
SUBLANES = 8
VMEM_LIMIT_BYTES = 56 * 1024 * 1024

COL_GZ = 0
COL_AQ = COL_GZ + 2 * D_MODEL
COL_MV = COL_AQ + ATTN_WIDTH
COL_MO = COL_MV + M_V_WIDTH
COL_MQ = COL_MO + M_V_WIDTH
COL_MK = COL_MQ + M_QK_WIDTH
COL_AK = COL_MK + M_QK_WIDTH
COL_IF = COL_AK + 2 * KV_WIDTH
IN_TILE_N = 2176
IN_WIDTH_PAD = -(-(COL_IF + LANES) // IN_TILE_N) * IN_TILE_N

SAMPLE_ROWS = SUBLANES
MOE_BLOCK_M = 512
MOE_FF_CHUNK = 1024
ROUTER_PAD = LANES


def _permute_w_in(w_in):
    splits = (ATTN_WIDTH, KV_WIDTH, KV_WIDTH, M_QK_WIDTH, M_QK_WIDTH, M_V_WIDTH, M_V_WIDTH, M_HEADS, M_HEADS,
              2 * D_MODEL)
    parts, off = [], 0
    for width in splits:
        parts.append(w_in[:, off:off + width])
        off += width
    aq, ak, av, mq, mk, mv, mo, mi, mf, gz = parts
    cols = jnp.concatenate([gz, aq, mv, mo, mq, mk, ak, av, mi, mf], axis=1)
    cols = jnp.pad(cols, ((0, 0), (0, IN_WIDTH_PAD - cols.shape[1])))
    return cols.astype(BF16)


def _params(*semantics):
    return pltpu.CompilerParams(dimension_semantics=semantics, vmem_limit_bytes=VMEM_LIMIT_BYTES)


def _log_sigmoid(x):
    return jnp.minimum(x, 0.0) - jnp.log1p(jnp.exp(-jnp.abs(x)))


def _inproj_body(tiles_a, xa_ref, xb_ref, g_ref, w_ref, z_ref, h_scr):
    def normalise(x_ref):
        x = x_ref[...]
        y = x * lax.rsqrt(jnp.mean(x * x, axis=-1, keepdims=True) + RMS_EPS)
        h_scr[...] = (y * g_ref[...]).astype(BF16)

    first = pl.program_id(1) == 0
    in_a = pl.program_id(0) < tiles_a

    @pl.when(first & in_a)
    def _():
        normalise(xa_ref)

    @pl.when(first & jnp.logical_not(in_a))
    def _():
        normalise(xb_ref)

    z_ref[...] = jnp.dot(h_scr[...], w_ref[...], preferred_element_type=F32)


def _inproj(x_a, x_b, g, w_bf16, tile_m):
    tiles_a, tiles_b = x_a.shape[0] // tile_m, x_b.shape[0] // tile_m
    n = w_bf16.shape[1]
    return pl.pallas_call(
        functools.partial(_inproj_body, tiles_a),
        grid=(tiles_a + tiles_b, n // IN_TILE_N),
        in_specs=[
            pl.BlockSpec((tile_m, D_MODEL), lambda i, j: (jnp.minimum(i, tiles_a - 1), 0)),
            pl.BlockSpec((tile_m, D_MODEL), lambda i, j: (jnp.maximum(i - tiles_a, 0), 0)),
            pl.BlockSpec((1, D_MODEL), lambda i, j: (0, 0)),
            pl.BlockSpec((D_MODEL, IN_TILE_N), lambda i, j: (0, j)),
        ],
        out_specs=pl.BlockSpec((tile_m, IN_TILE_N), lambda i, j: (i, j)),
        out_shape=jax.ShapeDtypeStruct(((tiles_a + tiles_b) * tile_m, n), F32),
        scratch_shapes=[pltpu.VMEM((tile_m, D_MODEL), BF16)],
        compiler_params=_params("arbitrary", "arbitrary"),
        name="inproj",
    )(x_a, x_b, g.reshape(1, D_MODEL), w_bf16)


assert KV_WIDTH == LANES and N_KV_HEADS == 2 and LANES == 2 * HEAD_DIM
Q_TILES = ATTN_WIDTH // LANES
TILES_PER_KV = Q_TILES // N_KV_HEADS
_NT = (((1,), (1,)), ((), ()))


def _kv_placements(x):
    lo = lax.broadcasted_iota(jnp.int32, x.shape, 1) < HEAD_DIM
    swapped = pltpu.roll(x, HEAD_DIM, axis=1)
    return ((jnp.where(lo, x, 0.0).astype(BF16), jnp.where(lo, 0.0, swapped).astype(BF16)),
            (jnp.where(lo, swapped, 0.0).astype(BF16), jnp.where(lo, 0.0, x).astype(BF16)))


def _slab_head(g, p, j):
    return g * Q_PER_KV + 2 * j + p


def _attn_prompt_body(q_ref, kvc_ref, kvp_ref, sink_ref, w_ref, a_ref, k_ref, v_ref, wb_ref):
    wb_ref[...] = w_ref[...].astype(BF16)
    n = pl.program_id(1)
    kc, vc = kvc_ref[:, :KV_WIDTH], kvc_ref[:, KV_WIDTH:]
    k_ref[...] = kc
    v_ref[...] = vc
    kk = _kv_placements(jnp.concatenate([kvp_ref[:, :KV_WIDTH], kc], axis=0))
    vv = _kv_placements(jnp.concatenate([kvp_ref[:, KV_WIDTH:], vc], axis=0))
    q = q_ref[...].astype(BF16)
    row = lax.broadcasted_iota(jnp.int32, (WINDOW, WINDOW), 0)
    col = lax.broadcasted_iota(jnp.int32, (WINDOW, WINDOW), 1)
    mask = jnp.concatenate([(col > row) & (n > 0), col <= row], axis=1)
    slabs, sinks = [], []
    for g in range(N_KV_HEADS):
        qt = jnp.concatenate([q[:, (TILES_PER_KV * g + j) * LANES:(TILES_PER_KV * g + j + 1) * LANES]
                              for j in range(TILES_PER_KV)], axis=0)
        for p in range(2):
            s = lax.dot_general(qt, kk[g][p], _NT, preferred_element_type=F32) * (HEAD_DIM ** -0.5)
            for j in range(TILES_PER_KV):
                h = _slab_head(g, p, j)
                slabs.append(jnp.where(mask, s[j * WINDOW:(j + 1) * WINDOW], NEG))
                sinks.append(jnp.broadcast_to(sink_ref[:, h:h + 1], (WINDOW, LANES)))
    s_all = jnp.concatenate(slabs, axis=0)
    sink = jnp.concatenate(sinks, axis=0)
    m = jnp.maximum(jnp.broadcast_to(jnp.max(s_all, axis=1, keepdims=True), sink.shape), sink)
    p_all = jnp.exp(s_all - jnp.concatenate([m, m], axis=1))
    p_hi = p_all.astype(BF16)
    p_lo = (p_all - p_hi.astype(F32)).astype(BF16)
    ones = jnp.ones((2 * WINDOW, LANES), BF16)
    denom = (jnp.dot(p_hi, ones, preferred_element_type=F32) + jnp.dot(p_lo, ones, preferred_element_type=F32)
             + jnp.exp(sink - m))
    inv = 1.0 / denom
    pn = (p_all * jnp.concatenate([inv, inv], axis=1)).astype(BF16)
    half = TILES_PER_KV * WINDOW
    for g in range(N_KV_HEADS):
        base = 2 * half * g
        o = (jnp.dot(pn[base:base + half], vv[g][0], preferred_element_type=F32)
             + jnp.dot(pn[base + half:base + 2 * half], vv[g][1], preferred_element_type=F32))
        for j in range(TILES_PER_KV):
            t = TILES_PER_KV * g + j
            a_ref[:, t * LANES:(t + 1) * LANES] = o[j * WINDOW:(j + 1) * WINDOW].astype(BF16)


def _cast_slices(w, steps):
    rows = w.reshape(-1, w.shape[-1])
    per_step = rows.shape[0] // steps
    assert per_step * steps == rows.shape[0] and per_step % 16 == 0
    return rows, per_step


def _attn_prompt(z_all, sinks_row, batch, seq, w):
    nb = seq // WINDOW
    w_rows, r1 = _cast_slices(w, batch * nb)
    step = lambda b, n: (b * nb + n, 0)
    a, k, v, wb = pl.pallas_call(
        _attn_prompt_body,
        grid=(batch, nb),
        in_specs=[
            pl.BlockSpec((WINDOW, ATTN_WIDTH), lambda b, n: (b * nb + n, COL_AQ // ATTN_WIDTH)),
            pl.BlockSpec((WINDOW, 2 * KV_WIDTH), lambda b, n: (b * nb + n, COL_AK // (2 * KV_WIDTH))),
            pl.BlockSpec((WINDOW, 2 * KV_WIDTH),
                         lambda b, n: (b * nb + jnp.maximum(n - 1, 0), COL_AK // (2 * KV_WIDTH))),
            pl.BlockSpec((1, LANES), lambda b, n: (0, 0)),
            pl.BlockSpec((r1, w_rows.shape[1]), step),
        ],
        out_specs=[
            pl.BlockSpec((WINDOW, ATTN_WIDTH), lambda b, n: (b * nb + n, 0)),
            pl.BlockSpec((None, WINDOW, KV_WIDTH), lambda b, n: (b, 0, 0)),
            pl.BlockSpec((None, WINDOW, KV_WIDTH), lambda b, n: (b, 0, 0)),
            pl.BlockSpec((r1, w_rows.shape[1]), step),
        ],
        out_shape=[
            jax.ShapeDtypeStruct((batch * seq, ATTN_WIDTH), BF16),
            jax.ShapeDtypeStruct((batch, WINDOW, KV_WIDTH), F32),
            jax.ShapeDtypeStruct((batch, WINDOW, KV_WIDTH), F32),
            jax.ShapeDtypeStruct(w_rows.shape, BF16),
        ],
        compiler_params=_params("arbitrary", "arbitrary"),
        name="attn_prompt",
    )(z_all, z_all, z_all, sinks_row, w_rows)
    return a, k, v, wb.reshape(w.shape)


ATTN_SAMPLE_GROUP = 8


def _attn_sample_body(n_valid, q_ref, kv_ref, ck_ref, cv_ref, sink_ref, a_ref, nk_ref, nv_ref, o_scr):
    rs = SAMPLE_ROWS
    slab = TILES_PER_KV * rs
    r8 = lax.broadcasted_iota(jnp.int32, (rs, KV_WIDTH), 0)
    qi_c = jnp.bitwise_and(lax.broadcasted_iota(jnp.int32, (slab, WINDOW), 0), rs - 1)
    mask_c = lax.broadcasted_iota(jnp.int32, (slab, WINDOW), 1) > qi_c
    qi_n = jnp.bitwise_and(lax.broadcasted_iota(jnp.int32, (slab, rs), 0), rs - 1)
    mask_n = lax.broadcasted_iota(jnp.int32, (slab, rs), 1) <= qi_n
    sc_list, sn_list, sink_list, values = [], [], [], []
    for b in range(ATTN_SAMPLE_GROUP):
        rows = slice(b * rs, (b + 1) * rs)
        q = q_ref[rows, :]
        kn, vn = kv_ref[rows, :KV_WIDTH], kv_ref[rows, KV_WIDTH:]
        ck, cv = ck_ref[b], cv_ref[b]
        for src, new, dst in ((ck, kn, nk_ref), (cv, vn, nv_ref)):
            rolled = pltpu.roll(src, WINDOW - n_valid, axis=0)
            dst[b] = rolled
            tail = jnp.where(r8 >= rs - n_valid, pltpu.roll(new, rs - n_valid, axis=0), rolled[WINDOW - rs:, :])
            dst[b, WINDOW - rs:, :] = tail
        kk_c, kk_n = _kv_placements(ck), _kv_placements(kn)
        values.append((_kv_placements(cv), _kv_placements(vn)))
        for g in range(N_KV_HEADS):
            qt = jnp.concatenate([q[:, (TILES_PER_KV * g + j) * LANES:(TILES_PER_KV * g + j + 1) * LANES]
                                  for j in range(TILES_PER_KV)], axis=0).astype(BF16)
            for p in range(2):
                s_c = lax.dot_general(qt, kk_c[g][p], _NT, preferred_element_type=F32) * (HEAD_DIM ** -0.5)
                s_n = lax.dot_general(qt, kk_n[g][p], _NT, preferred_element_type=F32) * (HEAD_DIM ** -0.5)
                sc_list.append(jnp.where(mask_c, s_c, NEG))
                sn_list.append(jnp.where(mask_n, s_n, NEG))
                for j in range(TILES_PER_KV):
                    h = _slab_head(g, p, j)
                    sink_list.append(jnp.broadcast_to(sink_ref[:, h:h + 1], (rs, 1)))
    s_c = jnp.concatenate(sc_list, axis=0)
    s_n = jnp.concatenate(sn_list, axis=0)
    sink = jnp.concatenate(sink_list, axis=0)
    m = jnp.maximum(jnp.maximum(jnp.max(s_c, axis=1, keepdims=True), jnp.max(s_n, axis=1, keepdims=True)), sink)
    p_c = jnp.exp(s_c - m)
    p_n = jnp.exp(s_n - m)
    inv = 1.0 / (jnp.sum(p_c, axis=1, keepdims=True) + jnp.sum(p_n, axis=1, keepdims=True) + jnp.exp(sink - m))
    p_c = (p_c * inv).astype(BF16)
    p_n = (p_n * inv).astype(BF16)
    for b in range(ATTN_SAMPLE_GROUP):
        vv_c, vv_n = values[b]
        for g in range(N_KV_HEADS):
            base = (b * N_KV_HEADS + g) * 2 * slab
            o = (jnp.dot(p_c[base:base + slab], vv_c[g][0], preferred_element_type=F32)
                 + jnp.dot(p_c[base + slab:base + 2 * slab], vv_c[g][1], preferred_element_type=F32)
                 + jnp.dot(p_n[base:base + slab], vv_n[g][0], preferred_element_type=F32)
                 + jnp.dot(p_n[base + slab:base + 2 * slab], vv_n[g][1], preferred_element_type=F32))
            for j in range(TILES_PER_KV):
                t = TILES_PER_KV * g + j
                o_scr[b * rs:(b + 1) * rs, t * LANES:(t + 1) * LANES] = o[j * rs:(j + 1) * rs]
    a_ref[...] = o_scr[...].astype(BF16)


def _attn_sample(z_all, sinks_row, cache_k, cache_v, row_off, n_valid):
    b = cache_k.shape[0]
    gr = ATTN_SAMPLE_GROUP
    rows = gr * SAMPLE_ROWS
    rb = row_off // rows
    return pl.pallas_call(
        functools.partial(_attn_sample_body, n_valid),
        grid=(b // gr,),
        in_specs=[
            pl.BlockSpec((rows, ATTN_WIDTH), lambda i: (rb + i, COL_AQ // ATTN_WIDTH)),
            pl.BlockSpec((rows, 2 * KV_WIDTH), lambda i: (rb + i, COL_AK // (2 * KV_WIDTH))),
            pl.BlockSpec((gr, WINDOW, KV_WIDTH), lambda i: (i, 0, 0)),
            pl.BlockSpec((gr, WINDOW, KV_WIDTH), lambda i: (i, 0, 0)),
            pl.BlockSpec((1, LANES), lambda i: (0, 0)),
        ],
        out_specs=[
            pl.BlockSpec((rows, ATTN_WIDTH), lambda i: (i, 0)),
            pl.BlockSpec((gr, WINDOW, KV_WIDTH), lambda i: (i, 0, 0)),
            pl.BlockSpec((gr, WINDOW, KV_WIDTH), lambda i: (i, 0, 0)),
        ],
        out_shape=[
            jax.ShapeDtypeStruct((b * SAMPLE_ROWS, ATTN_WIDTH), BF16),
            jax.ShapeDtypeStruct(cache_k.shape, F32),
            jax.ShapeDtypeStruct(cache_v.shape, F32),
        ],
        scratch_shapes=[pltpu.VMEM((rows, ATTN_WIDTH), F32)],
        compiler_params=_params("arbitrary"),
        name="attn_sample",
    )(z_all, z_all, cache_k, cache_v, sinks_row)


def _mlstm_gates(if_blk, bias_row, n_valid):
    length = if_blk.shape[0]
    g = if_blk + bias_row
    lane = lax.broadcasted_iota(jnp.int32, g.shape, 1)
    g = jnp.where(lane < M_HEADS, g, _log_sigmoid(g))
    if n_valid < length:
        row = lax.broadcasted_iota(jnp.int32, g.shape, 0)
        g = jnp.where(row < n_valid, g, jnp.where(lane < M_HEADS, NEG, 0.0))
    return g


def _mlstm_chunk(q, k, v, gates, gates_t, h, c, n, m):
    length = q.shape[0]
    i_col = gates[:, h:h + 1]
    f_col = gates[:, M_HEADS + h:M_HEADS + h + 1]
    i_row = gates_t[h:h + 1, :]
    f_row = gates_t[M_HEADS + h:M_HEADS + h + 1, :]
    t_idx = lax.broadcasted_iota(jnp.int32, (length, length), 0)
    s_idx = lax.broadcasted_iota(jnp.int32, (length, length), 1)
    causal = s_idx <= t_idx
    bcum_col = jnp.sum(jnp.where(causal, f_row, 0.0), axis=1, keepdims=True)
    bcum_row = jnp.sum(jnp.where(t_idx <= s_idx, f_col, 0.0), axis=0, keepdims=True)
    dlog = jnp.where(causal, bcum_col - bcum_row + i_row, NEG)
    inter = bcum_col + m
    mt = jnp.maximum(inter, jnp.max(dlog, axis=1, keepdims=True))
    w_inter = jnp.exp(inter - mt)
    qb = q.astype(BF16)
    ks = k * (M_QK_DIM ** -0.5)
    vb = v.astype(BF16)
    qk = lax.dot_general(qb, ks.astype(BF16), (((1,), (1,)), ((), ())), preferred_element_type=F32)
    qk = qk * jnp.exp(dlog - mt)
    num = (w_inter * jnp.dot(qb, c.astype(BF16), preferred_element_type=F32)
           + jnp.dot(qk.astype(BF16), vb, preferred_element_type=F32))
    den = w_inter * jnp.sum(q * n, axis=1, keepdims=True) + jnp.sum(qk, axis=1, keepdims=True)
    h_t = num / jnp.maximum(jnp.abs(den), jnp.exp(-mt))
    m_new = mt[length - 1:length, :]
    w_state = jnp.exp(bcum_col[length - 1:length, :] - bcum_col + i_col - m_new)
    decay = jnp.exp(inter[length - 1:length, :] - m_new)
    kw = ks * w_state
    c_new = decay * c + jnp.dot(kw.T.astype(BF16), vb, preferred_element_type=F32)
    n_new = decay * n + jnp.sum(kw, axis=0, keepdims=True)
    return h_t, c_new, n_new, m_new


def _mlstm_head_out(h_t, norm_w, o_gate):
    hs = h_t * lax.rsqrt(jnp.mean(h_t * h_t, axis=-1, keepdims=True) + RMS_EPS)
    return hs * norm_w * jax.nn.sigmoid(o_gate)


MLSTM_SEG = 256


MLSTM_BATCH = 2
MLSTM_PROMPT_CHUNK = 256


def _mlstm_prompt_body(*refs):
    nb = MLSTM_BATCH
    ins, rest = refs[:5 * nb], refs[5 * nb:]
    bias_ref, nw_ref, w_ref, mh_ref, c_out, n_out, m_out, wb_ref, c_scr, n_scr, m_scr = rest
    seg = pl.program_id(1)
    wb_ref[...] = w_ref[...].astype(BF16)

    @pl.when(seg == 0)
    def _():
        c_scr[...] = jnp.zeros_like(c_scr)
        n_scr[...] = jnp.zeros_like(n_scr)
        m_scr[...] = jnp.full_like(m_scr, NEG)

    def chunk_body(ci, carry):
        r = pl.multiple_of(ci * MLSTM_PROMPT_CHUNK, MLSTM_PROMPT_CHUNK)
        rows = pl.ds(r, MLSTM_PROMPT_CHUNK)
        state = [[(c_scr[u, h], n_scr[u, h:h + 1, :], m_scr[u, h:h + 1, 0:1]) for h in range(M_HEADS)]
                 for u in range(nb)]
        results = []
        for u in range(nb):
            q_ref, k_ref, v_ref, o_ref, if_ref = ins[5 * u:5 * u + 5]
            gates = _mlstm_gates(if_ref[rows, :], bias_ref[...], MLSTM_PROMPT_CHUNK)
            gates_t = gates.T
            for h in range(M_HEADS):
                qs = slice(h * M_QK_DIM, (h + 1) * M_QK_DIM)
                vs = slice(h * M_V_DIM, (h + 1) * M_V_DIM)
                h_t, c_new, n_new, m_new = _mlstm_chunk(
                    q_ref[rows, qs], k_ref[rows, qs], v_ref[rows, vs], gates, gates_t, h, *state[u][h])
                out = _mlstm_head_out(h_t, nw_ref[:, vs], o_ref[rows, vs]).astype(BF16)
                results.append((u, h, vs, out, c_new, n_new, m_new))
        for u, h, vs, out, c_new, n_new, m_new in results:
            c_scr[u, h] = c_new
            n_scr[u, h:h + 1, :] = n_new
            m_scr[u, h:h + 1, :] = jnp.broadcast_to(m_new, (1, LANES))
            mh_ref[u, rows, vs] = out
        return carry

    lax.fori_loop(0, MLSTM_SEG // MLSTM_PROMPT_CHUNK, chunk_body, 0)

    @pl.when(seg == pl.num_programs(1) - 1)
    def _():
        c_out[...] = c_scr[...]
        n_out[...] = n_scr[:, 0:M_HEADS, :]
        m_out[...] = m_scr[:, 0:M_HEADS, :]


def _mlstm_prompt(z_all, bias_row, norm_w_row, batch, seq, w):
    ns = seq // MLSTM_SEG
    nb = MLSTM_BATCH
    w_rows, wr = _cast_slices(w, (batch // nb) * ns)
    in_specs, args = [], []
    for u in range(nb):
        row = lambda p, s, u=u: (p * nb + u) * ns + s
        in_specs += [
            pl.BlockSpec((MLSTM_SEG, M_QK_WIDTH), lambda p, s, row=row: (row(p, s), COL_MQ // M_QK_WIDTH)),
            pl.BlockSpec((MLSTM_SEG, M_QK_WIDTH), lambda p, s, row=row: (row(p, s), COL_MK // M_QK_WIDTH)),
            pl.BlockSpec((MLSTM_SEG, M_V_WIDTH), lambda p, s, row=row: (row(p, s), COL_MV // M_V_WIDTH)),
            pl.BlockSpec((MLSTM_SEG, M_V_WIDTH), lambda p, s, row=row: (row(p, s), COL_MO // M_V_WIDTH)),
            pl.BlockSpec((MLSTM_SEG, LANES), lambda p, s, row=row: (row(p, s), COL_IF // LANES)),
        ]
        args += [z_all] * 5
    in_specs += [pl.BlockSpec((1, LANES), lambda p, s: (0, 0)), pl.BlockSpec((1, M_V_WIDTH), lambda p, s: (0, 0)),
                 pl.BlockSpec((wr, w_rows.shape[1]), lambda p, s: (p * ns + s, 0))]
    mh, c, n, m, wb = pl.pallas_call(
        _mlstm_prompt_body,
        grid=(batch // nb, ns),
        in_specs=in_specs,
        out_specs=[
            pl.BlockSpec((None, nb, MLSTM_SEG, M_V_WIDTH), lambda p, s: (p, 0, s, 0)),
            pl.BlockSpec((nb, M_HEADS, M_QK_DIM, M_V_DIM), lambda p, s: (p, 0, 0, 0)),
            pl.BlockSpec((nb, M_HEADS, M_QK_DIM), lambda p, s: (p, 0, 0)),
            pl.BlockSpec((nb, M_HEADS, LANES), lambda p, s: (p, 0, 0)),
            pl.BlockSpec((wr, w_rows.shape[1]), lambda p, s: (p * ns + s, 0)),
        ],
        out_shape=[
            jax.ShapeDtypeStruct((batch // nb, nb, seq, M_V_WIDTH), BF16),
            jax.ShapeDtypeStruct((batch, M_HEADS, M_QK_DIM, M_V_DIM), F32),
            jax.ShapeDtypeStruct((batch, M_HEADS, M_QK_DIM), F32),
            jax.ShapeDtypeStruct((batch, M_HEADS, LANES), F32),
            jax.ShapeDtypeStruct(w_rows.shape, BF16),
        ],
        scratch_shapes=[
            pltpu.VMEM((nb, M_HEADS, M_QK_DIM, M_V_DIM), F32),
            pltpu.VMEM((nb, SUBLANES, M_QK_DIM), F32),
            pltpu.VMEM((nb, SUBLANES, LANES), F32),
        ],
        compiler_params=_params("arbitrary", "arbitrary"),
        name="mlstm_prompt",
    )(*args, bias_row, norm_w_row, w_rows)
    return mh.reshape(batch * seq, M_V_WIDTH), c, n, m, wb.reshape(w.shape)


MLSTM_SAMPLE_GROUP = 8


def _mlstm_sample_body(n_valid, q_ref, k_ref, v_ref, o_ref, if_ref, bias_ref, nw_ref, c0_ref, n0_ref, m0_ref,
                       mh_ref, c_out, n_out, m_out, h_scr):
    for g in range(MLSTM_SAMPLE_GROUP):
        rows = slice(g * SAMPLE_ROWS, (g + 1) * SAMPLE_ROWS)
        gates = _mlstm_gates(if_ref[rows, :], bias_ref[...], n_valid)
        gates_t = jnp.transpose(jnp.concatenate([gates] * (LANES // SAMPLE_ROWS), axis=0))[:, :SAMPLE_ROWS]
        for h in range(M_HEADS):
            qs = slice(h * M_QK_DIM, (h + 1) * M_QK_DIM)
            vs = slice(h * M_V_DIM, (h + 1) * M_V_DIM)
            h_t, c_new, n_new, m_new = _mlstm_chunk(
                q_ref[rows, qs], k_ref[rows, qs], v_ref[rows, vs], gates, gates_t, h,
                c0_ref[g, h], n0_ref[g, h:h + 1, :], m0_ref[g:g + 1, h:h + 1])
            c_out[g, h] = c_new
            n_out[g, h:h + 1, :] = n_new
            m_out[g:g + 1, h:h + 1] = m_new
            h_scr[rows, vs] = _mlstm_head_out(h_t, nw_ref[:, vs], o_ref[rows, vs])
    mh_ref[...] = h_scr[...].astype(BF16)


def _mlstm_sample(z_all, bias_row, norm_w_row, c0, n0, m0, row_off, n_valid):
    b = c0.shape[0]
    gr = MLSTM_SAMPLE_GROUP
    rows = gr * SAMPLE_ROWS
    rb = row_off // rows
    m0g = m0.reshape(b // gr, gr, M_HEADS)
    outs = pl.pallas_call(
        functools.partial(_mlstm_sample_body, n_valid),
        grid=(b // gr,),
        in_specs=[
            pl.BlockSpec((rows, M_QK_WIDTH), lambda i: (rb + i, COL_MQ // M_QK_WIDTH)),
            pl.BlockSpec((rows, M_QK_WIDTH), lambda i: (rb + i, COL_MK // M_QK_WIDTH)),
            pl.BlockSpec((rows, M_V_WIDTH), lambda i: (rb + i, COL_MV // M_V_WIDTH)),
            pl.BlockSpec((rows, M_V_WIDTH), lambda i: (rb + i, COL_MO // M_V_WIDTH)),
            pl.BlockSpec((rows, LANES), lambda i: (rb + i, COL_IF // LANES)),
            pl.BlockSpec((1, LANES), lambda i: (0, 0)),
            pl.BlockSpec((1, M_V_WIDTH), lambda i: (0, 0)),
            pl.BlockSpec((gr, M_HEADS, M_QK_DIM, M_V_DIM), lambda i: (i, 0, 0, 0)),
            pl.BlockSpec((gr, M_HEADS, M_QK_DIM), lambda i: (i, 0, 0)),
            pl.BlockSpec((None, gr, M_HEADS), lambda i: (i, 0, 0)),
        ],
        out_specs=[
            pl.BlockSpec((rows, M_V_WIDTH), lambda i: (i, 0)),
            pl.BlockSpec((gr, M_HEADS, M_QK_DIM, M_V_DIM), lambda i: (i, 0, 0, 0)),
            pl.BlockSpec((gr, M_HEADS, M_QK_DIM), lambda i: (i, 0, 0)),
            pl.BlockSpec((None, gr, M_HEADS), lambda i: (i, 0, 0)),
        ],
        out_shape=[
            jax.ShapeDtypeStruct((b * SAMPLE_ROWS, M_V_WIDTH), BF16),
            jax.ShapeDtypeStruct(c0.shape, F32),
            jax.ShapeDtypeStruct(n0.shape, F32),
            jax.ShapeDtypeStruct(m0g.shape, F32),
        ],
        scratch_shapes=[pltpu.VMEM((rows, M_V_WIDTH), F32)],
        compiler_params=_params("arbitrary"),
        name="mlstm_sample",
    )(z_all, z_all, z_all, z_all, z_all, bias_row, norm_w_row, c0, n0, m0g)
    mh, c, n, m = outs
    return mh, c, n, m.reshape(b, M_HEADS)


def _merge_body(tiles_a, xa_ref, xb_ref, aa_ref, ab_ref, ma_ref, mb_ref, gza_ref, gzm_ref, wa_ref, wm_ref, wo_ref,
                gf_ref, wr_ref, br_ref, x1_ref, h2_ref, lg_ref):
    in_a = pl.program_id(0) < tiles_a
    x = jnp.where(in_a, xa_ref[...], xb_ref[...])
    a = jnp.where(in_a, aa_ref[...], ab_ref[...])
    mh = jnp.where(in_a, ma_ref[...], mb_ref[...])
    pa = jnp.dot(a, wa_ref[...], preferred_element_type=F32)
    pm = jnp.dot(mh, wm_ref[...], preferred_element_type=F32)
    merged = jax.nn.sigmoid(gza_ref[...]) * pa + jax.nn.sigmoid(gzm_ref[...]) * pm
    x1 = x + jnp.dot(merged.astype(BF16), wo_ref[...], preferred_element_type=F32)
    x1_ref[...] = x1
    h2 = x1 * lax.rsqrt(jnp.mean(x1 * x1, axis=-1, keepdims=True) + RMS_EPS) * gf_ref[...]
    h2_ref[...] = h2
    hi = h2.astype(BF16)
    lo = (h2 - hi.astype(F32)).astype(BF16)
    both = jnp.dot(hi, wr_ref[...], preferred_element_type=F32)
    lg_ref[...] = (both[:, :ROUTER_PAD] + both[:, ROUTER_PAD:]
                   + jnp.dot(lo, wr_ref[:, :ROUTER_PAD], preferred_element_type=F32) + br_ref[...])


def _merge(x_a, x_b, a_a, a_b, mh_a, mh_b, z_all, wa, wm, wo, g_ffn, wr_pad, br_pad, tile_m):
    tiles_a, tiles_b = x_a.shape[0] // tile_m, x_b.shape[0] // tile_m
    t_all = z_all.shape[0]
    const = lambda i: (0, 0)
    from_a = lambda i: (jnp.minimum(i, tiles_a - 1), 0)
    from_b = lambda i: (jnp.maximum(i - tiles_a, 0), 0)
    return pl.pallas_call(
        functools.partial(_merge_body, tiles_a),
        grid=(tiles_a + tiles_b,),
        in_specs=[
            pl.BlockSpec((tile_m, D_MODEL), from_a),
            pl.BlockSpec((tile_m, D_MODEL), from_b),
            pl.BlockSpec((tile_m, ATTN_WIDTH), from_a),
            pl.BlockSpec((tile_m, ATTN_WIDTH), from_b),
            pl.BlockSpec((tile_m, M_V_WIDTH), from_a),
            pl.BlockSpec((tile_m, M_V_WIDTH), from_b),
            pl.BlockSpec((tile_m, D_MODEL), lambda i: (i, COL_GZ // D_MODEL)),
            pl.BlockSpec((tile_m, D_MODEL), lambda i: (i, COL_GZ // D_MODEL + 1)),
            pl.BlockSpec((ATTN_WIDTH, D_MODEL), const, pipeline_mode=pl.Buffered(1)),
            pl.BlockSpec((M_V_WIDTH, D_MODEL), const, pipeline_mode=pl.Buffered(1)),
            pl.BlockSpec((D_MODEL, D_MODEL), const, pipeline_mode=pl.Buffered(1)),
            pl.BlockSpec((1, D_MODEL), const),
            pl.BlockSpec((D_MODEL, 2 * ROUTER_PAD), const),
            pl.BlockSpec((1, ROUTER_PAD), const),
        ],
        out_specs=[
            pl.BlockSpec((tile_m, D_MODEL), lambda i: (i, 0)),
            pl.BlockSpec((tile_m, D_MODEL), lambda i: (i, 0)),
            pl.BlockSpec((tile_m, ROUTER_PAD), lambda i: (i, 0)),
        ],
        out_shape=[
            jax.ShapeDtypeStruct((t_all, D_MODEL), F32),
            jax.ShapeDtypeStruct((t_all, D_MODEL), F32),
            jax.ShapeDtypeStruct((t_all, ROUTER_PAD), F32),
        ],
        compiler_params=_params("arbitrary"),
        name="merge",
    )(x_a, x_b, a_a, a_b, mh_a, mh_b, z_all, z_all, wa, wm, wo, g_ffn.reshape(1, D_MODEL), wr_pad, br_pad)


def _expert_body(be_ref, nb_ref, xs_ref, w1g_ref, w1u_ref, b1g_ref, b1u_ref, w2_ref, b2_ref, y_ref):
    i = pl.program_id(0)
    c = pl.program_id(1)
    used = i < nb_ref[0]

    @pl.when(used)
    def _():
        x = xs_ref[...].astype(BF16)
        zg = jnp.dot(x, w1g_ref[...], preferred_element_type=F32) + b1g_ref[...]
        zu = jnp.dot(x, w1u_ref[...], preferred_element_type=F32) + b1u_ref[...]
        gate = jnp.minimum(zg, SWIGLU_LIMIT)
        up = jnp.clip(zu, -SWIGLU_LIMIT, SWIGLU_LIMIT)
        act = gate * jax.nn.sigmoid(SWIGLU_ALPHA * gate) * (up + 1.0)
        y = jnp.dot(act.astype(BF16), w2_ref[...], preferred_element_type=F32)

        @pl.when(c == 0)
        def _():
            y_ref[...] = y + b2_ref[...]

        @pl.when(c != 0)
        def _():
            y_ref[...] += y

    @pl.when(jnp.logical_not(used) & (c == 0))
    def _():
        y_ref[...] = jnp.zeros_like(y_ref)


def _experts(block_e, n_used, xs, w1, b1, w2, b2):
    n_slots = xs.shape[0]
    n_blocks = n_slots // MOE_BLOCK_M
    n_chunks = D_FF // MOE_FF_CHUNK

    def chunk(i, c, nb):
        block = jnp.minimum(i, nb[0] - 1)
        step = jnp.where(i < nb[0], c, n_chunks - 1)
        return jnp.where(block % 2 == 0, step, n_chunks - 1 - step)

    grid_spec = pltpu.PrefetchScalarGridSpec(
        num_scalar_prefetch=2,
        grid=(n_blocks, n_chunks),
        in_specs=[
            pl.BlockSpec((MOE_BLOCK_M, D_MODEL), lambda i, c, be, nb: (jnp.minimum(i, nb[0] - 1), 0)),
            pl.BlockSpec((None, D_MODEL, MOE_FF_CHUNK), lambda i, c, be, nb: (be[i], 0, chunk(i, c, nb))),
            pl.BlockSpec((None, D_MODEL, MOE_FF_CHUNK),
                         lambda i, c, be, nb: (be[i], 0, n_chunks + chunk(i, c, nb))),
            pl.BlockSpec((None, 1, MOE_FF_CHUNK), lambda i, c, be, nb: (be[i], 0, chunk(i, c, nb))),
            pl.BlockSpec((None, 1, MOE_FF_CHUNK), lambda i, c, be, nb: (be[i], 0, n_chunks + chunk(i, c, nb))),
            pl.BlockSpec((None, MOE_FF_CHUNK, D_MODEL), lambda i, c, be, nb: (be[i], chunk(i, c, nb), 0)),
            pl.BlockSpec((None, 1, D_MODEL), lambda i, c, be, nb: (be[i], 0, 0)),
        ],
        out_specs=pl.BlockSpec((MOE_BLOCK_M, D_MODEL), lambda i, c, be, nb: (i, 0)),
    )
    return pl.pallas_call(
        _expert_body,
        grid_spec=grid_spec,
        out_shape=jax.ShapeDtypeStruct((n_slots, D_MODEL), F32),
        compiler_params=_params("arbitrary", "arbitrary"),
        name="experts",
    )(block_e, n_used, xs, w1, w1, b1, b1, w2, b2)


ROUTE_TILE = 1024
LANE_E = 0
LANE_RANK = TOP_K


def _route_body(lg_ref, meta_ref, gate_ref, cnt_ref, tri_scr, carry_scr):
    @pl.when(pl.program_id(0) == 0)
    def _():
        r = lax.broadcasted_iota(jnp.int32, (ROUTE_TILE, ROUTE_TILE), 0)
        c = lax.broadcasted_iota(jnp.int32, (ROUTE_TILE, ROUTE_TILE), 1)
        tri_scr[...] = jnp.where(c < r, 1.0, 0.0).astype(BF16)
        carry_scr[...] = jnp.zeros_like(carry_scr)

    lane_i = lax.broadcasted_iota(jnp.int32, (ROUTE_TILE, LANES), 1)
    lane = lane_i.astype(F32)
    x = jnp.where(lane_i < N_EXPERTS, lg_ref[...], -jnp.inf)
    vals, idxs, sels = [], [], []
    for _ in range(TOP_K):
        mx = jnp.max(x, axis=1, keepdims=True)
        idx = jnp.min(jnp.where(x == mx, lane, float(LANES)), axis=1, keepdims=True)
        sel = lane == idx
        x = jnp.where(sel, -jnp.inf, x)
        vals.append(mx)
        idxs.append(idx)
        sels.append(sel)
    weights = [jnp.exp(v - vals[0]) for v in vals]
    inv = 1.0 / (weights[0] + weights[1] + weights[2] + weights[3])
    onehot = sum(jnp.where(s, 1.0, 0.0) for s in sels)
    ranks = jnp.dot(tri_scr[...], onehot.astype(BF16), preferred_element_type=F32) + carry_scr[0:1, :]
    meta = jnp.zeros((ROUTE_TILE, LANES), F32)
    gate = jnp.zeros((ROUTE_TILE, LANES), F32)
    for k in range(TOP_K):
        rank_k = jnp.sum(jnp.where(sels[k], ranks, 0.0), axis=1, keepdims=True)
        meta = jnp.where(lane_i == LANE_E + k, idxs[k], meta)
        meta = jnp.where(lane_i == LANE_RANK + k, rank_k, meta)
        gate = jnp.where(lane_i == k, weights[k] * inv, gate)
    meta_ref[...] = meta.astype(jnp.int32)
    gate_ref[...] = gate
    carry_scr[0:1, :] = carry_scr[0:1, :] + jnp.sum(onehot, axis=0, keepdims=True)
    cnt_ref[...] = carry_scr[...]


def _route(logits):
    t = logits.shape[0]
    return pl.pallas_call(
        _route_body,
        grid=(t // ROUTE_TILE,),
        in_specs=[pl.BlockSpec((ROUTE_TILE, LANES), lambda i: (i, 0))],
        out_specs=[
            pl.BlockSpec((ROUTE_TILE, LANES), lambda i: (i, 0)),
            pl.BlockSpec((ROUTE_TILE, LANES), lambda i: (i, 0)),
            pl.BlockSpec((SUBLANES, LANES), lambda i: (0, 0)),
        ],
        out_shape=[
            jax.ShapeDtypeStruct((t, LANES), jnp.int32),
            jax.ShapeDtypeStruct((t, LANES), F32),
            jax.ShapeDtypeStruct((SUBLANES, LANES), F32),
        ],
        scratch_shapes=[pltpu.VMEM((ROUTE_TILE, ROUTE_TILE), BF16), pltpu.VMEM((SUBLANES, LANES), F32)],
        compiler_params=_params("arbitrary"),
        name="route",
    )(logits)


def _slot_tables(counts, n_blocks):
    padded = (counts + MOE_BLOCK_M - 1) // MOE_BLOCK_M * MOE_BLOCK_M
    pends = jnp.cumsum(padded)
    first_slot = jnp.arange(n_blocks, dtype=pends.dtype) * MOE_BLOCK_M
    block_e = jnp.minimum(jnp.sum(pends[None, :] <= first_slot[:, None], axis=1), N_EXPERTS - 1).astype(jnp.int32)
    n_used = (pends[-1] // MOE_BLOCK_M).astype(jnp.int32).reshape(1)
    return (pends - padded).astype(jnp.int32), pends.astype(jnp.int32), padded.astype(jnp.int32), block_e, n_used


DISPATCH_TILE = 1024
IDX_SLOTS = 3


def _dispatch_body(pend_ref, padded_ref, dest_hbm, h_ref, xs_hbm, idx_smem, zero_scr, idx_sem, row_sem, zero_sem):
    i = pl.program_id(0)
    n = pl.num_programs(0)

    width = DISPATCH_TILE * TOP_K

    def idx_copy(tile):
        s = lax.rem(tile, IDX_SLOTS)
        return pltpu.make_async_copy(dest_hbm.at[pl.ds(pl.multiple_of(tile * width, width), width)],
                                     idx_smem.at[pl.ds(pl.multiple_of(s * width, width), width)], idx_sem.at[s])

    def zero_copy(e):
        start = pl.multiple_of(pend_ref[e] - MOE_BLOCK_M, MOE_BLOCK_M)
        return pltpu.make_async_copy(zero_scr, xs_hbm.at[pl.ds(start, MOE_BLOCK_M), :], zero_sem)

    def row_copy(g, j, d):
        return pltpu.make_async_copy(h_ref.at[g, pl.ds(j, 1), :], xs_hbm.at[pl.ds(d, 1), :], row_sem)

    @pl.when(i == 0)
    def _():
        zero_scr[...] = jnp.zeros_like(zero_scr)
        for e in range(N_EXPERTS):
            @pl.when(padded_ref[e] > 0)
            def _():
                zero_copy(e).start()
        for e in range(N_EXPERTS):
            @pl.when(padded_ref[e] > 0)
            def _():
                zero_copy(e).wait()
        idx_copy(0).start()

    @pl.when(i + 1 < n)
    def _():
        idx_copy(i + 1).start()

    idx_copy(i).wait()
    base = lax.rem(i, IDX_SLOTS) * width

    def issue(g, c):
        for j in range(SUBLANES):
            for k in range(TOP_K):
                d = idx_smem[base + g * (SUBLANES * TOP_K) + (j * TOP_K + k)]
                row_copy(g, j, d).start(priority=k % 2)
        return c

    lax.fori_loop(0, DISPATCH_TILE // SUBLANES, issue, 0)
    for _ in range(TOP_K):
        pltpu.make_async_copy(h_ref, h_ref, row_sem).wait()


def _scatter_rows(h, dest, pends, padded, n_slots):
    t = h.shape[0]
    tiles = t // DISPATCH_TILE
    grid_spec = pltpu.PrefetchScalarGridSpec(
        num_scalar_prefetch=2,
        grid=(tiles,),
        in_specs=[
            pl.BlockSpec(memory_space=pl.ANY),
            pl.BlockSpec((DISPATCH_TILE // SUBLANES, SUBLANES, D_MODEL), lambda i, pe, pd: (i, 0, 0)),
        ],
        out_specs=pl.BlockSpec(memory_space=pl.ANY),
        scratch_shapes=[
            pltpu.SMEM((IDX_SLOTS * DISPATCH_TILE * TOP_K,), jnp.int32),
            pltpu.VMEM((MOE_BLOCK_M, D_MODEL), F32),
            pltpu.SemaphoreType.DMA((IDX_SLOTS,)),
            pltpu.SemaphoreType.DMA(()),
            pltpu.SemaphoreType.DMA(()),
        ],
    )
    return pl.pallas_call(
        _dispatch_body,
        grid_spec=grid_spec,
        out_shape=jax.ShapeDtypeStruct((n_slots, D_MODEL), F32),
        compiler_params=_params("arbitrary"),
        name="dispatch",
    )(pends, padded, dest.reshape(-1), h.reshape(t // SUBLANES, SUBLANES, D_MODEL))


COMBINE_TILE = 256


def _combine_body(tile_off, dest_hbm, x1_ref, gate_ref, gf_ref, ys_hbm, y_ref, idx_smem, buf, idx_sem, row_sem):
    i = pl.program_id(0)
    n = pl.num_programs(0)

    width = COMBINE_TILE * TOP_K

    def idx_copy(tile):
        s = lax.rem(tile, IDX_SLOTS)
        return pltpu.make_async_copy(dest_hbm.at[pl.ds(pl.multiple_of((tile_off + tile) * width, width), width)],
                                     idx_smem.at[pl.ds(pl.multiple_of(s * width, width), width)], idx_sem.at[s])

    def row_copy(d, b, k, g, j):
        return pltpu.make_async_copy(ys_hbm.at[pl.ds(d, 1), :], buf.at[b, k, g, pl.ds(j, 1), :], row_sem.at[b])

    def issue_rows(tile):
        base = lax.rem(tile, IDX_SLOTS) * width
        b = lax.rem(tile, 2)

        def body(g, c):
            for j in range(SUBLANES):
                for k in range(TOP_K):
                    d = idx_smem[base + g * (SUBLANES * TOP_K) + (j * TOP_K + k)]
                    row_copy(d, b, k, g, j).start(priority=k % 2)
            return c

        lax.fori_loop(0, COMBINE_TILE // SUBLANES, body, 0)

    @pl.when(i == 0)
    def _():
        idx_copy(0).start()
        idx_copy(0).wait()
        issue_rows(0)

        @pl.when(n > 1)
        def _():
            idx_copy(1).start()

    @pl.when(i + 1 < n)
    def _():
        idx_copy(i + 1).wait()
        issue_rows(i + 1)

    @pl.when(i + 2 < n)
    def _():
        idx_copy(i + 2).start()

    b = lax.rem(i, 2)

    pltpu.make_async_copy(buf.at[b], buf.at[b], row_sem.at[b]).wait()
    gate = gate_ref[...]
    acc = x1_ref[...]
    for k in range(TOP_K):
        acc = acc + buf[b, k].reshape(COMBINE_TILE, D_MODEL) * gate[:, k:k + 1]
    y_ref[...] = acc * lax.rsqrt(jnp.mean(acc * acc, axis=-1, keepdims=True) + RMS_EPS) * gf_ref[...]


def _combine(x1, gate, dest, ys, g_final, row_off, rows):
    t = x1.shape[0]
    tile_off = row_off // COMBINE_TILE
    return pl.pallas_call(
        functools.partial(_combine_body, tile_off),
        grid=(rows // COMBINE_TILE,),
        in_specs=[
            pl.BlockSpec(memory_space=pl.ANY),
            pl.BlockSpec((COMBINE_TILE, D_MODEL), lambda i: (tile_off + i, 0)),
            pl.BlockSpec((COMBINE_TILE, LANES), lambda i: (tile_off + i, 0)),
            pl.BlockSpec((1, D_MODEL), lambda i: (0, 0)),
            pl.BlockSpec(memory_space=pl.ANY),
        ],
        out_specs=pl.BlockSpec((COMBINE_TILE, D_MODEL), lambda i: (i, 0)),
        out_shape=jax.ShapeDtypeStruct((rows, D_MODEL), F32),
        scratch_shapes=[
            pltpu.SMEM((IDX_SLOTS * COMBINE_TILE * TOP_K,), jnp.int32),
            pltpu.VMEM((2, TOP_K, COMBINE_TILE // SUBLANES, SUBLANES, D_MODEL), F32),
            pltpu.SemaphoreType.DMA((IDX_SLOTS,)),
            pltpu.SemaphoreType.DMA((2,)),
        ],
        compiler_params=_params("arbitrary"),
        name="combine",
    )(dest.reshape(-1), x1, gate, g_final.reshape(1, D_MODEL), ys)


def _lane_row(v):
    return jnp.pad(v.astype(F32), (0, LANES - v.shape[0])).reshape(1, LANES)


def kernel(x_prompt, x_sample, cache_swa_k, cache_swa_v, state_mlstm_C, state_mlstm_n, state_mlstm_m,
           g_mix, w_in, b_if, attn_sinks, mlstm_norm_w, w_attn_proj, w_mlstm_proj, w_out,
           g_ffn, w_router, b_router, w_e1, b_e1, w_e2, b_e2, g_final):
    bp, sp = x_prompt.shape[0], x_prompt.shape[1]
    bs, ls = x_sample.shape[0], x_sample.shape[1]
    tp = bp * sp
    ts = bs * SAMPLE_ROWS
    t_all = tp + ts
    l = 0
    w_in_p = _permute_w_in(w_in[l])
    wa, wm, wo = w_attn_proj[l].astype(BF16), w_mlstm_proj[l].astype(BF16), w_out[l].astype(BF16)
    wr_f32 = jnp.pad(w_router[l], ((0, 0), (0, ROUTER_PAD - N_EXPERTS)))
    wr_hi = wr_f32.astype(BF16)
    wr_pad = jnp.concatenate([wr_hi, (wr_f32 - wr_hi.astype(F32)).astype(BF16)], axis=1)
    br_pad = _lane_row(b_router[l])
    b1, b2 = b_e1[l].reshape(N_EXPERTS, 1, 2 * D_FF), b_e2[l].reshape(N_EXPERTS, 1, D_MODEL)
    sinks_row = _lane_row(attn_sinks[l])
    bias_row = _lane_row(b_if[l])
    norm_w_row = mlstm_norm_w[l].reshape(1, M_V_WIDTH)

    xp = x_prompt.reshape(tp, D_MODEL)
    xs = jnp.pad(x_sample, ((0, 0), (0, SAMPLE_ROWS - ls), (0, 0))).reshape(ts, D_MODEL)
    z_all = _inproj(xp, xs, g_mix[l], w_in_p, 512)

    a_p, k1, v1, w1 = _attn_prompt(z_all, sinks_row, bp, sp, w_e1[l])
    a_s, k2, v2 = _attn_sample(z_all, sinks_row, cache_swa_k[l].reshape(bs, WINDOW, KV_WIDTH),
                               cache_swa_v[l].reshape(bs, WINDOW, KV_WIDTH), tp, ls)
    mh_p, c1, n1, m1, w2 = _mlstm_prompt(z_all, bias_row, norm_w_row, bp, sp, w_e2[l])
    mh_s, c2, n2, m2 = _mlstm_sample(z_all, bias_row, norm_w_row, state_mlstm_C[l], state_mlstm_n[l],
                                     state_mlstm_m[l], tp, ls)

    x1, h2, logits = _merge(xp, xs, a_p, a_s, mh_p, mh_s, z_all, wa, wm, wo, g_ffn[l], wr_pad, br_pad, 256)

    meta, gate, counts = _route(logits)
    n_blocks = -(-t_all * TOP_K // MOE_BLOCK_M) + N_EXPERTS
    pstarts, pends, padded, block_e, n_used = _slot_tables(counts[0, :N_EXPERTS].astype(jnp.int32), n_blocks)
    dest = pstarts[meta[:, LANE_E:LANE_E + TOP_K]] + meta[:, LANE_RANK:LANE_RANK + TOP_K]
    xs_sorted = _scatter_rows(h2, dest, pends, padded, n_blocks * MOE_BLOCK_M)
    ys = _experts(block_e, n_used, xs_sorted, w1, b1, w2, b2)
    y_prompt = _combine(x1, gate, dest, ys, g_final, 0, tp).reshape(bp, sp, D_MODEL)
    y_sample = _combine(x1, gate, dest, ys, g_final, tp, ts).reshape(bs, SAMPLE_ROWS, D_MODEL)[:, :ls]
    k1 = k1.reshape(bp, WINDOW, N_KV_HEADS, HEAD_DIM)
    v1 = v1.reshape(bp, WINDOW, N_KV_HEADS, HEAD_DIM)
    k2 = k2.reshape(bs, WINDOW, N_KV_HEADS, HEAD_DIM)
    v2 = v2.reshape(bs, WINDOW, N_KV_HEADS, HEAD_DIM)
    m1 = m1[:, :, 0]
    st = lambda t: t[None]
    return (y_prompt, y_sample, st(k1), st(v1), st(c1), st(n1), st(m1), st(k2), st(v2), st(c2), st(n2), st(m2))
```

```python
import functools

import jax
import jax.numpy as jnp
from jax import lax
from jax.experimental import pallas as pl
from jax.experimental.pallas import tpu as pltpu

D_MODEL = 2048
N_HEADS = 16
N_KV_HEADS = 2
HEAD_DIM = 64
Q_PER_KV = N_HEADS // N_KV_HEADS
WINDOW = 128
ATTN_WIDTH = N_HEADS * HEAD_DIM
KV_WIDTH = N_KV_HEADS * HEAD_DIM
M_HEADS = 4
M_QK_DIM = 128
M_V_DIM = 256
M_QK_WIDTH = M_HEADS * M_QK_DIM
M_V_WIDTH = M_HEADS * M_V_DIM
M_CHUNK = 64
N_EXPERTS = 32
TOP_K = 4
D_FF = D_MODEL
SWIGLU_LIMIT = 7.0
SWIGLU_ALPHA = 1.702
RMS_EPS = 1e-5
NEG = -1e30

F32 = jnp.float32
BF16 = jnp.bfloat16

LANES = 128
SUBLANES = 8
VMEM_LIMIT_BYTES = 56 * 1024 * 1024

COL_GZ = 0
COL_AQ = COL_GZ + 2 * D_MODEL
COL_MV = COL_AQ + ATTN_WIDTH
COL_MO = COL_MV + M_V_WIDTH
COL_MQ = COL_MO + M_V_WIDTH
COL_MK = COL_MQ + M_QK_WIDTH
COL_AK = COL_MK + M_QK_WIDTH
COL_IF = COL_AK + 2 * KV_WIDTH
IN_TILE_N = 2176
IN_WIDTH_PAD = -(-(COL_IF + LANES) // IN_TILE_N) * IN_TILE_N

SAMPLE_ROWS = SUBLANES
MOE_BLOCK_M = 512
MOE_FF_CHUNK = 1024
ROUTER_PAD = LANES


def _permute_w_in(w_in):
    splits = (ATTN_WIDTH, KV_WIDTH, KV_WIDTH, M_QK_WIDTH, M_QK_WIDTH, M_V_WIDTH, M_V_WIDTH, M_HEADS, M_HEADS,
              2 * D_MODEL)
    parts, off = [], 0
    for width in splits:
        parts.append(w_in[:, off:off + width])
        off += width
    aq, ak, av, mq, mk, mv, mo, mi, mf, gz = parts
    cols = jnp.concatenate([gz, aq, mv, mo, mq, mk, ak, av, mi, mf], axis=1)
    cols = jnp.pad(cols, ((0, 0), (0, IN_WIDTH_PAD - cols.shape[1])))
    return cols.astype(BF16)


def _params(*semantics):
    return pltpu.CompilerParams(dimension_semantics=semantics, vmem_limit_bytes=VMEM_LIMIT_BYTES)


def _log_sigmoid(x):
    return jnp.minimum(x, 0.0) - jnp.log1p(jnp.exp(-jnp.abs(x)))


def _inproj_body(tiles_a, xa_ref, xb_ref, g_ref, w_ref, z_ref, h_scr):
    def normalise(x_ref):
        x = x_ref[...]
        y = x * lax.rsqrt(jnp.mean(x * x, axis=-1, keepdims=True) + RMS_EPS)
        h_scr[...] = (y * g_ref[...]).astype(BF16)

    first = pl.program_id(1) == 0
    in_a = pl.program_id(0) < tiles_a

    @pl.when(first & in_a)
    def _():
        normalise(xa_ref)

    @pl.when(first & jnp.logical_not(in_a))
    def _():
        normalise(xb_ref)

    z_ref[...] = jnp.dot(h_scr[...], w_ref[...], preferred_element_type=F32)


def _inproj(x_a, x_b, g, w_bf16, tile_m):
    tiles_a, tiles_b = x_a.shape[0] // tile_m, x_b.shape[0] // tile_m
    n = w_bf16.shape[1]
    return pl.pallas_call(
        functools.partial(_inproj_body, tiles_a),
        grid=(tiles_a + tiles_b, n // IN_TILE_N),
        in_specs=[
            pl.BlockSpec((tile_m, D_MODEL), lambda i, j: (jnp.minimum(i, tiles_a - 1), 0)),
            pl.BlockSpec((tile_m, D_MODEL), lambda i, j: (jnp.maximum(i - tiles_a, 0), 0)),
            pl.BlockSpec((1, D_MODEL), lambda i, j: (0, 0)),
            pl.BlockSpec((D_MODEL, IN_TILE_N), lambda i, j: (0, j)),
        ],
        out_specs=pl.BlockSpec((tile_m, IN_TILE_N), lambda i, j: (i, j)),
        out_shape=jax.ShapeDtypeStruct(((tiles_a + tiles_b) * tile_m, n), F32),
        scratch_shapes=[pltpu.VMEM((tile_m, D_MODEL), BF16)],
        compiler_params=_params("arbitrary", "arbitrary"),
        name="inproj",
    )(x_a, x_b, g.reshape(1, D_MODEL), w_bf16)


assert KV_WIDTH == LANES and N_KV_HEADS == 2 and LANES == 2 * HEAD_DIM
Q_TILES = ATTN_WIDTH // LANES
TILES_PER_KV = Q_TILES // N_KV_HEADS
_NT = (((1,), (1,)), ((), ()))


def _kv_placements(x):
    lo = lax.broadcasted_iota(jnp.int32, x.shape, 1) < HEAD_DIM
    swapped = pltpu.roll(x, HEAD_DIM, axis=1)
    return ((jnp.where(lo, x, 0.0).astype(BF16), jnp.where(lo, 0.0, swapped).astype(BF16)),
            (jnp.where(lo, swapped, 0.0).astype(BF16), jnp.where(lo, 0.0, x).astype(BF16)))


def _slab_head(g, p, j):
    return g * Q_PER_KV + 2 * j + p


def _attn_prompt_body(q_ref, kvc_ref, kvp_ref, sink_ref, w_ref, a_ref, k_ref, v_ref, wb_ref):
    wb_ref[...] = w_ref[...].astype(BF16)
    n = pl.program_id(1)
    kc, vc = kvc_ref[:, :KV_WIDTH], kvc_ref[:, KV_WIDTH:]
    k_ref[...] = kc
    v_ref[...] = vc
    kk = _kv_placements(jnp.concatenate([kvp_ref[:, :KV_WIDTH], kc], axis=0))
    vv = _kv_placements(jnp.concatenate([kvp_ref[:, KV_WIDTH:], vc], axis=0))
    q = q_ref[...].astype(BF16)
    row = lax.broadcasted_iota(jnp.int32, (WINDOW, WINDOW), 0)
    col = lax.broadcasted_iota(jnp.int32, (WINDOW, WINDOW), 1)
    mask = jnp.concatenate([(col > row) & (n > 0), col <= row], axis=1)
    slabs, sinks = [], []
    for g in range(N_KV_HEADS):
        qt = jnp.concatenate([q[:, (TILES_PER_KV * g + j) * LANES:(TILES_PER_KV * g + j + 1) * LANES]
                              for j in range(TILES_PER_KV)], axis=0)
        for p in range(2):
            s = lax.dot_general(qt, kk[g][p], _NT, preferred_element_type=F32) * (HEAD_DIM ** -0.5)
            for j in range(TILES_PER_KV):
                h = _slab_head(g, p, j)
                slabs.append(jnp.where(mask, s[j * WINDOW:(j + 1) * WINDOW], NEG))
                sinks.append(jnp.broadcast_to(sink_ref[:, h:h + 1], (WINDOW, LANES)))
    s_all = jnp.concatenate(slabs, axis=0)
    sink = jnp.concatenate(sinks, axis=0)
    m = jnp.maximum(jnp.broadcast_to(jnp.max(s_all, axis=1, keepdims=True), sink.shape), sink)
    p_all = jnp.exp(s_all - jnp.concatenate([m, m], axis=1))
    p_hi = p_all.astype(BF16)
    p_lo = (p_all - p_hi.astype(F32)).astype(BF16)
    ones = jnp.ones((2 * WINDOW, LANES), BF16)
    denom = (jnp.dot(p_hi, ones, preferred_element_type=F32) + jnp.dot(p_lo, ones, preferred_element_type=F32)
             + jnp.exp(sink - m))
    inv = 1.0 / denom
    pn = (p_all * jnp.concatenate([inv, inv], axis=1)).astype(BF16)
    half = TILES_PER_KV * WINDOW
    for g in range(N_KV_HEADS):
        base = 2 * half * g
        o = (jnp.dot(pn[base:base + half], vv[g][0], preferred_element_type=F32)
             + jnp.dot(pn[base + half:base + 2 * half], vv[g][1], preferred_element_type=F32))
        for j in range(TILES_PER_KV):
            t = TILES_PER_KV * g + j
            a_ref[:, t * LANES:(t + 1) * LANES] = o[j * WINDOW:(j + 1) * WINDOW].astype(BF16)


def _cast_slices(w, steps):
    rows = w.reshape(-1, w.shape[-1])
    per_step = rows.shape[0] // steps
    assert per_step * steps == rows.shape[0] and per_step % 16 == 0
    return rows, per_step


def _attn_prompt(z_all, sinks_row, batch, seq, w):
    nb = seq // WINDOW
    w_rows, r1 = _cast_slices(w, batch * nb)
    step = lambda b, n: (b * nb + n, 0)
    a, k, v, wb = pl.pallas_call(
        _attn_prompt_body,
        grid=(batch, nb),
        in_specs=[
            pl.BlockSpec((WINDOW, ATTN_WIDTH), lambda b, n: (b * nb + n, COL_AQ // ATTN_WIDTH)),
            pl.BlockSpec((WINDOW, 2 * KV_WIDTH), lambda b, n: (b * nb + n, COL_AK // (2 * KV_WIDTH))),
            pl.BlockSpec((WINDOW, 2 * KV_WIDTH),
                         lambda b, n: (b * nb + jnp.maximum(n - 1, 0), COL_AK // (2 * KV_WIDTH))),
            pl.BlockSpec((1, LANES), lambda b, n: (0, 0)),
            pl.BlockSpec((r1, w_rows.shape[1]), step),
        ],
        out_specs=[
            pl.BlockSpec((WINDOW, ATTN_WIDTH), lambda b, n: (b * nb + n, 0)),
            pl.BlockSpec((None, WINDOW, KV_WIDTH), lambda b, n: (b, 0, 0)),
            pl.BlockSpec((None, WINDOW, KV_WIDTH), lambda b, n: (b, 0, 0)),
            pl.BlockSpec((r1, w_rows.shape[1]), step),
        ],
        out_shape=[
            jax.ShapeDtypeStruct((batch * seq, ATTN_WIDTH), BF16),
            jax.ShapeDtypeStruct((batch, WINDOW, KV_WIDTH), F32),
            jax.ShapeDtypeStruct((batch, WINDOW, KV_WIDTH), F32),
            jax.ShapeDtypeStruct(w_rows.shape, BF16),
        ],
        compiler_params=_params("arbitrary", "arbitrary"),
        name="attn_prompt",
    )(z_all, z_all, z_all, sinks_row, w_rows)
    return a, k, v, wb.reshape(w.shape)


ATTN_SAMPLE_GROUP = 8


def _attn_sample_body(n_valid, q_ref, kv_ref, ck_ref, cv_ref, sink_ref, a_ref, nk_ref, nv_ref, o_scr):
    rs = SAMPLE_ROWS
    slab = TILES_PER_KV * rs
    r8 = lax.broadcasted_iota(jnp.int32, (rs, KV_WIDTH), 0)
    qi_c = jnp.bitwise_and(lax.broadcasted_iota(jnp.int32, (slab, WINDOW), 0), rs - 1)
    mask_c = lax.broadcasted_iota(jnp.int32, (slab, WINDOW), 1) > qi_c
    qi_n = jnp.bitwise_and(lax.broadcasted_iota(jnp.int32, (slab, rs), 0), rs - 1)
    mask_n = lax.broadcasted_iota(jnp.int32, (slab, rs), 1) <= qi_n
    sc_list, sn_list, sink_list, values = [], [], [], []
    for b in range(ATTN_SAMPLE_GROUP):
        rows = slice(b * rs, (b + 1) * rs)
        q = q_ref[rows, :]
        kn, vn = kv_ref[rows, :KV_WIDTH], kv_ref[rows, KV_WIDTH:]
        ck, cv = ck_ref[b], cv_ref[b]
        for src, new, dst in ((ck, kn, nk_ref), (cv, vn, nv_ref)):
            rolled = pltpu.roll(src, WINDOW - n_valid, axis=0)
            dst[b] = rolled
            tail = jnp.where(r8 >= rs - n_valid, pltpu.roll(new, rs - n_valid, axis=0), rolled[WINDOW - rs:, :])
            dst[b, WINDOW - rs:, :] = tail
        kk_c, kk_n = _kv_placements(ck), _kv_placements(kn)
        values.append((_kv_placements(cv), _kv_placements(vn)))
        for g in range(N_KV_HEADS):
            qt = jnp.concatenate([q[:, (TILES_PER_KV * g + j) * LANES:(TILES_PER_KV * g + j + 1) * LANES]
                                  for j in range(TILES_PER_KV)], axis=0).astype(BF16)
            for p in range(2):
                s_c = lax.dot_general(qt, kk_c[g][p], _NT, preferred_element_type=F32) * (HEAD_DIM ** -0.5)
                s_n = lax.dot_general(qt, kk_n[g][p], _NT, preferred_element_type=F32) * (HEAD_DIM ** -0.5)
                sc_list.append(jnp.where(mask_c, s_c, NEG))
                sn_list.append(jnp.where(mask_n, s_n, NEG))
                for j in range(TILES_PER_KV):
                    h = _slab_head(g, p, j)
                    sink_list.append(jnp.broadcast_to(sink_ref[:, h:h + 1], (rs, 1)))
    s_c = jnp.concatenate(sc_list, axis=0)
    s_n = jnp.concatenate(sn_list, axis=0)
    sink = jnp.concatenate(sink_list, axis=0)
    m = jnp.maximum(jnp.maximum(jnp.max(s_c, axis=1, keepdims=True), jnp.max(s_n, axis=1, keepdims=True)), sink)
    p_c = jnp.exp(s_c - m)
    p_n = jnp.exp(s_n - m)
    inv = 1.0 / (jnp.sum(p_c, axis=1, keepdims=True) + jnp.sum(p_n, axis=1, keepdims=True) + jnp.exp(sink - m))
    p_c = (p_c * inv).astype(BF16)
    p_n = (p_n * inv).astype(BF16)
    for b in range(ATTN_SAMPLE_GROUP):
        vv_c, vv_n = values[b]
        for g in range(N_KV_HEADS):
            base = (b * N_KV_HEADS + g) * 2 * slab
            o = (jnp.dot(p_c[base:base + slab], vv_c[g][0], preferred_element_type=F32)
                 + jnp.dot(p_c[base + slab:base + 2 * slab], vv_c[g][1], preferred_element_type=F32)
                 + jnp.dot(p_n[base:base + slab], vv_n[g][0], preferred_element_type=F32)
                 + jnp.dot(p_n[base + slab:base + 2 * slab], vv_n[g][1], preferred_element_type=F32))
            for j in range(TILES_PER_KV):
                t = TILES_PER_KV * g + j
                o_scr[b * rs:(b + 1) * rs, t * LANES:(t + 1) * LANES] = o[j * rs:(j + 1) * rs]
    a_ref[...] = o_scr[...].astype(BF16)


def _attn_sample(z_all, sinks_row, cache_k, cache_v, row_off, n_valid):
    b = cache_k.shape[0]
    gr = ATTN_SAMPLE_GROUP
    rows = gr * SAMPLE_ROWS
    rb = row_off // rows
    return pl.pallas_call(
        functools.partial(_attn_sample_body, n_valid),
        grid=(b // gr,),
        in_specs=[
            pl.BlockSpec((rows, ATTN_WIDTH), lambda i: (rb + i, COL_AQ // ATTN_WIDTH)),
            pl.BlockSpec((rows, 2 * KV_WIDTH), lambda i: (rb + i, COL_AK // (2 * KV_WIDTH))),
            pl.BlockSpec((gr, WINDOW, KV_WIDTH), lambda i: (i, 0, 0)),
            pl.BlockSpec((gr, WINDOW, KV_WIDTH), lambda i: (i, 0, 0)),
            pl.BlockSpec((1, LANES), lambda i: (0, 0)),
        ],
        out_specs=[
            pl.BlockSpec((rows, ATTN_WIDTH), lambda i: (i, 0)),
            pl.BlockSpec((gr, WINDOW, KV_WIDTH), lambda i: (i, 0, 0)),
            pl.BlockSpec((gr, WINDOW, KV_WIDTH), lambda i: (i, 0, 0)),
        ],
        out_shape=[
            jax.ShapeDtypeStruct((b * SAMPLE_ROWS, ATTN_WIDTH), BF16),
            jax.ShapeDtypeStruct(cache_k.shape, F32),
            jax.ShapeDtypeStruct(cache_v.shape, F32),
        ],
        scratch_shapes=[pltpu.VMEM((rows, ATTN_WIDTH), F32)],
        compiler_params=_params("arbitrary"),
        name="attn_sample",
    )(z_all, z_all, cache_k, cache_v, sinks_row)


def _mlstm_gates(if_blk, bias_row, n_valid):
    length = if_blk.shape[0]
    g = if_blk + bias_row
    lane = lax.broadcasted_iota(jnp.int32, g.shape, 1)
    g = jnp.where(lane < M_HEADS, g, _log_sigmoid(g))
    if n_valid < length:
        row = lax.broadcasted_iota(jnp.int32, g.shape, 0)
        g = jnp.where(row < n_valid, g, jnp.where(lane < M_HEADS, NEG, 0.0))
    return g


def _lanes(col, width):
    if width <= LANES:
        return col[:, :width]
    return jnp.concatenate([col] * (width // LANES), axis=1)


def _row_stat(x, reduce):
    return jnp.broadcast_to(reduce(x, axis=1, keepdims=True), (x.shape[0], LANES))


def _mlstm_chunk(q, k, v, gates, gates_t, h, c, n, m):
    length = q.shape[0]
    i_rep = jnp.broadcast_to(gates[:, h:h + 1], (length, LANES))
    f_col = gates[:, M_HEADS + h:M_HEADS + h + 1]
    i_row = gates_t[h:h + 1, :]
    f_row = gates_t[M_HEADS + h:M_HEADS + h + 1, :]
    t_idx = lax.broadcasted_iota(jnp.int32, (length, length), 0)
    s_idx = lax.broadcasted_iota(jnp.int32, (length, length), 1)
    causal = s_idx <= t_idx
    bcum = _row_stat(jnp.where(causal, f_row, 0.0), jnp.sum)
    bcum_row = jnp.sum(jnp.where(t_idx <= s_idx, f_col, 0.0), axis=0, keepdims=True)
    dlog = jnp.where(causal, _lanes(bcum, length) - bcum_row + i_row, NEG)
    inter = bcum + m
    mt = jnp.maximum(inter, _row_stat(dlog, jnp.max))
    w_inter = jnp.exp(inter - mt)
    qb = q.astype(BF16)
    ks = k * (M_QK_DIM ** -0.5)
    vb = v.astype(BF16)
    qk = lax.dot_general(qb, ks.astype(BF16), (((1,), (1,)), ((), ())), preferred_element_type=F32)
    qk = qk * jnp.exp(dlog - _lanes(mt, length))
    num = (_lanes(w_inter, M_V_DIM) * jnp.dot(qb, c.astype(BF16), preferred_element_type=F32)
           + jnp.dot(qk.astype(BF16), vb, preferred_element_type=F32))
    den = w_inter * _row_stat(q * n, jnp.sum) + _row_stat(qk, jnp.sum)
    h_t = num * _lanes(1.0 / jnp.maximum(jnp.abs(den), jnp.exp(-mt)), M_V_DIM)
    m_new = mt[length - 1:length, 0:1]
    w_state = jnp.exp(bcum[length - 1:length, :] - bcum + i_rep - m_new)
    decay = jnp.exp(inter[length - 1:length, 0:1] - m_new)
    kw = ks * _lanes(w_state, M_QK_DIM)
    c_new = decay * c + jnp.dot(kw.T.astype(BF16), vb, preferred_element_type=F32)
    n_new = decay * n + jnp.sum(kw, axis=0, keepdims=True)
    return h_t, c_new, n_new, m_new


def _mlstm_head_out(h_t, norm_w, o_gate):
    scale = lax.rsqrt(_row_stat(h_t * h_t, jnp.sum) * (1.0 / h_t.shape[1]) + RMS_EPS)
    return h_t * _lanes(scale, h_t.shape[1]) * norm_w * jax.nn.sigmoid(o_gate)


MLSTM_SEG = 256


MLSTM_BATCH = 2
MLSTM_PROMPT_CHUNK = 256


def _mlstm_prompt_body(*refs):
    nb = MLSTM_BATCH
    ins, rest = refs[:5 * nb], refs[5 * nb:]
    bias_ref, nw_ref, w_ref, mh_ref, c_out, n_out, m_out, wb_ref, c_scr, n_scr, m_scr = rest
    seg = pl.program_id(1)
    wb_ref[...] = w_ref[...].astype(BF16)

    @pl.when(seg == 0)
    def _():
        c_scr[...] = jnp.zeros_like(c_scr)
        n_scr[...] = jnp.zeros_like(n_scr)
        m_scr[...] = jnp.full_like(m_scr, NEG)

    def chunk_body(ci, carry):
        r = pl.multiple_of(ci * MLSTM_PROMPT_CHUNK, MLSTM_PROMPT_CHUNK)
        rows = pl.ds(r, MLSTM_PROMPT_CHUNK)
        state = [[(c_scr[u, h], n_scr[u, h:h + 1, :], m_scr[u, h:h + 1, 0:1]) for h in range(M_HEADS)]
                 for u in range(nb)]
        results = []
        for u in range(nb):
            q_ref, k_ref, v_ref, o_ref, if_ref = ins[5 * u:5 * u + 5]
            gates = _mlstm_gates(if_ref[rows, :], bias_ref[...], MLSTM_PROMPT_CHUNK)
            gates_t = gates.T
            for h in range(M_HEADS):
                qs = slice(h * M_QK_DIM, (h + 1) * M_QK_DIM)
                vs = slice(h * M_V_DIM, (h + 1) * M_V_DIM)
                h_t, c_new, n_new, m_new = _mlstm_chunk(
                    q_ref[rows, qs], k_ref[rows, qs], v_ref[rows, vs], gates, gates_t, h, *state[u][h])
                out = _mlstm_head_out(h_t, nw_ref[:, vs], o_ref[rows, vs]).astype(BF16)
                results.append((u, h, vs, out, c_new, n_new, m_new))
        for u, h, vs, out, c_new, n_new, m_new in results:
            c_scr[u, h] = c_new
            n_scr[u, h:h + 1, :] = n_new
            m_scr[u, h:h + 1, :] = jnp.broadcast_to(m_new, (1, LANES))
            mh_ref[u, rows, vs] = out
        return carry

    lax.fori_loop(0, MLSTM_SEG // MLSTM_PROMPT_CHUNK, chunk_body, 0)

    @pl.when(seg == pl.num_programs(1) - 1)
    def _():
        c_out[...] = c_scr[...]
        n_out[...] = n_scr[:, 0:M_HEADS, :]
        m_out[...] = m_scr[:, 0:M_HEADS, :]


def _mlstm_prompt(z_all, bias_row, norm_w_row, batch, seq, w):
    ns = seq // MLSTM_SEG
    nb = MLSTM_BATCH
    w_rows, wr = _cast_slices(w, (batch // nb) * ns)
    in_specs, args = [], []
    for u in range(nb):
        row = lambda p, s, u=u: (p * nb + u) * ns + s
        in_specs += [
            pl.BlockSpec((MLSTM_SEG, M_QK_WIDTH), lambda p, s, row=row: (row(p, s), COL_MQ // M_QK_WIDTH)),
            pl.BlockSpec((MLSTM_SEG, M_QK_WIDTH), lambda p, s, row=row: (row(p, s), COL_MK // M_QK_WIDTH)),
            pl.BlockSpec((MLSTM_SEG, M_V_WIDTH), lambda p, s, row=row: (row(p, s), COL_MV // M_V_WIDTH)),
            pl.BlockSpec((MLSTM_SEG, M_V_WIDTH), lambda p, s, row=row: (row(p, s), COL_MO // M_V_WIDTH)),
            pl.BlockSpec((MLSTM_SEG, LANES), lambda p, s, row=row: (row(p, s), COL_IF // LANES)),
        ]
        args += [z_all] * 5
    in_specs += [pl.BlockSpec((1, LANES), lambda p, s: (0, 0)), pl.BlockSpec((1, M_V_WIDTH), lambda p, s: (0, 0)),
                 pl.BlockSpec((wr, w_rows.shape[1]), lambda p, s: (p * ns + s, 0))]
    mh, c, n, m, wb = pl.pallas_call(
        _mlstm_prompt_body,
        grid=(batch // nb, ns),
        in_specs=in_specs,
        out_specs=[
            pl.BlockSpec((None, nb, MLSTM_SEG, M_V_WIDTH), lambda p, s: (p, 0, s, 0)),
            pl.BlockSpec((nb, M_HEADS, M_QK_DIM, M_V_DIM), lambda p, s: (p, 0, 0, 0)),
            pl.BlockSpec((nb, M_HEADS, M_QK_DIM), lambda p, s: (p, 0, 0)),
            pl.BlockSpec((nb, M_HEADS, LANES), lambda p, s: (p, 0, 0)),
            pl.BlockSpec((wr, w_rows.shape[1]), lambda p, s: (p * ns + s, 0)),
        ],
        out_shape=[
            jax.ShapeDtypeStruct((batch // nb, nb, seq, M_V_WIDTH), BF16),
            jax.ShapeDtypeStruct((batch, M_HEADS, M_QK_DIM, M_V_DIM), F32),
            jax.ShapeDtypeStruct((batch, M_HEADS, M_QK_DIM), F32),
            jax.ShapeDtypeStruct((batch, M_HEADS, LANES), F32),
            jax.ShapeDtypeStruct(w_rows.shape, BF16),
        ],
        scratch_shapes=[
            pltpu.VMEM((nb, M_HEADS, M_QK_DIM, M_V_DIM), F32),
            pltpu.VMEM((nb, SUBLANES, M_QK_DIM), F32),
            pltpu.VMEM((nb, SUBLANES, LANES), F32),
        ],
        compiler_params=_params("arbitrary", "arbitrary"),
        name="mlstm_prompt",
    )(*args, bias_row, norm_w_row, w_rows)
    return mh.reshape(batch * seq, M_V_WIDTH), c, n, m, wb.reshape(w.shape)


MLSTM_SAMPLE_GROUP = 8


def _mlstm_sample_body(n_valid, q_ref, k_ref, v_ref, o_ref, if_ref, bias_ref, nw_ref, c0_ref, n0_ref, m0_ref,
                       mh_ref, c_out, n_out, m_out, h_scr):
    for g in range(MLSTM_SAMPLE_GROUP):
        rows = slice(g * SAMPLE_ROWS, (g + 1) * SAMPLE_ROWS)
        gates = _mlstm_gates(if_ref[rows, :], bias_ref[...], n_valid)
        gates_t = jnp.transpose(jnp.concatenate([gates] * (LANES // SAMPLE_ROWS), axis=0))[:, :SAMPLE_ROWS]
        for h in range(M_HEADS):
            qs = slice(h * M_QK_DIM, (h + 1) * M_QK_DIM)
            vs = slice(h * M_V_DIM, (h + 1) * M_V_DIM)
            h_t, c_new, n_new, m_new = _mlstm_chunk(
                q_ref[rows, qs], k_ref[rows, qs], v_ref[rows, vs], gates, gates_t, h,
                c0_ref[g, h], n0_ref[g, h:h + 1, :], m0_ref[g:g + 1, h:h + 1])
            c_out[g, h] = c_new
            n_out[g, h:h + 1, :] = n_new
            m_out[g:g + 1, h:h + 1] = m_new
            h_scr[rows, vs] = _mlstm_head_out(h_t, nw_ref[:, vs], o_ref[rows, vs])
    mh_ref[...] = h_scr[...].astype(BF16)


def _mlstm_sample(z_all, bias_row, norm_w_row, c0, n0, m0, row_off, n_valid):
    b = c0.shape[0]
    gr = MLSTM_SAMPLE_GROUP
    rows = gr * SAMPLE_ROWS
    rb = row_off // rows
    m0g = m0.reshape(b // gr, gr, M_HEADS)
    outs = pl.pallas_call(
        functools.partial(_mlstm_sample_body, n_valid),
        grid=(b // gr,),
        in_specs=[
            pl.BlockSpec((rows, M_QK_WIDTH), lambda i: (rb + i, COL_MQ // M_QK_WIDTH)),
            pl.BlockSpec((rows, M_QK_WIDTH), lambda i: (rb + i, COL_MK // M_QK_WIDTH)),
            pl.BlockSpec((rows, M_V_WIDTH), lambda i: (rb + i, COL_MV // M_V_WIDTH)),
            pl.BlockSpec((rows, M_V_WIDTH), lambda i: (rb + i, COL_MO // M_V_WIDTH)),
            pl.BlockSpec((rows, LANES), lambda i: (rb + i, COL_IF // LANES)),
            pl.BlockSpec((1, LANES), lambda i: (0, 0)),
            pl.BlockSpec((1, M_V_WIDTH), lambda i: (0, 0)),
            pl.BlockSpec((gr, M_HEADS, M_QK_DIM, M_V_DIM), lambda i: (i, 0, 0, 0)),
            pl.BlockSpec((gr, M_HEADS, M_QK_DIM), lambda i: (i, 0, 0)),
            pl.BlockSpec((None, gr, M_HEADS), lambda i: (i, 0, 0)),
        ],
        out_specs=[
            pl.BlockSpec((rows, M_V_WIDTH), lambda i: (i, 0)),
            pl.BlockSpec((gr, M_HEADS, M_QK_DIM, M_V_DIM), lambda i: (i, 0, 0, 0)),
            pl.BlockSpec((gr, M_HEADS, M_QK_DIM), lambda i: (i, 0, 0)),
            pl.BlockSpec((None, gr, M_HEADS), lambda i: (i, 0, 0)),
        ],
        out_shape=[
            jax.ShapeDtypeStruct((b * SAMPLE_ROWS, M_V_WIDTH), BF16),
            jax.ShapeDtypeStruct(c0.shape, F32),
            jax.ShapeDtypeStruct(n0.shape, F32),
            jax.ShapeDtypeStruct(m0g.shape, F32),
        ],
        scratch_shapes=[pltpu.VMEM((rows, M_V_WIDTH), F32)],
        compiler_params=_params("arbitrary"),
        name="mlstm_sample",
    )(z_all, z_all, z_all, z_all, z_all, bias_row, norm_w_row, c0, n0, m0g)
    mh, c, n, m = outs
    return mh, c, n, m.reshape(b, M_HEADS)


def _merge_body(tiles_a, xa_ref, xb_ref, aa_ref, ab_ref, ma_ref, mb_ref, gza_ref, gzm_ref, wa_ref, wm_ref, wo_ref,
                gf_ref, wr_ref, br_ref, x1_ref, h2_ref, lg_ref):
    in_a = pl.program_id(0) < tiles_a
    x = jnp.where(in_a, xa_ref[...], xb_ref[...])
    a = jnp.where(in_a, aa_ref[...], ab_ref[...])
    mh = jnp.where(in_a, ma_ref[...], mb_ref[...])
    pa = jnp.dot(a, wa_ref[...], preferred_element_type=F32)
    pm = jnp.dot(mh, wm_ref[...], preferred_element_type=F32)
    merged = jax.nn.sigmoid(gza_ref[...]) * pa + jax.nn.sigmoid(gzm_ref[...]) * pm
    x1 = x + jnp.dot(merged.astype(BF16), wo_ref[...], preferred_element_type=F32)
    x1_ref[...] = x1
    h2 = x1 * lax.rsqrt(jnp.mean(x1 * x1, axis=-1, keepdims=True) + RMS_EPS) * gf_ref[...]
    h2_ref[...] = h2
    hi = h2.astype(BF16)
    lo = (h2 - hi.astype(F32)).astype(BF16)
    both = jnp.dot(hi, wr_ref[...], preferred_element_type=F32)
    lg_ref[...] = (both[:, :ROUTER_PAD] + both[:, ROUTER_PAD:]
                   + jnp.dot(lo, wr_ref[:, :ROUTER_PAD], preferred_element_type=F32) + br_ref[...])


def _merge(x_a, x_b, a_a, a_b, mh_a, mh_b, z_all, wa, wm, wo, g_ffn, wr_pad, br_pad, tile_m):
    tiles_a, tiles_b = x_a.shape[0] // tile_m, x_b.shape[0] // tile_m
    t_all = z_all.shape[0]
    const = lambda i: (0, 0)
    from_a = lambda i: (jnp.minimum(i, tiles_a - 1), 0)
    from_b = lambda i: (jnp.maximum(i - tiles_a, 0), 0)
    return pl.pallas_call(
        functools.partial(_merge_body, tiles_a),
        grid=(tiles_a + tiles_b,),
        in_specs=[
            pl.BlockSpec((tile_m, D_MODEL), from_a),
            pl.BlockSpec((tile_m, D_MODEL), from_b),
            pl.BlockSpec((tile_m, ATTN_WIDTH), from_a),
            pl.BlockSpec((tile_m, ATTN_WIDTH), from_b),
            pl.BlockSpec((tile_m, M_V_WIDTH), from_a),
            pl.BlockSpec((tile_m, M_V_WIDTH), from_b),
            pl.BlockSpec((tile_m, D_MODEL), lambda i: (i, COL_GZ // D_MODEL)),
            pl.BlockSpec((tile_m, D_MODEL), lambda i: (i, COL_GZ // D_MODEL + 1)),
            pl.BlockSpec((ATTN_WIDTH, D_MODEL), const, pipeline_mode=pl.Buffered(1)),
            pl.BlockSpec((M_V_WIDTH, D_MODEL), const, pipeline_mode=pl.Buffered(1)),
            pl.BlockSpec((D_MODEL, D_MODEL), const, pipeline_mode=pl.Buffered(1)),
            pl.BlockSpec((1, D_MODEL), const),
            pl.BlockSpec((D_MODEL, 2 * ROUTER_PAD), const),
            pl.BlockSpec((1, ROUTER_PAD), const),
        ],
        out_specs=[
            pl.BlockSpec((tile_m, D_MODEL), lambda i: (i, 0)),
            pl.BlockSpec((tile_m, D_MODEL), lambda i: (i, 0)),
            pl.BlockSpec((tile_m, ROUTER_PAD), lambda i: (i, 0)),
        ],
        out_shape=[
            jax.ShapeDtypeStruct((t_all, D_MODEL), F32),
            jax.ShapeDtypeStruct((t_all, D_MODEL), F32),
            jax.ShapeDtypeStruct((t_all, ROUTER_PAD), F32),
        ],
        compiler_params=_params("arbitrary"),
        name="merge",
    )(x_a, x_b, a_a, a_b, mh_a, mh_b, z_all, z_all, wa, wm, wo, g_ffn.reshape(1, D_MODEL), wr_pad, br_pad)


def _expert_body(be_ref, nb_ref, xs_ref, w1g_ref, w1u_ref, b1g_ref, b1u_ref, w2_ref, b2_ref, y_ref):
    i = pl.program_id(0)
    c = pl.program_id(1)
    used = i < nb_ref[0]

    @pl.when(used)
    def _():
        x = xs_ref[...].astype(BF16)
        zg = jnp.dot(x, w1g_ref[...], preferred_element_type=F32) + b1g_ref[...]
        zu = jnp.dot(x, w1u_ref[...], preferred_element_type=F32) + b1u_ref[...]
        gate = jnp.minimum(zg, SWIGLU_LIMIT)
        up = jnp.clip(zu, -SWIGLU_LIMIT, SWIGLU_LIMIT)
        act = gate * jax.nn.sigmoid(SWIGLU_ALPHA * gate) * (up + 1.0)
        y = jnp.dot(act.astype(BF16), w2_ref[...], preferred_element_type=F32)

        @pl.when(c == 0)
        def _():
            y_ref[...] = y + b2_ref[...]

        @pl.when(c != 0)
        def _():
            y_ref[...] += y

    @pl.when(jnp.logical_not(used) & (c == 0))
    def _():
        y_ref[...] = jnp.zeros_like(y_ref)


def _experts(block_e, n_used, xs, w1, b1, w2, b2):
    n_slots = xs.shape[0]
    n_blocks = n_slots // MOE_BLOCK_M
    n_chunks = D_FF // MOE_FF_CHUNK

    def chunk(i, c, nb):
        block = jnp.minimum(i, nb[0] - 1)
        step = jnp.where(i < nb[0], c, n_chunks - 1)
        return jnp.where(block % 2 == 0, step, n_chunks - 1 - step)

    grid_spec = pltpu.PrefetchScalarGridSpec(
        num_scalar_prefetch=2,
        grid=(n_blocks, n_chunks),
        in_specs=[
            pl.BlockSpec((MOE_BLOCK_M, D_MODEL), lambda i, c, be, nb: (jnp.minimum(i, nb[0] - 1), 0)),
            pl.BlockSpec((None, D_MODEL, MOE_FF_CHUNK), lambda i, c, be, nb: (be[i], 0, chunk(i, c, nb))),
            pl.BlockSpec((None, D_MODEL, MOE_FF_CHUNK),
                         lambda i, c, be, nb: (be[i], 0, n_chunks + chunk(i, c, nb))),
            pl.BlockSpec((None, 1, MOE_FF_CHUNK), lambda i, c, be, nb: (be[i], 0, chunk(i, c, nb))),
            pl.BlockSpec((None, 1, MOE_FF_CHUNK), lambda i, c, be, nb: (be[i], 0, n_chunks + chunk(i, c, nb))),
            pl.BlockSpec((None, MOE_FF_CHUNK, D_MODEL), lambda i, c, be, nb: (be[i], chunk(i, c, nb), 0)),
            pl.BlockSpec((None, 1, D_MODEL), lambda i, c, be, nb: (be[i], 0, 0)),
        ],
        out_specs=pl.BlockSpec((MOE_BLOCK_M, D_MODEL), lambda i, c, be, nb: (i, 0)),
    )
    return pl.pallas_call(
        _expert_body,
        grid_spec=grid_spec,
        out_shape=jax.ShapeDtypeStruct((n_slots, D_MODEL), F32),
        compiler_params=_params("arbitrary", "arbitrary"),
        name="experts",
    )(block_e, n_used, xs, w1, w1, b1, b1, w2, b2)


ROUTE_TILE = 1024
LANE_E = 0
LANE_RANK = TOP_K


def _route_body(lg_ref, meta_ref, gate_ref, cnt_ref, tri_scr, carry_scr):
    @pl.when(pl.program_id(0) == 0)
    def _():
        r = lax.broadcasted_iota(jnp.int32, (ROUTE_TILE, ROUTE_TILE), 0)
        c = lax.broadcasted_iota(jnp.int32, (ROUTE_TILE, ROUTE_TILE), 1)
        tri_scr[...] = jnp.where(c < r, 1.0, 0.0).astype(BF16)
        carry_scr[...] = jnp.zeros_like(carry_scr)

    lane_i = lax.broadcasted_iota(jnp.int32, (ROUTE_TILE, LANES), 1)
    lane = lane_i.astype(F32)
    x = jnp.where(lane_i < N_EXPERTS, lg_ref[...], -jnp.inf)
    vals, idxs, sels = [], [], []
    for _ in range(TOP_K):
        mx = jnp.max(x, axis=1, keepdims=True)
        idx = jnp.min(jnp.where(x == mx, lane, float(LANES)), axis=1, keepdims=True)
        sel = lane == idx
        x = jnp.where(sel, -jnp.inf, x)
        vals.append(mx)
        idxs.append(idx)
        sels.append(sel)
    weights = [jnp.exp(v - vals[0]) for v in vals]
    inv = 1.0 / (weights[0] + weights[1] + weights[2] + weights[3])
    onehot = sum(jnp.where(s, 1.0, 0.0) for s in sels)
    ranks = jnp.dot(tri_scr[...], onehot.astype(BF16), preferred_element_type=F32) + carry_scr[0:1, :]
    meta = jnp.zeros((ROUTE_TILE, LANES), F32)
    gate = jnp.zeros((ROUTE_TILE, LANES), F32)
    for k in range(TOP_K):
        rank_k = jnp.sum(jnp.where(sels[k], ranks, 0.0), axis=1, keepdims=True)
        meta = jnp.where(lane_i == LANE_E + k, idxs[k], meta)
        meta = jnp.where(lane_i == LANE_RANK + k, rank_k, meta)
        gate = jnp.where(lane_i == k, weights[k] * inv, gate)
    meta_ref[...] = meta.astype(jnp.int32)
    gate_ref[...] = gate
    carry_scr[0:1, :] = carry_scr[0:1, :] + jnp.sum(onehot, axis=0, keepdims=True)
    cnt_ref[...] = carry_scr[...]


def _route(logits):
    t = logits.shape[0]
    return pl.pallas_call(
        _route_body,
        grid=(t // ROUTE_TILE,),
        in_specs=[pl.BlockSpec((ROUTE_TILE, LANES), lambda i: (i, 0))],
        out_specs=[
            pl.BlockSpec((ROUTE_TILE, LANES), lambda i: (i, 0)),
            pl.BlockSpec((ROUTE_TILE, LANES), lambda i: (i, 0)),
            pl.BlockSpec((SUBLANES, LANES), lambda i: (0, 0)),
        ],
        out_shape=[
            jax.ShapeDtypeStruct((t, LANES), jnp.int32),
            jax.ShapeDtypeStruct((t, LANES), F32),
            jax.ShapeDtypeStruct((SUBLANES, LANES), F32),
        ],
        scratch_shapes=[pltpu.VMEM((ROUTE_TILE, ROUTE_TILE), BF16), pltpu.VMEM((SUBLANES, LANES), F32)],
        compiler_params=_params("arbitrary"),
        name="route",
    )(logits)


def _slot_tables(counts, n_blocks):
    padded = (counts + MOE_BLOCK_M - 1) // MOE_BLOCK_M * MOE_BLOCK_M
    pends = jnp.cumsum(padded)
    first_slot = jnp.arange(n_blocks, dtype=pends.dtype) * MOE_BLOCK_M
    block_e = jnp.minimum(jnp.sum(pends[None, :] <= first_slot[:, None], axis=1), N_EXPERTS - 1).astype(jnp.int32)
    n_used = (pends[-1] // MOE_BLOCK_M).astype(jnp.int32).reshape(1)
    return (pends - padded).astype(jnp.int32), pends.astype(jnp.int32), padded.astype(jnp.int32), block_e, n_used


DISPATCH_TILE = 1024
IDX_SLOTS = 3


def _dispatch_body(pend_ref, padded_ref, dest_hbm, h_ref, xs_hbm, idx_smem, zero_scr, idx_sem, row_sem, zero_sem):
    i = pl.program_id(0)
    n = pl.num_programs(0)

    width = DISPATCH_TILE * TOP_K

    def idx_copy(tile):
        s = lax.rem(tile, IDX_SLOTS)
        return pltpu.make_async_copy(dest_hbm.at[pl.ds(pl.multiple_of(tile * width, width), width)],
                                     idx_smem.at[pl.ds(pl.multiple_of(s * width, width), width)], idx_sem.at[s])

    def zero_copy(e):
        start = pl.multiple_of(pend_ref[e] - MOE_BLOCK_M, MOE_BLOCK_M)
        return pltpu.make_async_copy(zero_scr, xs_hbm.at[pl.ds(start, MOE_BLOCK_M), :], zero_sem)

    def row_copy(g, j, d):
        return pltpu.make_async_copy(h_ref.at[g, pl.ds(j, 1), :], xs_hbm.at[pl.ds(d, 1), :], row_sem)

    @pl.when(i == 0)
    def _():
        zero_scr[...] = jnp.zeros_like(zero_scr)
        for e in range(N_EXPERTS):
            @pl.when(padded_ref[e] > 0)
            def _():
                zero_copy(e).start()
        for e in range(N_EXPERTS):
            @pl.when(padded_ref[e] > 0)
            def _():
                zero_copy(e).wait()
        idx_copy(0).start()

    @pl.when(i + 1 < n)
    def _():
        idx_copy(i + 1).start()

    idx_copy(i).wait()
    base = lax.rem(i, IDX_SLOTS) * width

    def issue(g, c):
        for j in range(SUBLANES):
            for k in range(TOP_K):
                d = idx_smem[base + g * (SUBLANES * TOP_K) + (j * TOP_K + k)]
                row_copy(g, j, d).start(priority=k % 2)
        return c

    lax.fori_loop(0, DISPATCH_TILE // SUBLANES, issue, 0)
    for _ in range(TOP_K):
        pltpu.make_async_copy(h_ref, h_ref, row_sem).wait()


def _scatter_rows(h, dest, pends, padded, n_slots):
    t = h.shape[0]
    tiles = t // DISPATCH_TILE
    grid_spec = pltpu.PrefetchScalarGridSpec(
        num_scalar_prefetch=2,
        grid=(tiles,),
        in_specs=[
            pl.BlockSpec(memory_space=pl.ANY),
            pl.BlockSpec((DISPATCH_TILE // SUBLANES, SUBLANES, D_MODEL), lambda i, pe, pd: (i, 0, 0)),
        ],
        out_specs=pl.BlockSpec(memory_space=pl.ANY),
        scratch_shapes=[
            pltpu.SMEM((IDX_SLOTS * DISPATCH_TILE * TOP_K,), jnp.int32),
            pltpu.VMEM((MOE_BLOCK_M, D_MODEL), F32),
            pltpu.SemaphoreType.DMA((IDX_SLOTS,)),
            pltpu.SemaphoreType.DMA(()),
            pltpu.SemaphoreType.DMA(()),
        ],
    )
    return pl.pallas_call(
        _dispatch_body,
        grid_spec=grid_spec,
        out_shape=jax.ShapeDtypeStruct((n_slots, D_MODEL), F32),
        compiler_params=_params("arbitrary"),
        name="dispatch",
    )(pends, padded, dest.reshape(-1), h.reshape(t // SUBLANES, SUBLANES, D_MODEL))


COMBINE_TILE = 256


def _combine_body(tile_off, dest_hbm, x1_ref, gate_ref, gf_ref, ys_hbm, y_ref, idx_smem, buf, idx_sem, row_sem):
    i = pl.program_id(0)
    n = pl.num_programs(0)

    width = COMBINE_TILE * TOP_K

    def idx_copy(tile):
        s = lax.rem(tile, IDX_SLOTS)
        return pltpu.make_async_copy(dest_hbm.at[pl.ds(pl.multiple_of((tile_off + tile) * width, width), width)],
                                     idx_smem.at[pl.ds(pl.multiple_of(s * width, width), width)], idx_sem.at[s])

    def row_copy(d, b, k, g, j):
        return pltpu.make_async_copy(ys_hbm.at[pl.ds(d, 1), :], buf.at[b, k, g, pl.ds(j, 1), :], row_sem.at[b])

    def issue_rows(tile):
        base = lax.rem(tile, IDX_SLOTS) * width
        b = lax.rem(tile, 2)

        def body(g, c):
            for j in range(SUBLANES):
                for k in range(TOP_K):
                    d = idx_smem[base + g * (SUBLANES * TOP_K) + (j * TOP_K + k)]
                    row_copy(d, b, k, g, j).start(priority=k % 2)
            return c

        lax.fori_loop(0, COMBINE_TILE // SUBLANES, body, 0)

    @pl.when(i == 0)
    def _():
        idx_copy(0).start()
        idx_copy(0).wait()
        issue_rows(0)

        @pl.when(n > 1)
        def _():
            idx_copy(1).start()

    @pl.when(i + 1 < n)
    def _():
        idx_copy(i + 1).wait()
        issue_rows(i + 1)

    @pl.when(i + 2 < n)
    def _():
        idx_copy(i + 2).start()

    b = lax.rem(i, 2)

    pltpu.make_async_copy(buf.at[b], buf.at[b], row_sem.at[b]).wait()
    gate = gate_ref[...]
    acc = x1_ref[...]
    for k in range(TOP_K):
        acc = acc + buf[b, k].reshape(COMBINE_TILE, D_MODEL) * gate[:, k:k + 1]
    y_ref[...] = acc * lax.rsqrt(jnp.mean(acc * acc, axis=-1, keepdims=True) + RMS_EPS) * gf_ref[...]


def _combine(x1, gate, dest, ys, g_final, row_off, rows):
    t = x1.shape[0]
    tile_off = row_off // COMBINE_TILE
    return pl.pallas_call(
        functools.partial(_combine_body, tile_off),
        grid=(rows // COMBINE_TILE,),
        in_specs=[
            pl.BlockSpec(memory_space=pl.ANY),
            pl.BlockSpec((COMBINE_TILE, D_MODEL), lambda i: (tile_off + i, 0)),
            pl.BlockSpec((COMBINE_TILE, LANES), lambda i: (tile_off + i, 0)),
            pl.BlockSpec((1, D_MODEL), lambda i: (0, 0)),
            pl.BlockSpec(memory_space=pl.ANY),
        ],
        out_specs=pl.BlockSpec((COMBINE_TILE, D_MODEL), lambda i: (i, 0)),
        out_shape=jax.ShapeDtypeStruct((rows, D_MODEL), F32),
        scratch_shapes=[
            pltpu.SMEM((IDX_SLOTS * COMBINE_TILE * TOP_K,), jnp.int32),
            pltpu.VMEM((2, TOP_K, COMBINE_TILE // SUBLANES, SUBLANES, D_MODEL), F32),
            pltpu.SemaphoreType.DMA((IDX_SLOTS,)),
            pltpu.SemaphoreType.DMA((2,)),
        ],
        compiler_params=_params("arbitrary"),
        name="combine",
    )(dest.reshape(-1), x1, gate, g_final.reshape(1, D_MODEL), ys)


def _lane_row(v):
    return jnp.pad(v.astype(F32), (0, LANES - v.shape[0])).reshape(1, LANES)


def kernel(x_prompt, x_sample, cache_swa_k, cache_swa_v, state_mlstm_C, state_mlstm_n, state_mlstm_m,
           g_mix, w_in, b_if, attn_sinks, mlstm_norm_w, w_attn_proj, w_mlstm_proj, w_out,
           g_ffn, w_router, b_router, w_e1, b_e1, w_e2, b_e2, g_final):
    bp, sp = x_prompt.shape[0], x_prompt.shape[1]
    bs, ls = x_sample.shape[0], x_sample.shape[1]
    tp = bp * sp
    ts = bs * SAMPLE_ROWS
    t_all = tp + ts
    l = 0
    w_in_p = _permute_w_in(w_in[l])
    wa, wm, wo = w_attn_proj[l].astype(BF16), w_mlstm_proj[l].astype(BF16), w_out[l].astype(BF16)
    wr_f32 = jnp.pad(w_router[l], ((0, 0), (0, ROUTER_PAD - N_EXPERTS)))
    wr_hi = wr_f32.astype(BF16)
    wr_pad = jnp.concatenate([wr_hi, (wr_f32 - wr_hi.astype(F32)).astype(BF16)], axis=1)
    br_pad = _lane_row(b_router[l])
    b1, b2 = b_e1[l].reshape(N_EXPERTS, 1, 2 * D_FF), b_e2[l].reshape(N_EXPERTS, 1, D_MODEL)
    sinks_row = _lane_row(attn_sinks[l])
    bias_row = _lane_row(b_if[l])
    norm_w_row = mlstm_norm_w[l].reshape(1, M_V_WIDTH)

    xp = x_prompt.reshape(tp, D_MODEL)
    xs = jnp.pad(x_sample, ((0, 0), (0, SAMPLE_ROWS - ls), (0, 0))).reshape(ts, D_MODEL)
    z_all = _inproj(xp, xs, g_mix[l], w_in_p, 512)

    a_p, k1, v1, w1 = _attn_prompt(z_all, sinks_row, bp, sp, w_e1[l])
    a_s, k2, v2 = _attn_sample(z_all, sinks_row, cache_swa_k[l].reshape(bs, WINDOW, KV_WIDTH),
                               cache_swa_v[l].reshape(bs, WINDOW, KV_WIDTH), tp, ls)
    mh_p, c1, n1, m1, w2 = _mlstm_prompt(z_all, bias_row, norm_w_row, bp, sp, w_e2[l])
    mh_s, c2, n2, m2 = _mlstm_sample(z_all, bias_row, norm_w_row, state_mlstm_C[l], state_mlstm_n[l],
                                     state_mlstm_m[l], tp, ls)

    x1, h2, logits = _merge(xp, xs, a_p, a_s, mh_p, mh_s, z_all, wa, wm, wo, g_ffn[l], wr_pad, br_pad, 256)

    meta, gate, counts = _route(logits)
    n_blocks = -(-t_all * TOP_K // MOE_BLOCK_M) + N_EXPERTS
    pstarts, pends, padded, block_e, n_used = _slot_tables(counts[0, :N_EXPERTS].astype(jnp.int32), n_blocks)
    dest = pstarts[meta[:, LANE_E:LANE_E + TOP_K]] + meta[:, LANE_RANK:LANE_RANK + TOP_K]
    xs_sorted = _scatter_rows(h2, dest, pends, padded, n_blocks * MOE_BLOCK_M)
    ys = _experts(block_e, n_used, xs_sorted, w1, b1, w2, b2)
    y_prompt = _combine(x1, gate, dest, ys, g_final, 0, tp).reshape(bp, sp, D_MODEL)
    y_sample = _combine(x1, gate, dest, ys, g_final, tp, ts).reshape(bs, SAMPLE_ROWS, D_MODEL)[:, :ls]
    k1 = k1.reshape(bp, WINDOW, N_KV_HEADS, HEAD_DIM)
    v1 = v1.reshape(bp, WINDOW, N_KV_HEADS, HEAD_DIM)
    k2 = k2.reshape(bs, WINDOW, N_KV_HEADS, HEAD_DIM)
    v2 = v2.reshape(bs, WINDOW, N_KV_HEADS, HEAD_DIM)
    m1 = m1[:, :, 0]
    st = lambda t: t[None]
    return (y_prompt, y_sample, st(k1), st(v1), st(c1), st(n1), st(m1), st(k2), st(v2), st(c2), st(n2), st(m2))
```

```python
import functools

import jax
import jax.numpy as jnp
from jax import lax
from jax.experimental import pallas as pl
from jax.experimental.pallas import tpu as pltpu

D_MODEL = 2048
N_HEADS = 16
N_KV_HEADS = 2
HEAD_DIM = 64
Q_PER_KV = N_HEADS // N_KV_HEADS
WINDOW = 128
ATTN_WIDTH = N_HEADS * HEAD_DIM
KV_WIDTH = N_KV_HEADS * HEAD_DIM
M_HEADS = 4
M_QK_DIM = 128
M_V_DIM = 256
M_QK_WIDTH = M_HEADS * M_QK_DIM
M_V_WIDTH = M_HEADS * M_V_DIM
M_CHUNK = 64
N_EXPERTS = 32
TOP_K = 4
D_FF = D_MODEL
SWIGLU_LIMIT = 7.0
SWIGLU_ALPHA = 1.702
RMS_EPS = 1e-5
NEG = -1e30

F32 = jnp.float32
BF16 = jnp.bfloat16

LANES = 128
SUBLANES = 8
VMEM_LIMIT_BYTES = 56 * 1024 * 1024

COL_GZ = 0
COL_AQ = COL_GZ + 2 * D_MODEL
COL_MV = COL_AQ + ATTN_WIDTH
COL_MO = COL_MV + M_V_WIDTH
COL_MQ = COL_MO + M_V_WIDTH
COL_MK = COL_MQ + M_QK_WIDTH
COL_AK = COL_MK + M_QK_WIDTH
COL_IF = COL_AK + 2 * KV_WIDTH
IN_TILE_N = 2176
IN_WIDTH_PAD = -(-(COL_IF + LANES) // IN_TILE_N) * IN_TILE_N

SAMPLE_ROWS = SUBLANES
MOE_BLOCK_M = 512
MOE_FF_CHUNK = 1024
ROUTER_PAD = LANES


def _permute_w_in(w_in):
    splits = (ATTN_WIDTH, KV_WIDTH, KV_WIDTH, M_QK_WIDTH, M_QK_WIDTH, M_V_WIDTH, M_V_WIDTH, M_HEADS, M_HEADS,
              2 * D_MODEL)
    parts, off = [], 0
    for width in splits:
        parts.append(w_in[:, off:off + width])
        off += width
    aq, ak, av, mq, mk, mv, mo, mi, mf, gz = parts
    cols = jnp.concatenate([gz, aq, mv, mo, mq, mk, ak, av, mi, mf], axis=1)
    cols = jnp.pad(cols, ((0, 0), (0, IN_WIDTH_PAD - cols.shape[1])))
    return cols.astype(BF16)


def _params(*semantics):
    return pltpu.CompilerParams(dimension_semantics=semantics, vmem_limit_bytes=VMEM_LIMIT_BYTES)


def _log_sigmoid(x):
    return jnp.minimum(x, 0.0) - jnp.log1p(jnp.exp(-jnp.abs(x)))


def _inproj_body(tiles_a, xa_ref, xb_ref, g_ref, w_ref, z_ref, h_scr):
    def normalise(x_ref):
        x = x_ref[...]
        y = x * lax.rsqrt(jnp.mean(x * x, axis=-1, keepdims=True) + RMS_EPS)
        h_scr[...] = (y * g_ref[...]).astype(BF16)

    first = pl.program_id(1) == 0
    in_a = pl.program_id(0) < tiles_a

    @pl.when(first & in_a)
    def _():
        normalise(xa_ref)

    @pl.when(first & jnp.logical_not(in_a))
    def _():
        normalise(xb_ref)

    z_ref[...] = jnp.dot(h_scr[...], w_ref[...], preferred_element_type=F32)


def _inproj(x_a, x_b, g, w_bf16, tile_m):
    tiles_a, tiles_b = x_a.shape[0] // tile_m, x_b.shape[0] // tile_m
    n = w_bf16.shape[1]
    return pl.pallas_call(
        functools.partial(_inproj_body, tiles_a),
        grid=(tiles_a + tiles_b, n // IN_TILE_N),
        in_specs=[
            pl.BlockSpec((tile_m, D_MODEL), lambda i, j: (jnp.minimum(i, tiles_a - 1), 0)),
            pl.BlockSpec((tile_m, D_MODEL), lambda i, j: (jnp.maximum(i - tiles_a, 0), 0)),
            pl.BlockSpec((1, D_MODEL), lambda i, j: (0, 0)),
            pl.BlockSpec((D_MODEL, IN_TILE_N), lambda i, j: (0, j)),
        ],
        out_specs=pl.BlockSpec((tile_m, IN_TILE_N), lambda i, j: (i, j)),
        out_shape=jax.ShapeDtypeStruct(((tiles_a + tiles_b) * tile_m, n), F32),
        scratch_shapes=[pltpu.VMEM((tile_m, D_MODEL), BF16)],
        compiler_params=_params("arbitrary", "arbitrary"),
        name="inproj",
    )(x_a, x_b, g.reshape(1, D_MODEL), w_bf16)


assert KV_WIDTH == LANES and N_KV_HEADS == 2 and LANES == 2 * HEAD_DIM
Q_TILES = ATTN_WIDTH // LANES
TILES_PER_KV = Q_TILES // N_KV_HEADS
_NT = (((1,), (1,)), ((), ()))


def _kv_placements(x):
    lo = lax.broadcasted_iota(jnp.int32, x.shape, 1) < HEAD_DIM
    swapped = pltpu.roll(x, HEAD_DIM, axis=1)
    return ((jnp.where(lo, x, 0.0).astype(BF16), jnp.where(lo, 0.0, swapped).astype(BF16)),
            (jnp.where(lo, swapped, 0.0).astype(BF16), jnp.where(lo, 0.0, x).astype(BF16)))


def _slab_head(g, p, j):
    return g * Q_PER_KV + 2 * j + p


ATTN_BLOCKS = 2


def _attn_block(q, kv_cur, kv_prev, has_prev, sink_ref, a_ref, rows):
    kc, vc = kv_cur[:, :KV_WIDTH], kv_cur[:, KV_WIDTH:]
    kk = _kv_placements(jnp.concatenate([kv_prev[:, :KV_WIDTH], kc], axis=0))
    vv = _kv_placements(jnp.concatenate([kv_prev[:, KV_WIDTH:], vc], axis=0))
    q = q.astype(BF16)
    row = lax.broadcasted_iota(jnp.int32, (WINDOW, WINDOW), 0)
    col = lax.broadcasted_iota(jnp.int32, (WINDOW, WINDOW), 1)
    mask = jnp.concatenate([(col > row) & has_prev, col <= row], axis=1)
    slabs, sinks = [], []
    for g in range(N_KV_HEADS):
        qt = jnp.concatenate([q[:, (TILES_PER_KV * g + j) * LANES:(TILES_PER_KV * g + j + 1) * LANES]
                              for j in range(TILES_PER_KV)], axis=0)
        for p in range(2):
            s = lax.dot_general(qt, kk[g][p], _NT, preferred_element_type=F32) * (HEAD_DIM ** -0.5)
            for j in range(TILES_PER_KV):
                h = _slab_head(g, p, j)
                slabs.append(jnp.where(mask, s[j * WINDOW:(j + 1) * WINDOW], NEG))
                sinks.append(jnp.broadcast_to(sink_ref[:, h:h + 1], (WINDOW, LANES)))
    s_all = jnp.concatenate(slabs, axis=0)
    sink = jnp.concatenate(sinks, axis=0)
    m = jnp.maximum(jnp.broadcast_to(jnp.max(s_all, axis=1, keepdims=True), sink.shape), sink)
    p_all = jnp.exp(s_all - jnp.concatenate([m, m], axis=1))
    p_hi = p_all.astype(BF16)
    p_lo = (p_all - p_hi.astype(F32)).astype(BF16)
    ones = jnp.ones((2 * WINDOW, LANES), BF16)
    denom = (jnp.dot(p_hi, ones, preferred_element_type=F32) + jnp.dot(p_lo, ones, preferred_element_type=F32)
             + jnp.exp(sink - m))
    inv = 1.0 / denom
    pn = (p_all * jnp.concatenate([inv, inv], axis=1)).astype(BF16)
    half = TILES_PER_KV * WINDOW
    for g in range(N_KV_HEADS):
        base = 2 * half * g
        o = (jnp.dot(pn[base:base + half], vv[g][0], preferred_element_type=F32)
             + jnp.dot(pn[base + half:base + 2 * half], vv[g][1], preferred_element_type=F32))
        for j in range(TILES_PER_KV):
            t = TILES_PER_KV * g + j
            a_ref[rows, t * LANES:(t + 1) * LANES] = o[j * WINDOW:(j + 1) * WINDOW].astype(BF16)


def _attn_prompt_body(q_ref, kvc_ref, kvp_ref, sink_ref, w_ref, a_ref, k_ref, v_ref, wb_ref):
    wb_ref[...] = w_ref[...].astype(BF16)
    n = pl.program_id(1)
    for u in range(ATTN_BLOCKS):
        rows = slice(u * WINDOW, (u + 1) * WINDOW)
        if u == 0:
            kv_prev, has_prev = kvp_ref[...], n > 0
        else:
            kv_prev, has_prev = kvc_ref[(u - 1) * WINDOW:u * WINDOW, :], True
        _attn_block(q_ref[rows, :], kvc_ref[rows, :], kv_prev, has_prev, sink_ref, a_ref, rows)
    last = slice((ATTN_BLOCKS - 1) * WINDOW, ATTN_BLOCKS * WINDOW)
    k_ref[...] = kvc_ref[last, :KV_WIDTH]
    v_ref[...] = kvc_ref[last, KV_WIDTH:]


def _cast_slices(w, steps):
    rows = w.reshape(-1, w.shape[-1])
    per_step = rows.shape[0] // steps
    assert per_step * steps == rows.shape[0] and per_step % 16 == 0
    return rows, per_step


def _attn_prompt(z_all, sinks_row, batch, seq, w):
    span = ATTN_BLOCKS * WINDOW
    nb = seq // span
    w_rows, r1 = _cast_slices(w, batch * nb)
    step = lambda b, n: (b * nb + n, 0)
    a, k, v, wb = pl.pallas_call(
        _attn_prompt_body,
        grid=(batch, nb),
        in_specs=[
            pl.BlockSpec((span, ATTN_WIDTH), lambda b, n: (b * nb + n, COL_AQ // ATTN_WIDTH)),
            pl.BlockSpec((span, 2 * KV_WIDTH), lambda b, n: (b * nb + n, COL_AK // (2 * KV_WIDTH))),
            pl.BlockSpec((WINDOW, 2 * KV_WIDTH),
                         lambda b, n: (jnp.maximum((b * nb + n) * ATTN_BLOCKS - 1, 0), COL_AK // (2 * KV_WIDTH))),
            pl.BlockSpec((1, LANES), lambda b, n: (0, 0)),
            pl.BlockSpec((r1, w_rows.shape[1]), step),
        ],
        out_specs=[
            pl.BlockSpec((span, ATTN_WIDTH), lambda b, n: (b * nb + n, 0)),
            pl.BlockSpec((None, WINDOW, KV_WIDTH), lambda b, n: (b, 0, 0)),
            pl.BlockSpec((None, WINDOW, KV_WIDTH), lambda b, n: (b, 0, 0)),
            pl.BlockSpec((r1, w_rows.shape[1]), step),
        ],
        out_shape=[
            jax.ShapeDtypeStruct((batch * seq, ATTN_WIDTH), BF16),
            jax.ShapeDtypeStruct((batch, WINDOW, KV_WIDTH), F32),
            jax.ShapeDtypeStruct((batch, WINDOW, KV_WIDTH), F32),
            jax.ShapeDtypeStruct(w_rows.shape, BF16),
        ],
        compiler_params=_params("arbitrary", "arbitrary"),
        name="attn_prompt",
    )(z_all, z_all, z_all, sinks_row, w_rows)
    return a, k, v, wb.reshape(w.shape)


ATTN_SAMPLE_GROUP = 8


def _attn_sample_body(n_valid, q_ref, kv_ref, ck_ref, cv_ref, sink_ref, a_ref, nk_ref, nv_ref, o_scr):
    rs = SAMPLE_ROWS
    slab = TILES_PER_KV * rs
    r8 = lax.broadcasted_iota(jnp.int32, (rs, KV_WIDTH), 0)
    qi_c = jnp.bitwise_and(lax.broadcasted_iota(jnp.int32, (slab, WINDOW), 0), rs - 1)
    mask_c = lax.broadcasted_iota(jnp.int32, (slab, WINDOW), 1) > qi_c
    qi_n = jnp.bitwise_and(lax.broadcasted_iota(jnp.int32, (slab, rs), 0), rs - 1)
    mask_n = lax.broadcasted_iota(jnp.int32, (slab, rs), 1) <= qi_n
    sc_list, sn_list, sink_list, values = [], [], [], []
    for b in range(ATTN_SAMPLE_GROUP):
        rows = slice(b * rs, (b + 1) * rs)
        q = q_ref[rows, :]
        kn, vn = kv_ref[rows, :KV_WIDTH], kv_ref[rows, KV_WIDTH:]
        ck, cv = ck_ref[b], cv_ref[b]
        for src, new, dst in ((ck, kn, nk_ref), (cv, vn, nv_ref)):
            rolled = pltpu.roll(src, WINDOW - n_valid, axis=0)
            dst[b] = rolled
            tail = jnp.where(r8 >= rs - n_valid, pltpu.roll(new, rs - n_valid, axis=0), rolled[WINDOW - rs:, :])
            dst[b, WINDOW - rs:, :] = tail
        kk_c, kk_n = _kv_placements(ck), _kv_placements(kn)
        values.append((_kv_placements(cv), _kv_placements(vn)))
        for g in range(N_KV_HEADS):
            qt = jnp.concatenate([q[:, (TILES_PER_KV * g + j) * LANES:(TILES_PER_KV * g + j + 1) * LANES]
                                  for j in range(TILES_PER_KV)], axis=0).astype(BF16)
            for p in range(2):
                s_c = lax.dot_general(qt, kk_c[g][p], _NT, preferred_element_type=F32) * (HEAD_DIM ** -0.5)
                s_n = lax.dot_general(qt, kk_n[g][p], _NT, preferred_element_type=F32) * (HEAD_DIM ** -0.5)
                sc_list.append(jnp.where(mask_c, s_c, NEG))
                sn_list.append(jnp.where(mask_n, s_n, NEG))
                for j in range(TILES_PER_KV):
                    h = _slab_head(g, p, j)
                    sink_list.append(jnp.broadcast_to(sink_ref[:, h:h + 1], (rs, 1)))
    s_c = jnp.concatenate(sc_list, axis=0)
    s_n = jnp.concatenate(sn_list, axis=0)
    sink = jnp.concatenate(sink_list, axis=0)
    m = jnp.maximum(jnp.maximum(jnp.max(s_c, axis=1, keepdims=True), jnp.max(s_n, axis=1, keepdims=True)), sink)
    p_c = jnp.exp(s_c - m)
    p_n = jnp.exp(s_n - m)
    inv = 1.0 / (jnp.sum(p_c, axis=1, keepdims=True) + jnp.sum(p_n, axis=1, keepdims=True) + jnp.exp(sink - m))
    p_c = (p_c * inv).astype(BF16)
    p_n = (p_n * inv).astype(BF16)
    for b in range(ATTN_SAMPLE_GROUP):
        vv_c, vv_n = values[b]
        for g in range(N_KV_HEADS):
            base = (b * N_KV_HEADS + g) * 2 * slab
            o = (jnp.dot(p_c[base:base + slab], vv_c[g][0], preferred_element_type=F32)
                 + jnp.dot(p_c[base + slab:base + 2 * slab], vv_c[g][1], preferred_element_type=F32)
                 + jnp.dot(p_n[base:base + slab], vv_n[g][0], preferred_element_type=F32)
                 + jnp.dot(p_n[base + slab:base + 2 * slab], vv_n[g][1], preferred_element_type=F32))
            for j in range(TILES_PER_KV):
                t = TILES_PER_KV * g + j
                o_scr[b * rs:(b + 1) * rs, t * LANES:(t + 1) * LANES] = o[j * rs:(j + 1) * rs]
    a_ref[...] = o_scr[...].astype(BF16)


def _attn_sample(z_all, sinks_row, cache_k, cache_v, row_off, n_valid):
    b = cache_k.shape[0]
    gr = ATTN_SAMPLE_GROUP
    rows = gr * SAMPLE_ROWS
    rb = row_off // rows
    return pl.pallas_call(
        functools.partial(_attn_sample_body, n_valid),
        grid=(b // gr,),
        in_specs=[
            pl.BlockSpec((rows, ATTN_WIDTH), lambda i: (rb + i, COL_AQ // ATTN_WIDTH)),
            pl.BlockSpec((rows, 2 * KV_WIDTH), lambda i: (rb + i, COL_AK // (2 * KV_WIDTH))),
            pl.BlockSpec((gr, WINDOW, KV_WIDTH), lambda i: (i, 0, 0)),
            pl.BlockSpec((gr, WINDOW, KV_WIDTH), lambda i: (i, 0, 0)),
            pl.BlockSpec((1, LANES), lambda i: (0, 0)),
        ],
        out_specs=[
            pl.BlockSpec((rows, ATTN_WIDTH), lambda i: (i, 0)),
            pl.BlockSpec((gr, WINDOW, KV_WIDTH), lambda i: (i, 0, 0)),
            pl.BlockSpec((gr, WINDOW, KV_WIDTH), lambda i: (i, 0, 0)),
        ],
        out_shape=[
            jax.ShapeDtypeStruct((b * SAMPLE_ROWS, ATTN_WIDTH), BF16),
            jax.ShapeDtypeStruct(cache_k.shape, F32),
            jax.ShapeDtypeStruct(cache_v.shape, F32),
        ],
        scratch_shapes=[pltpu.VMEM((rows, ATTN_WIDTH), F32)],
        compiler_params=_params("arbitrary"),
        name="attn_sample",
    )(z_all, z_all, cache_k, cache_v, sinks_row)


def _mlstm_gates(if_blk, bias_row, n_valid):
    length = if_blk.shape[0]
    g = if_blk + bias_row
    lane = lax.broadcasted_iota(jnp.int32, g.shape, 1)
    g = jnp.where(lane < M_HEADS, g, _log_sigmoid(g))
    if n_valid < length:
        row = lax.broadcasted_iota(jnp.int32, g.shape, 0)
        g = jnp.where(row < n_valid, g, jnp.where(lane < M_HEADS, NEG, 0.0))
    return g


def _lanes(col, width):
    if width <= LANES:
        return col[:, :width]
    return jnp.concatenate([col] * (width // LANES), axis=1)


def _row_stat(x, reduce):
    return jnp.broadcast_to(reduce(x, axis=1, keepdims=True), (x.shape[0], LANES))


def _mlstm_chunk(q, k, v, gates, gates_t, h, c, n, m):
    length = q.shape[0]
    i_rep = jnp.broadcast_to(gates[:, h:h + 1], (length, LANES))
    f_col = gates[:, M_HEADS + h:M_HEADS + h + 1]
    i_row = gates_t[h:h + 1, :]
    f_row = gates_t[M_HEADS + h:M_HEADS + h + 1, :]
    t_idx = lax.broadcasted_iota(jnp.int32, (length, length), 0)
    s_idx = lax.broadcasted_iota(jnp.int32, (length, length), 1)
    causal = s_idx <= t_idx
    bcum = _row_stat(jnp.where(causal, f_row, 0.0), jnp.sum)
    bcum_row = jnp.sum(jnp.where(t_idx <= s_idx, f_col, 0.0), axis=0, keepdims=True)
    dlog = jnp.where(causal, _lanes(bcum, length) - bcum_row + i_row, NEG)
    inter = bcum + m
    mt = jnp.maximum(inter, _row_stat(dlog, jnp.max))
    w_inter = jnp.exp(inter - mt)
    qb = q.astype(BF16)
    ks = k * (M_QK_DIM ** -0.5)
    vb = v.astype(BF16)
    qk = lax.dot_general(qb, ks.astype(BF16), (((1,), (1,)), ((), ())), preferred_element_type=F32)
    qk = qk * jnp.exp(dlog - _lanes(mt, length))
    num = (_lanes(w_inter, M_V_DIM) * jnp.dot(qb, c.astype(BF16), preferred_element_type=F32)
           + jnp.dot(qk.astype(BF16), vb, preferred_element_type=F32))
    den = w_inter * _row_stat(q * n, jnp.sum) + _row_stat(qk, jnp.sum)
    h_t = num * _lanes(1.0 / jnp.maximum(jnp.abs(den), jnp.exp(-mt)), M_V_DIM)
    m_new = mt[length - 1:length, 0:1]
    w_state = jnp.exp(bcum[length - 1:length, :] - bcum + i_rep - m_new)
    decay = jnp.exp(inter[length - 1:length, 0:1] - m_new)
    kw = ks * _lanes(w_state, M_QK_DIM)
    c_new = decay * c + jnp.dot(kw.T.astype(BF16), vb, preferred_element_type=F32)
    n_new = decay * n + jnp.sum(kw, axis=0, keepdims=True)
    return h_t, c_new, n_new, m_new


def _mlstm_head_out(h_t, norm_w, o_gate):
    scale = lax.rsqrt(_row_stat(h_t * h_t, jnp.sum) * (1.0 / h_t.shape[1]) + RMS_EPS)
    return h_t * _lanes(scale, h_t.shape[1]) * norm_w * jax.nn.sigmoid(o_gate)


MLSTM_SEG = 256


MLSTM_BATCH = 2
MLSTM_PROMPT_CHUNK = 256


def _mlstm_prompt_body(*refs):
    nb = MLSTM_BATCH
    ins, rest = refs[:5 * nb], refs[5 * nb:]
    bias_ref, nw_ref, w_ref, mh_ref, c_out, n_out, m_out, wb_ref, c_scr, n_scr, m_scr = rest
    seg = pl.program_id(1)
    wb_ref[...] = w_ref[...].astype(BF16)

    @pl.when(seg == 0)
    def _():
        c_scr[...] = jnp.zeros_like(c_scr)
        n_scr[...] = jnp.zeros_like(n_scr)
        m_scr[...] = jnp.full_like(m_scr, NEG)

    def chunk_body(ci, carry):
        r = pl.multiple_of(ci * MLSTM_PROMPT_CHUNK, MLSTM_PROMPT_CHUNK)
        rows = pl.ds(r, MLSTM_PROMPT_CHUNK)
        state = [[(c_scr[u, h], n_scr[u, h:h + 1, :], m_scr[u, h:h + 1, 0:1]) for h in range(M_HEADS)]
                 for u in range(nb)]
        results = []
        for u in range(nb):
            q_ref, k_ref, v_ref, o_ref, if_ref = ins[5 * u:5 * u + 5]
            gates = _mlstm_gates(if_ref[rows, :], bias_ref[...], MLSTM_PROMPT_CHUNK)
            gates_t = gates.T
            for h in range(M_HEADS):
                qs = slice(h * M_QK_DIM, (h + 1) * M_QK_DIM)
                vs = slice(h * M_V_DIM, (h + 1) * M_V_DIM)
                h_t, c_new, n_new, m_new = _mlstm_chunk(
                    q_ref[rows, qs], k_ref[rows, qs], v_ref[rows, vs], gates, gates_t, h, *state[u][h])
                out = _mlstm_head_out(h_t, nw_ref[:, vs], o_ref[rows, vs]).astype(BF16)
                results.append((u, h, vs, out, c_new, n_new, m_new))
        for u, h, vs, out, c_new, n_new, m_new in results:
            c_scr[u, h] = c_new
            n_scr[u, h:h + 1, :] = n_new
            m_scr[u, h:h + 1, :] = jnp.broadcast_to(m_new, (1, LANES))
            mh_ref[u, rows, vs] = out
        return carry

    lax.fori_loop(0, MLSTM_SEG // MLSTM_PROMPT_CHUNK, chunk_body, 0)

    @pl.when(seg == pl.num_programs(1) - 1)
    def _():
        c_out[...] = c_scr[...]
        n_out[...] = n_scr[:, 0:M_HEADS, :]
        m_out[...] = m_scr[:, 0:M_HEADS, :]


def _mlstm_prompt(z_all, bias_row, norm_w_row, batch, seq, w):
    ns = seq // MLSTM_SEG
    nb = MLSTM_BATCH
    w_rows, wr = _cast_slices(w, (batch // nb) * ns)
    in_specs, args = [], []
    for u in range(nb):
        row = lambda p, s, u=u: (p * nb + u) * ns + s
        in_specs += [
            pl.BlockSpec((MLSTM_SEG, M_QK_WIDTH), lambda p, s, row=row: (row(p, s), COL_MQ // M_QK_WIDTH)),
            pl.BlockSpec((MLSTM_SEG, M_QK_WIDTH), lambda p, s, row=row: (row(p, s), COL_MK // M_QK_WIDTH)),
            pl.BlockSpec((MLSTM_SEG, M_V_WIDTH), lambda p, s, row=row: (row(p, s), COL_MV // M_V_WIDTH)),
            pl.BlockSpec((MLSTM_SEG, M_V_WIDTH), lambda p, s, row=row: (row(p, s), COL_MO // M_V_WIDTH)),
            pl.BlockSpec((MLSTM_SEG, LANES), lambda p, s, row=row: (row(p, s), COL_IF // LANES)),
        ]
        args += [z_all] * 5
    in_specs += [pl.BlockSpec((1, LANES), lambda p, s: (0, 0)), pl.BlockSpec((1, M_V_WIDTH), lambda p, s: (0, 0)),
                 pl.BlockSpec((wr, w_rows.shape[1]), lambda p, s: (p * ns + s, 0))]
    mh, c, n, m, wb = pl.pallas_call(
        _mlstm_prompt_body,
        grid=(batch // nb, ns),
        in_specs=in_specs,
        out_specs=[
            pl.BlockSpec((None, nb, MLSTM_SEG, M_V_WIDTH), lambda p, s: (p, 0, s, 0)),
            pl.BlockSpec((nb, M_HEADS, M_QK_DIM, M_V_DIM), lambda p, s: (p, 0, 0, 0)),
            pl.BlockSpec((nb, M_HEADS, M_QK_DIM), lambda p, s: (p, 0, 0)),
            pl.BlockSpec((nb, M_HEADS, LANES), lambda p, s: (p, 0, 0)),
            pl.BlockSpec((wr, w_rows.shape[1]), lambda p, s: (p * ns + s, 0)),
        ],
        out_shape=[
            jax.ShapeDtypeStruct((batch // nb, nb, seq, M_V_WIDTH), BF16),
            jax.ShapeDtypeStruct((batch, M_HEADS, M_QK_DIM, M_V_DIM), F32),
            jax.ShapeDtypeStruct((batch, M_HEADS, M_QK_DIM), F32),
            jax.ShapeDtypeStruct((batch, M_HEADS, LANES), F32),
            jax.ShapeDtypeStruct(w_rows.shape, BF16),
        ],
        scratch_shapes=[
            pltpu.VMEM((nb, M_HEADS, M_QK_DIM, M_V_DIM), F32),
            pltpu.VMEM((nb, SUBLANES, M_QK_DIM), F32),
            pltpu.VMEM((nb, SUBLANES, LANES), F32),
        ],
        compiler_params=_params("arbitrary", "arbitrary"),
        name="mlstm_prompt",
    )(*args, bias_row, norm_w_row, w_rows)
    return mh.reshape(batch * seq, M_V_WIDTH), c, n, m, wb.reshape(w.shape)


MLSTM_SAMPLE_GROUP = 8


def _mlstm_sample_body(n_valid, q_ref, k_ref, v_ref, o_ref, if_ref, bias_ref, nw_ref, c0_ref, n0_ref, m0_ref,
                       mh_ref, c_out, n_out, m_out, h_scr):
    for g in range(MLSTM_SAMPLE_GROUP):
        rows = slice(g * SAMPLE_ROWS, (g + 1) * SAMPLE_ROWS)
        gates = _mlstm_gates(if_ref[rows, :], bias_ref[...], n_valid)
        gates_t = jnp.transpose(jnp.concatenate([gates] * (LANES // SAMPLE_ROWS), axis=0))[:, :SAMPLE_ROWS]
        for h in range(M_HEADS):
            qs = slice(h * M_QK_DIM, (h + 1) * M_QK_DIM)
            vs = slice(h * M_V_DIM, (h + 1) * M_V_DIM)
            h_t, c_new, n_new, m_new = _mlstm_chunk(
                q_ref[rows, qs], k_ref[rows, qs], v_ref[rows, vs], gates, gates_t, h,
                c0_ref[g, h], n0_ref[g, h:h + 1, :], m0_ref[g:g + 1, h:h + 1])
            c_out[g, h] = c_new
            n_out[g, h:h + 1, :] = n_new
            m_out[g:g + 1, h:h + 1] = m_new
            h_scr[rows, vs] = _mlstm_head_out(h_t, nw_ref[:, vs], o_ref[rows, vs])
    mh_ref[...] = h_scr[...].astype(BF16)


def _mlstm_sample(z_all, bias_row, norm_w_row, c0, n0, m0, row_off, n_valid):
    b = c0.shape[0]
    gr = MLSTM_SAMPLE_GROUP
    rows = gr * SAMPLE_ROWS
    rb = row_off // rows
    m0g = m0.reshape(b // gr, gr, M_HEADS)
    outs = pl.pallas_call(
        functools.partial(_mlstm_sample_body, n_valid),
        grid=(b // gr,),
        in_specs=[
            pl.BlockSpec((rows, M_QK_WIDTH), lambda i: (rb + i, COL_MQ // M_QK_WIDTH)),
            pl.BlockSpec((rows, M_QK_WIDTH), lambda i: (rb + i, COL_MK // M_QK_WIDTH)),
            pl.BlockSpec((rows, M_V_WIDTH), lambda i: (rb + i, COL_MV // M_V_WIDTH)),
            pl.BlockSpec((rows, M_V_WIDTH), lambda i: (rb + i, COL_MO // M_V_WIDTH)),
            pl.BlockSpec((rows, LANES), lambda i: (rb + i, COL_IF // LANES)),
            pl.BlockSpec((1, LANES), lambda i: (0, 0)),
            pl.BlockSpec((1, M_V_WIDTH), lambda i: (0, 0)),
            pl.BlockSpec((gr, M_HEADS, M_QK_DIM, M_V_DIM), lambda i: (i, 0, 0, 0)),
            pl.BlockSpec((gr, M_HEADS, M_QK_DIM), lambda i: (i, 0, 0)),
            pl.BlockSpec((None, gr, M_HEADS), lambda i: (i, 0, 0)),
        ],
        out_specs=[
            pl.BlockSpec((rows, M_V_WIDTH), lambda i: (i, 0)),
            pl.BlockSpec((gr, M_HEADS, M_QK_DIM, M_V_DIM), lambda i: (i, 0, 0, 0)),
            pl.BlockSpec((gr, M_HEADS, M_QK_DIM), lambda i: (i, 0, 0)),
            pl.BlockSpec((None, gr, M_HEADS), lambda i: (i, 0, 0)),
        ],
        out_shape=[
            jax.ShapeDtypeStruct((b * SAMPLE_ROWS, M_V_WIDTH), BF16),
            jax.ShapeDtypeStruct(c0.shape, F32),
            jax.ShapeDtypeStruct(n0.shape, F32),
            jax.ShapeDtypeStruct(m0g.shape, F32),
        ],
        scratch_shapes=[pltpu.VMEM((rows, M_V_WIDTH), F32)],
        compiler_params=_params("arbitrary"),
        name="mlstm_sample",
    )(z_all, z_all, z_all, z_all, z_all, bias_row, norm_w_row, c0, n0, m0g)
    mh, c, n, m = outs
    return mh, c, n, m.reshape(b, M_HEADS)


def _merge_body(tiles_a, xa_ref, xb_ref, aa_ref, ab_ref, ma_ref, mb_ref, gza_ref, gzm_ref, wa_ref, wm_ref, wo_ref,
                gf_ref, wr_ref, br_ref, x1_ref, h2_ref, lg_ref):
    in_a = pl.program_id(0) < tiles_a
    x = jnp.where(in_a, xa_ref[...], xb_ref[...])
    a = jnp.where(in_a, aa_ref[...], ab_ref[...])
    mh = jnp.where(in_a, ma_ref[...], mb_ref[...])
    pa = jnp.dot(a, wa_ref[...], preferred_element_type=F32)
    pm = jnp.dot(mh, wm_ref[...], preferred_element_type=F32)
    merged = jax.nn.sigmoid(gza_ref[...]) * pa + jax.nn.sigmoid(gzm_ref[...]) * pm
    x1 = x + jnp.dot(merged.astype(BF16), wo_ref[...], preferred_element_type=F32)
    x1_ref[...] = x1
    h2 = x1 * lax.rsqrt(jnp.mean(x1 * x1, axis=-1, keepdims=True) + RMS_EPS) * gf_ref[...]
    h2_ref[...] = h2
    hi = h2.astype(BF16)
    lo = (h2 - hi.astype(F32)).astype(BF16)
    both = jnp.dot(hi, wr_ref[...], preferred_element_type=F32)
    lg_ref[...] = (both[:, :ROUTER_PAD] + both[:, ROUTER_PAD:]
                   + jnp.dot(lo, wr_ref[:, :ROUTER_PAD], preferred_element_type=F32) + br_ref[...])


def _merge(x_a, x_b, a_a, a_b, mh_a, mh_b, z_all, wa, wm, wo, g_ffn, wr_pad, br_pad, tile_m):
    tiles_a, tiles_b = x_a.shape[0] // tile_m, x_b.shape[0] // tile_m
    t_all = z_all.shape[0]
    const = lambda i: (0, 0)
    from_a = lambda i: (jnp.minimum(i, tiles_a - 1), 0)
    from_b = lambda i: (jnp.maximum(i - tiles_a, 0), 0)
    return pl.pallas_call(
        functools.partial(_merge_body, tiles_a),
        grid=(tiles_a + tiles_b,),
        in_specs=[
            pl.BlockSpec((tile_m, D_MODEL), from_a),
            pl.BlockSpec((tile_m, D_MODEL), from_b),
            pl.BlockSpec((tile_m, ATTN_WIDTH), from_a),
            pl.BlockSpec((tile_m, ATTN_WIDTH), from_b),
            pl.BlockSpec((tile_m, M_V_WIDTH), from_a),
            pl.BlockSpec((tile_m, M_V_WIDTH), from_b),
            pl.BlockSpec((tile_m, D_MODEL), lambda i: (i, COL_GZ // D_MODEL)),
            pl.BlockSpec((tile_m, D_MODEL), lambda i: (i, COL_GZ // D_MODEL + 1)),
            pl.BlockSpec((ATTN_WIDTH, D_MODEL), const, pipeline_mode=pl.Buffered(1)),
            pl.BlockSpec((M_V_WIDTH, D_MODEL), const, pipeline_mode=pl.Buffered(1)),
            pl.BlockSpec((D_MODEL, D_MODEL), const, pipeline_mode=pl.Buffered(1)),
            pl.BlockSpec((1, D_MODEL), const),
            pl.BlockSpec((D_MODEL, 2 * ROUTER_PAD), const),
            pl.BlockSpec((1, ROUTER_PAD), const),
        ],
        out_specs=[
            pl.BlockSpec((tile_m, D_MODEL), lambda i: (i, 0)),
            pl.BlockSpec((tile_m, D_MODEL), lambda i: (i, 0)),
            pl.BlockSpec((tile_m, ROUTER_PAD), lambda i: (i, 0)),
        ],
        out_shape=[
            jax.ShapeDtypeStruct((t_all, D_MODEL), F32),
            jax.ShapeDtypeStruct((t_all, D_MODEL), F32),
            jax.ShapeDtypeStruct((t_all, ROUTER_PAD), F32),
        ],
        compiler_params=_params("arbitrary"),
        name="merge",
    )(x_a, x_b, a_a, a_b, mh_a, mh_b, z_all, z_all, wa, wm, wo, g_ffn.reshape(1, D_MODEL), wr_pad, br_pad)


def _expert_body(be_ref, nb_ref, xs_ref, w1g_ref, w1u_ref, b1g_ref, b1u_ref, w2_ref, b2_ref, y_ref):
    i = pl.program_id(0)
    c = pl.program_id(1)
    used = i < nb_ref[0]

    @pl.when(used)
    def _():
        x = xs_ref[...].astype(BF16)
        zg = jnp.dot(x, w1g_ref[...], preferred_element_type=F32) + b1g_ref[...]
        zu = jnp.dot(x, w1u_ref[...], preferred_element_type=F32) + b1u_ref[...]
        gate = jnp.minimum(zg, SWIGLU_LIMIT)
        up = jnp.clip(zu, -SWIGLU_LIMIT, SWIGLU_LIMIT)
        act = gate * jax.nn.sigmoid(SWIGLU_ALPHA * gate) * (up + 1.0)
        y = jnp.dot(act.astype(BF16), w2_ref[...], preferred_element_type=F32)

        @pl.when(c == 0)
        def _():
            y_ref[...] = y + b2_ref[...]

        @pl.when(c != 0)
        def _():
            y_ref[...] += y

    @pl.when(jnp.logical_not(used) & (c == 0))
    def _():
        y_ref[...] = jnp.zeros_like(y_ref)


def _experts(block_e, n_used, xs, w1, b1, w2, b2):
    n_slots = xs.shape[0]
    n_blocks = n_slots // MOE_BLOCK_M
    n_chunks = D_FF // MOE_FF_CHUNK

    def chunk(i, c, nb):
        block = jnp.minimum(i, nb[0] - 1)
        step = jnp.where(i < nb[0], c, n_chunks - 1)
        return jnp.where(block % 2 == 0, step, n_chunks - 1 - step)

    grid_spec = pltpu.PrefetchScalarGridSpec(
        num_scalar_prefetch=2,
        grid=(n_blocks, n_chunks),
        in_specs=[
            pl.BlockSpec((MOE_BLOCK_M, D_MODEL), lambda i, c, be, nb: (jnp.minimum(i, nb[0] - 1), 0)),
            pl.BlockSpec((None, D_MODEL, MOE_FF_CHUNK), lambda i, c, be, nb: (be[i], 0, chunk(i, c, nb))),
            pl.BlockSpec((None, D_MODEL, MOE_FF_CHUNK),
                         lambda i, c, be, nb: (be[i], 0, n_chunks + chunk(i, c, nb))),
            pl.BlockSpec((None, 1, MOE_FF_CHUNK), lambda i, c, be, nb: (be[i], 0, chunk(i, c, nb))),
            pl.BlockSpec((None, 1, MOE_FF_CHUNK), lambda i, c, be, nb: (be[i], 0, n_chunks + chunk(i, c, nb))),
            pl.BlockSpec((None, MOE_FF_CHUNK, D_MODEL), lambda i, c, be, nb: (be[i], chunk(i, c, nb), 0)),
            pl.BlockSpec((None, 1, D_MODEL), lambda i, c, be, nb: (be[i], 0, 0)),
        ],
        out_specs=pl.BlockSpec((MOE_BLOCK_M, D_MODEL), lambda i, c, be, nb: (i, 0)),
    )
    return pl.pallas_call(
        _expert_body,
        grid_spec=grid_spec,
        out_shape=jax.ShapeDtypeStruct((n_slots, D_MODEL), F32),
        compiler_params=_params("arbitrary", "arbitrary"),
        name="experts",
    )(block_e, n_used, xs, w1, w1, b1, b1, w2, b2)


ROUTE_TILE = 1024
LANE_E = 0
LANE_RANK = TOP_K


def _route_body(lg_ref, meta_ref, gate_ref, cnt_ref, tri_scr, carry_scr):
    @pl.when(pl.program_id(0) == 0)
    def _():
        r = lax.broadcasted_iota(jnp.int32, (ROUTE_TILE, ROUTE_TILE), 0)
        c = lax.broadcasted_iota(jnp.int32, (ROUTE_TILE, ROUTE_TILE), 1)
        tri_scr[...] = jnp.where(c < r, 1.0, 0.0).astype(BF16)
        carry_scr[...] = jnp.zeros_like(carry_scr)

    lane_i = lax.broadcasted_iota(jnp.int32, (ROUTE_TILE, LANES), 1)
    lane = lane_i.astype(F32)
    x = jnp.where(lane_i < N_EXPERTS, lg_ref[...], -jnp.inf)
    vals, idxs, sels = [], [], []
    for _ in range(TOP_K):
        mx = jnp.max(x, axis=1, keepdims=True)
        idx = jnp.min(jnp.where(x == mx, lane, float(LANES)), axis=1, keepdims=True)
        sel = lane == idx
        x = jnp.where(sel, -jnp.inf, x)
        vals.append(mx)
        idxs.append(idx)
        sels.append(sel)
    weights = [jnp.exp(v - vals[0]) for v in vals]
    inv = 1.0 / (weights[0] + weights[1] + weights[2] + weights[3])
    onehot = sum(jnp.where(s, 1.0, 0.0) for s in sels)
    ranks = jnp.dot(tri_scr[...], onehot.astype(BF16), preferred_element_type=F32) + carry_scr[0:1, :]
    meta = jnp.zeros((ROUTE_TILE, LANES), F32)
    gate = jnp.zeros((ROUTE_TILE, LANES), F32)
    for k in range(TOP_K):
        rank_k = jnp.sum(jnp.where(sels[k], ranks, 0.0), axis=1, keepdims=True)
        meta = jnp.where(lane_i == LANE_E + k, idxs[k], meta)
        meta = jnp.where(lane_i == LANE_RANK + k, rank_k, meta)
        gate = jnp.where(lane_i == k, weights[k] * inv, gate)
    meta_ref[...] = meta.astype(jnp.int32)
    gate_ref[...] = gate
    carry_scr[0:1, :] = carry_scr[0:1, :] + jnp.sum(onehot, axis=0, keepdims=True)
    cnt_ref[...] = carry_scr[...]


def _route(logits):
    t = logits.shape[0]
    return pl.pallas_call(
        _route_body,
        grid=(t // ROUTE_TILE,),
        in_specs=[pl.BlockSpec((ROUTE_TILE, LANES), lambda i: (i, 0))],
        out_specs=[
            pl.BlockSpec((ROUTE_TILE, LANES), lambda i: (i, 0)),
            pl.BlockSpec((ROUTE_TILE, LANES), lambda i: (i, 0)),
            pl.BlockSpec((SUBLANES, LANES), lambda i: (0, 0)),
        ],
        out_shape=[
            jax.ShapeDtypeStruct((t, LANES), jnp.int32),
            jax.ShapeDtypeStruct((t, LANES), F32),
            jax.ShapeDtypeStruct((SUBLANES, LANES), F32),
        ],
        scratch_shapes=[pltpu.VMEM((ROUTE_TILE, ROUTE_TILE), BF16), pltpu.VMEM((SUBLANES, LANES), F32)],
        compiler_params=_params("arbitrary"),
        name="route",
    )(logits)


def _slot_tables(counts, n_blocks):
    padded = (counts + MOE_BLOCK_M - 1) // MOE_BLOCK_M * MOE_BLOCK_M
    pends = jnp.cumsum(padded)
    first_slot = jnp.arange(n_blocks, dtype=pends.dtype) * MOE_BLOCK_M
    block_e = jnp.minimum(jnp.sum(pends[None, :] <= first_slot[:, None], axis=1), N_EXPERTS - 1).astype(jnp.int32)
    n_used = (pends[-1] // MOE_BLOCK_M).astype(jnp.int32).reshape(1)
    return (pends - padded).astype(jnp.int32), pends.astype(jnp.int32), padded.astype(jnp.int32), block_e, n_used


DISPATCH_TILE = 1024
IDX_SLOTS = 3


def _dispatch_body(pend_ref, padded_ref, dest_hbm, h_ref, xs_hbm, idx_smem, zero_scr, idx_sem, row_sem, zero_sem):
    i = pl.program_id(0)
    n = pl.num_programs(0)

    width = DISPATCH_TILE * TOP_K

    def idx_copy(tile):
        s = lax.rem(tile, IDX_SLOTS)
        return pltpu.make_async_copy(dest_hbm.at[pl.ds(pl.multiple_of(tile * width, width), width)],
                                     idx_smem.at[pl.ds(pl.multiple_of(s * width, width), width)], idx_sem.at[s])

    def zero_copy(e):
        start = pl.multiple_of(pend_ref[e] - MOE_BLOCK_M, MOE_BLOCK_M)
        return pltpu.make_async_copy(zero_scr, xs_hbm.at[pl.ds(start, MOE_BLOCK_M), :], zero_sem)

    def row_copy(g, j, d):
        return pltpu.make_async_copy(h_ref.at[g, pl.ds(j, 1), :], xs_hbm.at[pl.ds(d, 1), :], row_sem)

    @pl.when(i == 0)
    def _():
        zero_scr[...] = jnp.zeros_like(zero_scr)
        for e in range(N_EXPERTS):
            @pl.when(padded_ref[e] > 0)
            def _():
                zero_copy(e).start()
        for e in range(N_EXPERTS):
            @pl.when(padded_ref[e] > 0)
            def _():
                zero_copy(e).wait()
        idx_copy(0).start()

    @pl.when(i + 1 < n)
    def _():
        idx_copy(i + 1).start()

    idx_copy(i).wait()
    base = lax.rem(i, IDX_SLOTS) * width

    def issue(g, c):
        for j in range(SUBLANES):
            for k in range(TOP_K):
                d = idx_smem[base + g * (SUBLANES * TOP_K) + (j * TOP_K + k)]
                row_copy(g, j, d).start(priority=k % 2)
        return c

    lax.fori_loop(0, DISPATCH_TILE // SUBLANES, issue, 0)
    for _ in range(TOP_K):
        pltpu.make_async_copy(h_ref, h_ref, row_sem).wait()


def _scatter_rows(h, dest, pends, padded, n_slots):
    t = h.shape[0]
    tiles = t // DISPATCH_TILE
    grid_spec = pltpu.PrefetchScalarGridSpec(
        num_scalar_prefetch=2,
        grid=(tiles,),
        in_specs=[
            pl.BlockSpec(memory_space=pl.ANY),
            pl.BlockSpec((DISPATCH_TILE // SUBLANES, SUBLANES, D_MODEL), lambda i, pe, pd: (i, 0, 0)),
        ],
        out_specs=pl.BlockSpec(memory_space=pl.ANY),
        scratch_shapes=[
            pltpu.SMEM((IDX_SLOTS * DISPATCH_TILE * TOP_K,), jnp.int32),
            pltpu.VMEM((MOE_BLOCK_M, D_MODEL), F32),
            pltpu.SemaphoreType.DMA((IDX_SLOTS,)),
            pltpu.SemaphoreType.DMA(()),
            pltpu.SemaphoreType.DMA(()),
        ],
    )
    return pl.pallas_call(
        _dispatch_body,
        grid_spec=grid_spec,
        out_shape=jax.ShapeDtypeStruct((n_slots, D_MODEL), F32),
        compiler_params=_params("arbitrary"),
        name="dispatch",
    )(pends, padded, dest.reshape(-1), h.reshape(t // SUBLANES, SUBLANES, D_MODEL))


COMBINE_TILE = 256


def _combine_body(tile_off, dest_hbm, x1_ref, gate_ref, gf_ref, ys_hbm, y_ref, idx_smem, buf, idx_sem, row_sem):
    i = pl.program_id(0)
    n = pl.num_programs(0)

    width = COMBINE_TILE * TOP_K

    def idx_copy(tile):
        s = lax.rem(tile, IDX_SLOTS)
        return pltpu.make_async_copy(dest_hbm.at[pl.ds(pl.multiple_of((tile_off + tile) * width, width), width)],
                                     idx_smem.at[pl.ds(pl.multiple_of(s * width, width), width)], idx_sem.at[s])

    def row_copy(d, b, k, g, j):
        return pltpu.make_async_copy(ys_hbm.at[pl.ds(d, 1), :], buf.at[b, k, g, pl.ds(j, 1), :], row_sem.at[b])

    def issue_rows(tile):
        base = lax.rem(tile, IDX_SLOTS) * width
        b = lax.rem(tile, 2)

        def body(g, c):
            for j in range(SUBLANES):
                for k in range(TOP_K):
                    d = idx_smem[base + g * (SUBLANES * TOP_K) + (j * TOP_K + k)]
                    row_copy(d, b, k, g, j).start(priority=k % 2)
            return c

        lax.fori_loop(0, COMBINE_TILE // SUBLANES, body, 0)

    @pl.when(i == 0)
    def _():
        idx_copy(0).start()
        idx_copy(0).wait()
        issue_rows(0)

        @pl.when(n > 1)
        def _():
            idx_copy(1).start()

    @pl.when(i + 1 < n)
    def _():
        idx_copy(i + 1).wait()
        issue_rows(i + 1)

    @pl.when(i + 2 < n)
    def _():
        idx_copy(i + 2).start()

    b = lax.rem(i, 2)

    pltpu.make_async_copy(buf.at[b], buf.at[b], row_sem.at[b]).wait()
    gate = gate_ref[...]
    acc = x1_ref[...]
    for k in range(TOP_K):
        acc = acc + buf[b, k].reshape(COMBINE_TILE, D_MODEL) * gate[:, k:k + 1]
    y_ref[...] = acc * lax.rsqrt(jnp.mean(acc * acc, axis=-1, keepdims=True) + RMS_EPS) * gf_ref[...]


def _combine(x1, gate, dest, ys, g_final, row_off, rows):
    t = x1.shape[0]
    tile_off = row_off // COMBINE_TILE
    return pl.pallas_call(
        functools.partial(_combine_body, tile_off),
        grid=(rows // COMBINE_TILE,),
        in_specs=[
            pl.BlockSpec(memory_space=pl.ANY),
            pl.BlockSpec((COMBINE_TILE, D_MODEL), lambda i: (tile_off + i, 0)),
            pl.BlockSpec((COMBINE_TILE, LANES), lambda i: (tile_off + i, 0)),
            pl.BlockSpec((1, D_MODEL), lambda i: (0, 0)),
            pl.BlockSpec(memory_space=pl.ANY),
        ],
        out_specs=pl.BlockSpec((COMBINE_TILE, D_MODEL), lambda i: (i, 0)),
        out_shape=jax.ShapeDtypeStruct((rows, D_MODEL), F32),
        scratch_shapes=[
            pltpu.SMEM((IDX_SLOTS * COMBINE_TILE * TOP_K,), jnp.int32),
            pltpu.VMEM((2, TOP_K, COMBINE_TILE // SUBLANES, SUBLANES, D_MODEL), F32),
            pltpu.SemaphoreType.DMA((IDX_SLOTS,)),
            pltpu.SemaphoreType.DMA((2,)),
        ],
        compiler_params=_params("arbitrary"),
        name="combine",
    )(dest.reshape(-1), x1, gate, g_final.reshape(1, D_MODEL), ys)


def _lane_row(v):
    return jnp.pad(v.astype(F32), (0, LANES - v.shape[0])).reshape(1, LANES)


def kernel(x_prompt, x_sample, cache_swa_k, cache_swa_v, state_mlstm_C, state_mlstm_n, state_mlstm_m,
           g_mix, w_in, b_if, attn_sinks, mlstm_norm_w, w_attn_proj, w_mlstm_proj, w_out,
           g_ffn, w_router, b_router, w_e1, b_e1, w_e2, b_e2, g_final):
    bp, sp = x_prompt.shape[0], x_prompt.shape[1]
    bs, ls = x_sample.shape[0], x_sample.shape[1]
    tp = bp * sp
    ts = bs * SAMPLE_ROWS
    t_all = tp + ts
    l = 0
    w_in_p = _permute_w_in(w_in[l])
    wa, wm, wo = w_attn_proj[l].astype(BF16), w_mlstm_proj[l].astype(BF16), w_out[l].astype(BF16)
    wr_f32 = jnp.pad(w_router[l], ((0, 0), (0, ROUTER_PAD - N_EXPERTS)))
    wr_hi = wr_f32.astype(BF16)
    wr_pad = jnp.concatenate([wr_hi, (wr_f32 - wr_hi.astype(F32)).astype(BF16)], axis=1)
    br_pad = _lane_row(b_router[l])
    b1, b2 = b_e1[l].reshape(N_EXPERTS, 1, 2 * D_FF), b_e2[l].reshape(N_EXPERTS, 1, D_MODEL)
    sinks_row = _lane_row(attn_sinks[l])
    bias_row = _lane_row(b_if[l])
    norm_w_row = mlstm_norm_w[l].reshape(1, M_V_WIDTH)

    xp = x_prompt.reshape(tp, D_MODEL)
    xs = jnp.pad(x_sample, ((0, 0), (0, SAMPLE_ROWS - ls), (0, 0))).reshape(ts, D_MODEL)
    z_all = _inproj(xp, xs, g_mix[l], w_in_p, 512)

    a_p, k1, v1, w1 = _attn_prompt(z_all, sinks_row, bp, sp, w_e1[l])
    a_s, k2, v2 = _attn_sample(z_all, sinks_row, cache_swa_k[l].reshape(bs, WINDOW, KV_WIDTH),
                               cache_swa_v[l].reshape(bs, WINDOW, KV_WIDTH), tp, ls)
    mh_p, c1, n1, m1, w2 = _mlstm_prompt(z_all, bias_row, norm_w_row, bp, sp, w_e2[l])
    mh_s, c2, n2, m2 = _mlstm_sample(z_all, bias_row, norm_w_row, state_mlstm_C[l], state_mlstm_n[l],
                                     state_mlstm_m[l], tp, ls)

    x1, h2, logits = _merge(xp, xs, a_p, a_s, mh_p, mh_s, z_all, wa, wm, wo, g_ffn[l], wr_pad, br_pad, 256)

    meta, gate, counts = _route(logits)
    n_blocks = -(-t_all * TOP_K // MOE_BLOCK_M) + N_EXPERTS
    pstarts, pends, padded, block_e, n_used = _slot_tables(counts[0, :N_EXPERTS].astype(jnp.int32), n_blocks)
    dest = pstarts[meta[:, LANE_E:LANE_E + TOP_K]] + meta[:, LANE_RANK:LANE_RANK + TOP_K]
    xs_sorted = _scatter_rows(h2, dest, pends, padded, n_blocks * MOE_BLOCK_M)
    ys = _experts(block_e, n_used, xs_sorted, w1, b1, w2, b2)
    y_prompt = _combine(x1, gate, dest, ys, g_final, 0, tp).reshape(bp, sp, D_MODEL)
    y_sample = _combine(x1, gate, dest, ys, g_final, tp, ts).reshape(bs, SAMPLE_ROWS, D_MODEL)[:, :ls]
    k1 = k1.reshape(bp, WINDOW, N_KV_HEADS, HEAD_DIM)
    v1 = v1.reshape(bp, WINDOW, N_KV_HEADS, HEAD_DIM)
    k2 = k2.reshape(bs, WINDOW, N_KV_HEADS, HEAD_DIM)
    v2 = v2.reshape(bs, WINDOW, N_KV_HEADS, HEAD_DIM)
    m1 = m1[:, :, 0]
    st = lambda t: t[None]
    return (y_prompt, y_sample, st(k1), st(v1), st(c1), st(n1), st(m1), st(k2), st(v2), st(c2), st(n2), st(m2))
```

```python
import functools

import jax
import jax.numpy as jnp
from jax import lax
from jax.experimental import pallas as pl
from jax.experimental.pallas import tpu as pltpu

D_MODEL = 2048
N_HEADS = 16
N_KV_HEADS = 2
HEAD_DIM = 64
Q_PER_KV = N_HEADS // N_KV_HEADS
WINDOW = 128
ATTN_WIDTH = N_HEADS * HEAD_DIM
KV_WIDTH = N_KV_HEADS * HEAD_DIM
M_HEADS = 4
M_QK_DIM = 128
M_V_DIM = 256
M_QK_WIDTH = M_HEADS * M_QK_DIM
M_V_WIDTH = M_HEADS * M_V_DIM
M_CHUNK = 64
N_EXPERTS = 32
TOP_K = 4
D_FF = D_MODEL
SWIGLU_LIMIT = 7.0
SWIGLU_ALPHA = 1.702
RMS_EPS = 1e-5
NEG = -1e30

F32 = jnp.float32
BF16 = jnp.bfloat16

LANES = 128
SUBLANES = 8
VMEM_LIMIT_BYTES = 56 * 1024 * 1024

COL_GZ = 0
COL_AQ = COL_GZ + 2 * D_MODEL
COL_MV = COL_AQ + ATTN_WIDTH
COL_MO = COL_MV + M_V_WIDTH
COL_MQ = COL_MO + M_V_WIDTH
COL_MK = COL_MQ + M_QK_WIDTH
COL_AK = COL_MK + M_QK_WIDTH
COL_IF = COL_AK + 2 * KV_WIDTH
IN_TILE_N = 2176
IN_WIDTH_PAD = -(-(COL_IF + LANES) // IN_TILE_N) * IN_TILE_N

SAMPLE_ROWS = SUBLANES
MOE_BLOCK_M = 512
MOE_FF_CHUNK = 1024
ROUTER_PAD = LANES


def _permute_w_in(w_in):
    splits = (ATTN_WIDTH, KV_WIDTH, KV_WIDTH, M_QK_WIDTH, M_QK_WIDTH, M_V_WIDTH, M_V_WIDTH, M_HEADS, M_HEADS,
              2 * D_MODEL)
    parts, off = [], 0
    for width in splits:
        parts.append(w_in[:, off:off + width])
        off += width
    aq, ak, av, mq, mk, mv, mo, mi, mf, gz = parts
    cols = jnp.concatenate([gz, aq, mv, mo, mq, mk, ak, av, mi, mf], axis=1)
    cols = jnp.pad(cols, ((0, 0), (0, IN_WIDTH_PAD - cols.shape[1])))
    return cols.astype(BF16)


def _params(*semantics):
    return pltpu.CompilerParams(dimension_semantics=semantics, vmem_limit_bytes=VMEM_LIMIT_BYTES)


def _log_sigmoid(x):
    return jnp.minimum(x, 0.0) - jnp.log1p(jnp.exp(-jnp.abs(x)))


def _inproj_body(tiles_a, xa_ref, xb_ref, g_ref, w_ref, z_ref, h_scr):
    def normalise(x_ref):
        x = x_ref[...]
        y = x * lax.rsqrt(jnp.mean(x * x, axis=-1, keepdims=True) + RMS_EPS)
        h_scr[...] = (y * g_ref[...]).astype(BF16)

    first = pl.program_id(1) == 0
    in_a = pl.program_id(0) < tiles_a

    @pl.when(first & in_a)
    def _():
        normalise(xa_ref)

    @pl.when(first & jnp.logical_not(in_a))
    def _():
        normalise(xb_ref)

    z_ref[...] = jnp.dot(h_scr[...], w_ref[...], preferred_element_type=F32)


def _inproj(x_a, x_b, g, w_bf16, tile_m):
    tiles_a, tiles_b = x_a.shape[0] // tile_m, x_b.shape[0] // tile_m
    n = w_bf16.shape[1]
    return pl.pallas_call(
        functools.partial(_inproj_body, tiles_a),
        grid=(tiles_a + tiles_b, n // IN_TILE_N),
        in_specs=[
            pl.BlockSpec((tile_m, D_MODEL), lambda i, j: (jnp.minimum(i, tiles_a - 1), 0)),
            pl.BlockSpec((tile_m, D_MODEL), lambda i, j: (jnp.maximum(i - tiles_a, 0), 0)),
            pl.BlockSpec((1, D_MODEL), lambda i, j: (0, 0)),
            pl.BlockSpec((D_MODEL, IN_TILE_N), lambda i, j: (0, j)),
        ],
        out_specs=pl.BlockSpec((tile_m, IN_TILE_N), lambda i, j: (i, j)),
        out_shape=jax.ShapeDtypeStruct(((tiles_a + tiles_b) * tile_m, n), F32),
        scratch_shapes=[pltpu.VMEM((tile_m, D_MODEL), BF16)],
        compiler_params=_params("arbitrary", "arbitrary"),
        name="inproj",
    )(x_a, x_b, g.reshape(1, D_MODEL), w_bf16)


assert KV_WIDTH == LANES and N_KV_HEADS == 2 and LANES == 2 * HEAD_DIM
Q_TILES = ATTN_WIDTH // LANES
TILES_PER_KV = Q_TILES // N_KV_HEADS
_NT = (((1,), (1,)), ((), ()))


def _kv_placements(x):
    lo = lax.broadcasted_iota(jnp.int32, x.shape, 1) < HEAD_DIM
    swapped = pltpu.roll(x, HEAD_DIM, axis=1)
    return ((jnp.where(lo, x, 0.0).astype(BF16), jnp.where(lo, 0.0, swapped).astype(BF16)),
            (jnp.where(lo, swapped, 0.0).astype(BF16), jnp.where(lo, 0.0, x).astype(BF16)))


def _slab_head(g, p, j):
    return g * Q_PER_KV + 2 * j + p


ATTN_BLOCKS = 2


def _attn_block(q, kv_cur, kv_prev, has_prev, sink_ref, a_ref, rows):
    kc, vc = kv_cur[:, :KV_WIDTH], kv_cur[:, KV_WIDTH:]
    kk = _kv_placements(jnp.concatenate([kv_prev[:, :KV_WIDTH], kc], axis=0))
    vv = _kv_placements(jnp.concatenate([kv_prev[:, KV_WIDTH:], vc], axis=0))
    q = q.astype(BF16)
    row = lax.broadcasted_iota(jnp.int32, (WINDOW, WINDOW), 0)
    col = lax.broadcasted_iota(jnp.int32, (WINDOW, WINDOW), 1)
    mask = jnp.concatenate([(col > row) & has_prev, col <= row], axis=1)
    slabs, sinks = [], []
    for g in range(N_KV_HEADS):
        qt = jnp.concatenate([q[:, (TILES_PER_KV * g + j) * LANES:(TILES_PER_KV * g + j + 1) * LANES]
                              for j in range(TILES_PER_KV)], axis=0)
        for p in range(2):
            s = lax.dot_general(qt, kk[g][p], _NT, preferred_element_type=F32) * (HEAD_DIM ** -0.5)
            for j in range(TILES_PER_KV):
                h = _slab_head(g, p, j)
                slabs.append(jnp.where(mask, s[j * WINDOW:(j + 1) * WINDOW], NEG))
                sinks.append(jnp.broadcast_to(sink_ref[:, h:h + 1], (WINDOW, LANES)))
    s_all = jnp.concatenate(slabs, axis=0)
    sink = jnp.concatenate(sinks, axis=0)
    m = jnp.maximum(jnp.broadcast_to(jnp.max(s_all, axis=1, keepdims=True), sink.shape), sink)
    p_all = jnp.exp(s_all - jnp.concatenate([m, m], axis=1))
    p_hi = p_all.astype(BF16)
    p_lo = (p_all - p_hi.astype(F32)).astype(BF16)
    ones = jnp.ones((2 * WINDOW, LANES), BF16)
    denom = (jnp.dot(p_hi, ones, preferred_element_type=F32) + jnp.dot(p_lo, ones, preferred_element_type=F32)
             + jnp.exp(sink - m))
    inv = 1.0 / denom
    pn = (p_all * jnp.concatenate([inv, inv], axis=1)).astype(BF16)
    half = TILES_PER_KV * WINDOW
    for g in range(N_KV_HEADS):
        base = 2 * half * g
        o = (jnp.dot(pn[base:base + half], vv[g][0], preferred_element_type=F32)
             + jnp.dot(pn[base + half:base + 2 * half], vv[g][1], preferred_element_type=F32))
        for j in range(TILES_PER_KV):
            t = TILES_PER_KV * g + j
            a_ref[rows, t * LANES:(t + 1) * LANES] = o[j * WINDOW:(j + 1) * WINDOW].astype(BF16)


def _attn_prompt_body(q_ref, kvc_ref, kvp_ref, sink_ref, w_ref, a_ref, k_ref, v_ref, wb_ref):
    wb_ref[...] = w_ref[...].astype(BF16)
    n = pl.program_id(1)
    for u in range(ATTN_BLOCKS):
        rows = slice(u * WINDOW, (u + 1) * WINDOW)
        if u == 0:
            kv_prev, has_prev = kvp_ref[...], n > 0
        else:
            kv_prev, has_prev = kvc_ref[(u - 1) * WINDOW:u * WINDOW, :], True
        _attn_block(q_ref[rows, :], kvc_ref[rows, :], kv_prev, has_prev, sink_ref, a_ref, rows)
    last = slice((ATTN_BLOCKS - 1) * WINDOW, ATTN_BLOCKS * WINDOW)
    k_ref[...] = kvc_ref[last, :KV_WIDTH]
    v_ref[...] = kvc_ref[last, KV_WIDTH:]


def _cast_slices(w, steps):
    rows = w.reshape(-1, w.shape[-1])
    per_step = rows.shape[0] // steps
    assert per_step * steps == rows.shape[0] and per_step % 16 == 0
    return rows, per_step


def _attn_prompt(z_all, sinks_row, batch, seq, w):
    span = ATTN_BLOCKS * WINDOW
    nb = seq // span
    w_rows, r1 = _cast_slices(w, batch * nb)
    step = lambda b, n: (b * nb + n, 0)
    a, k, v, wb = pl.pallas_call(
        _attn_prompt_body,
        grid=(batch, nb),
        in_specs=[
            pl.BlockSpec((span, ATTN_WIDTH), lambda b, n: (b * nb + n, COL_AQ // ATTN_WIDTH)),
            pl.BlockSpec((span, 2 * KV_WIDTH), lambda b, n: (b * nb + n, COL_AK // (2 * KV_WIDTH))),
            pl.BlockSpec((WINDOW, 2 * KV_WIDTH),
                         lambda b, n: (jnp.maximum((b * nb + n) * ATTN_BLOCKS - 1, 0), COL_AK // (2 * KV_WIDTH))),
            pl.BlockSpec((1, LANES), lambda b, n: (0, 0)),
            pl.BlockSpec((r1, w_rows.shape[1]), step),
        ],
        out_specs=[
            pl.BlockSpec((span, ATTN_WIDTH), lambda b, n: (b * nb + n, 0)),
            pl.BlockSpec((None, WINDOW, KV_WIDTH), lambda b, n: (b, 0, 0)),
            pl.BlockSpec((None, WINDOW, KV_WIDTH), lambda b, n: (b, 0, 0)),
            pl.BlockSpec((r1, w_rows.shape[1]), step),
        ],
        out_shape=[
            jax.ShapeDtypeStruct((batch * seq, ATTN_WIDTH), BF16),
            jax.ShapeDtypeStruct((batch, WINDOW, KV_WIDTH), F32),
            jax.ShapeDtypeStruct((batch, WINDOW, KV_WIDTH), F32),
            jax.ShapeDtypeStruct(w_rows.shape, BF16),
        ],
        compiler_params=_params("arbitrary", "arbitrary"),
        name="attn_prompt",
    )(z_all, z_all, z_all, sinks_row, w_rows)
    return a, k, v, wb.reshape(w.shape)


ATTN_SAMPLE_GROUP = 8


def _attn_sample_body(n_valid, q_ref, kv_ref, ck_ref, cv_ref, sink_ref, a_ref, nk_ref, nv_ref, o_scr):
    rs = SAMPLE_ROWS
    slab = TILES_PER_KV * rs
    r8 = lax.broadcasted_iota(jnp.int32, (rs, KV_WIDTH), 0)
    qi_c = jnp.bitwise_and(lax.broadcasted_iota(jnp.int32, (slab, WINDOW), 0), rs - 1)
    mask_c = lax.broadcasted_iota(jnp.int32, (slab, WINDOW), 1) > qi_c
    qi_n = jnp.bitwise_and(lax.broadcasted_iota(jnp.int32, (slab, rs), 0), rs - 1)
    mask_n = lax.broadcasted_iota(jnp.int32, (slab, rs), 1) <= qi_n
    sc_list, sn_list, sink_list, values = [], [], [], []
    for b in range(ATTN_SAMPLE_GROUP):
        rows = slice(b * rs, (b + 1) * rs)
        q = q_ref[rows, :]
        kn, vn = kv_ref[rows, :KV_WIDTH], kv_ref[rows, KV_WIDTH:]
        ck, cv = ck_ref[b], cv_ref[b]
        for src, new, dst in ((ck, kn, nk_ref), (cv, vn, nv_ref)):
            rolled = pltpu.roll(src, WINDOW - n_valid, axis=0)
            dst[b] = rolled
            tail = jnp.where(r8 >= rs - n_valid, pltpu.roll(new, rs - n_valid, axis=0), rolled[WINDOW - rs:, :])
            dst[b, WINDOW - rs:, :] = tail
        kk_c, kk_n = _kv_placements(ck), _kv_placements(kn)
        values.append((_kv_placements(cv), _kv_placements(vn)))
        for g in range(N_KV_HEADS):
            qt = jnp.concatenate([q[:, (TILES_PER_KV * g + j) * LANES:(TILES_PER_KV * g + j + 1) * LANES]
                                  for j in range(TILES_PER_KV)], axis=0).astype(BF16)
            for p in range(2):
                s_c = lax.dot_general(qt, kk_c[g][p], _NT, preferred_element_type=F32) * (HEAD_DIM ** -0.5)
                s_n = lax.dot_general(qt, kk_n[g][p], _NT, preferred_element_type=F32) * (HEAD_DIM ** -0.5)
                sc_list.append(jnp.where(mask_c, s_c, NEG))
                sn_list.append(jnp.where(mask_n, s_n, NEG))
                for j in range(TILES_PER_KV):
                    h = _slab_head(g, p, j)
                    sink_list.append(jnp.broadcast_to(sink_ref[:, h:h + 1], (rs, 1)))
    s_c = jnp.concatenate(sc_list, axis=0)
    s_n = jnp.concatenate(sn_list, axis=0)
    sink = jnp.concatenate(sink_list, axis=0)
    m = jnp.maximum(jnp.maximum(jnp.max(s_c, axis=1, keepdims=True), jnp.max(s_n, axis=1, keepdims=True)), sink)
    p_c = jnp.exp(s_c - m)
    p_n = jnp.exp(s_n - m)
    inv = 1.0 / (jnp.sum(p_c, axis=1, keepdims=True) + jnp.sum(p_n, axis=1, keepdims=True) + jnp.exp(sink - m))
    p_c = (p_c * inv).astype(BF16)
    p_n = (p_n * inv).astype(BF16)
    for b in range(ATTN_SAMPLE_GROUP):
        vv_c, vv_n = values[b]
        for g in range(N_KV_HEADS):
            base = (b * N_KV_HEADS + g) * 2 * slab
            o = (jnp.dot(p_c[base:base + slab], vv_c[g][0], preferred_element_type=F32)
                 + jnp.dot(p_c[base + slab:base + 2 * slab], vv_c[g][1], preferred_element_type=F32)
                 + jnp.dot(p_n[base:base + slab], vv_n[g][0], preferred_element_type=F32)
                 + jnp.dot(p_n[base + slab:base + 2 * slab], vv_n[g][1], preferred_element_type=F32))
            for j in range(TILES_PER_KV):
                t = TILES_PER_KV * g + j
                o_scr[b * rs:(b + 1) * rs, t * LANES:(t + 1) * LANES] = o[j * rs:(j + 1) * rs]
    a_ref[...] = o_scr[...].astype(BF16)


def _attn_sample(z_all, sinks_row, cache_k, cache_v, row_off, n_valid):
    b = cache_k.shape[0]
    gr = ATTN_SAMPLE_GROUP
    rows = gr * SAMPLE_ROWS
    rb = row_off // rows
    return pl.pallas_call(
        functools.partial(_attn_sample_body, n_valid),
        grid=(b // gr,),
        in_specs=[
            pl.BlockSpec((rows, ATTN_WIDTH), lambda i: (rb + i, COL_AQ // ATTN_WIDTH)),
            pl.BlockSpec((rows, 2 * KV_WIDTH), lambda i: (rb + i, COL_AK // (2 * KV_WIDTH))),
            pl.BlockSpec((gr, WINDOW, KV_WIDTH), lambda i: (i, 0, 0)),
            pl.BlockSpec((gr, WINDOW, KV_WIDTH), lambda i: (i, 0, 0)),
            pl.BlockSpec((1, LANES), lambda i: (0, 0)),
        ],
        out_specs=[
            pl.BlockSpec((rows, ATTN_WIDTH), lambda i: (i, 0)),
            pl.BlockSpec((gr, WINDOW, KV_WIDTH), lambda i: (i, 0, 0)),
            pl.BlockSpec((gr, WINDOW, KV_WIDTH), lambda i: (i, 0, 0)),
        ],
        out_shape=[
            jax.ShapeDtypeStruct((b * SAMPLE_ROWS, ATTN_WIDTH), BF16),
            jax.ShapeDtypeStruct(cache_k.shape, F32),
            jax.ShapeDtypeStruct(cache_v.shape, F32),
        ],
        scratch_shapes=[pltpu.VMEM((rows, ATTN_WIDTH), F32)],
        compiler_params=_params("arbitrary"),
        name="attn_sample",
    )(z_all, z_all, cache_k, cache_v, sinks_row)


def _mlstm_gates(if_blk, bias_row, n_valid):
    length = if_blk.shape[0]
    g = if_blk + bias_row
    lane = lax.broadcasted_iota(jnp.int32, g.shape, 1)
    g = jnp.where(lane < M_HEADS, g, _log_sigmoid(g))
    if n_valid < length:
        row = lax.broadcasted_iota(jnp.int32, g.shape, 0)
        g = jnp.where(row < n_valid, g, jnp.where(lane < M_HEADS, NEG, 0.0))
    return g


def _lanes(col, width):
    if width <= LANES:
        return col[:, :width]
    return jnp.concatenate([col] * (width // LANES), axis=1)


def _row_stat(x, reduce):
    return jnp.broadcast_to(reduce(x, axis=1, keepdims=True), (x.shape[0], LANES))


def _mlstm_chunk(q, k, v, gates, gates_t, h, c, n, m):
    length = q.shape[0]
    i_rep = jnp.broadcast_to(gates[:, h:h + 1], (length, LANES))
    f_col = gates[:, M_HEADS + h:M_HEADS + h + 1]
    i_row = gates_t[h:h + 1, :]
    f_row = gates_t[M_HEADS + h:M_HEADS + h + 1, :]
    t_idx = lax.broadcasted_iota(jnp.int32, (length, length), 0)
    s_idx = lax.broadcasted_iota(jnp.int32, (length, length), 1)
    causal = s_idx <= t_idx
    bcum = _row_stat(jnp.where(causal, f_row, 0.0), jnp.sum)
    bcum_row = jnp.sum(jnp.where(t_idx <= s_idx, f_col, 0.0), axis=0, keepdims=True)
    dlog = jnp.where(causal, _lanes(bcum, length) - bcum_row + i_row, NEG)
    inter = bcum + m
    mt = jnp.maximum(inter, _row_stat(dlog, jnp.max))
    w_inter = jnp.exp(inter - mt)
    qb = q.astype(BF16)
    ks = k * (M_QK_DIM ** -0.5)
    vb = v.astype(BF16)
    qk = lax.dot_general(qb, ks.astype(BF16), (((1,), (1,)), ((), ())), preferred_element_type=F32)
    qk = qk * jnp.exp(dlog - _lanes(mt, length))
    num = (_lanes(w_inter, M_V_DIM) * jnp.dot(qb, c.astype(BF16), preferred_element_type=F32)
           + jnp.dot(qk.astype(BF16), vb, preferred_element_type=F32))
    den = w_inter * _row_stat(q * n, jnp.sum) + _row_stat(qk, jnp.sum)
    h_t = num * _lanes(1.0 / jnp.maximum(jnp.abs(den), jnp.exp(-mt)), M_V_DIM)
    m_new = mt[length - 1:length, 0:1]
    w_state = jnp.exp(bcum[length - 1:length, :] - bcum + i_rep - m_new)
    decay = jnp.exp(inter[length - 1:length, 0:1] - m_new)
    kw = ks * _lanes(w_state, M_QK_DIM)
    c_new = decay * c + jnp.dot(kw.T.astype(BF16), vb, preferred_element_type=F32)
    n_new = decay * n + jnp.sum(kw, axis=0, keepdims=True)
    return h_t, c_new, n_new, m_new


def _mlstm_head_out(h_t, norm_w, o_gate):
    scale = lax.rsqrt(_row_stat(h_t * h_t, jnp.sum) * (1.0 / h_t.shape[1]) + RMS_EPS)
    return h_t * _lanes(scale, h_t.shape[1]) * norm_w * jax.nn.sigmoid(o_gate)


MLSTM_SEG = 256


MLSTM_BATCH = 2
MLSTM_PROMPT_CHUNK = 256


def _mlstm_prompt_body(*refs):
    nb = MLSTM_BATCH
    ins, rest = refs[:5 * nb], refs[5 * nb:]
    bias_ref, nw_ref, w_ref, mh_ref, c_out, n_out, m_out, wb_ref, c_scr, n_scr, m_scr = rest
    seg = pl.program_id(1)
    wb_ref[...] = w_ref[...].astype(BF16)

    @pl.when(seg == 0)
    def _():
        c_scr[...] = jnp.zeros_like(c_scr)
        n_scr[...] = jnp.zeros_like(n_scr)
        m_scr[...] = jnp.full_like(m_scr, NEG)

    def chunk_body(ci, carry):
        r = pl.multiple_of(ci * MLSTM_PROMPT_CHUNK, MLSTM_PROMPT_CHUNK)
        rows = pl.ds(r, MLSTM_PROMPT_CHUNK)
        state = [[(c_scr[u, h], n_scr[u, h:h + 1, :], m_scr[u, h:h + 1, 0:1]) for h in range(M_HEADS)]
                 for u in range(nb)]
        results = []
        for u in range(nb):
            q_ref, k_ref, v_ref, o_ref, if_ref = ins[5 * u:5 * u + 5]
            gates = _mlstm_gates(if_ref[rows, :], bias_ref[...], MLSTM_PROMPT_CHUNK)
            gates_t = gates.T
            for h in range(M_HEADS):
                qs = slice(h * M_QK_DIM, (h + 1) * M_QK_DIM)
                vs = slice(h * M_V_DIM, (h + 1) * M_V_DIM)
                h_t, c_new, n_new, m_new = _mlstm_chunk(
                    q_ref[rows, qs], k_ref[rows, qs], v_ref[rows, vs], gates, gates_t, h, *state[u][h])
                out = _mlstm_head_out(h_t, nw_ref[:, vs], o_ref[rows, vs]).astype(BF16)
                results.append((u, h, vs, out, c_new, n_new, m_new))
        for u, h, vs, out, c_new, n_new, m_new in results:
            c_scr[u, h] = c_new
            n_scr[u, h:h + 1, :] = n_new
            m_scr[u, h:h + 1, :] = jnp.broadcast_to(m_new, (1, LANES))
            mh_ref[u, rows, vs] = out
        return carry

    lax.fori_loop(0, MLSTM_SEG // MLSTM_PROMPT_CHUNK, chunk_body, 0)

    @pl.when(seg == pl.num_programs(1) - 1)
    def _():
        c_out[...] = c_scr[...]
        n_out[...] = n_scr[:, 0:M_HEADS, :]
        m_out[...] = m_scr[:, 0:M_HEADS, :]


def _mlstm_prompt(z_all, bias_row, norm_w_row, batch, seq, w):
    ns = seq // MLSTM_SEG
    nb = MLSTM_BATCH
    w_rows, wr = _cast_slices(w, (batch // nb) * ns)
    in_specs, args = [], []
    for u in range(nb):
        row = lambda p, s, u=u: (p * nb + u) * ns + s
        in_specs += [
            pl.BlockSpec((MLSTM_SEG, M_QK_WIDTH), lambda p, s, row=row: (row(p, s), COL_MQ // M_QK_WIDTH)),
            pl.BlockSpec((MLSTM_SEG, M_QK_WIDTH), lambda p, s, row=row: (row(p, s), COL_MK // M_QK_WIDTH)),
            pl.BlockSpec((MLSTM_SEG, M_V_WIDTH), lambda p, s, row=row: (row(p, s), COL_MV // M_V_WIDTH)),
            pl.BlockSpec((MLSTM_SEG, M_V_WIDTH), lambda p, s, row=row: (row(p, s), COL_MO // M_V_WIDTH)),
            pl.BlockSpec((MLSTM_SEG, LANES), lambda p, s, row=row: (row(p, s), COL_IF // LANES)),
        ]
        args += [z_all] * 5
    in_specs += [pl.BlockSpec((1, LANES), lambda p, s: (0, 0)), pl.BlockSpec((1, M_V_WIDTH), lambda p, s: (0, 0)),
                 pl.BlockSpec((wr, w_rows.shape[1]), lambda p, s: (p * ns + s, 0))]
    mh, c, n, m, wb = pl.pallas_call(
        _mlstm_prompt_body,
        grid=(batch // nb, ns),
        in_specs=in_specs,
        out_specs=[
            pl.BlockSpec((None, nb, MLSTM_SEG, M_V_WIDTH), lambda p, s: (p, 0, s, 0)),
            pl.BlockSpec((nb, M_HEADS, M_QK_DIM, M_V_DIM), lambda p, s: (p, 0, 0, 0)),
            pl.BlockSpec((nb, M_HEADS, M_QK_DIM), lambda p, s: (p, 0, 0)),
            pl.BlockSpec((nb, M_HEADS, LANES), lambda p, s: (p, 0, 0)),
            pl.BlockSpec((wr, w_rows.shape[1]), lambda p, s: (p * ns + s, 0)),
        ],
        out_shape=[
            jax.ShapeDtypeStruct((batch // nb, nb, seq, M_V_WIDTH), BF16),
            jax.ShapeDtypeStruct((batch, M_HEADS, M_QK_DIM, M_V_DIM), F32),
            jax.ShapeDtypeStruct((batch, M_HEADS, M_QK_DIM), F32),
            jax.ShapeDtypeStruct((batch, M_HEADS, LANES), F32),
            jax.ShapeDtypeStruct(w_rows.shape, BF16),
        ],
        scratch_shapes=[
            pltpu.VMEM((nb, M_HEADS, M_QK_DIM, M_V_DIM), F32),
            pltpu.VMEM((nb, SUBLANES, M_QK_DIM), F32),
            pltpu.VMEM((nb, SUBLANES, LANES), F32),
        ],
        compiler_params=_params("arbitrary", "arbitrary"),
        name="mlstm_prompt",
    )(*args, bias_row, norm_w_row, w_rows)
    return mh.reshape(batch * seq, M_V_WIDTH), c, n, m, wb.reshape(w.shape)


MLSTM_SAMPLE_GROUP = 8


def _mlstm_sample_body(n_valid, q_ref, k_ref, v_ref, o_ref, if_ref, bias_ref, nw_ref, c0_ref, n0_ref, m0_ref,
                       mh_ref, c_out, n_out, m_out, h_scr):
    for g in range(MLSTM_SAMPLE_GROUP):
        rows = slice(g * SAMPLE_ROWS, (g + 1) * SAMPLE_ROWS)
        gates = _mlstm_gates(if_ref[rows, :], bias_ref[...], n_valid)
        gates_t = jnp.transpose(jnp.concatenate([gates] * (LANES // SAMPLE_ROWS), axis=0))[:, :SAMPLE_ROWS]
        for h in range(M_HEADS):
            qs = slice(h * M_QK_DIM, (h + 1) * M_QK_DIM)
            vs = slice(h * M_V_DIM, (h + 1) * M_V_DIM)
            h_t, c_new, n_new, m_new = _mlstm_chunk(
                q_ref[rows, qs], k_ref[rows, qs], v_ref[rows, vs], gates, gates_t, h,
                c0_ref[g, h], n0_ref[g, h:h + 1, :], m0_ref[g:g + 1, h:h + 1])
            c_out[g, h] = c_new
            n_out[g, h:h + 1, :] = n_new
            m_out[g:g + 1, h:h + 1] = m_new
            h_scr[rows, vs] = _mlstm_head_out(h_t, nw_ref[:, vs], o_ref[rows, vs])
    mh_ref[...] = h_scr[...].astype(BF16)


def _mlstm_sample(z_all, bias_row, norm_w_row, c0, n0, m0, row_off, n_valid):
    b = c0.shape[0]
    gr = MLSTM_SAMPLE_GROUP
    rows = gr * SAMPLE_ROWS
    rb = row_off // rows
    m0g = m0.reshape(b // gr, gr, M_HEADS)
    outs = pl.pallas_call(
        functools.partial(_mlstm_sample_body, n_valid),
        grid=(b // gr,),
        in_specs=[
            pl.BlockSpec((rows, M_QK_WIDTH), lambda i: (rb + i, COL_MQ // M_QK_WIDTH)),
            pl.BlockSpec((rows, M_QK_WIDTH), lambda i: (rb + i, COL_MK // M_QK_WIDTH)),
            pl.BlockSpec((rows, M_V_WIDTH), lambda i: (rb + i, COL_MV // M_V_WIDTH)),
            pl.BlockSpec((rows, M_V_WIDTH), lambda i: (rb + i, COL_MO // M_V_WIDTH)),
            pl.BlockSpec((rows, LANES), lambda i: (rb + i, COL_IF // LANES)),
            pl.BlockSpec((1, LANES), lambda i: (0, 0)),
            pl.BlockSpec((1, M_V_WIDTH), lambda i: (0, 0)),
            pl.BlockSpec((gr, M_HEADS, M_QK_DIM, M_V_DIM), lambda i: (i, 0, 0, 0)),
            pl.BlockSpec((gr, M_HEADS, M_QK_DIM), lambda i: (i, 0, 0)),
            pl.BlockSpec((None, gr, M_HEADS), lambda i: (i, 0, 0)),
        ],
        out_specs=[
            pl.BlockSpec((rows, M_V_WIDTH), lambda i: (i, 0)),
            pl.BlockSpec((gr, M_HEADS, M_QK_DIM, M_V_DIM), lambda i: (i, 0, 0, 0)),
            pl.BlockSpec((gr, M_HEADS, M_QK_DIM), lambda i: (i, 0, 0)),
            pl.BlockSpec((None, gr, M_HEADS), lambda i: (i, 0, 0)),
        ],
        out_shape=[
            jax.ShapeDtypeStruct((b * SAMPLE_ROWS, M_V_WIDTH), BF16),
            jax.ShapeDtypeStruct(c0.shape, F32),
            jax.ShapeDtypeStruct(n0.shape, F32),
            jax.ShapeDtypeStruct(m0g.shape, F32),
        ],
        scratch_shapes=[pltpu.VMEM((rows, M_V_WIDTH), F32)],
        compiler_params=_params("arbitrary"),
        name="mlstm_sample",
    )(z_all, z_all, z_all, z_all, z_all, bias_row, norm_w_row, c0, n0, m0g)
    mh, c, n, m = outs
    return mh, c, n, m.reshape(b, M_HEADS)


def _merge_body(tiles_a, xa_ref, xb_ref, aa_ref, ab_ref, ma_ref, mb_ref, gza_ref, gzm_ref, wa_ref, wm_ref, wo_ref,
                gf_ref, wr_ref, br_ref, x1_ref, h2_ref, lg_ref):
    in_a = pl.program_id(0) < tiles_a
    x = jnp.where(in_a, xa_ref[...], xb_ref[...])
    a = jnp.where(in_a, aa_ref[...], ab_ref[...])
    mh = jnp.where(in_a, ma_ref[...], mb_ref[...])
    pa = jnp.dot(a, wa_ref[...], preferred_element_type=F32)
    pm = jnp.dot(mh, wm_ref[...], preferred_element_type=F32)
    merged = jax.nn.sigmoid(gza_ref[...]) * pa + jax.nn.sigmoid(gzm_ref[...]) * pm
    x1 = x + jnp.dot(merged.astype(BF16), wo_ref[...], preferred_element_type=F32)
    x1_ref[...] = x1
    h2 = x1 * lax.rsqrt(jnp.mean(x1 * x1, axis=-1, keepdims=True) + RMS_EPS) * gf_ref[...]
    h2_ref[...] = h2
    hi = h2.astype(BF16)
    lo = (h2 - hi.astype(F32)).astype(BF16)
    both = jnp.dot(hi, wr_ref[...], preferred_element_type=F32)
    lg_ref[...] = (both[:, :ROUTER_PAD] + both[:, ROUTER_PAD:]
                   + jnp.dot(lo, wr_ref[:, :ROUTER_PAD], preferred_element_type=F32) + br_ref[...])


def _merge(x_a, x_b, a_a, a_b, mh_a, mh_b, z_all, wa, wm, wo, g_ffn, wr_pad, br_pad, tile_m):
    tiles_a, tiles_b = x_a.shape[0] // tile_m, x_b.shape[0] // tile_m
    t_all = z_all.shape[0]
    const = lambda i: (0, 0)
    from_a = lambda i: (jnp.minimum(i, tiles_a - 1), 0)
    from_b = lambda i: (jnp.maximum(i - tiles_a, 0), 0)
    return pl.pallas_call(
        functools.partial(_merge_body, tiles_a),
        grid=(tiles_a + tiles_b,),
        in_specs=[
            pl.BlockSpec((tile_m, D_MODEL), from_a),
            pl.BlockSpec((tile_m, D_MODEL), from_b),
            pl.BlockSpec((tile_m, ATTN_WIDTH), from_a),
            pl.BlockSpec((tile_m, ATTN_WIDTH), from_b),
            pl.BlockSpec((tile_m, M_V_WIDTH), from_a),
            pl.BlockSpec((tile_m, M_V_WIDTH), from_b),
            pl.BlockSpec((tile_m, D_MODEL), lambda i: (i, COL_GZ // D_MODEL)),
            pl.BlockSpec((tile_m, D_MODEL), lambda i: (i, COL_GZ // D_MODEL + 1)),
            pl.BlockSpec((ATTN_WIDTH, D_MODEL), const, pipeline_mode=pl.Buffered(1)),
            pl.BlockSpec((M_V_WIDTH, D_MODEL), const, pipeline_mode=pl.Buffered(1)),
            pl.BlockSpec((D_MODEL, D_MODEL), const, pipeline_mode=pl.Buffered(1)),
            pl.BlockSpec((1, D_MODEL), const),
            pl.BlockSpec((D_MODEL, 2 * ROUTER_PAD), const),
            pl.BlockSpec((1, ROUTER_PAD), const),
        ],
        out_specs=[
            pl.BlockSpec((tile_m, D_MODEL), lambda i: (i, 0)),
            pl.BlockSpec((tile_m, D_MODEL), lambda i: (i, 0)),
            pl.BlockSpec((tile_m, ROUTER_PAD), lambda i: (i, 0)),
        ],
        out_shape=[
            jax.ShapeDtypeStruct((t_all, D_MODEL), F32),
            jax.ShapeDtypeStruct((t_all, D_MODEL), F32),
            jax.ShapeDtypeStruct((t_all, ROUTER_PAD), F32),
        ],
        compiler_params=_params("arbitrary"),
        name="merge",
    )(x_a, x_b, a_a, a_b, mh_a, mh_b, z_all, z_all, wa, wm, wo, g_ffn.reshape(1, D_MODEL), wr_pad, br_pad)


def _expert_body(be_ref, nb_ref, xs_ref, w1g_ref, w1u_ref, b1g_ref, b1u_ref, w2_ref, b2_ref, y_ref):
    i = pl.program_id(0)
    c = pl.program_id(1)
    used = i < nb_ref[0]

    @pl.when(used)
    def _():
        x = xs_ref[...].astype(BF16)
        zg = jnp.dot(x, w1g_ref[...], preferred_element_type=F32) + b1g_ref[...]
        zu = jnp.dot(x, w1u_ref[...], preferred_element_type=F32) + b1u_ref[...]
        gate = jnp.minimum(zg, SWIGLU_LIMIT)
        up = jnp.clip(zu, -SWIGLU_LIMIT, SWIGLU_LIMIT)
        act = gate * jax.nn.sigmoid(SWIGLU_ALPHA * gate) * (up + 1.0)
        y = jnp.dot(act.astype(BF16), w2_ref[...], preferred_element_type=F32)

        @pl.when(c == 0)
        def _():
            y_ref[...] = y + b2_ref[...]

        @pl.when(c != 0)
        def _():
            y_ref[...] += y

    @pl.when(jnp.logical_not(used) & (c == 0))
    def _():
        y_ref[...] = jnp.zeros_like(y_ref)


def _experts(block_e, n_used, xs, w1, b1, w2, b2):
    n_slots = xs.shape[0]
    n_blocks = n_slots // MOE_BLOCK_M
    n_chunks = D_FF // MOE_FF_CHUNK

    def chunk(i, c, nb):
        block = jnp.minimum(i, nb[0] - 1)
        step = jnp.where(i < nb[0], c, n_chunks - 1)
        return jnp.where(block % 2 == 0, step, n_chunks - 1 - step)

    grid_spec = pltpu.PrefetchScalarGridSpec(
        num_scalar_prefetch=2,
        grid=(n_blocks, n_chunks),
        in_specs=[
            pl.BlockSpec((MOE_BLOCK_M, D_MODEL), lambda i, c, be, nb: (jnp.minimum(i, nb[0] - 1), 0)),
            pl.BlockSpec((None, D_MODEL, MOE_FF_CHUNK), lambda i, c, be, nb: (be[i], 0, chunk(i, c, nb))),
            pl.BlockSpec((None, D_MODEL, MOE_FF_CHUNK),
                         lambda i, c, be, nb: (be[i], 0, n_chunks + chunk(i, c, nb))),
            pl.BlockSpec((None, 1, MOE_FF_CHUNK), lambda i, c, be, nb: (be[i], 0, chunk(i, c, nb))),
            pl.BlockSpec((None, 1, MOE_FF_CHUNK), lambda i, c, be, nb: (be[i], 0, n_chunks + chunk(i, c, nb))),
            pl.BlockSpec((None, MOE_FF_CHUNK, D_MODEL), lambda i, c, be, nb: (be[i], chunk(i, c, nb), 0)),
            pl.BlockSpec((None, 1, D_MODEL), lambda i, c, be, nb: (be[i], 0, 0)),
        ],
        out_specs=pl.BlockSpec((MOE_BLOCK_M, D_MODEL), lambda i, c, be, nb: (i, 0)),
    )
    return pl.pallas_call(
        _expert_body,
        grid_spec=grid_spec,
        out_shape=jax.ShapeDtypeStruct((n_slots, D_MODEL), F32),
        compiler_params=_params("arbitrary", "arbitrary"),
        name="experts",
    )(block_e, n_used, xs, w1, w1, b1, b1, w2, b2)


ROUTE_TILE = 1024
LANE_E = 0
LANE_RANK = TOP_K


def _route_body(lg_ref, meta_ref, gate_ref, cnt_ref, tri_scr, carry_scr):
    @pl.when(pl.program_id(0) == 0)
    def _():
        r = lax.broadcasted_iota(jnp.int32, (ROUTE_TILE, ROUTE_TILE), 0)
        c = lax.broadcasted_iota(jnp.int32, (ROUTE_TILE, ROUTE_TILE), 1)
        tri_scr[...] = jnp.where(c < r, 1.0, 0.0).astype(BF16)
        carry_scr[...] = jnp.zeros_like(carry_scr)

    lane_i = lax.broadcasted_iota(jnp.int32, (ROUTE_TILE, LANES), 1)
    lane = lane_i.astype(F32)
    x = jnp.where(lane_i < N_EXPERTS, lg_ref[...], -jnp.inf)
    vals, idxs, sels = [], [], []
    for _ in range(TOP_K):
        mx = jnp.max(x, axis=1, keepdims=True)
        idx = jnp.min(jnp.where(x == mx, lane, float(LANES)), axis=1, keepdims=True)
        sel = lane == idx
        x = jnp.where(sel, -jnp.inf, x)
        vals.append(mx)
        idxs.append(idx)
        sels.append(sel)
    weights = [jnp.exp(v - vals[0]) for v in vals]
    inv = 1.0 / (weights[0] + weights[1] + weights[2] + weights[3])
    onehot = sum(jnp.where(s, 1.0, 0.0) for s in sels)
    ranks = jnp.dot(tri_scr[...], onehot.astype(BF16), preferred_element_type=F32) + carry_scr[0:1, :]
    meta = jnp.zeros((ROUTE_TILE, LANES), F32)
    gate = jnp.zeros((ROUTE_TILE, LANES), F32)
    for k in range(TOP_K):
        rank_k = jnp.sum(jnp.where(sels[k], ranks, 0.0), axis=1, keepdims=True)
        meta = jnp.where(lane_i == LANE_E + k, idxs[k], meta)
        meta = jnp.where(lane_i == LANE_RANK + k, rank_k, meta)
        gate = jnp.where(lane_i == k, weights[k] * inv, gate)
    meta_ref[...] = meta.astype(jnp.int32)
    gate_ref[...] = gate
    carry_scr[0:1, :] = carry_scr[0:1, :] + jnp.sum(onehot, axis=0, keepdims=True)
    cnt_ref[...] = carry_scr[...]


def _route(logits):
    t = logits.shape[0]
    return pl.pallas_call(
        _route_body,
        grid=(t // ROUTE_TILE,),
        in_specs=[pl.BlockSpec((ROUTE_TILE, LANES), lambda i: (i, 0))],
        out_specs=[
            pl.BlockSpec((ROUTE_TILE, LANES), lambda i: (i, 0)),
            pl.BlockSpec((ROUTE_TILE, LANES), lambda i: (i, 0)),
            pl.BlockSpec((SUBLANES, LANES), lambda i: (0, 0)),
        ],
        out_shape=[
            jax.ShapeDtypeStruct((t, LANES), jnp.int32),
            jax.ShapeDtypeStruct((t, LANES), F32),
            jax.ShapeDtypeStruct((SUBLANES, LANES), F32),
        ],
        scratch_shapes=[pltpu.VMEM((ROUTE_TILE, ROUTE_TILE), BF16), pltpu.VMEM((SUBLANES, LANES), F32)],
        compiler_params=_params("arbitrary"),
        name="route",
    )(logits)


def _slot_tables(counts, n_blocks):
    padded = (counts + MOE_BLOCK_M - 1) // MOE_BLOCK_M * MOE_BLOCK_M
    pends = jnp.cumsum(padded)
    first_slot = jnp.arange(n_blocks, dtype=pends.dtype) * MOE_BLOCK_M
    block_e = jnp.minimum(jnp.sum(pends[None, :] <= first_slot[:, None], axis=1), N_EXPERTS - 1).astype(jnp.int32)
    n_used = (pends[-1] // MOE_BLOCK_M).astype(jnp.int32).reshape(1)
    return (pends - padded).astype(jnp.int32), pends.astype(jnp.int32), padded.astype(jnp.int32), block_e, n_used


DISPATCH_TILE = 1024
IDX_SLOTS = 3


def _dispatch_body(pend_ref, padded_ref, dest_hbm, h_ref, xs_hbm, idx_smem, zero_scr, idx_sem, row_sem, zero_sem):
    i = pl.program_id(0)
    n = pl.num_programs(0)

    width = DISPATCH_TILE * TOP_K

    def idx_copy(tile):
        s = lax.rem(tile, IDX_SLOTS)
        return pltpu.make_async_copy(dest_hbm.at[pl.ds(pl.multiple_of(tile * width, width), width)],
                                     idx_smem.at[pl.ds(pl.multiple_of(s * width, width), width)], idx_sem.at[s])

    def zero_copy(e):
        start = pl.multiple_of(pend_ref[e] - MOE_BLOCK_M, MOE_BLOCK_M)
        return pltpu.make_async_copy(zero_scr, xs_hbm.at[pl.ds(start, MOE_BLOCK_M), :], zero_sem)

    def row_copy(g, j, d):
        return pltpu.make_async_copy(h_ref.at[g, pl.ds(j, 1), :], xs_hbm.at[pl.ds(d, 1), :], row_sem)

    @pl.when(i == 0)
    def _():
        zero_scr[...] = jnp.zeros_like(zero_scr)
        for e in range(N_EXPERTS):
            @pl.when(padded_ref[e] > 0)
            def _():
                zero_copy(e).start()
        for e in range(N_EXPERTS):
            @pl.when(padded_ref[e] > 0)
            def _():
                zero_copy(e).wait()
        idx_copy(0).start()

    @pl.when(i + 1 < n)
    def _():
        idx_copy(i + 1).start()

    idx_copy(i).wait()
    base = lax.rem(i, IDX_SLOTS) * width

    def issue(g, c):
        for j in range(SUBLANES):
            for k in range(TOP_K):
                d = idx_smem[base + g * (SUBLANES * TOP_K) + (j * TOP_K + k)]
                row_copy(g, j, d).start(priority=k % 2)
        return c

    lax.fori_loop(0, DISPATCH_TILE // SUBLANES, issue, 0)
    for _ in range(TOP_K):
        pltpu.make_async_copy(h_ref, h_ref, row_sem).wait()


def _scatter_rows(h, dest, pends, padded, n_slots):
    t = h.shape[0]
    tiles = t // DISPATCH_TILE
    grid_spec = pltpu.PrefetchScalarGridSpec(
        num_scalar_prefetch=2,
        grid=(tiles,),
        in_specs=[
            pl.BlockSpec(memory_space=pl.ANY),
            pl.BlockSpec((DISPATCH_TILE // SUBLANES, SUBLANES, D_MODEL), lambda i, pe, pd: (i, 0, 0)),
        ],
        out_specs=pl.BlockSpec(memory_space=pl.ANY),
        scratch_shapes=[
            pltpu.SMEM((IDX_SLOTS * DISPATCH_TILE * TOP_K,), jnp.int32),
            pltpu.VMEM((MOE_BLOCK_M, D_MODEL), F32),
            pltpu.SemaphoreType.DMA((IDX_SLOTS,)),
            pltpu.SemaphoreType.DMA(()),
            pltpu.SemaphoreType.DMA(()),
        ],
    )
    return pl.pallas_call(
        _dispatch_body,
        grid_spec=grid_spec,
        out_shape=jax.ShapeDtypeStruct((n_slots, D_MODEL), F32),
        compiler_params=_params("arbitrary"),
        name="dispatch",
    )(pends, padded, dest.reshape(-1), h.reshape(t // SUBLANES, SUBLANES, D_MODEL))


COMBINE_TILE = 128


def _combine_body(tile_off, dest_hbm, x1_ref, gate_ref, gf_ref, ys_hbm, y_ref, idx_smem, buf, idx_sem, row_sem):
    i = pl.program_id(0)
    n = pl.num_programs(0)

    width = COMBINE_TILE * TOP_K

    def idx_copy(tile):
        s = lax.rem(tile, IDX_SLOTS)
        return pltpu.make_async_copy(dest_hbm.at[pl.ds(pl.multiple_of((tile_off + tile) * width, width), width)],
                                     idx_smem.at[pl.ds(pl.multiple_of(s * width, width), width)], idx_sem.at[s])

    def row_copy(d, b, k, g, j):
        return pltpu.make_async_copy(ys_hbm.at[pl.ds(d, 1), :], buf.at[b, k, g, pl.ds(j, 1), :], row_sem.at[b])

    def issue_rows(tile):
        base = lax.rem(tile, IDX_SLOTS) * width
        b = lax.rem(tile, 2)

        def body(g, c):
            for j in range(SUBLANES):
                for k in range(TOP_K):
                    d = idx_smem[base + g * (SUBLANES * TOP_K) + (j * TOP_K + k)]
                    row_copy(d, b, k, g, j).start(priority=k % 2)
            return c

        lax.fori_loop(0, COMBINE_TILE // SUBLANES, body, 0)

    @pl.when(i == 0)
    def _():
        idx_copy(0).start()
        idx_copy(0).wait()
        issue_rows(0)

        @pl.when(n > 1)
        def _():
            idx_copy(1).start()

    @pl.when(i + 1 < n)
    def _():
        idx_copy(i + 1).wait()
        issue_rows(i + 1)

    @pl.when(i + 2 < n)
    def _():
        idx_copy(i + 2).start()

    b = lax.rem(i, 2)

    pltpu.make_async_copy(buf.at[b], buf.at[b], row_sem.at[b]).wait()
    gate = gate_ref[...]
    acc = x1_ref[...]
    for k in range(TOP_K):
        acc = acc + buf[b, k].reshape(COMBINE_TILE, D_MODEL) * gate[:, k:k + 1]
    y_ref[...] = acc * lax.rsqrt(jnp.mean(acc * acc, axis=-1, keepdims=True) + RMS_EPS) * gf_ref[...]


def _combine(x1, gate, dest, ys, g_final, row_off, rows):
    t = x1.shape[0]
    tile_off = row_off // COMBINE_TILE
    return pl.pallas_call(
        functools.partial(_combine_body, tile_off),
        grid=(rows // COMBINE_TILE,),
        in_specs=[
            pl.BlockSpec(memory_space=pl.ANY),
            pl.BlockSpec((COMBINE_TILE, D_MODEL), lambda i: (tile_off + i, 0)),
            pl.BlockSpec((COMBINE_TILE, LANES), lambda i: (tile_off + i, 0)),
            pl.BlockSpec((1, D_MODEL), lambda i: (0, 0)),
            pl.BlockSpec(memory_space=pl.ANY),
        ],
        out_specs=pl.BlockSpec((COMBINE_TILE, D_MODEL), lambda i: (i, 0)),
        out_shape=jax.ShapeDtypeStruct((rows, D_MODEL), F32),
        scratch_shapes=[
            pltpu.SMEM((IDX_SLOTS * COMBINE_TILE * TOP_K,), jnp.int32),
            pltpu.VMEM((2, TOP_K, COMBINE_TILE // SUBLANES, SUBLANES, D_MODEL), F32),
            pltpu.SemaphoreType.DMA((IDX_SLOTS,)),
            pltpu.SemaphoreType.DMA((2,)),
        ],
        compiler_params=_params("arbitrary"),
        name="combine",
    )(dest.reshape(-1), x1, gate, g_final.reshape(1, D_MODEL), ys)


def _lane_row(v):
    return jnp.pad(v.astype(F32), (0, LANES - v.shape[0])).reshape(1, LANES)


def kernel(x_prompt, x_sample, cache_swa_k, cache_swa_v, state_mlstm_C, state_mlstm_n, state_mlstm_m,
           g_mix, w_in, b_if, attn_sinks, mlstm_norm_w, w_attn_proj, w_mlstm_proj, w_out,
           g_ffn, w_router, b_router, w_e1, b_e1, w_e2, b_e2, g_final):
    bp, sp = x_prompt.shape[0], x_prompt.shape[1]
    bs, ls = x_sample.shape[0], x_sample.shape[1]
    tp = bp * sp
    ts = bs * SAMPLE_ROWS
    t_all = tp + ts
    l = 0
    w_in_p = _permute_w_in(w_in[l])
    wa, wm, wo = w_attn_proj[l].astype(BF16), w_mlstm_proj[l].astype(BF16), w_out[l].astype(BF16)
    wr_f32 = jnp.pad(w_router[l], ((0, 0), (0, ROUTER_PAD - N_EXPERTS)))
    wr_hi = wr_f32.astype(BF16)
    wr_pad = jnp.concatenate([wr_hi, (wr_f32 - wr_hi.astype(F32)).astype(BF16)], axis=1)
    br_pad = _lane_row(b_router[l])
    b1, b2 = b_e1[l].reshape(N_EXPERTS, 1, 2 * D_FF), b_e2[l].reshape(N_EXPERTS, 1, D_MODEL)
    sinks_row = _lane_row(attn_sinks[l])
    bias_row = _lane_row(b_if[l])
    norm_w_row = mlstm_norm_w[l].reshape(1, M_V_WIDTH)

    xp = x_prompt.reshape(tp, D_MODEL)
    xs = jnp.pad(x_sample, ((0, 0), (0, SAMPLE_ROWS - ls), (0, 0))).reshape(ts, D_MODEL)
    z_all = _inproj(xp, xs, g_mix[l], w_in_p, 512)

    a_p, k1, v1, w1 = _attn_prompt(z_all, sinks_row, bp, sp, w_e1[l])
    a_s, k2, v2 = _attn_sample(z_all, sinks_row, cache_swa_k[l].reshape(bs, WINDOW, KV_WIDTH),
                               cache_swa_v[l].reshape(bs, WINDOW, KV_WIDTH), tp, ls)
    mh_p, c1, n1, m1, w2 = _mlstm_prompt(z_all, bias_row, norm_w_row, bp, sp, w_e2[l])
    mh_s, c2, n2, m2 = _mlstm_sample(z_all, bias_row, norm_w_row, state_mlstm_C[l], state_mlstm_n[l],
                                     state_mlstm_m[l], tp, ls)

    x1, h2, logits = _merge(xp, xs, a_p, a_s, mh_p, mh_s, z_all, wa, wm, wo, g_ffn[l], wr_pad, br_pad, 256)

    meta, gate, counts = _route(logits)
    n_blocks = -(-t_all * TOP_K // MOE_BLOCK_M) + N_EXPERTS
    pstarts, pends, padded, block_e, n_used = _slot_tables(counts[0, :N_EXPERTS].astype(jnp.int32), n_blocks)
    dest = pstarts[meta[:, LANE_E:LANE_E + TOP_K]] + meta[:, LANE_RANK:LANE_RANK + TOP_K]
    xs_sorted = _scatter_rows(h2, dest, pends, padded, n_blocks * MOE_BLOCK_M)
    ys = _experts(block_e, n_used, xs_sorted, w1, b1, w2, b2)
    y_prompt = _combine(x1, gate, dest, ys, g_final, 0, tp).reshape(bp, sp, D_MODEL)
    y_sample = _combine(x1, gate, dest, ys, g_final, tp, ts).reshape(bs, SAMPLE_ROWS, D_MODEL)[:, :ls]
    k1 = k1.reshape(bp, WINDOW, N_KV_HEADS, HEAD_DIM)
    v1 = v1.reshape(bp, WINDOW, N_KV_HEADS, HEAD_DIM)
    k2 = k2.reshape(bs, WINDOW, N_KV_HEADS, HEAD_DIM)
    v2 = v2.reshape(bs, WINDOW, N_KV_HEADS, HEAD_DIM)
    m1 = m1[:, :, 0]
    st = lambda t: t[None]
    return (y_prompt, y_sample, st(k1), st(v1), st(c1), st(n1), st(m1), st(k2), st(v2), st(c2), st(n2), st(m2))
```

```python
import functools

import jax
import jax.numpy as jnp
from jax import lax
from jax.experimental import pallas as pl
from jax.experimental.pallas import tpu as pltpu

D_MODEL = 2048
N_HEADS = 16
N_KV_HEADS = 2
HEAD_DIM = 64
Q_PER_KV = N_HEADS // N_KV_HEADS
WINDOW = 128
ATTN_WIDTH = N_HEADS * HEAD_DIM
KV_WIDTH = N_KV_HEADS * HEAD_DIM
M_HEADS = 4
M_QK_DIM = 128
M_V_DIM = 256
M_QK_WIDTH = M_HEADS * M_QK_DIM
M_V_WIDTH = M_HEADS * M_V_DIM
M_CHUNK = 64
N_EXPERTS = 32
TOP_K = 4
D_FF = D_MODEL
SWIGLU_LIMIT = 7.0
SWIGLU_ALPHA = 1.702
RMS_EPS = 1e-5
NEG = -1e30

F32 = jnp.float32
BF16 = jnp.bfloat16

LANES = 128
SUBLANES = 8
VMEM_LIMIT_BYTES = 56 * 1024 * 1024

COL_GZ = 0
COL_AQ = COL_GZ + 2 * D_MODEL
COL_MV = COL_AQ + ATTN_WIDTH
COL_MO = COL_MV + M_V_WIDTH
COL_MQ = COL_MO + M_V_WIDTH
COL_MK = COL_MQ + M_QK_WIDTH
COL_AK = COL_MK + M_QK_WIDTH
COL_IF = COL_AK + 2 * KV_WIDTH
IN_TILE_N = 2176
IN_WIDTH_PAD = -(-(COL_IF + LANES) // IN_TILE_N) * IN_TILE_N

SAMPLE_ROWS = SUBLANES
MOE_BLOCK_M = 512
MOE_FF_CHUNK = 1024
EXPERT_FF_PARTS = 2
ROUTER_PAD = LANES


def _permute_w_in(w_in):
    splits = (ATTN_WIDTH, KV_WIDTH, KV_WIDTH, M_QK_WIDTH, M_QK_WIDTH, M_V_WIDTH, M_V_WIDTH, M_HEADS, M_HEADS,
              2 * D_MODEL)
    parts, off = [], 0
    for width in splits:
        parts.append(w_in[:, off:off + width])
        off += width
    aq, ak, av, mq, mk, mv, mo, mi, mf, gz = parts
    cols = jnp.concatenate([gz, aq, mv, mo, mq, mk, ak, av, mi, mf], axis=1)
    cols = jnp.pad(cols, ((0, 0), (0, IN_WIDTH_PAD - cols.shape[1])))
    return cols.astype(BF16)


def _params(*semantics):
    return pltpu.CompilerParams(dimension_semantics=semantics, vmem_limit_bytes=VMEM_LIMIT_BYTES)


def _log_sigmoid(x):
    return jnp.minimum(x, 0.0) - jnp.log1p(jnp.exp(-jnp.abs(x)))


def _inproj_body(tiles_a, xa_ref, xb_ref, g_ref, w_ref, z_ref, h_scr):
    def normalise(x_ref):
        x = x_ref[...]
        y = x * lax.rsqrt(jnp.mean(x * x, axis=-1, keepdims=True) + RMS_EPS)
        h_scr[...] = (y * g_ref[...]).astype(BF16)

    first = pl.program_id(1) == 0
    in_a = pl.program_id(0) < tiles_a

    @pl.when(first & in_a)
    def _():
        normalise(xa_ref)

    @pl.when(first & jnp.logical_not(in_a))
    def _():
        normalise(xb_ref)

    z_ref[...] = jnp.dot(h_scr[...], w_ref[...], preferred_element_type=F32)


def _inproj(x_a, x_b, g, w_bf16, tile_m):
    tiles_a, tiles_b = x_a.shape[0] // tile_m, x_b.shape[0] // tile_m
    n = w_bf16.shape[1]
    return pl.pallas_call(
        functools.partial(_inproj_body, tiles_a),
        grid=(tiles_a + tiles_b, n // IN_TILE_N),
        in_specs=[
            pl.BlockSpec((tile_m, D_MODEL), lambda i, j: (jnp.minimum(i, tiles_a - 1), 0)),
            pl.BlockSpec((tile_m, D_MODEL), lambda i, j: (jnp.maximum(i - tiles_a, 0), 0)),
            pl.BlockSpec((1, D_MODEL), lambda i, j: (0, 0)),
            pl.BlockSpec((D_MODEL, IN_TILE_N), lambda i, j: (0, j)),
        ],
        out_specs=pl.BlockSpec((tile_m, IN_TILE_N), lambda i, j: (i, j)),
        out_shape=jax.ShapeDtypeStruct(((tiles_a + tiles_b) * tile_m, n), F32),
        scratch_shapes=[pltpu.VMEM((tile_m, D_MODEL), BF16)],
        compiler_params=_params("arbitrary", "arbitrary"),
        name="inproj",
    )(x_a, x_b, g.reshape(1, D_MODEL), w_bf16)


assert KV_WIDTH == LANES and N_KV_HEADS == 2 and LANES == 2 * HEAD_DIM
Q_TILES = ATTN_WIDTH // LANES
TILES_PER_KV = Q_TILES // N_KV_HEADS
_NT = (((1,), (1,)), ((), ()))


def _kv_placements(x):
    lo = lax.broadcasted_iota(jnp.int32, x.shape, 1) < HEAD_DIM
    swapped = pltpu.roll(x, HEAD_DIM, axis=1)
    return ((jnp.where(lo, x, 0.0).astype(BF16), jnp.where(lo, 0.0, swapped).astype(BF16)),
            (jnp.where(lo, swapped, 0.0).astype(BF16), jnp.where(lo, 0.0, x).astype(BF16)))


def _slab_head(g, p, j):
    return g * Q_PER_KV + 2 * j + p


ATTN_BLOCKS = 2


def _attn_block(q, kv_cur, kv_prev, has_prev, sink_ref, a_ref, rows):
    kc, vc = kv_cur[:, :KV_WIDTH], kv_cur[:, KV_WIDTH:]
    kk = _kv_placements(jnp.concatenate([kv_prev[:, :KV_WIDTH], kc], axis=0))
    vv = _kv_placements(jnp.concatenate([kv_prev[:, KV_WIDTH:], vc], axis=0))
    q = q.astype(BF16)
    row = lax.broadcasted_iota(jnp.int32, (WINDOW, WINDOW), 0)
    col = lax.broadcasted_iota(jnp.int32, (WINDOW, WINDOW), 1)
    mask = jnp.concatenate([(col > row) & has_prev, col <= row], axis=1)
    slabs, sinks = [], []
    for g in range(N_KV_HEADS):
        qt = jnp.concatenate([q[:, (TILES_PER_KV * g + j) * LANES:(TILES_PER_KV * g + j + 1) * LANES]
                              for j in range(TILES_PER_KV)], axis=0)
        for p in range(2):
            s = lax.dot_general(qt, kk[g][p], _NT, preferred_element_type=F32) * (HEAD_DIM ** -0.5)
            for j in range(TILES_PER_KV):
                h = _slab_head(g, p, j)
                slabs.append(jnp.where(mask, s[j * WINDOW:(j + 1) * WINDOW], NEG))
                sinks.append(jnp.broadcast_to(sink_ref[:, h:h + 1], (WINDOW, LANES)))
    s_all = jnp.concatenate(slabs, axis=0)
    sink = jnp.concatenate(sinks, axis=0)
    m = jnp.maximum(jnp.broadcast_to(jnp.max(s_all, axis=1, keepdims=True), sink.shape), sink)
    p_all = jnp.exp(s_all - jnp.concatenate([m, m], axis=1))
    p_hi = p_all.astype(BF16)
    p_lo = (p_all - p_hi.astype(F32)).astype(BF16)
    ones = jnp.ones((2 * WINDOW, LANES), BF16)
    denom = (jnp.dot(p_hi, ones, preferred_element_type=F32) + jnp.dot(p_lo, ones, preferred_element_type=F32)
             + jnp.exp(sink - m))
    inv = 1.0 / denom
    pn = (p_all * jnp.concatenate([inv, inv], axis=1)).astype(BF16)
    half = TILES_PER_KV * WINDOW
    for g in range(N_KV_HEADS):
        base = 2 * half * g
        o = (jnp.dot(pn[base:base + half], vv[g][0], preferred_element_type=F32)
             + jnp.dot(pn[base + half:base + 2 * half], vv[g][1], preferred_element_type=F32))
        for j in range(TILES_PER_KV):
            t = TILES_PER_KV * g + j
            a_ref[rows, t * LANES:(t + 1) * LANES] = o[j * WINDOW:(j + 1) * WINDOW].astype(BF16)


def _attn_prompt_body(q_ref, kvc_ref, kvp_ref, sink_ref, w_ref, a_ref, k_ref, v_ref, wb_ref):
    wb_ref[...] = w_ref[...].astype(BF16)
    n = pl.program_id(1)
    for u in range(ATTN_BLOCKS):
        rows = slice(u * WINDOW, (u + 1) * WINDOW)
        if u == 0:
            kv_prev, has_prev = kvp_ref[...], n > 0
        else:
            kv_prev, has_prev = kvc_ref[(u - 1) * WINDOW:u * WINDOW, :], True
        _attn_block(q_ref[rows, :], kvc_ref[rows, :], kv_prev, has_prev, sink_ref, a_ref, rows)
    last = slice((ATTN_BLOCKS - 1) * WINDOW, ATTN_BLOCKS * WINDOW)
    k_ref[...] = kvc_ref[last, :KV_WIDTH]
    v_ref[...] = kvc_ref[last, KV_WIDTH:]


def _cast_slices(w, steps):
    rows = w.reshape(-1, w.shape[-1])
    per_step = rows.shape[0] // steps
    assert per_step * steps == rows.shape[0] and per_step % 16 == 0
    return rows, per_step


def _attn_prompt(z_all, sinks_row, batch, seq, w):
    span = ATTN_BLOCKS * WINDOW
    nb = seq // span
    w_rows, r1 = _cast_slices(w, batch * nb)
    step = lambda b, n: (b * nb + n, 0)
    a, k, v, wb = pl.pallas_call(
        _attn_prompt_body,
        grid=(batch, nb),
        in_specs=[
            pl.BlockSpec((span, ATTN_WIDTH), lambda b, n: (b * nb + n, COL_AQ // ATTN_WIDTH)),
            pl.BlockSpec((span, 2 * KV_WIDTH), lambda b, n: (b * nb + n, COL_AK // (2 * KV_WIDTH))),
            pl.BlockSpec((WINDOW, 2 * KV_WIDTH),
                         lambda b, n: (jnp.maximum((b * nb + n) * ATTN_BLOCKS - 1, 0), COL_AK // (2 * KV_WIDTH))),
            pl.BlockSpec((1, LANES), lambda b, n: (0, 0)),
            pl.BlockSpec((r1, w_rows.shape[1]), step),
        ],
        out_specs=[
            pl.BlockSpec((span, ATTN_WIDTH), lambda b, n: (b * nb + n, 0)),
            pl.BlockSpec((None, WINDOW, KV_WIDTH), lambda b, n: (b, 0, 0)),
            pl.BlockSpec((None, WINDOW, KV_WIDTH), lambda b, n: (b, 0, 0)),
            pl.BlockSpec((r1, w_rows.shape[1]), step),
        ],
        out_shape=[
            jax.ShapeDtypeStruct((batch * seq, ATTN_WIDTH), BF16),
            jax.ShapeDtypeStruct((batch, WINDOW, KV_WIDTH), F32),
            jax.ShapeDtypeStruct((batch, WINDOW, KV_WIDTH), F32),
            jax.ShapeDtypeStruct(w_rows.shape, BF16),
        ],
        compiler_params=_params("arbitrary", "arbitrary"),
        name="attn_prompt",
    )(z_all, z_all, z_all, sinks_row, w_rows)
    return a, k, v, wb.reshape(w.shape)


ATTN_SAMPLE_GROUP = 8


def _attn_sample_body(n_valid, q_ref, kv_ref, ck_ref, cv_ref, sink_ref, a_ref, nk_ref, nv_ref, o_scr):
    rs = SAMPLE_ROWS
    slab = TILES_PER_KV * rs
    r8 = lax.broadcasted_iota(jnp.int32, (rs, KV_WIDTH), 0)
    qi_c = jnp.bitwise_and(lax.broadcasted_iota(jnp.int32, (slab, WINDOW), 0), rs - 1)
    mask_c = lax.broadcasted_iota(jnp.int32, (slab, WINDOW), 1) > qi_c
    qi_n = jnp.bitwise_and(lax.broadcasted_iota(jnp.int32, (slab, rs), 0), rs - 1)
    mask_n = lax.broadcasted_iota(jnp.int32, (slab, rs), 1) <= qi_n
    sc_list, sn_list, sink_list, values = [], [], [], []
    for b in range(ATTN_SAMPLE_GROUP):
        rows = slice(b * rs, (b + 1) * rs)
        q = q_ref[rows, :]
        kn, vn = kv_ref[rows, :KV_WIDTH], kv_ref[rows, KV_WIDTH:]
        ck, cv = ck_ref[b], cv_ref[b]
        for src, new, dst in ((ck, kn, nk_ref), (cv, vn, nv_ref)):
            rolled = pltpu.roll(src, WINDOW - n_valid, axis=0)
            dst[b] = rolled
            tail = jnp.where(r8 >= rs - n_valid, pltpu.roll(new, rs - n_valid, axis=0), rolled[WINDOW - rs:, :])
            dst[b, WINDOW - rs:, :] = tail
        kk_c, kk_n = _kv_placements(ck), _kv_placements(kn)
        values.append((_kv_placements(cv), _kv_placements(vn)))
        for g in range(N_KV_HEADS):
            qt = jnp.concatenate([q[:, (TILES_PER_KV * g + j) * LANES:(TILES_PER_KV * g + j + 1) * LANES]
                                  for j in range(TILES_PER_KV)], axis=0).astype(BF16)
            for p in range(2):
                s_c = lax.dot_general(qt, kk_c[g][p], _NT, preferred_element_type=F32) * (HEAD_DIM ** -0.5)
                s_n = lax.dot_general(qt, kk_n[g][p], _NT, preferred_element_type=F32) * (HEAD_DIM ** -0.5)
                sc_list.append(jnp.where(mask_c, s_c, NEG))
                sn_list.append(jnp.where(mask_n, s_n, NEG))
                for j in range(TILES_PER_KV):
                    h = _slab_head(g, p, j)
                    sink_list.append(jnp.broadcast_to(sink_ref[:, h:h + 1], (rs, 1)))
    s_c = jnp.concatenate(sc_list, axis=0)
    s_n = jnp.concatenate(sn_list, axis=0)
    sink = jnp.concatenate(sink_list, axis=0)
    m = jnp.maximum(jnp.maximum(jnp.max(s_c, axis=1, keepdims=True), jnp.max(s_n, axis=1, keepdims=True)), sink)
    p_c = jnp.exp(s_c - m)
    p_n = jnp.exp(s_n - m)
    inv = 1.0 / (jnp.sum(p_c, axis=1, keepdims=True) + jnp.sum(p_n, axis=1, keepdims=True) + jnp.exp(sink - m))
    p_c = (p_c * inv).astype(BF16)
    p_n = (p_n * inv).astype(BF16)
    for b in range(ATTN_SAMPLE_GROUP):
        vv_c, vv_n = values[b]
        for g in range(N_KV_HEADS):
            base = (b * N_KV_HEADS + g) * 2 * slab
            o = (jnp.dot(p_c[base:base + slab], vv_c[g][0], preferred_element_type=F32)
                 + jnp.dot(p_c[base + slab:base + 2 * slab], vv_c[g][1], preferred_element_type=F32)
                 + jnp.dot(p_n[base:base + slab], vv_n[g][0], preferred_element_type=F32)
                 + jnp.dot(p_n[base + slab:base + 2 * slab], vv_n[g][1], preferred_element_type=F32))
            for j in range(TILES_PER_KV):
                t = TILES_PER_KV * g + j
                o_scr[b * rs:(b + 1) * rs, t * LANES:(t + 1) * LANES] = o[j * rs:(j + 1) * rs]
    a_ref[...] = o_scr[...].astype(BF16)


def _attn_sample(z_all, sinks_row, cache_k, cache_v, row_off, n_valid):
    b = cache_k.shape[0]
    gr = ATTN_SAMPLE_GROUP
    rows = gr * SAMPLE_ROWS
    rb = row_off // rows
    return pl.pallas_call(
        functools.partial(_attn_sample_body, n_valid),
        grid=(b // gr,),
        in_specs=[
            pl.BlockSpec((rows, ATTN_WIDTH), lambda i: (rb + i, COL_AQ // ATTN_WIDTH)),
            pl.BlockSpec((rows, 2 * KV_WIDTH), lambda i: (rb + i, COL_AK // (2 * KV_WIDTH))),
            pl.BlockSpec((gr, WINDOW, KV_WIDTH), lambda i: (i, 0, 0)),
            pl.BlockSpec((gr, WINDOW, KV_WIDTH), lambda i: (i, 0, 0)),
            pl.BlockSpec((1, LANES), lambda i: (0, 0)),
        ],
        out_specs=[
            pl.BlockSpec((rows, ATTN_WIDTH), lambda i: (i, 0)),
            pl.BlockSpec((gr, WINDOW, KV_WIDTH), lambda i: (i, 0, 0)),
            pl.BlockSpec((gr, WINDOW, KV_WIDTH), lambda i: (i, 0, 0)),
        ],
        out_shape=[
            jax.ShapeDtypeStruct((b * SAMPLE_ROWS, ATTN_WIDTH), BF16),
            jax.ShapeDtypeStruct(cache_k.shape, F32),
            jax.ShapeDtypeStruct(cache_v.shape, F32),
        ],
        scratch_shapes=[pltpu.VMEM((rows, ATTN_WIDTH), F32)],
        compiler_params=_params("arbitrary"),
        name="attn_sample",
    )(z_all, z_all, cache_k, cache_v, sinks_row)


def _mlstm_gates(if_blk, bias_row, n_valid):
    length = if_blk.shape[0]
    g = if_blk + bias_row
    lane = lax.broadcasted_iota(jnp.int32, g.shape, 1)
    g = jnp.where(lane < M_HEADS, g, _log_sigmoid(g))
    if n_valid < length:
        row = lax.broadcasted_iota(jnp.int32, g.shape, 0)
        g = jnp.where(row < n_valid, g, jnp.where(lane < M_HEADS, NEG, 0.0))
    return g


def _lanes(col, width):
    if width <= LANES:
        return col[:, :width]
    return jnp.concatenate([col] * (width // LANES), axis=1)


def _row_stat(x, reduce):
    return jnp.broadcast_to(reduce(x, axis=1, keepdims=True), (x.shape[0], LANES))


def _mlstm_chunk(q, k, v, gates, gates_t, h, c, n, m):
    length = q.shape[0]
    i_rep = jnp.broadcast_to(gates[:, h:h + 1], (length, LANES))
    f_col = gates[:, M_HEADS + h:M_HEADS + h + 1]
    i_row = gates_t[h:h + 1, :]
    f_row = gates_t[M_HEADS + h:M_HEADS + h + 1, :]
    t_idx = lax.broadcasted_iota(jnp.int32, (length, length), 0)
    s_idx = lax.broadcasted_iota(jnp.int32, (length, length), 1)
    causal = s_idx <= t_idx
    bcum = _row_stat(jnp.where(causal, f_row, 0.0), jnp.sum)
    bcum_row = jnp.sum(jnp.where(t_idx <= s_idx, f_col, 0.0), axis=0, keepdims=True)
    dlog = jnp.where(causal, _lanes(bcum, length) - bcum_row + i_row, NEG)
    inter = bcum + m
    mt = jnp.maximum(inter, _row_stat(dlog, jnp.max))
    w_inter = jnp.exp(inter - mt)
    qb = q.astype(BF16)
    ks = k * (M_QK_DIM ** -0.5)
    vb = v.astype(BF16)
    qk = lax.dot_general(qb, ks.astype(BF16), (((1,), (1,)), ((), ())), preferred_element_type=F32)
    qk = qk * jnp.exp(dlog - _lanes(mt, length))
    num = (_lanes(w_inter, M_V_DIM) * jnp.dot(qb, c.astype(BF16), preferred_element_type=F32)
           + jnp.dot(qk.astype(BF16), vb, preferred_element_type=F32))
    den = w_inter * _row_stat(q * n, jnp.sum) + _row_stat(qk, jnp.sum)
    h_t = num * _lanes(1.0 / jnp.maximum(jnp.abs(den), jnp.exp(-mt)), M_V_DIM)
    m_new = mt[length - 1:length, 0:1]
    w_state = jnp.exp(bcum[length - 1:length, :] - bcum + i_rep - m_new)
    decay = jnp.exp(inter[length - 1:length, 0:1] - m_new)
    kw = ks * _lanes(w_state, M_QK_DIM)
    c_new = decay * c + jnp.dot(kw.T.astype(BF16), vb, preferred_element_type=F32)
    n_new = decay * n + jnp.sum(kw, axis=0, keepdims=True)
    return h_t, c_new, n_new, m_new


def _mlstm_head_out(h_t, norm_w, o_gate):
    scale = lax.rsqrt(_row_stat(h_t * h_t, jnp.sum) * (1.0 / h_t.shape[1]) + RMS_EPS)
    return h_t * _lanes(scale, h_t.shape[1]) * norm_w * jax.nn.sigmoid(o_gate)


MLSTM_SEG = 256


MLSTM_BATCH = 2
MLSTM_PROMPT_CHUNK = 256


def _mlstm_prompt_body(*refs):
    nb = MLSTM_BATCH
    ins, rest = refs[:5 * nb], refs[5 * nb:]
    bias_ref, nw_ref, w_ref, mh_ref, c_out, n_out, m_out, wb_ref, c_scr, n_scr, m_scr = rest
    seg = pl.program_id(1)
    wb_ref[...] = w_ref[...].astype(BF16)

    @pl.when(seg == 0)
    def _():
        c_scr[...] = jnp.zeros_like(c_scr)
        n_scr[...] = jnp.zeros_like(n_scr)
        m_scr[...] = jnp.full_like(m_scr, NEG)

    def chunk_body(ci, carry):
        r = pl.multiple_of(ci * MLSTM_PROMPT_CHUNK, MLSTM_PROMPT_CHUNK)
        rows = pl.ds(r, MLSTM_PROMPT_CHUNK)
        state = [[(c_scr[u, h], n_scr[u, h:h + 1, :], m_scr[u, h:h + 1, 0:1]) for h in range(M_HEADS)]
                 for u in range(nb)]
        results = []
        for u in range(nb):
            q_ref, k_ref, v_ref, o_ref, if_ref = ins[5 * u:5 * u + 5]
            gates = _mlstm_gates(if_ref[rows, :], bias_ref[...], MLSTM_PROMPT_CHUNK)
            gates_t = gates.T
            for h in range(M_HEADS):
                qs = slice(h * M_QK_DIM, (h + 1) * M_QK_DIM)
                vs = slice(h * M_V_DIM, (h + 1) * M_V_DIM)
                h_t, c_new, n_new, m_new = _mlstm_chunk(
                    q_ref[rows, qs], k_ref[rows, qs], v_ref[rows, vs], gates, gates_t, h, *state[u][h])
                out = _mlstm_head_out(h_t, nw_ref[:, vs], o_ref[rows, vs]).astype(BF16)
                results.append((u, h, vs, out, c_new, n_new, m_new))
        for u, h, vs, out, c_new, n_new, m_new in results:
            c_scr[u, h] = c_new
            n_scr[u, h:h + 1, :] = n_new
            m_scr[u, h:h + 1, :] = jnp.broadcast_to(m_new, (1, LANES))
            mh_ref[u, rows, vs] = out
        return carry

    lax.fori_loop(0, MLSTM_SEG // MLSTM_PROMPT_CHUNK, chunk_body, 0)

    @pl.when(seg == pl.num_programs(1) - 1)
    def _():
        c_out[...] = c_scr[...]
        n_out[...] = n_scr[:, 0:M_HEADS, :]
        m_out[...] = m_scr[:, 0:M_HEADS, :]


def _mlstm_prompt(z_all, bias_row, norm_w_row, batch, seq, w):
    ns = seq // MLSTM_SEG
    nb = MLSTM_BATCH
    w_rows, wr = _cast_slices(w, (batch // nb) * ns)
    in_specs, args = [], []
    for u in range(nb):
        row = lambda p, s, u=u: (p * nb + u) * ns + s
        in_specs += [
            pl.BlockSpec((MLSTM_SEG, M_QK_WIDTH), lambda p, s, row=row: (row(p, s), COL_MQ // M_QK_WIDTH)),
            pl.BlockSpec((MLSTM_SEG, M_QK_WIDTH), lambda p, s, row=row: (row(p, s), COL_MK // M_QK_WIDTH)),
            pl.BlockSpec((MLSTM_SEG, M_V_WIDTH), lambda p, s, row=row: (row(p, s), COL_MV // M_V_WIDTH)),
            pl.BlockSpec((MLSTM_SEG, M_V_WIDTH), lambda p, s, row=row: (row(p, s), COL_MO // M_V_WIDTH)),
            pl.BlockSpec((MLSTM_SEG, LANES), lambda p, s, row=row: (row(p, s), COL_IF // LANES)),
        ]
        args += [z_all] * 5
    in_specs += [pl.BlockSpec((1, LANES), lambda p, s: (0, 0)), pl.BlockSpec((1, M_V_WIDTH), lambda p, s: (0, 0)),
                 pl.BlockSpec((wr, w_rows.shape[1]), lambda p, s: (p * ns + s, 0))]
    mh, c, n, m, wb = pl.pallas_call(
        _mlstm_prompt_body,
        grid=(batch // nb, ns),
        in_specs=in_specs,
        out_specs=[
            pl.BlockSpec((None, nb, MLSTM_SEG, M_V_WIDTH), lambda p, s: (p, 0, s, 0)),
            pl.BlockSpec((nb, M_HEADS, M_QK_DIM, M_V_DIM), lambda p, s: (p, 0, 0, 0)),
            pl.BlockSpec((nb, M_HEADS, M_QK_DIM), lambda p, s: (p, 0, 0)),
            pl.BlockSpec((nb, M_HEADS, LANES), lambda p, s: (p, 0, 0)),
            pl.BlockSpec((wr, w_rows.shape[1]), lambda p, s: (p * ns + s, 0)),
        ],
        out_shape=[
            jax.ShapeDtypeStruct((batch // nb, nb, seq, M_V_WIDTH), BF16),
            jax.ShapeDtypeStruct((batch, M_HEADS, M_QK_DIM, M_V_DIM), F32),
            jax.ShapeDtypeStruct((batch, M_HEADS, M_QK_DIM), F32),
            jax.ShapeDtypeStruct((batch, M_HEADS, LANES), F32),
            jax.ShapeDtypeStruct(w_rows.shape, BF16),
        ],
        scratch_shapes=[
            pltpu.VMEM((nb, M_HEADS, M_QK_DIM, M_V_DIM), F32),
            pltpu.VMEM((nb, SUBLANES, M_QK_DIM), F32),
            pltpu.VMEM((nb, SUBLANES, LANES), F32),
        ],
        compiler_params=_params("arbitrary", "arbitrary"),
        name="mlstm_prompt",
    )(*args, bias_row, norm_w_row, w_rows)
    return mh.reshape(batch * seq, M_V_WIDTH), c, n, m, wb.reshape(w.shape)


MLSTM_SAMPLE_GROUP = 8


def _mlstm_sample_body(n_valid, q_ref, k_ref, v_ref, o_ref, if_ref, bias_ref, nw_ref, c0_ref, n0_ref, m0_ref,
                       mh_ref, c_out, n_out, m_out, h_scr):
    for g in range(MLSTM_SAMPLE_GROUP):
        rows = slice(g * SAMPLE_ROWS, (g + 1) * SAMPLE_ROWS)
        gates = _mlstm_gates(if_ref[rows, :], bias_ref[...], n_valid)
        gates_t = jnp.transpose(jnp.concatenate([gates] * (LANES // SAMPLE_ROWS), axis=0))[:, :SAMPLE_ROWS]
        for h in range(M_HEADS):
            qs = slice(h * M_QK_DIM, (h + 1) * M_QK_DIM)
            vs = slice(h * M_V_DIM, (h + 1) * M_V_DIM)
            h_t, c_new, n_new, m_new = _mlstm_chunk(
                q_ref[rows, qs], k_ref[rows, qs], v_ref[rows, vs], gates, gates_t, h,
                c0_ref[g, h], n0_ref[g, h:h + 1, :], m0_ref[g:g + 1, h:h + 1])
            c_out[g, h] = c_new
            n_out[g, h:h + 1, :] = n_new
            m_out[g:g + 1, h:h + 1] = m_new
            h_scr[rows, vs] = _mlstm_head_out(h_t, nw_ref[:, vs], o_ref[rows, vs])
    mh_ref[...] = h_scr[...].astype(BF16)


def _mlstm_sample(z_all, bias_row, norm_w_row, c0, n0, m0, row_off, n_valid):
    b = c0.shape[0]
    gr = MLSTM_SAMPLE_GROUP
    rows = gr * SAMPLE_ROWS
    rb = row_off // rows
    m0g = m0.reshape(b // gr, gr, M_HEADS)
    outs = pl.pallas_call(
        functools.partial(_mlstm_sample_body, n_valid),
        grid=(b // gr,),
        in_specs=[
            pl.BlockSpec((rows, M_QK_WIDTH), lambda i: (rb + i, COL_MQ // M_QK_WIDTH)),
            pl.BlockSpec((rows, M_QK_WIDTH), lambda i: (rb + i, COL_MK // M_QK_WIDTH)),
            pl.BlockSpec((rows, M_V_WIDTH), lambda i: (rb + i, COL_MV // M_V_WIDTH)),
            pl.BlockSpec((rows, M_V_WIDTH), lambda i: (rb + i, COL_MO // M_V_WIDTH)),
            pl.BlockSpec((rows, LANES), lambda i: (rb + i, COL_IF // LANES)),
            pl.BlockSpec((1, LANES), lambda i: (0, 0)),
            pl.BlockSpec((1, M_V_WIDTH), lambda i: (0, 0)),
            pl.BlockSpec((gr, M_HEADS, M_QK_DIM, M_V_DIM), lambda i: (i, 0, 0, 0)),
            pl.BlockSpec((gr, M_HEADS, M_QK_DIM), lambda i: (i, 0, 0)),
            pl.BlockSpec((None, gr, M_HEADS), lambda i: (i, 0, 0)),
        ],
        out_specs=[
            pl.BlockSpec((rows, M_V_WIDTH), lambda i: (i, 0)),
            pl.BlockSpec((gr, M_HEADS, M_QK_DIM, M_V_DIM), lambda i: (i, 0, 0, 0)),
            pl.BlockSpec((gr, M_HEADS, M_QK_DIM), lambda i: (i, 0, 0)),
            pl.BlockSpec((None, gr, M_HEADS), lambda i: (i, 0, 0)),
        ],
        out_shape=[
            jax.ShapeDtypeStruct((b * SAMPLE_ROWS, M_V_WIDTH), BF16),
            jax.ShapeDtypeStruct(c0.shape, F32),
            jax.ShapeDtypeStruct(n0.shape, F32),
            jax.ShapeDtypeStruct(m0g.shape, F32),
        ],
        scratch_shapes=[pltpu.VMEM((rows, M_V_WIDTH), F32)],
        compiler_params=_params("arbitrary"),
        name="mlstm_sample",
    )(z_all, z_all, z_all, z_all, z_all, bias_row, norm_w_row, c0, n0, m0g)
    mh, c, n, m = outs
    return mh, c, n, m.reshape(b, M_HEADS)


def _merge_body(tiles_a, xa_ref, xb_ref, aa_ref, ab_ref, ma_ref, mb_ref, gza_ref, gzm_ref, wa_ref, wm_ref, wo_ref,
                gf_ref, wr_ref, br_ref, x1_ref, h2_ref, lg_ref):
    in_a = pl.program_id(0) < tiles_a
    x = jnp.where(in_a, xa_ref[...], xb_ref[...])
    a = jnp.where(in_a, aa_ref[...], ab_ref[...])
    mh = jnp.where(in_a, ma_ref[...], mb_ref[...])
    pa = jnp.dot(a, wa_ref[...], preferred_element_type=F32)
    pm = jnp.dot(mh, wm_ref[...], preferred_element_type=F32)
    merged = jax.nn.sigmoid(gza_ref[...]) * pa + jax.nn.sigmoid(gzm_ref[...]) * pm
    x1 = x + jnp.dot(merged.astype(BF16), wo_ref[...], preferred_element_type=F32)
    x1_ref[...] = x1
    h2 = x1 * lax.rsqrt(jnp.mean(x1 * x1, axis=-1, keepdims=True) + RMS_EPS) * gf_ref[...]
    h2_ref[...] = h2
    hi = h2.astype(BF16)
    lo = (h2 - hi.astype(F32)).astype(BF16)
    both = jnp.dot(hi, wr_ref[...], preferred_element_type=F32)
    lg_ref[...] = (both[:, :ROUTER_PAD] + both[:, ROUTER_PAD:]
                   + jnp.dot(lo, wr_ref[:, :ROUTER_PAD], preferred_element_type=F32) + br_ref[...])


def _merge(x_a, x_b, a_a, a_b, mh_a, mh_b, z_all, wa, wm, wo, g_ffn, wr_pad, br_pad, tile_m):
    tiles_a, tiles_b = x_a.shape[0] // tile_m, x_b.shape[0] // tile_m
    t_all = z_all.shape[0]
    const = lambda i: (0, 0)
    from_a = lambda i: (jnp.minimum(i, tiles_a - 1), 0)
    from_b = lambda i: (jnp.maximum(i - tiles_a, 0), 0)
    return pl.pallas_call(
        functools.partial(_merge_body, tiles_a),
        grid=(tiles_a + tiles_b,),
        in_specs=[
            pl.BlockSpec((tile_m, D_MODEL), from_a),
            pl.BlockSpec((tile_m, D_MODEL), from_b),
            pl.BlockSpec((tile_m, ATTN_WIDTH), from_a),
            pl.BlockSpec((tile_m, ATTN_WIDTH), from_b),
            pl.BlockSpec((tile_m, M_V_WIDTH), from_a),
            pl.BlockSpec((tile_m, M_V_WIDTH), from_b),
            pl.BlockSpec((tile_m, D_MODEL), lambda i: (i, COL_GZ // D_MODEL)),
            pl.BlockSpec((tile_m, D_MODEL), lambda i: (i, COL_GZ // D_MODEL + 1)),
            pl.BlockSpec((ATTN_WIDTH, D_MODEL), const, pipeline_mode=pl.Buffered(1)),
            pl.BlockSpec((M_V_WIDTH, D_MODEL), const, pipeline_mode=pl.Buffered(1)),
            pl.BlockSpec((D_MODEL, D_MODEL), const, pipeline_mode=pl.Buffered(1)),
            pl.BlockSpec((1, D_MODEL), const),
            pl.BlockSpec((D_MODEL, 2 * ROUTER_PAD), const),
            pl.BlockSpec((1, ROUTER_PAD), const),
        ],
        out_specs=[
            pl.BlockSpec((tile_m, D_MODEL), lambda i: (i, 0)),
            pl.BlockSpec((tile_m, D_MODEL), lambda i: (i, 0)),
            pl.BlockSpec((tile_m, ROUTER_PAD), lambda i: (i, 0)),
        ],
        out_shape=[
            jax.ShapeDtypeStruct((t_all, D_MODEL), F32),
            jax.ShapeDtypeStruct((t_all, D_MODEL), F32),
            jax.ShapeDtypeStruct((t_all, ROUTER_PAD), F32),
        ],
        compiler_params=_params("arbitrary"),
        name="merge",
    )(x_a, x_b, a_a, a_b, mh_a, mh_b, z_all, z_all, wa, wm, wo, g_ffn.reshape(1, D_MODEL), wr_pad, br_pad)


def _expert_body(be_ref, nb_ref, xs_ref, w1g_ref, w1u_ref, b1g_ref, b1u_ref, w2_ref, b2_ref, y_ref):
    i = pl.program_id(0)
    c = pl.program_id(1)
    used = i < nb_ref[0]

    @pl.when(used)
    def _():
        x = xs_ref[...].astype(BF16)
        y = None
        for part in range(EXPERT_FF_PARTS):
            cs = slice(part * (MOE_FF_CHUNK // EXPERT_FF_PARTS), (part + 1) * (MOE_FF_CHUNK // EXPERT_FF_PARTS))
            zg = jnp.dot(x, w1g_ref[:, cs], preferred_element_type=F32) + b1g_ref[:, cs]
            zu = jnp.dot(x, w1u_ref[:, cs], preferred_element_type=F32) + b1u_ref[:, cs]
            gate = jnp.minimum(zg, SWIGLU_LIMIT)
            up = jnp.clip(zu, -SWIGLU_LIMIT, SWIGLU_LIMIT)
            act = gate * jax.nn.sigmoid(SWIGLU_ALPHA * gate) * (up + 1.0)
            yp = jnp.dot(act.astype(BF16), w2_ref[cs, :], preferred_element_type=F32)
            y = yp if y is None else y + yp

        @pl.when(c == 0)
        def _():
            y_ref[...] = y + b2_ref[...]

        @pl.when(c != 0)
        def _():
            y_ref[...] += y

    @pl.when(jnp.logical_not(used) & (c == 0))
    def _():
        y_ref[...] = jnp.zeros_like(y_ref)


def _experts(block_e, n_used, xs, w1, b1, w2, b2):
    n_slots = xs.shape[0]
    n_blocks = n_slots // MOE_BLOCK_M
    n_chunks = D_FF // MOE_FF_CHUNK

    def chunk(i, c, nb):
        block = jnp.minimum(i, nb[0] - 1)
        step = jnp.where(i < nb[0], c, n_chunks - 1)
        return jnp.where(block % 2 == 0, step, n_chunks - 1 - step)

    grid_spec = pltpu.PrefetchScalarGridSpec(
        num_scalar_prefetch=2,
        grid=(n_blocks, n_chunks),
        in_specs=[
            pl.BlockSpec((MOE_BLOCK_M, D_MODEL), lambda i, c, be, nb: (jnp.minimum(i, nb[0] - 1), 0)),
            pl.BlockSpec((None, D_MODEL, MOE_FF_CHUNK), lambda i, c, be, nb: (be[i], 0, chunk(i, c, nb))),
            pl.BlockSpec((None, D_MODEL, MOE_FF_CHUNK),
                         lambda i, c, be, nb: (be[i], 0, n_chunks + chunk(i, c, nb))),
            pl.BlockSpec((None, 1, MOE_FF_CHUNK), lambda i, c, be, nb: (be[i], 0, chunk(i, c, nb))),
            pl.BlockSpec((None, 1, MOE_FF_CHUNK), lambda i, c, be, nb: (be[i], 0, n_chunks + chunk(i, c, nb))),
            pl.BlockSpec((None, MOE_FF_CHUNK, D_MODEL), lambda i, c, be, nb: (be[i], chunk(i, c, nb), 0)),
            pl.BlockSpec((None, 1, D_MODEL), lambda i, c, be, nb: (be[i], 0, 0)),
        ],
        out_specs=pl.BlockSpec((MOE_BLOCK_M, D_MODEL), lambda i, c, be, nb: (i, 0)),
    )
    return pl.pallas_call(
        _expert_body,
        grid_spec=grid_spec,
        out_shape=jax.ShapeDtypeStruct((n_slots, D_MODEL), F32),
        compiler_params=_params("arbitrary", "arbitrary"),
        name="experts",
    )(block_e, n_used, xs, w1, w1, b1, b1, w2, b2)


ROUTE_TILE = 1024
LANE_E = 0
LANE_RANK = TOP_K


def _route_body(lg_ref, meta_ref, gate_ref, cnt_ref, tri_scr, carry_scr):
    @pl.when(pl.program_id(0) == 0)
    def _():
        r = lax.broadcasted_iota(jnp.int32, (ROUTE_TILE, ROUTE_TILE), 0)
        c = lax.broadcasted_iota(jnp.int32, (ROUTE_TILE, ROUTE_TILE), 1)
        tri_scr[...] = jnp.where(c < r, 1.0, 0.0).astype(BF16)
        carry_scr[...] = jnp.zeros_like(carry_scr)

    lane_i = lax.broadcasted_iota(jnp.int32, (ROUTE_TILE, LANES), 1)
    lane = lane_i.astype(F32)
    x = jnp.where(lane_i < N_EXPERTS, lg_ref[...], -jnp.inf)
    vals, idxs, sels = [], [], []
    for _ in range(TOP_K):
        mx = jnp.max(x, axis=1, keepdims=True)
        idx = jnp.min(jnp.where(x == mx, lane, float(LANES)), axis=1, keepdims=True)
        sel = lane == idx
        x = jnp.where(sel, -jnp.inf, x)
        vals.append(mx)
        idxs.append(idx)
        sels.append(sel)
    weights = [jnp.exp(v - vals[0]) for v in vals]
    inv = 1.0 / (weights[0] + weights[1] + weights[2] + weights[3])
    onehot = sum(jnp.where(s, 1.0, 0.0) for s in sels)
    ranks = jnp.dot(tri_scr[...], onehot.astype(BF16), preferred_element_type=F32) + carry_scr[0:1, :]
    meta = jnp.zeros((ROUTE_TILE, LANES), F32)
    gate = jnp.zeros((ROUTE_TILE, LANES), F32)
    for k in range(TOP_K):
        rank_k = jnp.sum(jnp.where(sels[k], ranks, 0.0), axis=1, keepdims=True)
        meta = jnp.where(lane_i == LANE_E + k, idxs[k], meta)
        meta = jnp.where(lane_i == LANE_RANK + k, rank_k, meta)
        gate = jnp.where(lane_i == k, weights[k] * inv, gate)
    meta_ref[...] = meta.astype(jnp.int32)
    gate_ref[...] = gate
    carry_scr[0:1, :] = carry_scr[0:1, :] + jnp.sum(onehot, axis=0, keepdims=True)
    cnt_ref[...] = carry_scr[...]


def _route(logits):
    t = logits.shape[0]
    return pl.pallas_call(
        _route_body,
        grid=(t // ROUTE_TILE,),
        in_specs=[pl.BlockSpec((ROUTE_TILE, LANES), lambda i: (i, 0))],
        out_specs=[
            pl.BlockSpec((ROUTE_TILE, LANES), lambda i: (i, 0)),
            pl.BlockSpec((ROUTE_TILE, LANES), lambda i: (i, 0)),
            pl.BlockSpec((SUBLANES, LANES), lambda i: (0, 0)),
        ],
        out_shape=[
            jax.ShapeDtypeStruct((t, LANES), jnp.int32),
            jax.ShapeDtypeStruct((t, LANES), F32),
            jax.ShapeDtypeStruct((SUBLANES, LANES), F32),
        ],
        scratch_shapes=[pltpu.VMEM((ROUTE_TILE, ROUTE_TILE), BF16), pltpu.VMEM((SUBLANES, LANES), F32)],
        compiler_params=_params("arbitrary"),
        name="route",
    )(logits)


def _slot_tables(counts, n_blocks):
    padded = (counts + MOE_BLOCK_M - 1) // MOE_BLOCK_M * MOE_BLOCK_M
    pends = jnp.cumsum(padded)
    first_slot = jnp.arange(n_blocks, dtype=pends.dtype) * MOE_BLOCK_M
    block_e = jnp.minimum(jnp.sum(pends[None, :] <= first_slot[:, None], axis=1), N_EXPERTS - 1).astype(jnp.int32)
    n_used = (pends[-1] // MOE_BLOCK_M).astype(jnp.int32).reshape(1)
    return (pends - padded).astype(jnp.int32), pends.astype(jnp.int32), padded.astype(jnp.int32), block_e, n_used


DISPATCH_TILE = 1024
IDX_SLOTS = 3


def _dispatch_body(pend_ref, padded_ref, dest_hbm, h_ref, xs_hbm, idx_smem, zero_scr, idx_sem, row_sem, zero_sem):
    i = pl.program_id(0)
    n = pl.num_programs(0)

    width = DISPATCH_TILE * TOP_K

    def idx_copy(tile):
        s = lax.rem(tile, IDX_SLOTS)
        return pltpu.make_async_copy(dest_hbm.at[pl.ds(pl.multiple_of(tile * width, width), width)],
                                     idx_smem.at[pl.ds(pl.multiple_of(s * width, width), width)], idx_sem.at[s])

    def zero_copy(e):
        start = pl.multiple_of(pend_ref[e] - MOE_BLOCK_M, MOE_BLOCK_M)
        return pltpu.make_async_copy(zero_scr, xs_hbm.at[pl.ds(start, MOE_BLOCK_M), :], zero_sem)

    def row_copy(g, j, d):
        return pltpu.make_async_copy(h_ref.at[g, pl.ds(j, 1), :], xs_hbm.at[pl.ds(d, 1), :], row_sem)

    @pl.when(i == 0)
    def _():
        zero_scr[...] = jnp.zeros_like(zero_scr)
        for e in range(N_EXPERTS):
            @pl.when(padded_ref[e] > 0)
            def _():
                zero_copy(e).start()
        for e in range(N_EXPERTS):
            @pl.when(padded_ref[e] > 0)
            def _():
                zero_copy(e).wait()
        idx_copy(0).start()

    @pl.when(i + 1 < n)
    def _():
        idx_copy(i + 1).start()

    idx_copy(i).wait()
    base = lax.rem(i, IDX_SLOTS) * width

    def issue(g, c):
        for j in range(SUBLANES):
            for k in range(TOP_K):
                d = idx_smem[base + g * (SUBLANES * TOP_K) + (j * TOP_K + k)]
                row_copy(g, j, d).start(priority=k % 2)
        return c

    lax.fori_loop(0, DISPATCH_TILE // SUBLANES, issue, 0)
    for _ in range(TOP_K):
        pltpu.make_async_copy(h_ref, h_ref, row_sem).wait()


def _scatter_rows(h, dest, pends, padded, n_slots):
    t = h.shape[0]
    tiles = t // DISPATCH_TILE
    grid_spec = pltpu.PrefetchScalarGridSpec(
        num_scalar_prefetch=2,
        grid=(tiles,),
        in_specs=[
            pl.BlockSpec(memory_space=pl.ANY),
            pl.BlockSpec((DISPATCH_TILE // SUBLANES, SUBLANES, D_MODEL), lambda i, pe, pd: (i, 0, 0)),
        ],
        out_specs=pl.BlockSpec(memory_space=pl.ANY),
        scratch_shapes=[
            pltpu.SMEM((IDX_SLOTS * DISPATCH_TILE * TOP_K,), jnp.int32),
            pltpu.VMEM((MOE_BLOCK_M, D_MODEL), F32),
            pltpu.SemaphoreType.DMA((IDX_SLOTS,)),
            pltpu.SemaphoreType.DMA(()),
            pltpu.SemaphoreType.DMA(()),
        ],
    )
    return pl.pallas_call(
        _dispatch_body,
        grid_spec=grid_spec,
        out_shape=jax.ShapeDtypeStruct((n_slots, D_MODEL), F32),
        compiler_params=_params("arbitrary"),
        name="dispatch",
    )(pends, padded, dest.reshape(-1), h.reshape(t // SUBLANES, SUBLANES, D_MODEL))


COMBINE_TILE = 256


def _combine_body(tile_off, dest_hbm, x1_ref, gate_ref, gf_ref, ys_hbm, y_ref, idx_smem, buf, idx_sem, row_sem):
    i = pl.program_id(0)
    n = pl.num_programs(0)

    width = COMBINE_TILE * TOP_K

    def idx_copy(tile):
        s = lax.rem(tile, IDX_SLOTS)
        return pltpu.make_async_copy(dest_hbm.at[pl.ds(pl.multiple_of((tile_off + tile) * width, width), width)],
                                     idx_smem.at[pl.ds(pl.multiple_of(s * width, width), width)], idx_sem.at[s])

    def row_copy(d, b, k, g, j):
        return pltpu.make_async_copy(ys_hbm.at[pl.ds(d, 1), :], buf.at[b, k, g, pl.ds(j, 1), :], row_sem.at[b])

    def issue_rows(tile):
        base = lax.rem(tile, IDX_SLOTS) * width
        b = lax.rem(tile, 2)

        def body(g, c):
            for j in range(SUBLANES):
                for k in range(TOP_K):
                    d = idx_smem[base + g * (SUBLANES * TOP_K) + (j * TOP_K + k)]
                    row_copy(d, b, k, g, j).start(priority=k % 2)
            return c

        lax.fori_loop(0, COMBINE_TILE // SUBLANES, body, 0)

    @pl.when(i == 0)
    def _():
        idx_copy(0).start()
        idx_copy(0).wait()
        issue_rows(0)

        @pl.when(n > 1)
        def _():
            idx_copy(1).start()

    @pl.when(i + 1 < n)
    def _():
        idx_copy(i + 1).wait()
        issue_rows(i + 1)

    @pl.when(i + 2 < n)
    def _():
        idx_copy(i + 2).start()

    b = lax.rem(i, 2)

    pltpu.make_async_copy(buf.at[b], buf.at[b], row_sem.at[b]).wait()
    gate = gate_ref[...]
    acc = x1_ref[...]
    for k in range(TOP_K):
        acc = acc + buf[b, k].reshape(COMBINE_TILE, D_MODEL) * gate[:, k:k + 1]
    y_ref[...] = acc * lax.rsqrt(jnp.mean(acc * acc, axis=-1, keepdims=True) + RMS_EPS) * gf_ref[...]


def _combine(x1, gate, dest, ys, g_final, row_off, rows):
    t = x1.shape[0]
    tile_off = row_off // COMBINE_TILE
    return pl.pallas_call(
        functools.partial(_combine_body, tile_off),
        grid=(rows // COMBINE_TILE,),
        in_specs=[
            pl.BlockSpec(memory_space=pl.ANY),
            pl.BlockSpec((COMBINE_TILE, D_MODEL), lambda i: (tile_off + i, 0)),
            pl.BlockSpec((COMBINE_TILE, LANES), lambda i: (tile_off + i, 0)),
            pl.BlockSpec((1, D_MODEL), lambda i: (0, 0)),
            pl.BlockSpec(memory_space=pl.ANY),
        ],
        out_specs=pl.BlockSpec((COMBINE_TILE, D_MODEL), lambda i: (i, 0)),
        out_shape=jax.ShapeDtypeStruct((rows, D_MODEL), F32),
        scratch_shapes=[
            pltpu.SMEM((IDX_SLOTS * COMBINE_TILE * TOP_K,), jnp.int32),
            pltpu.VMEM((2, TOP_K, COMBINE_TILE // SUBLANES, SUBLANES, D_MODEL), F32),
            pltpu.SemaphoreType.DMA((IDX_SLOTS,)),
            pltpu.SemaphoreType.DMA((2,)),
        ],
        compiler_params=_params("arbitrary"),
        name="combine",
    )(dest.reshape(-1), x1, gate, g_final.reshape(1, D_MODEL), ys)


def _lane_row(v):
    return jnp.pad(v.astype(F32), (0, LANES - v.shape[0])).reshape(1, LANES)


def kernel(x_prompt, x_sample, cache_swa_k, cache_swa_v, state_mlstm_C, state_mlstm_n, state_mlstm_m,
           g_mix, w_in, b_if, attn_sinks, mlstm_norm_w, w_attn_proj, w_mlstm_proj, w_out,
           g_ffn, w_router, b_router, w_e1, b_e1, w_e2, b_e2, g_final):
    bp, sp = x_prompt.shape[0], x_prompt.shape[1]
    bs, ls = x_sample.shape[0], x_sample.shape[1]
    tp = bp * sp
    ts = bs * SAMPLE_ROWS
    t_all = tp + ts
    l = 0
    w_in_p = _permute_w_in(w_in[l])
    wa, wm, wo = w_attn_proj[l].astype(BF16), w_mlstm_proj[l].astype(BF16), w_out[l].astype(BF16)
    wr_f32 = jnp.pad(w_router[l], ((0, 0), (0, ROUTER_PAD - N_EXPERTS)))
    wr_hi = wr_f32.astype(BF16)
    wr_pad = jnp.concatenate([wr_hi, (wr_f32 - wr_hi.astype(F32)).astype(BF16)], axis=1)
    br_pad = _lane_row(b_router[l])
    b1, b2 = b_e1[l].reshape(N_EXPERTS, 1, 2 * D_FF), b_e2[l].reshape(N_EXPERTS, 1, D_MODEL)
    sinks_row = _lane_row(attn_sinks[l])
    bias_row = _lane_row(b_if[l])
    norm_w_row = mlstm_norm_w[l].reshape(1, M_V_WIDTH)

    xp = x_prompt.reshape(tp, D_MODEL)
    xs = jnp.pad(x_sample, ((0, 0), (0, SAMPLE_ROWS - ls), (0, 0))).reshape(ts, D_MODEL)
    z_all = _inproj(xp, xs, g_mix[l], w_in_p, 512)

    a_p, k1, v1, w1 = _attn_prompt(z_all, sinks_row, bp, sp, w_e1[l])
    a_s, k2, v2 = _attn_sample(z_all, sinks_row, cache_swa_k[l].reshape(bs, WINDOW, KV_WIDTH),
                               cache_swa_v[l].reshape(bs, WINDOW, KV_WIDTH), tp, ls)
    mh_p, c1, n1, m1, w2 = _mlstm_prompt(z_all, bias_row, norm_w_row, bp, sp, w_e2[l])
    mh_s, c2, n2, m2 = _mlstm_sample(z_all, bias_row, norm_w_row, state_mlstm_C[l], state_mlstm_n[l],
                                     state_mlstm_m[l], tp, ls)

    x1, h2, logits = _merge(xp, xs, a_p, a_s, mh_p, mh_s, z_all, wa, wm, wo, g_ffn[l], wr_pad, br_pad, 256)

    meta, gate, counts = _route(logits)
    n_blocks = -(-t_all * TOP_K // MOE_BLOCK_M) + N_EXPERTS
    pstarts, pends, padded, block_e, n_used = _slot_tables(counts[0, :N_EXPERTS].astype(jnp.int32), n_blocks)
    dest = pstarts[meta[:, LANE_E:LANE_E + TOP_K]] + meta[:, LANE_RANK:LANE_RANK + TOP_K]
    xs_sorted = _scatter_rows(h2, dest, pends, padded, n_blocks * MOE_BLOCK_M)
    ys = _experts(block_e, n_used, xs_sorted, w1, b1, w2, b2)
    y_prompt = _combine(x1, gate, dest, ys, g_final, 0, tp).reshape(bp, sp, D_MODEL)
    y_sample = _combine(x1, gate, dest, ys, g_final, tp, ts).reshape(bs, SAMPLE_ROWS, D_MODEL)[:, :ls]
    k1 = k1.reshape(bp, WINDOW, N_KV_HEADS, HEAD_DIM)
    v1 = v1.reshape(bp, WINDOW, N_KV_HEADS, HEAD_DIM)
    k2 = k2.reshape(bs, WINDOW, N_KV_HEADS, HEAD_DIM)
    v2 = v2.reshape(bs, WINDOW, N_KV_HEADS, HEAD_DIM)
    m1 = m1[:, :, 0]
    st = lambda t: t[None]
    return (y_prompt, y_sample, st(k1), st(v1), st(c1), st(n1), st(m1), st(k2), st(v2), st(c2), st(n2), st(m2))
```
